```python
import jax, jax.numpy as jnp
from jax import lax
import numpy as np

D_MODEL = 2048
BATCH = 2
SEQ = 4096
DEPTH = 1

CHUNK = 64
Q_BLOCK = 128
ROPE_THETA = 500000.0
N_MEM = 256
EPS = 1e-6

MLA_HEADS = 8
MLA_NOPE = 128
MLA_ROPE = 64
MLA_V = 128
MLA_Q_LORA = 512
MLA_KV_LORA = 256
DSA_HEADS = 8
DSA_HEAD_DIM = 128
DSA_ROT = DSA_HEAD_DIM // 4
IDX_HEADS = 16
IDX_DIM = 64
IDX_ROT = IDX_DIM // 4
TOPK_MAX = 256
X_HEADS = 4
X_HEAD_DIM = 128
D_FF = 4 * D_MODEL

MIX_WIDTH = MLA_HEADS * MLA_V + DSA_HEADS * DSA_HEAD_DIM
IN_SPLITS = (
    MLA_Q_LORA,
    MLA_KV_LORA,
    MLA_ROPE,
    DSA_HEADS * DSA_HEAD_DIM,
    DSA_HEADS * DSA_HEAD_DIM,
    DSA_HEADS * DSA_HEAD_DIM,
    IDX_HEADS * IDX_DIM,
    IDX_DIM,
    IDX_HEADS,
)
D_IN = int(sum(IN_SPLITS))

kernel_name = "hymba_mla_dsa_stream_block"


def rmsnorm(x, g):
    xf = x.astype(jnp.float32)
    y = xf * lax.rsqrt(jnp.mean(xf * xf, axis=-1, keepdims=True) + EPS)
    return (y * g.astype(jnp.float32)).astype(x.dtype)


def rope(x, pos, rot_dim):
    half = rot_dim // 2
    inv_freq = ROPE_THETA ** (-jnp.arange(half, dtype=jnp.float32) / half)
    ang = pos.astype(jnp.float32)[..., None] * inv_freq
    ang = ang.reshape(ang.shape[:2] + (1,) * (x.ndim - 3) + (half,))
    cos, sin = jnp.cos(ang), jnp.sin(ang)
    xf = x.astype(jnp.float32)
    x1, x2, rest = xf[..., :half], xf[..., half:rot_dim], xf[..., rot_dim:]
    out = jnp.concatenate([x1 * cos - x2 * sin, x2 * cos + x1 * sin, rest], axis=-1)
    return out.astype(x.dtype)


def to_blocks(a):
    b, s = a.shape[:2]
    a = a.reshape((b, s // Q_BLOCK, Q_BLOCK) + a.shape[2:])
    return jnp.moveaxis(a, 1, 0)


def from_blocks(a):
    a = jnp.moveaxis(a, 0, 1)
    return a.reshape((a.shape[0], a.shape[1] * a.shape[2]) + a.shape[3:])


def query_chunks(i):
    return (i * Q_BLOCK + jnp.arange(Q_BLOCK)) // CHUNK


def mla_mixer(c_q, c_kv, k_rope, pos, g_cq, g_ckv, w_qb, w_kvb):
    b, s, _ = c_q.shape
    q = (rmsnorm(c_q, g_cq) @ w_qb).reshape(b, s, MLA_HEADS, MLA_NOPE + MLA_ROPE)
    q_nope = q[..., :MLA_NOPE]
    q_rope = rope(q[..., MLA_NOPE:], pos, MLA_ROPE)
    kv = (rmsnorm(c_kv, g_ckv) @ w_kvb).reshape(b, s, MLA_HEADS, MLA_NOPE + MLA_V)
    k_nope, v = kv[..., :MLA_NOPE], kv[..., MLA_NOPE:]
    k_rope = rope(k_rope, pos, MLA_ROPE)
    scale = (MLA_NOPE + MLA_ROPE) ** -0.5
    key_chunk = jnp.arange(s) // CHUNK

    def block(args):
        i, qn, qr = args
        sc = (jnp.einsum('bqhd,bkhd->bhqk', qn, k_nope)
              + jnp.einsum('bqhr,bkr->bhqk', qr, k_rope)).astype(jnp.float32) * scale
        allowed = key_chunk[None, :] <= query_chunks(i)[:, None]
        sc = jnp.where(allowed[None, None], sc, -jnp.inf)
        p = jax.nn.softmax(sc, axis=-1).astype(v.dtype)
        return jnp.einsum('bhqk,bkhd->bqhd', p, v)

    o = lax.map(block, (jnp.arange(s // Q_BLOCK), to_blocks(q_nope), to_blocks(q_rope)))
    return from_blocks(o).reshape(b, s, MLA_HEADS * MLA_V)


def dsa_mixer(q, k, v, q_idx, k_idx, w_idx, pos):
    b, s, _ = q.shape
    q = rope(q.reshape(b, s, DSA_HEADS, DSA_HEAD_DIM), pos, DSA_ROT)
    k = rope(k.reshape(b, s, DSA_HEADS, DSA_HEAD_DIM), pos, DSA_ROT)
    v = v.reshape(b, s, DSA_HEADS, DSA_HEAD_DIM)
    q_idx = rope(q_idx.reshape(b, s, IDX_HEADS, IDX_DIM), pos, IDX_ROT)
    k_idx = rope(k_idx, pos, IDX_ROT)
    w_idx = w_idx * (IDX_HEADS ** -0.5)
    n_sel = min(TOPK_MAX, s // 4)
    key_chunk = jnp.arange(s) // CHUNK
    gather = jax.vmap(lambda arr, idx: arr[idx])

    def block(args):
        i, qb, qib, wb = args
        qc = query_chunks(i)
        logits = jax.nn.relu(jnp.einsum('bqhd,bkd->bqhk', qib, k_idx).astype(jnp.float32) * (IDX_DIM ** -0.5))
        score = jnp.einsum('bqhk,bqh->bqk', logits, wb.astype(jnp.float32))
        allowed = key_chunk[None, :] <= qc[:, None]
        score = jnp.where(allowed[None], score, -jnp.inf)
        _, sel = lax.top_k(score, n_sel)
        sel_ok = (sel // CHUNK) <= qc[None, :, None]
        k_sel = gather(k, sel)
        v_sel = gather(v, sel)
        sc = jnp.einsum('bqhd,bqnhd->bhqn', qb, k_sel).astype(jnp.float32) * (DSA_HEAD_DIM ** -0.5)
        sc = jnp.where(sel_ok[:, None], sc, -jnp.inf)
        p = jax.nn.softmax(sc, axis=-1).astype(v.dtype)
        return jnp.einsum('bhqn,bqnhd->bqhd', p, v_sel)

    o = lax.map(block, (jnp.arange(s // Q_BLOCK), to_blocks(q), to_blocks(q_idx), to_blocks(w_idx)))
    return from_blocks(o).reshape(b, s, DSA_HEADS * DSA_HEAD_DIM)


def memory_cross_attention(h, mem_n, wq, wk, wv, wo):
    b, s, _ = h.shape
    m = mem_n.shape[1]
    q = (h @ wq).reshape(b, s, X_HEADS, X_HEAD_DIM)
    k = (mem_n @ wk).reshape(b, m, X_HEADS, X_HEAD_DIM)
    v = (mem_n @ wv).reshape(b, m, X_HEADS, X_HEAD_DIM)
    sc = jnp.einsum('bshd,bmhd->bhsm', q, k).astype(jnp.float32) * (X_HEAD_DIM ** -0.5)
    p = jax.nn.softmax(sc, axis=-1).astype(v.dtype)
    o = jnp.einsum('bhsm,bmhd->bshd', p, v).reshape(b, s, X_HEADS * X_HEAD_DIM)
    return o @ wo


def setup_inputs(seed: int = 0) -> dict:
    key = jax.random.key(seed)
    ks = jax.random.split(key, 24)
    f32 = jnp.float32

    def w(k, shape, fan_in):
        return jax.random.normal(k, (DEPTH,) + shape, f32) * (fan_in ** -0.5)

    def gain(k, n):
        return 1.0 + 0.02 * jax.random.normal(k, (DEPTH, n), f32)

    x = jax.random.normal(ks[0], (BATCH, SEQ, D_MODEL), f32)
    mem = jax.random.normal(ks[1], (BATCH, N_MEM, D_MODEL), f32)
    offset = jax.random.randint(ks[2], (BATCH, 1), 0, 16, dtype=jnp.int32) * CHUNK
    positions = (offset + jnp.arange(SEQ, dtype=jnp.int32)[None, :]).astype(jnp.int32)
    return {
        "x": x,
        "mem": mem,
        "positions": positions,
        "g_mix": gain(ks[3], D_MODEL),
        "w_in": w(ks[4], (D_MODEL, D_IN), D_MODEL),
        "g_cq": gain(ks[5], MLA_Q_LORA),
        "g_ckv": gain(ks[6], MLA_KV_LORA),
        "w_qb": w(ks[7], (MLA_Q_LORA, MLA_HEADS * (MLA_NOPE + MLA_ROPE)), MLA_Q_LORA),
        "w_kvb": w(ks[8], (MLA_KV_LORA, MLA_HEADS * (MLA_NOPE + MLA_V)), MLA_KV_LORA),
        "w_out": w(ks[9], (MIX_WIDTH, D_MODEL), MIX_WIDTH),
        "g_cross": gain(ks[10], D_MODEL),
        "g_mem": gain(ks[11], D_MODEL),
        "w_q_cross": w(ks[12], (D_MODEL, X_HEADS * X_HEAD_DIM), D_MODEL),
        "w_k_cross": w(ks[13], (D_MODEL, X_HEADS * X_HEAD_DIM), D_MODEL),
        "w_v_cross": w(ks[14], (D_MODEL, X_HEADS * X_HEAD_DIM), D_MODEL),
        "w_o_cross": w(ks[15], (X_HEADS * X_HEAD_DIM, D_MODEL), X_HEADS * X_HEAD_DIM),
        "g_mlp": gain(ks[16], D_MODEL),
        "w_up": w(ks[17], (D_MODEL, D_FF), D_MODEL),
        "w_down": w(ks[18], (D_FF, D_MODEL), D_FF),
        "g_final": 1.0 + 0.02 * jax.random.normal(ks[19], (D_MODEL,), f32),
    }


def reference(x, mem, positions, g_mix, w_in, g_cq, g_ckv, w_qb, w_kvb, w_out,
              g_cross, g_mem, w_q_cross, w_k_cross, w_v_cross, w_o_cross,
              g_mlp, w_up, w_down, g_final):
    offsets = np.cumsum(np.array(IN_SPLITS))[:-1].tolist()
    for l in range(DEPTH):
        h = rmsnorm(x, g_mix[l])
        c_q, c_kv, k_rope, q_d, k_d, v_d, q_idx, k_idx, w_idx = jnp.split(h @ w_in[l], offsets, axis=-1)
        a = mla_mixer(c_q, c_kv, k_rope, positions, g_cq[l], g_ckv[l], w_qb[l], w_kvb[l])
        bb = dsa_mixer(q_d, k_d, v_d, q_idx, k_idx, w_idx, positions)
        x = x + jnp.concatenate([a, bb], axis=-1) @ w_out[l]
        hc = rmsnorm(x, g_cross[l])
        mem_n = rmsnorm(mem, g_mem[l])
        x = x + memory_cross_attention(hc, mem_n, w_q_cross[l], w_k_cross[l], w_v_cross[l], w_o_cross[l])
        hm = rmsnorm(x, g_mlp[l])
        x = x + jnp.square(jax.nn.relu(hm @ w_up[l])) @ w_down[l]
    return rmsnorm(x, g_final)
```

```python
import functools

import jax
import jax.numpy as jnp
from jax import lax
from jax.experimental import pallas as pl
from jax.experimental.pallas import tpu as pltpu

F32 = jnp.float32
BF16 = jnp.bfloat16
I32 = jnp.int32

D_MODEL = 2048
CHUNK = 64
ROPE_THETA = 500000.0
N_MEM = 256
EPS = 1e-6
MLA_HEADS = 8
MLA_NOPE = 128
MLA_ROPE = 64
MLA_V = 128
MLA_Q_LORA = 512
MLA_KV_LORA = 256
DSA_HEADS = 8
DSA_HEAD_DIM = 128
DSA_ROT = DSA_HEAD_DIM // 4
IDX_HEADS = 16
IDX_DIM = 64
IDX_ROT = IDX_DIM // 4
TOPK_MAX = 256
X_HEADS = 4
X_HEAD_DIM = 128
D_FF = 4 * D_MODEL

LANES = 128
MLA_QK_PAD = 256
NEG = -1e30
INT_MIN = -2 ** 31
VMEM_LIMIT = 56 * 1024 * 1024


def _params(sem, vmem=VMEM_LIMIT):
    return pltpu.CompilerParams(dimension_semantics=sem, vmem_limit_bytes=vmem)


def _rms(xf, g):
    return xf * lax.rsqrt(jnp.mean(xf * xf, axis=-1, keepdims=True) + EPS) * g


def _rope_coeffs(pos, invf, half, period):
    ang = pos.astype(F32) * invf
    cos, sin = jnp.cos(ang), jnp.sin(ang)
    lane = lax.broadcasted_iota(I32, (1, LANES), 1) & (period - 1)
    in_lo = lane < half
    in_hi = (lane >= half) & (lane < 2 * half)
    c = jnp.where(in_lo | in_hi, cos, 1.0)
    s_lo = jnp.where(in_lo, -sin, 0.0)
    s_hi = jnp.where(in_hi, sin, 0.0)
    return c, s_lo, s_hi


def _apply_rope(x, coeffs, half):
    c, s_lo, s_hi = coeffs
    return x * c + pltpu.roll(x, LANES - half, 1) * s_lo + pltpu.roll(x, half, 1) * s_hi


def _invf_lanes(half, period):
    inv_freq = ROPE_THETA ** (-jnp.arange(half, dtype=F32) / half)
    lane = jnp.arange(LANES) % period
    return inv_freq[lane % half].reshape(1, LANES)


def _dot(a, b):
    return jnp.dot(a, b, preferred_element_type=F32)


def _dot_nt(a, b):
    return lax.dot_general(a, b, (((1,), (1,)), ((), ())), preferred_element_type=F32)


def _proj_q_kernel(x_ref, pos_ref, g_ref, wcq_ref, gcq_ref, wqb_ref, invf_ref, q_ref):
    h = _rms(x_ref[...], g_ref[...]).astype(BF16)
    cq = _dot(h, wcq_ref[...])
    q = _dot(_rms(cq, gcq_ref[...]).astype(BF16), wqb_ref[...])
    coeffs = _rope_coeffs(pos_ref[...], invf_ref[...], MLA_ROPE // 2, LANES)
    scale = (MLA_NOPE + MLA_ROPE) ** -0.5
    for hh in range(MLA_HEADS):
        b0 = hh * MLA_QK_PAD
        q_ref[:, b0:b0 + LANES] = (q[:, b0:b0 + LANES] * scale).astype(BF16)
        r = _apply_rope(q[:, b0 + LANES:b0 + 2 * LANES], coeffs, MLA_ROPE // 2)
        q_ref[:, b0 + LANES:b0 + 2 * LANES] = (r * scale).astype(BF16)


def _proj_kv_kernel(x_ref, pos_ref, g_ref, w_ref, gckv_ref, wkvb_ref, invf_ref, k_ref, v_ref):
    h = _rms(x_ref[...], g_ref[...]).astype(BF16)
    r = _dot(h, w_ref[...])
    kv = _dot(_rms(r[:, :MLA_KV_LORA], gckv_ref[...]).astype(BF16), wkvb_ref[...])
    coeffs = _rope_coeffs(pos_ref[...], invf_ref[...], MLA_ROPE // 2, LANES)
    kr = _apply_rope(r[:, MLA_KV_LORA:], coeffs, MLA_ROPE // 2).astype(BF16)
    for hh in range(MLA_HEADS):
        b0 = hh * MLA_QK_PAD
        k_ref[:, b0:b0 + LANES] = kv[:, hh * LANES:(hh + 1) * LANES].astype(BF16)
        k_ref[:, b0 + LANES:b0 + 2 * LANES] = kr
    v_ref[...] = kv[:, MLA_HEADS * MLA_NOPE:].astype(BF16)


def _proj_dsa_kernel(x_ref, pos_ref, g_ref, w_ref, invf_ref, o_ref, h_scr):
    j = pl.program_id(1)

    @pl.when(j == 0)
    def _():
        h_scr[...] = _rms(x_ref[...], g_ref[...]).astype(BF16)

    r = _dot(h_scr[...], w_ref[...])

    @pl.when(j == 2)
    def _():
        o_ref[...] = r.astype(BF16)

    @pl.when(j < 2)
    def _():
        coeffs = _rope_coeffs(pos_ref[...], invf_ref[...], DSA_ROT // 2, LANES)
        scale = jnp.where(j == 0, DSA_HEAD_DIM ** -0.5, 1.0).astype(F32)
        for hh in range(DSA_HEADS):
            sl = slice(hh * LANES, (hh + 1) * LANES)
            o_ref[:, sl] = (_apply_rope(r[:, sl], coeffs, DSA_ROT // 2) * scale).astype(BF16)


def _proj_idx_kernel(x_ref, pos_ref, g_ref, w_ref, invf_ref, qi_ref, ki_ref, wi_ref):
    h = _rms(x_ref[...], g_ref[...]).astype(BF16)
    r = _dot(h, w_ref[...])
    coeffs = _rope_coeffs(pos_ref[...], invf_ref[...], IDX_ROT // 2, IDX_DIM)
    lane = lax.broadcasted_iota(I32, (1, LANES), 1)
    nq = IDX_HEADS * IDX_DIM
    for p in range(IDX_HEADS // 2):
        t = _apply_rope(r[:, p * LANES:(p + 1) * LANES], coeffs, IDX_ROT // 2) * (IDX_DIM ** -0.5)
        qi_ref[:, (2 * p) * LANES:(2 * p + 1) * LANES] = jnp.where(lane < IDX_DIM, t, 0.0).astype(BF16)
        qi_ref[:, (2 * p + 1) * LANES:(2 * p + 2) * LANES] = jnp.where(lane >= IDX_DIM, t, 0.0).astype(BF16)
    ki_ref[...] = _apply_rope(r[:, nq:nq + LANES], coeffs, IDX_ROT // 2).astype(BF16)
    wi_ref[...] = r[:, nq + LANES:nq + 2 * LANES] * (IDX_HEADS ** -0.5)


def _softmax_step(s, v, m_scr, l_scr, acc_scr):
    m_prev = m_scr[...]
    m_new = jnp.maximum(m_prev, jnp.max(s, axis=-1, keepdims=True))
    alpha = jnp.exp(m_prev - m_new)
    p = jnp.exp(s - m_new)
    l_scr[...] = alpha * l_scr[...] + jnp.sum(p, axis=-1, keepdims=True)
    acc_scr[...] = alpha * acc_scr[...] + _dot(p.astype(BF16), v)
    m_scr[...] = m_new


def _mla_attn_kernel(q_ref, k_ref, v_ref, o_ref, m_scr, l_scr, acc_scr, *, tq):
    i = pl.program_id(2)
    q = q_ref[...]
    m_scr[...] = jnp.full(m_scr.shape, NEG, F32)
    l_scr[...] = jnp.zeros(l_scr.shape, F32)
    acc_scr[...] = jnp.zeros(acc_scr.shape, F32)

    def full_tile(j, c):
        off = pl.multiple_of(j * tq, tq)
        s = _dot_nt(q, k_ref[pl.ds(off, tq), :])
        _softmax_step(s, v_ref[pl.ds(off, tq), :], m_scr, l_scr, acc_scr)
        return c

    lax.fori_loop(0, i, full_tile, 0)

    off = pl.multiple_of(i * tq, tq)
    row_chunk = lax.broadcasted_iota(I32, (tq, 1), 0) // CHUNK
    col_chunk = lax.broadcasted_iota(I32, (1, tq), 1) // CHUNK
    s = _dot_nt(q, k_ref[pl.ds(off, tq), :])
    s = jnp.where(col_chunk <= row_chunk, s, NEG)
    _softmax_step(s, v_ref[pl.ds(off, tq), :], m_scr, l_scr, acc_scr)
    o_ref[...] = (acc_scr[...] / l_scr[...]).astype(BF16)


def _dsa_kernel(q_ref, k_ref, v_ref, qi_ref, ki_ref, wi_ref, o_ref,
                key_scr, bias_scr, cnt_scr, m_scr, l_scr, acc_scr, *, tq, tk, n_sel, seq):
    i = pl.program_id(1)
    n_valid = ((i + 1) * tq + tk - 1) // tk
    row_chunk = (i * tq + lax.broadcasted_iota(I32, (tq, 1), 0)) // CHUNK
    neg_inf_key = I32(0x807FFFFF - 2 ** 32)

    def col_ids(off):
        return off + lax.broadcasted_iota(I32, (1, tk), 1)

    wi = wi_ref[...]

    def score_tile(j, c):
        off = pl.multiple_of(j * tk, tk)
        kj = ki_ref[pl.ds(off, tk), :]
        sc = jnp.zeros((tq, tk), F32)
        for hh in range(IDX_HEADS):
            lg = _dot_nt(qi_ref[:, hh * LANES:(hh + 1) * LANES], kj)
            sc = sc + jnp.maximum(lg, 0.0) * wi[:, hh:hh + 1]
        sc = jnp.where(sc == 0.0, 0.0, sc)
        sc = jnp.where(col_ids(off) // CHUNK <= row_chunk, sc, -jnp.inf)
        bits = pltpu.bitcast(sc, I32)
        key_scr[:, pl.ds(off, tk)] = bits ^ ((bits >> 31) & I32(0x7FFFFFFF))
        return c

    lax.fori_loop(0, n_valid, score_tile, 0)

    def count(pred):
        cnt_scr[...] = jnp.zeros(cnt_scr.shape, I32)

        def body(j, c):
            off = pl.multiple_of(j * tk, tk)
            hit = pred(key_scr[:, pl.ds(off, tk)], col_ids(off)).astype(I32)
            part = hit[:, :LANES]
            for cc in range(1, tk // LANES):
                part = part + hit[:, cc * LANES:(cc + 1) * LANES]
            cnt_scr[...] += part
            return c

        lax.fori_loop(0, n_valid, body, 0)
        return jnp.sum(cnt_scr[...], axis=-1, keepdims=True)

    def thr_bit(it, tb):
        cand_b = tb | (I32(1) << (31 - it))
        cand = cand_b ^ I32(INT_MIN)
        cnt = count(lambda key, col: key >= cand)
        return jnp.where(cnt >= n_sel, cand_b, tb)

    thr = lax.fori_loop(0, 32, thr_bit, jnp.zeros((tq, 1), I32)) ^ I32(INT_MIN)

    cnt_gt = count(lambda key, col: key > thr)
    cnt_ge = count(lambda key, col: key >= thr)
    need = n_sel - cnt_gt
    tied = (cnt_ge > n_sel) & (thr != neg_inf_key)
    any_tied = jnp.max(tied.astype(I32)) > 0
    idx_bits = (2 * seq - 1).bit_length()

    def tie_cut():
        def cut_bit(it, jc):
            cand = jc | (I32(1) << (idx_bits - 1 - it))
            cnt = count(lambda key, col: (key == thr) & (col < cand))
            return jnp.where(cnt <= need, cand, jc)
        return lax.fori_loop(0, idx_bits, cut_bit, jnp.zeros((tq, 1), I32))

    jcut = lax.cond(any_tied, tie_cut, lambda: jnp.full((tq, 1), 2 ** idx_bits - 1, I32))

    def bias_tile(j, c):
        off = pl.multiple_of(j * tk, tk)
        key = key_scr[:, pl.ds(off, tk)]
        col = col_ids(off)
        sel = (key > thr) | ((key == thr) & (col < jcut))
        sel = sel & (col // CHUNK <= row_chunk)
        bias_scr[:, pl.ds(off, tk)] = jnp.where(sel, 0.0, NEG).astype(F32)
        return c

    lax.fori_loop(0, n_valid, bias_tile, 0)

    for hh in range(DSA_HEADS):
        sl = slice(hh * DSA_HEAD_DIM, (hh + 1) * DSA_HEAD_DIM)
        qh = q_ref[:, sl]
        m_scr[...] = jnp.full(m_scr.shape, NEG, F32)
        l_scr[...] = jnp.zeros(l_scr.shape, F32)
        acc_scr[...] = jnp.zeros(acc_scr.shape, F32)

        def kv_tile(j, c):
            off = pl.multiple_of(j * tk, tk)
            s = _dot_nt(qh, k_ref[pl.ds(off, tk), sl]) + bias_scr[:, pl.ds(off, tk)]
            _softmax_step(s, v_ref[pl.ds(off, tk), sl], m_scr, l_scr, acc_scr)
            return c

        lax.fori_loop(0, n_valid, kv_tile, 0)
        o_ref[:, sl] = (acc_scr[...] / l_scr[...]).astype(BF16)


def _mem_kv_kernel(mem_ref, g_ref, wk_ref, wv_ref, k_ref, v_ref):
    mn = _rms(mem_ref[...], g_ref[...]).astype(BF16)
    k_ref[...] = _dot(mn, wk_ref[...]).astype(BF16)
    v_ref[...] = _dot(mn, wv_ref[...]).astype(BF16)


def _post_kernel(a_ref, b_ref, x_ref, woa_ref, wob_ref, gc_ref, wq_ref, km_ref, vm_ref, wo_ref, gm_ref,
                 x2_ref, hm_ref):
    x1 = x_ref[...] + _dot(a_ref[...], woa_ref[...]) + _dot(b_ref[...], wob_ref[...])
    hc = _rms(x1, gc_ref[...]).astype(BF16)
    qc = (_dot(hc, wq_ref[...]) * (X_HEAD_DIM ** -0.5)).astype(BF16)
    outs = []
    for hh in range(X_HEADS):
        sl = slice(hh * X_HEAD_DIM, (hh + 1) * X_HEAD_DIM)
        s = _dot_nt(qc[:, sl], km_ref[:, sl])
        p = jnp.exp(s - jnp.max(s, axis=-1, keepdims=True))
        o = _dot(p.astype(BF16), vm_ref[:, sl]) / jnp.sum(p, axis=-1, keepdims=True)
        outs.append(o.astype(BF16))
    x2 = x1 + _dot(jnp.concatenate(outs, axis=-1), wo_ref[...])
    x2_ref[...] = x2
    hm_ref[...] = _rms(x2, gm_ref[...]).astype(BF16)


def _mlp_kernel(hm_ref, x2_ref, wu_ref, wd_ref, gf_ref, y_ref, acc_scr):
    f = pl.program_id(1)

    @pl.when(f == 0)
    def _():
        acc_scr[...] = x2_ref[...]

    u = jnp.maximum(_dot(hm_ref[...], wu_ref[...]), 0.0)
    acc_scr[...] += _dot((u * u).astype(BF16), wd_ref[...])

    @pl.when(f == pl.num_programs(1) - 1)
    def _():
        y_ref[...] = _rms(acc_scr[...], gf_ref[...])


def _const(shape):
    return pl.BlockSpec(shape, lambda *_: (0,) * len(shape))


def kernel(x, mem, positions, g_mix, w_in, g_cq, g_ckv, w_qb, w_kvb, w_out, g_cross, g_mem,
           w_q_cross, w_k_cross, w_v_cross, w_o_cross, g_mlp, w_up, w_down, g_final):
    assert w_in.shape[0] == 1, "one layer"
    batch, seq, d = x.shape
    t = batch * seq
    n_sel = min(TOPK_MAX, seq // 4)
    x2d = x.reshape(t, d)
    pos = positions.reshape(t, 1)
    row = lambda g: g.reshape(1, -1).astype(F32)

    w = w_in[0]
    o = [0]
    for n in (MLA_Q_LORA, MLA_KV_LORA, MLA_ROPE, DSA_HEADS * DSA_HEAD_DIM, DSA_HEADS * DSA_HEAD_DIM,
              DSA_HEADS * DSA_HEAD_DIM, IDX_HEADS * IDX_DIM, IDX_DIM, IDX_HEADS):
        o.append(o[-1] + n)
    zeros = lambda n: jnp.zeros((d, n), w.dtype)
    w_cq = w[:, o[0]:o[1]].astype(BF16)
    w_ckv = jnp.concatenate([w[:, o[1]:o[3]], zeros(LANES - MLA_ROPE)], axis=1).astype(BF16)
    w_dsa = w[:, o[3]:o[6]].astype(BF16)
    w_idx = jnp.concatenate([w[:, o[6]:o[7]], w[:, o[7]:o[8]], w[:, o[7]:o[8]], w[:, o[8]:o[9]],
                             zeros(LANES - IDX_HEADS)], axis=1).astype(BF16)
    wqb = w_qb[0].reshape(MLA_Q_LORA, MLA_HEADS, MLA_NOPE + MLA_ROPE)
    wqb = jnp.pad(wqb, ((0, 0), (0, 0), (0, MLA_QK_PAD - MLA_NOPE - MLA_ROPE)))
    wqb = wqb.reshape(MLA_Q_LORA, MLA_HEADS * MLA_QK_PAD).astype(BF16)
    wkvb = w_kvb[0].reshape(MLA_KV_LORA, MLA_HEADS, MLA_NOPE + MLA_V)
    wkvb = jnp.concatenate([wkvb[:, :, :MLA_NOPE].reshape(MLA_KV_LORA, -1),
                            wkvb[:, :, MLA_NOPE:].reshape(MLA_KV_LORA, -1)], axis=1).astype(BF16)
    mla_w = MLA_HEADS * MLA_V
    w_out_a = w_out[0, :mla_w].astype(BF16)
    w_out_b = w_out[0, mla_w:].astype(BF16)
    wq_c, wk_c, wv_c, wo_c = (a[0].astype(BF16) for a in (w_q_cross, w_k_cross, w_v_cross, w_o_cross))
    wu, wd = w_up[0].astype(BF16), w_down[0].astype(BF16)
    invf_mla = _invf_lanes(MLA_ROPE // 2, LANES)
    invf_dsa = _invf_lanes(DSA_ROT // 2, LANES)
    invf_idx = _invf_lanes(IDX_ROT // 2, IDX_DIM)

    tm = 512
    nt = t // tm
    x_spec = pl.BlockSpec((tm, d), lambda i, *_: (i, 0))
    pos_spec = pl.BlockSpec((tm, 1), lambda i, *_: (i, 0))
    qk_w = MLA_HEADS * MLA_QK_PAD

    q_mla = pl.pallas_call(
        _proj_q_kernel, name="proj_q", grid=(nt,),
        in_specs=[x_spec, pos_spec, _const((1, d)), _const((d, MLA_Q_LORA)), _const((1, MLA_Q_LORA)),
                  _const((MLA_Q_LORA, qk_w)), _const((1, LANES))],
        out_specs=pl.BlockSpec((tm, qk_w), lambda i: (i, 0)),
        out_shape=jax.ShapeDtypeStruct((t, qk_w), BF16),
        compiler_params=_params(("parallel",)),
    )(x2d, pos, row(g_mix), w_cq, row(g_cq), wqb, invf_mla)

    k_mla, v_mla = pl.pallas_call(
        _proj_kv_kernel, name="proj_kv", grid=(nt,),
        in_specs=[x_spec, pos_spec, _const((1, d)), _const((d, MLA_KV_LORA + LANES)), _const((1, MLA_KV_LORA)),
                  _const((MLA_KV_LORA, 2 * mla_w)), _const((1, LANES))],
        out_specs=[pl.BlockSpec((tm, qk_w), lambda i: (i, 0)), pl.BlockSpec((tm, mla_w), lambda i: (i, 0))],
        out_shape=[jax.ShapeDtypeStruct((t, qk_w), BF16), jax.ShapeDtypeStruct((t, mla_w), BF16)],
        compiler_params=_params(("parallel",)),
    )(x2d, pos, row(g_mix), w_ckv, row(g_ckv), wkvb, invf_mla)

    dsa_w = DSA_HEADS * DSA_HEAD_DIM
    qkv_d = pl.pallas_call(
        _proj_dsa_kernel, name="proj_dsa", grid=(nt, 3),
        in_specs=[x_spec, pos_spec, _const((1, d)), pl.BlockSpec((d, dsa_w), lambda i, j: (0, j)),
                  _const((1, LANES))],
        out_specs=pl.BlockSpec((None, tm, dsa_w), lambda i, j: (j, i, 0)),
        out_shape=jax.ShapeDtypeStruct((3, t, dsa_w), BF16),
        scratch_shapes=[pltpu.VMEM((tm, d), BF16)],
        compiler_params=_params(("parallel", "arbitrary")),
    )(x2d, pos, row(g_mix), w_dsa, invf_dsa)

    qi_w = IDX_HEADS * LANES
    q_idx, k_idx, w_idx_s = pl.pallas_call(
        _proj_idx_kernel, name="proj_idx", grid=(nt,),
        in_specs=[x_spec, pos_spec, _const((1, d)), _const((d, IDX_HEADS * IDX_DIM + 2 * LANES)),
                  _const((1, LANES))],
        out_specs=[pl.BlockSpec((tm, qi_w), lambda i: (i, 0)), pl.BlockSpec((tm, LANES), lambda i: (i, 0)),
                   pl.BlockSpec((tm, LANES), lambda i: (i, 0))],
        out_shape=[jax.ShapeDtypeStruct((t, qi_w), BF16), jax.ShapeDtypeStruct((t, LANES), BF16),
                   jax.ShapeDtypeStruct((t, LANES), F32)],
        compiler_params=_params(("parallel",)),
    )(x2d, pos, row(g_mix), w_idx, invf_idx)

    tq = 512
    a_mla = pl.pallas_call(
        functools.partial(_mla_attn_kernel, tq=tq), name="mla_attn",
        grid=(batch, MLA_HEADS, seq // tq),
        in_specs=[pl.BlockSpec((None, tq, MLA_QK_PAD), lambda b, h, i: (b, i, h)),
                  pl.BlockSpec((None, seq, MLA_QK_PAD), lambda b, h, i: (b, 0, h)),
                  pl.BlockSpec((None, seq, MLA_V), lambda b, h, i: (b, 0, h))],
        out_specs=pl.BlockSpec((None, tq, MLA_V), lambda b, h, i: (b, i, h)),
        out_shape=jax.ShapeDtypeStruct((batch, seq, mla_w), BF16),
        scratch_shapes=[pltpu.VMEM((tq, 1), F32), pltpu.VMEM((tq, 1), F32), pltpu.VMEM((tq, MLA_V), F32)],
        compiler_params=_params(("parallel", "parallel", "arbitrary")),
    )(q_mla.reshape(batch, seq, qk_w), k_mla.reshape(batch, seq, qk_w), v_mla.reshape(batch, seq, mla_w))

    tqd, tkd = 256, 512
    qkv4 = qkv_d.reshape(3, batch, seq, dsa_w)
    b_dsa = pl.pallas_call(
        functools.partial(_dsa_kernel, tq=tqd, tk=tkd, n_sel=n_sel, seq=seq), name="dsa",
        grid=(batch, seq // tqd),
        in_specs=[pl.BlockSpec((None, None, tqd, dsa_w), lambda b, i: (0, b, i, 0)),
                  pl.BlockSpec((None, None, seq, dsa_w), lambda b, i: (1, b, 0, 0), pipeline_mode=pl.Buffered(1)),
                  pl.BlockSpec((None, None, seq, dsa_w), lambda b, i: (2, b, 0, 0), pipeline_mode=pl.Buffered(1)),
                  pl.BlockSpec((None, tqd, qi_w), lambda b, i: (b, i, 0)),
                  pl.BlockSpec((None, seq, LANES), lambda b, i: (b, 0, 0)),
                  pl.BlockSpec((None, tqd, LANES), lambda b, i: (b, i, 0))],
        out_specs=pl.BlockSpec((None, tqd, dsa_w), lambda b, i: (b, i, 0)),
        out_shape=jax.ShapeDtypeStruct((batch, seq, dsa_w), BF16),
        scratch_shapes=[pltpu.VMEM((tqd, seq), I32), pltpu.VMEM((tqd, seq), F32), pltpu.VMEM((tqd, LANES), I32),
                        pltpu.VMEM((tqd, 1), F32), pltpu.VMEM((tqd, 1), F32), pltpu.VMEM((tqd, DSA_HEAD_DIM), F32)],
        compiler_params=_params(("parallel", "arbitrary")),
    )(qkv4, qkv4, qkv4, q_idx.reshape(batch, seq, qi_w), k_idx.reshape(batch, seq, LANES),
      w_idx_s.reshape(batch, seq, LANES))

    xw = X_HEADS * X_HEAD_DIM
    n_mem = mem.shape[1]
    k_mem, v_mem = pl.pallas_call(
        _mem_kv_kernel, name="mem_kv", grid=(batch,),
        in_specs=[pl.BlockSpec((None, n_mem, d), lambda b: (b, 0, 0)), _const((1, d)), _const((d, xw)),
                  _const((d, xw))],
        out_specs=[pl.BlockSpec((None, n_mem, xw), lambda b: (b, 0, 0))] * 2,
        out_shape=[jax.ShapeDtypeStruct((batch, n_mem, xw), BF16)] * 2,
        compiler_params=_params(("parallel",)),
    )(mem, row(g_mem), wk_c, wv_c)

    tp = 256
    per_b = seq // tp
    mem_spec = pl.BlockSpec((None, n_mem, xw), lambda i: (i // per_b, 0, 0))
    x2, hm = pl.pallas_call(
        _post_kernel, name="post", grid=(t // tp,),
        in_specs=[pl.BlockSpec((tp, mla_w), lambda i: (i, 0)), pl.BlockSpec((tp, dsa_w), lambda i: (i, 0)),
                  pl.BlockSpec((tp, d), lambda i: (i, 0)), _const((mla_w, d)), _const((dsa_w, d)), _const((1, d)),
                  _const((d, xw)), mem_spec, mem_spec, _const((xw, d)), _const((1, d))],
        out_specs=[pl.BlockSpec((tp, d), lambda i: (i, 0))] * 2,
        out_shape=[jax.ShapeDtypeStruct((t, d), F32), jax.ShapeDtypeStruct((t, d), BF16)],
        compiler_params=_params(("parallel",)),
    )(a_mla.reshape(t, mla_w), b_dsa.reshape(t, dsa_w), x2d, w_out_a, w_out_b, row(g_cross), wq_c,
      k_mem, v_mem, wo_c, row(g_mlp))

    tmm, tf = 512, 512
    y = pl.pallas_call(
        _mlp_kernel, name="mlp", grid=(t // tmm, D_FF // tf),
        in_specs=[pl.BlockSpec((tmm, d), lambda i, f: (i, 0)), pl.BlockSpec((tmm, d), lambda i, f: (i, 0)),
                  pl.BlockSpec((d, tf), lambda i, f: (0, f)), pl.BlockSpec((tf, d), lambda i, f: (f, 0)),
                  _const((1, d))],
        out_specs=pl.BlockSpec((tmm, d), lambda i, f: (i, 0)),
        out_shape=jax.ShapeDtypeStruct((t, d), F32),
        scratch_shapes=[pltpu.VMEM((tmm, d), F32)],
        compiler_params=_params(("parallel", "arbitrary")),
    )(hm, x2, wu, wd, row(g_final))
    return y.reshape(batch, seq, d)
```

```python
import functools

import jax
import jax.numpy as jnp
from jax import lax
from jax.experimental import pallas as pl
from jax.experimental.pallas import tpu as pltpu

F32 = jnp.float32
BF16 = jnp.bfloat16
I32 = jnp.int32

D_MODEL = 2048
CHUNK = 64
ROPE_THETA = 500000.0
N_MEM = 256
EPS = 1e-6
MLA_HEADS = 8
MLA_NOPE = 128
MLA_ROPE = 64
MLA_V = 128
MLA_Q_LORA = 512
MLA_KV_LORA = 256
DSA_HEADS = 8
DSA_HEAD_DIM = 128
DSA_ROT = DSA_HEAD_DIM // 4
IDX_HEADS = 16
IDX_DIM = 64
IDX_ROT = IDX_DIM // 4
TOPK_MAX = 256
X_HEADS = 4
X_HEAD_DIM = 128
D_FF = 4 * D_MODEL

LANES = 128
MLA_QK_PAD = 256
NEG = -1e30
INT_MIN = -2 ** 31
LOG2E = 1.4426950408889634
VMEM_LIMIT = 56 * 1024 * 1024


def _params(sem, vmem=VMEM_LIMIT):
    return pltpu.CompilerParams(dimension_semantics=sem, vmem_limit_bytes=vmem)


def _rms(xf, g):
    return xf * lax.rsqrt(jnp.mean(xf * xf, axis=-1, keepdims=True) + EPS) * g


def _rope_coeffs(pos, invf, half, period):
    ang = pos.astype(F32) * invf
    cos, sin = jnp.cos(ang), jnp.sin(ang)
    lane = lax.broadcasted_iota(I32, (1, LANES), 1) & (period - 1)
    in_lo = lane < half
    in_hi = (lane >= half) & (lane < 2 * half)
    c = jnp.where(in_lo | in_hi, cos, 1.0)
    s_lo = jnp.where(in_lo, -sin, 0.0)
    s_hi = jnp.where(in_hi, sin, 0.0)
    return c, s_lo, s_hi


def _apply_rope(x, coeffs, half):
    c, s_lo, s_hi = coeffs
    return x * c + pltpu.roll(x, LANES - half, 1) * s_lo + pltpu.roll(x, half, 1) * s_hi


def _invf_lanes(half, period):
    inv_freq = ROPE_THETA ** (-jnp.arange(half, dtype=F32) / half)
    lane = jnp.arange(LANES) % period
    return inv_freq[lane % half].reshape(1, LANES)


def _dot(a, b):
    return jnp.dot(a, b, preferred_element_type=F32)


def _dot_nt(a, b):
    return lax.dot_general(a, b, (((1,), (1,)), ((), ())), preferred_element_type=F32)


def _proj_q_kernel(x_ref, pos_ref, g_ref, wcq_ref, gcq_ref, wqb_ref, invf_ref, q_ref):
    h = _rms(x_ref[...], g_ref[...]).astype(BF16)
    cq = _dot(h, wcq_ref[...])
    q = _dot(_rms(cq, gcq_ref[...]).astype(BF16), wqb_ref[...])
    coeffs = _rope_coeffs(pos_ref[...], invf_ref[...], MLA_ROPE // 2, LANES)
    scale = (MLA_NOPE + MLA_ROPE) ** -0.5 * LOG2E
    for hh in range(MLA_HEADS):
        b0 = hh * MLA_QK_PAD
        q_ref[:, b0:b0 + LANES] = (q[:, b0:b0 + LANES] * scale).astype(BF16)
        r = _apply_rope(q[:, b0 + LANES:b0 + 2 * LANES], coeffs, MLA_ROPE // 2)
        q_ref[:, b0 + LANES:b0 + 2 * LANES] = (r * scale).astype(BF16)


def _proj_kv_kernel(x_ref, pos_ref, g_ref, w_ref, gckv_ref, wk_ref, wvt_ref, invf_ref, k_ref, vt_ref):
    h = _rms(x_ref[...], g_ref[...]).astype(BF16)
    r = _dot(h, w_ref[...])
    ckv = _rms(r[:, :MLA_KV_LORA], gckv_ref[...]).astype(BF16)
    kn = _dot(ckv, wk_ref[...])
    coeffs = _rope_coeffs(pos_ref[...], invf_ref[...], MLA_ROPE // 2, LANES)
    kr = _apply_rope(r[:, MLA_KV_LORA:], coeffs, MLA_ROPE // 2).astype(BF16)
    for hh in range(MLA_HEADS):
        b0 = hh * MLA_QK_PAD
        k_ref[:, b0:b0 + LANES] = kn[:, hh * LANES:(hh + 1) * LANES].astype(BF16)
        k_ref[:, b0 + LANES:b0 + 2 * LANES] = kr
    vt_ref[...] = _dot_nt(wvt_ref[...], ckv).astype(BF16)


def _proj_dsa_kernel(x_ref, pos_ref, g_ref, w_ref, wvt_ref, invf_ref, qk_ref, vt_ref, h_scr):
    j = pl.program_id(1)

    @pl.when(j == 0)
    def _():
        h_scr[...] = _rms(x_ref[...], g_ref[...]).astype(BF16)

    @pl.when(j == 2)
    def _():
        vt_ref[...] = _dot_nt(wvt_ref[...], h_scr[...]).astype(BF16)

    @pl.when(j < 2)
    def _():
        r = _dot(h_scr[...], w_ref[...])
        coeffs = _rope_coeffs(pos_ref[...], invf_ref[...], DSA_ROT // 2, LANES)
        scale = jnp.where(j == 0, DSA_HEAD_DIM ** -0.5 * LOG2E, 1.0).astype(F32)
        for hh in range(DSA_HEADS):
            sl = slice(hh * LANES, (hh + 1) * LANES)
            qk_ref[:, sl] = (_apply_rope(r[:, sl], coeffs, DSA_ROT // 2) * scale).astype(BF16)


def _proj_idx_kernel(x_ref, pos_ref, g_ref, w_ref, invf_ref, qi_ref, ki_ref, wi_ref):
    h = _rms(x_ref[...], g_ref[...]).astype(BF16)
    r = _dot(h, w_ref[...])
    coeffs = _rope_coeffs(pos_ref[...], invf_ref[...], IDX_ROT // 2, IDX_DIM)
    lane = lax.broadcasted_iota(I32, (1, LANES), 1)
    nq = IDX_HEADS * IDX_DIM
    for p in range(IDX_HEADS // 2):
        t = _apply_rope(r[:, p * LANES:(p + 1) * LANES], coeffs, IDX_ROT // 2) * (IDX_DIM ** -0.5)
        qi_ref[:, (2 * p) * LANES:(2 * p + 1) * LANES] = jnp.where(lane < IDX_DIM, t, 0.0).astype(BF16)
        qi_ref[:, (2 * p + 1) * LANES:(2 * p + 2) * LANES] = jnp.where(lane >= IDX_DIM, t, 0.0).astype(BF16)
    ki_ref[...] = _apply_rope(r[:, nq:nq + LANES], coeffs, IDX_ROT // 2).astype(BF16)
    wi_ref[...] = r[:, nq + LANES:nq + 2 * LANES] * (IDX_HEADS ** -0.5)


SUM_ROWS = 16


def _flash_sweep_t(n_tiles, tk, heads, qk, vt, adjust, adjust_last, m_scr, acc_scr, pipelined):
    m_scr[...] = jnp.full(m_scr.shape, NEG, F32)
    acc_scr[...] = jnp.zeros(acc_scr.shape, F32)
    ones = jnp.ones((SUM_ROWS, tk), BF16)

    def finish(h, s_t, off, adj):
        s_t = adj(s_t, off)
        m_prev = m_scr[h]
        m_new = jnp.maximum(m_prev, jnp.max(s_t, axis=0, keepdims=True))
        p = jnp.exp2(s_t - m_new).astype(BF16)
        pv = _dot(jnp.concatenate([vt(h, off), ones], axis=0), p)
        acc_scr[h] = jnp.exp2(m_prev - m_new) * acc_scr[h] + pv
        m_scr[h] = m_new

    off_last = pl.multiple_of((n_tiles - 1) * tk, tk)
    if pipelined:
        def body(j, s_cur):
            off = pl.multiple_of(j * tk, tk)
            off_n = pl.multiple_of((j + 1) * tk, tk)
            s_new = [qk(h, off_n) for h in range(min(2, heads))]
            for h in range(heads):
                finish(h, s_cur[h], off, adjust)
                if h + 2 < heads:
                    s_new.append(qk(h + 2, off_n))
            return tuple(s_new)

        s_last = lax.fori_loop(0, n_tiles - 1, body, tuple(qk(h, 0) for h in range(heads)))
        for h in range(heads):
            finish(h, s_last[h], off_last, adjust_last)
    else:
        def tile(off, adj):
            s_ts = [qk(h, off) for h in range(heads)]
            for h in range(heads):
                finish(h, s_ts[h], off, adj)

        def body(j, c):
            tile(pl.multiple_of(j * tk, tk), adjust)
            return c

        lax.fori_loop(0, n_tiles - 1, body, 0)
        tile(off_last, adjust_last)


def _attn_out_t(acc, dv):
    return (acc[:dv] / acc[dv:dv + 1]).T.astype(BF16)


def _mla_attn_kernel(q_ref, k_ref, vt_ref, o_ref, m_scr, acc_scr, *, tq, tk, hp):
    i = pl.program_id(2)
    n_tiles = (i * tq) // tk + 1
    qry_chunk = (i * tq + lax.broadcasted_iota(I32, (1, tq), 1)) // CHUNK

    def qk(h, off):
        qs = slice(h * MLA_QK_PAD, (h + 1) * MLA_QK_PAD)
        return _dot_nt(k_ref[pl.ds(off, tk), qs], q_ref[:, qs])

    def vt(h, off):
        return vt_ref[h * MLA_V:(h + 1) * MLA_V, pl.ds(off, tk)]

    def causal(s_t, off):
        key_chunk = (off + lax.broadcasted_iota(I32, (tk, 1), 0)) // CHUNK
        return jnp.where(key_chunk <= qry_chunk, s_t, NEG)

    _flash_sweep_t(n_tiles, tk, hp, qk, vt, lambda s_t, off: s_t, causal, m_scr, acc_scr, pipelined=False)
    for h in range(hp):
        o_ref[:, h * MLA_V:(h + 1) * MLA_V] = _attn_out_t(acc_scr[h], MLA_V)


def _dsa_kernel(q_ref, k_ref, vt_ref, qi_ref, ki_ref, wi_ref, o_ref,
                key_scr, bias_scr, cnt_scr, m_scr, acc_scr, *, tq, tk, n_sel, seq):
    i = pl.program_id(1)
    n_valid = ((i + 1) * tq + tk - 1) // tk
    qry_chunk = (i * tq + lax.broadcasted_iota(I32, (1, tq), 1)) // CHUNK
    neg_inf_key = I32(0x807FFFFF - 2 ** 32)

    def key_ids(off):
        return off + lax.broadcasted_iota(I32, (tk, 1), 0)

    wi_t = wi_ref[...].T

    def score_tile(j, c):
        off = pl.multiple_of(j * tk, tk)
        kj = ki_ref[pl.ds(off, tk), :]
        sc = jnp.zeros((tk, tq), F32)
        for hh in range(IDX_HEADS):
            lg = _dot_nt(kj, qi_ref[:, hh * LANES:(hh + 1) * LANES])
            sc = sc + jnp.maximum(lg, 0.0) * wi_t[hh:hh + 1, :]
        sc = jnp.where(sc == 0.0, 0.0, sc)
        sc = jnp.where(key_ids(off) // CHUNK <= qry_chunk, sc, -jnp.inf)
        bits = pltpu.bitcast(sc, I32)
        key_scr[pl.ds(off, tk), :] = bits ^ ((bits >> 31) & I32(0x7FFFFFFF))
        return c

    lax.fori_loop(0, n_valid, score_tile, 0)

    sub = cnt_scr.shape[0]

    def count(pred):
        cnt_scr[...] = jnp.zeros(cnt_scr.shape, I32)

        def body(j, c):
            off = pl.multiple_of(j * tk, tk)
            hit = pred(key_scr[pl.ds(off, tk), :], key_ids(off)).astype(I32)
            part = hit[:sub]
            for cc in range(1, tk // sub):
                part = part + hit[cc * sub:(cc + 1) * sub]
            cnt_scr[...] += part
            return c

        lax.fori_loop(0, n_valid, body, 0)
        return jnp.sum(cnt_scr[...], axis=0, keepdims=True)

    def thr_bit(it, tb):
        cand_b = tb | (I32(1) << (31 - it))
        cand = cand_b ^ I32(INT_MIN)
        cnt = count(lambda key, kid: key >= cand)
        return jnp.where(cnt >= n_sel, cand_b, tb)

    thr = lax.fori_loop(0, 32, thr_bit, jnp.zeros((1, tq), I32)) ^ I32(INT_MIN)

    cnt_gt = count(lambda key, kid: key > thr)
    cnt_ge = count(lambda key, kid: key >= thr)
    need = n_sel - cnt_gt
    tied = (cnt_ge > n_sel) & (thr != neg_inf_key)
    any_tied = jnp.max(tied.astype(I32)) > 0
    idx_bits = (2 * seq - 1).bit_length()

    def tie_cut():
        def cut_bit(it, jc):
            cand = jc | (I32(1) << (idx_bits - 1 - it))
            cnt = count(lambda key, kid: (key == thr) & (kid < cand))
            return jnp.where(cnt <= need, cand, jc)
        return lax.fori_loop(0, idx_bits, cut_bit, jnp.zeros((1, tq), I32))

    jcut = lax.cond(any_tied, tie_cut, lambda: jnp.full((1, tq), 2 ** idx_bits - 1, I32))

    def bias_tile(j, c):
        off = pl.multiple_of(j * tk, tk)
        key = key_scr[pl.ds(off, tk), :]
        kid = key_ids(off)
        sel = (key > thr) | ((key == thr) & (kid < jcut))
        sel = sel & (kid // CHUNK <= qry_chunk)
        bias_scr[pl.ds(off, tk), :] = jnp.where(sel, 0.0, NEG).astype(F32)
        return c

    lax.fori_loop(0, n_valid, bias_tile, 0)

    hp = m_scr.shape[0]

    def biased(s_t, off):
        return s_t + bias_scr[pl.ds(off, tk), :]

    for g in range(DSA_HEADS // hp):
        def head_cols(h, g=g):
            return slice((g * hp + h) * DSA_HEAD_DIM, (g * hp + h + 1) * DSA_HEAD_DIM)

        def qk(h, off):
            return _dot_nt(k_ref[pl.ds(off, tk), head_cols(h)], q_ref[:, head_cols(h)])

        def vt(h, off):
            return vt_ref[head_cols(h), pl.ds(off, tk)]

        _flash_sweep_t(n_valid, tk, hp, qk, vt, biased, biased, m_scr, acc_scr, pipelined=True)
        for h in range(hp):
            o_ref[:, head_cols(h)] = _attn_out_t(acc_scr[h], DSA_HEAD_DIM)


def _mem_kv_kernel(mem_ref, g_ref, wk_ref, wv_ref, k_ref, v_ref):
    mn = _rms(mem_ref[...], g_ref[...]).astype(BF16)
    k_ref[...] = _dot(mn, wk_ref[...]).astype(BF16)
    v_ref[...] = _dot(mn, wv_ref[...]).astype(BF16)


def _post_kernel(a_ref, b_ref, x_ref, woa_ref, wob_ref, gc_ref, wq_ref, km_ref, vm_ref, wo_ref, gm_ref,
                 x2_ref, hm_ref):
    x1 = x_ref[...] + _dot(a_ref[...], woa_ref[...]) + _dot(b_ref[...], wob_ref[...])
    hc = _rms(x1, gc_ref[...]).astype(BF16)
    qc = (_dot(hc, wq_ref[...]) * (X_HEAD_DIM ** -0.5)).astype(BF16)
    outs = []
    for hh in range(X_HEADS):
        sl = slice(hh * X_HEAD_DIM, (hh + 1) * X_HEAD_DIM)
        s = _dot_nt(qc[:, sl], km_ref[:, sl])
        p = jnp.exp(s - jnp.max(s, axis=-1, keepdims=True))
        o = _dot(p.astype(BF16), vm_ref[:, sl]) / jnp.sum(p, axis=-1, keepdims=True)
        outs.append(o.astype(BF16))
    x2 = x1 + _dot(jnp.concatenate(outs, axis=-1), wo_ref[...])
    x2_ref[...] = x2
    hm_ref[...] = _rms(x2, gm_ref[...]).astype(BF16)


def _mlp_kernel(hm_ref, x2_ref, wu_ref, wd_ref, gf_ref, y_ref, acc_scr):
    f = pl.program_id(1)

    @pl.when(f == 0)
    def _():
        acc_scr[...] = x2_ref[...]

    u = jnp.maximum(_dot(hm_ref[...], wu_ref[...]), 0.0)
    acc_scr[...] += _dot((u * u).astype(BF16), wd_ref[...])

    @pl.when(f == pl.num_programs(1) - 1)
    def _():
        y_ref[...] = _rms(acc_scr[...], gf_ref[...])


def _const(shape):
    return pl.BlockSpec(shape, lambda *_: (0,) * len(shape))


def kernel(x, mem, positions, g_mix, w_in, g_cq, g_ckv, w_qb, w_kvb, w_out, g_cross, g_mem,
           w_q_cross, w_k_cross, w_v_cross, w_o_cross, g_mlp, w_up, w_down, g_final):
    assert w_in.shape[0] == 1, "one layer"
    batch, seq, d = x.shape
    t = batch * seq
    n_sel = min(TOPK_MAX, seq // 4)
    x2d = x.reshape(t, d)
    pos = positions.reshape(t, 1)
    row = lambda g: g.reshape(1, -1).astype(F32)

    w = w_in[0]
    o = [0]
    for n in (MLA_Q_LORA, MLA_KV_LORA, MLA_ROPE, DSA_HEADS * DSA_HEAD_DIM, DSA_HEADS * DSA_HEAD_DIM,
              DSA_HEADS * DSA_HEAD_DIM, IDX_HEADS * IDX_DIM, IDX_DIM, IDX_HEADS):
        o.append(o[-1] + n)
    zeros = lambda n: jnp.zeros((d, n), w.dtype)
    w_cq = w[:, o[0]:o[1]].astype(BF16)
    w_ckv = jnp.concatenate([w[:, o[1]:o[3]], zeros(LANES - MLA_ROPE)], axis=1).astype(BF16)
    w_dsa_qk = w[:, o[3]:o[5]].astype(BF16)
    w_dsa_vt = w[:, o[5]:o[6]].T.astype(BF16)
    w_idx = jnp.concatenate([w[:, o[6]:o[7]], w[:, o[7]:o[8]], w[:, o[7]:o[8]], w[:, o[8]:o[9]],
                             zeros(LANES - IDX_HEADS)], axis=1).astype(BF16)
    wqb = w_qb[0].reshape(MLA_Q_LORA, MLA_HEADS, MLA_NOPE + MLA_ROPE)
    wqb = jnp.pad(wqb, ((0, 0), (0, 0), (0, MLA_QK_PAD - MLA_NOPE - MLA_ROPE)))
    wqb = wqb.reshape(MLA_Q_LORA, MLA_HEADS * MLA_QK_PAD).astype(BF16)
    wkvb = w_kvb[0].reshape(MLA_KV_LORA, MLA_HEADS, MLA_NOPE + MLA_V)
    wkvb_k = wkvb[:, :, :MLA_NOPE].reshape(MLA_KV_LORA, -1).astype(BF16)
    wkvb_vt = wkvb[:, :, MLA_NOPE:].reshape(MLA_KV_LORA, -1).T.astype(BF16)
    mla_w = MLA_HEADS * MLA_V
    w_out_a = w_out[0, :mla_w].astype(BF16)
    w_out_b = w_out[0, mla_w:].astype(BF16)
    wq_c, wk_c, wv_c, wo_c = (a[0].astype(BF16) for a in (w_q_cross, w_k_cross, w_v_cross, w_o_cross))
    wu, wd = w_up[0].astype(BF16), w_down[0].astype(BF16)
    invf_mla = _invf_lanes(MLA_ROPE // 2, LANES)
    invf_dsa = _invf_lanes(DSA_ROT // 2, LANES)
    invf_idx = _invf_lanes(IDX_ROT // 2, IDX_DIM)

    tm = 512
    nt = t // tm
    x_spec = pl.BlockSpec((tm, d), lambda i, *_: (i, 0))
    pos_spec = pl.BlockSpec((tm, 1), lambda i, *_: (i, 0))
    qk_w = MLA_HEADS * MLA_QK_PAD

    q_mla = pl.pallas_call(
        _proj_q_kernel, name="proj_q", grid=(nt,),
        in_specs=[x_spec, pos_spec, _const((1, d)), _const((d, MLA_Q_LORA)), _const((1, MLA_Q_LORA)),
                  _const((MLA_Q_LORA, qk_w)), _const((1, LANES))],
        out_specs=pl.BlockSpec((tm, qk_w), lambda i: (i, 0)),
        out_shape=jax.ShapeDtypeStruct((t, qk_w), BF16),
        compiler_params=_params(("parallel",)),
    )(x2d, pos, row(g_mix), w_cq, row(g_cq), wqb, invf_mla)

    k_mla, vt_mla = pl.pallas_call(
        _proj_kv_kernel, name="proj_kv", grid=(nt,),
        in_specs=[x_spec, pos_spec, _const((1, d)), _const((d, MLA_KV_LORA + LANES)), _const((1, MLA_KV_LORA)),
                  _const((MLA_KV_LORA, mla_w)), _const((mla_w, MLA_KV_LORA)), _const((1, LANES))],
        out_specs=[pl.BlockSpec((tm, qk_w), lambda i: (i, 0)), pl.BlockSpec((mla_w, tm), lambda i: (0, i))],
        out_shape=[jax.ShapeDtypeStruct((t, qk_w), BF16), jax.ShapeDtypeStruct((mla_w, t), BF16)],
        compiler_params=_params(("parallel",)),
    )(x2d, pos, row(g_mix), w_ckv, row(g_ckv), wkvb_k, wkvb_vt, invf_mla)

    dsa_w = DSA_HEADS * DSA_HEAD_DIM
    qk_d, vt_d = pl.pallas_call(
        _proj_dsa_kernel, name="proj_dsa", grid=(nt, 3),
        in_specs=[x_spec, pos_spec, _const((1, d)),
                  pl.BlockSpec((d, dsa_w), lambda i, j: (0, jnp.minimum(j, 1))),
                  _const((dsa_w, d)), _const((1, LANES))],
        out_specs=[pl.BlockSpec((None, tm, dsa_w), lambda i, j: (jnp.minimum(j, 1), i, 0)),
                   pl.BlockSpec((dsa_w, tm), lambda i, j: (0, i))],
        out_shape=[jax.ShapeDtypeStruct((2, t, dsa_w), BF16), jax.ShapeDtypeStruct((dsa_w, t), BF16)],
        scratch_shapes=[pltpu.VMEM((tm, d), BF16)],
        compiler_params=_params(("parallel", "arbitrary")),
    )(x2d, pos, row(g_mix), w_dsa_qk, w_dsa_vt, invf_dsa)

    qi_w = IDX_HEADS * LANES
    q_idx, k_idx, w_idx_s = pl.pallas_call(
        _proj_idx_kernel, name="proj_idx", grid=(nt,),
        in_specs=[x_spec, pos_spec, _const((1, d)), _const((d, IDX_HEADS * IDX_DIM + 2 * LANES)),
                  _const((1, LANES))],
        out_specs=[pl.BlockSpec((tm, qi_w), lambda i: (i, 0)), pl.BlockSpec((tm, LANES), lambda i: (i, 0)),
                   pl.BlockSpec((tm, LANES), lambda i: (i, 0))],
        out_shape=[jax.ShapeDtypeStruct((t, qi_w), BF16), jax.ShapeDtypeStruct((t, LANES), BF16),
                   jax.ShapeDtypeStruct((t, LANES), F32)],
        compiler_params=_params(("parallel",)),
    )(x2d, pos, row(g_mix), w_idx, invf_idx)

    tq, tk, hp = 256, 512, 4
    a_mla = pl.pallas_call(
        functools.partial(_mla_attn_kernel, tq=tq, tk=tk, hp=hp), name="mla_attn",
        grid=(batch, MLA_HEADS // hp, seq // tq),
        in_specs=[pl.BlockSpec((None, tq, hp * MLA_QK_PAD), lambda b, h, i: (b, i, h)),
                  pl.BlockSpec((None, seq, hp * MLA_QK_PAD), lambda b, h, i: (b, 0, h)),
                  pl.BlockSpec((hp * MLA_V, seq), lambda b, h, i: (h, b))],
        out_specs=pl.BlockSpec((None, tq, hp * MLA_V), lambda b, h, i: (b, i, h)),
        out_shape=jax.ShapeDtypeStruct((batch, seq, mla_w), BF16),
        scratch_shapes=[pltpu.VMEM((hp, 1, tq), F32), pltpu.VMEM((hp, MLA_V + SUM_ROWS, tq), F32)],
        compiler_params=_params(("parallel", "parallel", "arbitrary")),
    )(q_mla.reshape(batch, seq, qk_w), k_mla.reshape(batch, seq, qk_w), vt_mla)

    qk4 = qk_d.reshape(2, batch, seq, dsa_w)
    b_dsa = pl.pallas_call(
        functools.partial(_dsa_kernel, tq=tq, tk=tk, n_sel=n_sel, seq=seq), name="dsa",
        grid=(batch, seq // tq),
        in_specs=[pl.BlockSpec((None, None, tq, dsa_w), lambda b, i: (0, b, i, 0)),
                  pl.BlockSpec((None, None, seq, dsa_w), lambda b, i: (1, b, 0, 0), pipeline_mode=pl.Buffered(1)),
                  pl.BlockSpec((dsa_w, seq), lambda b, i: (0, b), pipeline_mode=pl.Buffered(1)),
                  pl.BlockSpec((None, tq, qi_w), lambda b, i: (b, i, 0)),
                  pl.BlockSpec((None, seq, LANES), lambda b, i: (b, 0, 0)),
                  pl.BlockSpec((None, tq, LANES), lambda b, i: (b, i, 0))],
        out_specs=pl.BlockSpec((None, tq, dsa_w), lambda b, i: (b, i, 0)),
        out_shape=jax.ShapeDtypeStruct((batch, seq, dsa_w), BF16),
        scratch_shapes=[pltpu.VMEM((seq, tq), I32), pltpu.VMEM((seq, tq), F32), pltpu.VMEM((32, tq), I32),
                        pltpu.VMEM((hp, 1, tq), F32), pltpu.VMEM((hp, DSA_HEAD_DIM + SUM_ROWS, tq), F32)],
        compiler_params=_params(("parallel", "arbitrary")),
    )(qk4, qk4, vt_d, q_idx.reshape(batch, seq, qi_w), k_idx.reshape(batch, seq, LANES),
      w_idx_s.reshape(batch, seq, LANES))

    xw = X_HEADS * X_HEAD_DIM
    n_mem = mem.shape[1]
    k_mem, v_mem = pl.pallas_call(
        _mem_kv_kernel, name="mem_kv", grid=(batch,),
        in_specs=[pl.BlockSpec((None, n_mem, d), lambda b: (b, 0, 0)), _const((1, d)), _const((d, xw)),
                  _const((d, xw))],
        out_specs=[pl.BlockSpec((None, n_mem, xw), lambda b: (b, 0, 0))] * 2,
        out_shape=[jax.ShapeDtypeStruct((batch, n_mem, xw), BF16)] * 2,
        compiler_params=_params(("parallel",)),
    )(mem, row(g_mem), wk_c, wv_c)

    tp = 256
    per_b = seq // tp
    mem_spec = pl.BlockSpec((None, n_mem, xw), lambda i: (i // per_b, 0, 0))
    x2, hm = pl.pallas_call(
        _post_kernel, name="post", grid=(t // tp,),
        in_specs=[pl.BlockSpec((tp, mla_w), lambda i: (i, 0)), pl.BlockSpec((tp, dsa_w), lambda i: (i, 0)),
                  pl.BlockSpec((tp, d), lambda i: (i, 0)), _const((mla_w, d)), _const((dsa_w, d)), _const((1, d)),
                  _const((d, xw)), mem_spec, mem_spec, _const((xw, d)), _const((1, d))],
        out_specs=[pl.BlockSpec((tp, d), lambda i: (i, 0))] * 2,
        out_shape=[jax.ShapeDtypeStruct((t, d), F32), jax.ShapeDtypeStruct((t, d), BF16)],
        compiler_params=_params(("parallel",)),
    )(a_mla.reshape(t, mla_w), b_dsa.reshape(t, dsa_w), x2d, w_out_a, w_out_b, row(g_cross), wq_c,
      k_mem, v_mem, wo_c, row(g_mlp))

    tmm, tf = 512, 512
    y = pl.pallas_call(
        _mlp_kernel, name="mlp", grid=(t // tmm, D_FF // tf),
        in_specs=[pl.BlockSpec((tmm, d), lambda i, f: (i, 0)), pl.BlockSpec((tmm, d), lambda i, f: (i, 0)),
                  pl.BlockSpec((d, tf), lambda i, f: (0, f)), pl.BlockSpec((tf, d), lambda i, f: (f, 0)),
                  _const((1, d))],
        out_specs=pl.BlockSpec((tmm, d), lambda i, f: (i, 0)),
        out_shape=jax.ShapeDtypeStruct((t, d), F32),
        scratch_shapes=[pltpu.VMEM((tmm, d), F32)],
        compiler_params=_params(("parallel", "arbitrary")),
    )(hm, x2, wu, wd, row(g_final))
    return y.reshape(batch, seq, d)
```

```python
import functools

import jax
import jax.numpy as jnp
from jax import lax
from jax.experimental import pallas as pl
from jax.experimental.pallas import tpu as pltpu

F32 = jnp.float32
BF16 = jnp.bfloat16
I32 = jnp.int32

D_MODEL = 2048
CHUNK = 64
ROPE_THETA = 500000.0
N_MEM = 256
EPS = 1e-6
MLA_HEADS = 8
MLA_NOPE = 128
MLA_ROPE = 64
MLA_V = 128
MLA_Q_LORA = 512
MLA_KV_LORA = 256
DSA_HEADS = 8
DSA_HEAD_DIM = 128
DSA_ROT = DSA_HEAD_DIM // 4
IDX_HEADS = 16
IDX_DIM = 64
IDX_ROT = IDX_DIM // 4
TOPK_MAX = 256
X_HEADS = 4
X_HEAD_DIM = 128
D_FF = 4 * D_MODEL

LANES = 128
MLA_QK_PAD = 256
NEG = -1e30
INT_MIN = -2 ** 31
LOG2E = 1.4426950408889634
VMEM_LIMIT = 56 * 1024 * 1024


def _params(sem, vmem=VMEM_LIMIT):
    return pltpu.CompilerParams(dimension_semantics=sem, vmem_limit_bytes=vmem)


def _rms(xf, g):
    return xf * lax.rsqrt(jnp.mean(xf * xf, axis=-1, keepdims=True) + EPS) * g


def _apply_rope(x, coeffs, half):
    c, s_lo, s_hi = coeffs
    return x * c + pltpu.roll(x, LANES - half, 1) * s_lo + pltpu.roll(x, half, 1) * s_hi


def _dot(a, b):
    return jnp.dot(a, b, preferred_element_type=F32)


def _dot_nt(a, b):
    return lax.dot_general(a, b, (((1,), (1,)), ((), ())), preferred_element_type=F32)


ROPE_VARIANTS = ((MLA_ROPE // 2, LANES), (DSA_ROT // 2, LANES), (IDX_ROT // 2, IDX_DIM))


def _rope_lane_plan():
    offs, o = [], 0
    for half, _ in ROPE_VARIANTS:
        offs.append(o)
        o += 2 * half
    assert o <= LANES
    return offs


def _prep_kernel(x_ref, pos_ref, g_ref, invf_ref, h_ref, rope_ref):
    h_ref[...] = _rms(x_ref[...], g_ref[...]).astype(BF16)
    ang = pos_ref[...].astype(F32) * invf_ref[...]
    cos, sin = jnp.cos(ang), jnp.sin(ang)
    lane = lax.broadcasted_iota(I32, (1, LANES), 1)
    for v, ((half, period), off) in enumerate(zip(ROPE_VARIANTS, _rope_lane_plan())):
        c = jnp.ones(cos.shape, F32)
        s_lo = jnp.zeros(cos.shape, F32)
        s_hi = jnp.zeros(cos.shape, F32)
        for base in range(0, LANES, period):
            shift = (base - off) % LANES
            cs = cos if shift == 0 else pltpu.roll(cos, shift, 1)
            sn = sin if shift == 0 else pltpu.roll(sin, shift, 1)
            in_lo = (lane >= base) & (lane < base + half)
            in_hi = (lane >= base + half) & (lane < base + 2 * half)
            c = jnp.where(in_lo | in_hi, cs, c)
            s_lo = jnp.where(in_lo, -sn, s_lo)
            s_hi = jnp.where(in_hi, sn, s_hi)
        rope_ref[:, (3 * v) * LANES:(3 * v + 1) * LANES] = c
        rope_ref[:, (3 * v + 1) * LANES:(3 * v + 2) * LANES] = s_lo
        rope_ref[:, (3 * v + 2) * LANES:(3 * v + 3) * LANES] = s_hi


def _rope_tiles(rope_ref):
    return rope_ref[:, :LANES], rope_ref[:, LANES:2 * LANES], rope_ref[:, 2 * LANES:]


def _proj_q_kernel(h_ref, rope_ref, wcq_ref, gcq_ref, wqb_ref, q_ref):
    cq = _dot(h_ref[...], wcq_ref[...])
    q = _dot(_rms(cq, gcq_ref[...]).astype(BF16), wqb_ref[...])
    coeffs = _rope_tiles(rope_ref)
    scale = (MLA_NOPE + MLA_ROPE) ** -0.5 * LOG2E
    for hh in range(MLA_HEADS):
        b0 = hh * MLA_QK_PAD
        q_ref[:, b0:b0 + LANES] = (q[:, b0:b0 + LANES] * scale).astype(BF16)
        r = _apply_rope(q[:, b0 + LANES:b0 + 2 * LANES], coeffs, MLA_ROPE // 2)
        q_ref[:, b0 + LANES:b0 + 2 * LANES] = (r * scale).astype(BF16)


def _proj_kv_kernel(h_ref, rope_ref, w_ref, gckv_ref, wk_ref, wvt_ref, k_ref, vt_ref):
    r = _dot(h_ref[...], w_ref[...])
    ckv = _rms(r[:, :MLA_KV_LORA], gckv_ref[...]).astype(BF16)
    kn = _dot(ckv, wk_ref[...])
    coeffs = _rope_tiles(rope_ref)
    kr = _apply_rope(r[:, MLA_KV_LORA:], coeffs, MLA_ROPE // 2).astype(BF16)
    for hh in range(MLA_HEADS):
        b0 = hh * MLA_QK_PAD
        k_ref[:, b0:b0 + LANES] = kn[:, hh * LANES:(hh + 1) * LANES].astype(BF16)
        k_ref[:, b0 + LANES:b0 + 2 * LANES] = kr
    vt_ref[...] = _dot_nt(wvt_ref[...], ckv).astype(BF16)


def _proj_dsa_kernel(h_ref, rope_ref, w_ref, wvt_ref, qk_ref, vt_ref):
    j = pl.program_id(1)

    @pl.when(j == 2)
    def _():
        vt_ref[...] = _dot_nt(wvt_ref[...], h_ref[...]).astype(BF16)

    @pl.when(j < 2)
    def _():
        r = _dot(h_ref[...], w_ref[...])
        coeffs = _rope_tiles(rope_ref)
        scale = jnp.where(j == 0, DSA_HEAD_DIM ** -0.5 * LOG2E, 1.0).astype(F32)
        for hh in range(DSA_HEADS):
            sl = slice(hh * LANES, (hh + 1) * LANES)
            qk_ref[:, sl] = (_apply_rope(r[:, sl], coeffs, DSA_ROT // 2) * scale).astype(BF16)


def _proj_idx_kernel(h_ref, rope_ref, w_ref, qi_ref, ki_ref, wi_ref):
    r = _dot(h_ref[...], w_ref[...])
    coeffs = _rope_tiles(rope_ref)
    lane = lax.broadcasted_iota(I32, (1, LANES), 1)
    nq = IDX_HEADS * IDX_DIM
    for p in range(IDX_HEADS // 2):
        t = _apply_rope(r[:, p * LANES:(p + 1) * LANES], coeffs, IDX_ROT // 2) * (IDX_DIM ** -0.5)
        qi_ref[:, (2 * p) * LANES:(2 * p + 1) * LANES] = jnp.where(lane < IDX_DIM, t, 0.0).astype(BF16)
        qi_ref[:, (2 * p + 1) * LANES:(2 * p + 2) * LANES] = jnp.where(lane >= IDX_DIM, t, 0.0).astype(BF16)
    ki_ref[...] = _apply_rope(r[:, nq:nq + LANES], coeffs, IDX_ROT // 2).astype(BF16)
    wi_ref[...] = r[:, nq + LANES:nq + 2 * LANES] * (IDX_HEADS ** -0.5)


SUM_ROWS = 16


def _flash_sweep_t(n_tiles, tk, heads, qk, vt, adjust, adjust_last, m_scr, acc_scr, pipelined):
    m_scr[...] = jnp.full(m_scr.shape, NEG, F32)
    acc_scr[...] = jnp.zeros(acc_scr.shape, F32)
    ones = jnp.ones((SUM_ROWS, tk), BF16)

    def finish(h, s_t, off, adj):
        s_t = adj(s_t, off)
        m_prev = m_scr[h]
        m_new = jnp.maximum(m_prev, jnp.max(s_t, axis=0, keepdims=True))
        p = jnp.exp2(s_t - m_new).astype(BF16)
        pv = _dot(jnp.concatenate([vt(h, off), ones], axis=0), p)
        acc_scr[h] = jnp.exp2(m_prev - m_new) * acc_scr[h] + pv
        m_scr[h] = m_new

    off_last = pl.multiple_of((n_tiles - 1) * tk, tk)
    if pipelined:
        def body(j, s_cur):
            off = pl.multiple_of(j * tk, tk)
            off_n = pl.multiple_of((j + 1) * tk, tk)
            s_new = [qk(h, off_n) for h in range(min(2, heads))]
            for h in range(heads):
                finish(h, s_cur[h], off, adjust)
                if h + 2 < heads:
                    s_new.append(qk(h + 2, off_n))
            return tuple(s_new)

        s_last = lax.fori_loop(0, n_tiles - 1, body, tuple(qk(h, 0) for h in range(heads)))
        for h in range(heads):
            finish(h, s_last[h], off_last, adjust_last)
    else:
        def tile(off, adj):
            s_ts = [qk(h, off) for h in range(heads)]
            for h in range(heads):
                finish(h, s_ts[h], off, adj)

        def body(j, c):
            tile(pl.multiple_of(j * tk, tk), adjust)
            return c

        lax.fori_loop(0, n_tiles - 1, body, 0)
        tile(off_last, adjust_last)


def _attn_out_t(acc, dv):
    return (acc[:dv] / acc[dv:dv + 1]).T.astype(BF16)


def _mla_attn_kernel(q_ref, k_ref, vt_ref, o_ref, m_scr, acc_scr, *, tq, tk, hp):
    i = pl.program_id(2)
    n_tiles = (i * tq) // tk + 1
    qry_chunk = (i * tq + lax.broadcasted_iota(I32, (1, tq), 1)) // CHUNK

    def qk(h, off):
        qs = slice(h * MLA_QK_PAD, (h + 1) * MLA_QK_PAD)
        return _dot_nt(k_ref[pl.ds(off, tk), qs], q_ref[:, qs])

    def vt(h, off):
        return vt_ref[h * MLA_V:(h + 1) * MLA_V, pl.ds(off, tk)]

    def causal(s_t, off):
        key_chunk = (off + lax.broadcasted_iota(I32, (tk, 1), 0)) // CHUNK
        return jnp.where(key_chunk <= qry_chunk, s_t, NEG)

    _flash_sweep_t(n_tiles, tk, hp, qk, vt, lambda s_t, off: s_t, causal, m_scr, acc_scr, pipelined=False)
    for h in range(hp):
        o_ref[:, h * MLA_V:(h + 1) * MLA_V] = _attn_out_t(acc_scr[h], MLA_V)


def _dsa_kernel(q_ref, k_ref, vt_ref, qi_ref, ki_ref, wi_ref, o_ref,
                key_scr, bias_scr, cnt_scr, m_scr, acc_scr, *, tq, tk, n_sel, seq):
    i = pl.program_id(1)
    n_valid = ((i + 1) * tq + tk - 1) // tk
    qry_chunk = (i * tq + lax.broadcasted_iota(I32, (1, tq), 1)) // CHUNK
    neg_inf_key = I32(0x807FFFFF - 2 ** 32)

    def key_ids(off):
        return off + lax.broadcasted_iota(I32, (tk, 1), 0)

    wi_t = wi_ref[...].T

    def score_tile(j, c):
        off = pl.multiple_of(j * tk, tk)
        kj = ki_ref[pl.ds(off, tk), :]
        sc = jnp.zeros((tk, tq), F32)
        for hh in range(IDX_HEADS):
            lg = _dot_nt(kj, qi_ref[:, hh * LANES:(hh + 1) * LANES])
            sc = sc + jnp.maximum(lg, 0.0) * wi_t[hh:hh + 1, :]
        sc = jnp.where(sc == 0.0, 0.0, sc)
        sc = jnp.where(key_ids(off) // CHUNK <= qry_chunk, sc, -jnp.inf)
        bits = pltpu.bitcast(sc, I32)
        key_scr[pl.ds(off, tk), :] = bits ^ ((bits >> 31) & I32(0x7FFFFFFF))
        return c

    lax.fori_loop(0, n_valid, score_tile, 0)

    sub = cnt_scr.shape[0]

    def count(pred):
        cnt_scr[...] = jnp.zeros(cnt_scr.shape, I32)

        def body(j, c):
            off = pl.multiple_of(j * tk, tk)
            hit = pred(key_scr[pl.ds(off, tk), :], key_ids(off)).astype(I32)
            part = hit[:sub]
            for cc in range(1, tk // sub):
                part = part + hit[cc * sub:(cc + 1) * sub]
            cnt_scr[...] += part
            return c

        lax.fori_loop(0, n_valid, body, 0)
        return jnp.sum(cnt_scr[...], axis=0, keepdims=True)

    def thr_bit(it, tb):
        cand_b = tb | (I32(1) << (31 - it))
        cand = cand_b ^ I32(INT_MIN)
        cnt = count(lambda key, kid: key >= cand)
        return jnp.where(cnt >= n_sel, cand_b, tb)

    thr = lax.fori_loop(0, 32, thr_bit, jnp.zeros((1, tq), I32)) ^ I32(INT_MIN)

    cnt_gt = count(lambda key, kid: key > thr)
    cnt_ge = count(lambda key, kid: key >= thr)
    need = n_sel - cnt_gt
    tied = (cnt_ge > n_sel) & (thr != neg_inf_key)
    any_tied = jnp.max(tied.astype(I32)) > 0
    idx_bits = (2 * seq - 1).bit_length()

    def tie_cut():
        def cut_bit(it, jc):
            cand = jc | (I32(1) << (idx_bits - 1 - it))
            cnt = count(lambda key, kid: (key == thr) & (kid < cand))
            return jnp.where(cnt <= need, cand, jc)
        return lax.fori_loop(0, idx_bits, cut_bit, jnp.zeros((1, tq), I32))

    jcut = lax.cond(any_tied, tie_cut, lambda: jnp.full((1, tq), 2 ** idx_bits - 1, I32))

    def bias_tile(j, c):
        off = pl.multiple_of(j * tk, tk)
        key = key_scr[pl.ds(off, tk), :]
        kid = key_ids(off)
        sel = (key > thr) | ((key == thr) & (kid < jcut))
        sel = sel & (kid // CHUNK <= qry_chunk)
        bias_scr[pl.ds(off, tk), :] = jnp.where(sel, 0.0, NEG).astype(F32)
        return c

    lax.fori_loop(0, n_valid, bias_tile, 0)

    hp = m_scr.shape[0]

    def biased(s_t, off):
        return s_t + bias_scr[pl.ds(off, tk), :]

    for g in range(DSA_HEADS // hp):
        def head_cols(h, g=g):
            return slice((g * hp + h) * DSA_HEAD_DIM, (g * hp + h + 1) * DSA_HEAD_DIM)

        def qk(h, off):
            return _dot_nt(k_ref[pl.ds(off, tk), head_cols(h)], q_ref[:, head_cols(h)])

        def vt(h, off):
            return vt_ref[head_cols(h), pl.ds(off, tk)]

        _flash_sweep_t(n_valid, tk, hp, qk, vt, biased, biased, m_scr, acc_scr, pipelined=True)
        for h in range(hp):
            o_ref[:, head_cols(h)] = _attn_out_t(acc_scr[h], DSA_HEAD_DIM)


def _mem_kv_kernel(mem_ref, g_ref, wk_ref, wv_ref, k_ref, v_ref):
    mn = _rms(mem_ref[...], g_ref[...]).astype(BF16)
    k_ref[...] = _dot(mn, wk_ref[...]).astype(BF16)
    v_ref[...] = _dot(mn, wv_ref[...]).astype(BF16)


def _post_kernel(a_ref, b_ref, x_ref, woa_ref, wob_ref, gc_ref, wq_ref, km_ref, vm_ref, wo_ref, gm_ref,
                 x2_ref, hm_ref):
    x1 = x_ref[...] + _dot(a_ref[...], woa_ref[...]) + _dot(b_ref[...], wob_ref[...])
    hc = _rms(x1, gc_ref[...]).astype(BF16)
    qc = (_dot(hc, wq_ref[...]) * (X_HEAD_DIM ** -0.5)).astype(BF16)
    outs = []
    for hh in range(X_HEADS):
        sl = slice(hh * X_HEAD_DIM, (hh + 1) * X_HEAD_DIM)
        s = _dot_nt(qc[:, sl], km_ref[:, sl])
        p = jnp.exp(s - jnp.max(s, axis=-1, keepdims=True))
        o = _dot(p.astype(BF16), vm_ref[:, sl]) / jnp.sum(p, axis=-1, keepdims=True)
        outs.append(o.astype(BF16))
    x2 = x1 + _dot(jnp.concatenate(outs, axis=-1), wo_ref[...])
    x2_ref[...] = x2
    hm_ref[...] = _rms(x2, gm_ref[...]).astype(BF16)


def _mlp_kernel(hm_ref, x2_ref, wu_ref, wd_ref, gf_ref, y_ref):
    f = pl.program_id(1)

    @pl.when(f == 0)
    def _():
        y_ref[...] = x2_ref[...]

    u = jnp.maximum(_dot(hm_ref[...], wu_ref[...]), 0.0)
    y_ref[...] += _dot((u * u).astype(BF16), wd_ref[...])

    @pl.when(f == pl.num_programs(1) - 1)
    def _():
        y_ref[...] = _rms(y_ref[...], gf_ref[...])


def _const(shape):
    return pl.BlockSpec(shape, lambda *_: (0,) * len(shape))


def kernel(x, mem, positions, g_mix, w_in, g_cq, g_ckv, w_qb, w_kvb, w_out, g_cross, g_mem,
           w_q_cross, w_k_cross, w_v_cross, w_o_cross, g_mlp, w_up, w_down, g_final):
    assert w_in.shape[0] == 1, "one layer"
    batch, seq, d = x.shape
    t = batch * seq
    n_sel = min(TOPK_MAX, seq // 4)
    x2d = x.reshape(t, d)
    pos = positions.reshape(t, 1)
    row = lambda g: g.reshape(1, -1).astype(F32)

    w = w_in[0]
    o = [0]
    for n in (MLA_Q_LORA, MLA_KV_LORA, MLA_ROPE, DSA_HEADS * DSA_HEAD_DIM, DSA_HEADS * DSA_HEAD_DIM,
              DSA_HEADS * DSA_HEAD_DIM, IDX_HEADS * IDX_DIM, IDX_DIM, IDX_HEADS):
        o.append(o[-1] + n)
    zeros = lambda n: jnp.zeros((d, n), w.dtype)
    w_cq = w[:, o[0]:o[1]].astype(BF16)
    w_ckv = jnp.concatenate([w[:, o[1]:o[3]], zeros(LANES - MLA_ROPE)], axis=1).astype(BF16)
    w_dsa_qk = w[:, o[3]:o[5]].astype(BF16)
    w_dsa_vt = w[:, o[5]:o[6]].T.astype(BF16)
    w_idx = jnp.concatenate([w[:, o[6]:o[7]], w[:, o[7]:o[8]], w[:, o[7]:o[8]], w[:, o[8]:o[9]],
                             zeros(LANES - IDX_HEADS)], axis=1).astype(BF16)
    wqb = w_qb[0].reshape(MLA_Q_LORA, MLA_HEADS, MLA_NOPE + MLA_ROPE)
    wqb = jnp.pad(wqb, ((0, 0), (0, 0), (0, MLA_QK_PAD - MLA_NOPE - MLA_ROPE)))
    wqb = wqb.reshape(MLA_Q_LORA, MLA_HEADS * MLA_QK_PAD).astype(BF16)
    wkvb = w_kvb[0].reshape(MLA_KV_LORA, MLA_HEADS, MLA_NOPE + MLA_V)
    wkvb_k = wkvb[:, :, :MLA_NOPE].reshape(MLA_KV_LORA, -1).astype(BF16)
    wkvb_vt = wkvb[:, :, MLA_NOPE:].reshape(MLA_KV_LORA, -1).T.astype(BF16)
    mla_w = MLA_HEADS * MLA_V
    w_out_a = w_out[0, :mla_w].astype(BF16)
    w_out_b = w_out[0, mla_w:].astype(BF16)
    wq_c, wk_c, wv_c, wo_c = (a[0].astype(BF16) for a in (w_q_cross, w_k_cross, w_v_cross, w_o_cross))
    wu, wd = w_up[0].astype(BF16), w_down[0].astype(BF16)
    invf = jnp.zeros((LANES,), F32)
    for (half, _), off in zip(ROPE_VARIANTS, _rope_lane_plan()):
        inv_freq = ROPE_THETA ** (-jnp.arange(half, dtype=F32) / half)
        invf = invf.at[off:off + 2 * half].set(jnp.concatenate([inv_freq, inv_freq]))
    invf = invf.reshape(1, LANES)

    tm = 512
    nt = t // tm
    qk_w = MLA_HEADS * MLA_QK_PAD
    n_rope = 3 * LANES

    h_mix, rope = pl.pallas_call(
        _prep_kernel, name="prep", grid=(nt,),
        in_specs=[pl.BlockSpec((tm, d), lambda i: (i, 0)), pl.BlockSpec((tm, 1), lambda i: (i, 0)),
                  _const((1, d)), _const((1, LANES))],
        out_specs=[pl.BlockSpec((tm, d), lambda i: (i, 0)),
                   pl.BlockSpec((tm, len(ROPE_VARIANTS) * n_rope), lambda i: (i, 0))],
        out_shape=[jax.ShapeDtypeStruct((t, d), BF16),
                   jax.ShapeDtypeStruct((t, len(ROPE_VARIANTS) * n_rope), F32)],
        compiler_params=_params(("parallel",)),
    )(x2d, pos, row(g_mix), invf)
    h_spec = pl.BlockSpec((tm, d), lambda i, *_: (i, 0))
    rope_spec = lambda v: pl.BlockSpec((tm, n_rope), lambda i, *_: (i, v))

    q_mla = pl.pallas_call(
        _proj_q_kernel, name="proj_q", grid=(nt,),
        in_specs=[h_spec, rope_spec(0), _const((d, MLA_Q_LORA)), _const((1, MLA_Q_LORA)),
                  _const((MLA_Q_LORA, qk_w))],
        out_specs=pl.BlockSpec((tm, qk_w), lambda i: (i, 0)),
        out_shape=jax.ShapeDtypeStruct((t, qk_w), BF16),
        compiler_params=_params(("parallel",)),
    )(h_mix, rope, w_cq, row(g_cq), wqb)

    k_mla, vt_mla = pl.pallas_call(
        _proj_kv_kernel, name="proj_kv", grid=(nt,),
        in_specs=[h_spec, rope_spec(0), _const((d, MLA_KV_LORA + LANES)), _const((1, MLA_KV_LORA)),
                  _const((MLA_KV_LORA, mla_w)), _const((mla_w, MLA_KV_LORA))],
        out_specs=[pl.BlockSpec((tm, qk_w), lambda i: (i, 0)), pl.BlockSpec((mla_w, tm), lambda i: (0, i))],
        out_shape=[jax.ShapeDtypeStruct((t, qk_w), BF16), jax.ShapeDtypeStruct((mla_w, t), BF16)],
        compiler_params=_params(("parallel",)),
    )(h_mix, rope, w_ckv, row(g_ckv), wkvb_k, wkvb_vt)

    dsa_w = DSA_HEADS * DSA_HEAD_DIM
    qk_d, vt_d = pl.pallas_call(
        _proj_dsa_kernel, name="proj_dsa", grid=(nt, 3),
        in_specs=[h_spec, rope_spec(1),
                  pl.BlockSpec((d, dsa_w), lambda i, j: (0, jnp.minimum(j, 1))),
                  _const((dsa_w, d))],
        out_specs=[pl.BlockSpec((None, tm, dsa_w), lambda i, j: (jnp.minimum(j, 1), i, 0)),
                   pl.BlockSpec((dsa_w, tm), lambda i, j: (0, i))],
        out_shape=[jax.ShapeDtypeStruct((2, t, dsa_w), BF16), jax.ShapeDtypeStruct((dsa_w, t), BF16)],
        compiler_params=_params(("parallel", "arbitrary")),
    )(h_mix, rope, w_dsa_qk, w_dsa_vt)

    qi_w = IDX_HEADS * LANES
    q_idx, k_idx, w_idx_s = pl.pallas_call(
        _proj_idx_kernel, name="proj_idx", grid=(nt,),
        in_specs=[h_spec, rope_spec(2), _const((d, IDX_HEADS * IDX_DIM + 2 * LANES))],
        out_specs=[pl.BlockSpec((tm, qi_w), lambda i: (i, 0)), pl.BlockSpec((tm, LANES), lambda i: (i, 0)),
                   pl.BlockSpec((tm, LANES), lambda i: (i, 0))],
        out_shape=[jax.ShapeDtypeStruct((t, qi_w), BF16), jax.ShapeDtypeStruct((t, LANES), BF16),
                   jax.ShapeDtypeStruct((t, LANES), F32)],
        compiler_params=_params(("parallel",)),
    )(h_mix, rope, w_idx)

    tq, tk, hp = 256, 512, 4
    a_mla = pl.pallas_call(
        functools.partial(_mla_attn_kernel, tq=tq, tk=tk, hp=hp), name="mla_attn",
        grid=(batch, MLA_HEADS // hp, seq // tq),
        in_specs=[pl.BlockSpec((None, tq, hp * MLA_QK_PAD), lambda b, h, i: (b, i, h)),
                  pl.BlockSpec((None, seq, hp * MLA_QK_PAD), lambda b, h, i: (b, 0, h)),
                  pl.BlockSpec((hp * MLA_V, seq), lambda b, h, i: (h, b))],
        out_specs=pl.BlockSpec((None, tq, hp * MLA_V), lambda b, h, i: (b, i, h)),
        out_shape=jax.ShapeDtypeStruct((batch, seq, mla_w), BF16),
        scratch_shapes=[pltpu.VMEM((hp, 1, tq), F32), pltpu.VMEM((hp, MLA_V + SUM_ROWS, tq), F32)],
        compiler_params=_params(("parallel", "parallel", "arbitrary")),
    )(q_mla.reshape(batch, seq, qk_w), k_mla.reshape(batch, seq, qk_w), vt_mla)

    qk4 = qk_d.reshape(2, batch, seq, dsa_w)
    tq, tk, hp = 256, 512, 4
    b_dsa = pl.pallas_call(
        functools.partial(_dsa_kernel, tq=tq, tk=tk, n_sel=n_sel, seq=seq), name="dsa",
        grid=(batch, seq // tq),
        in_specs=[pl.BlockSpec((None, None, tq, dsa_w), lambda b, i: (0, b, i, 0)),
                  pl.BlockSpec((None, None, seq, dsa_w), lambda b, i: (1, b, 0, 0), pipeline_mode=pl.Buffered(1)),
                  pl.BlockSpec((dsa_w, seq), lambda b, i: (0, b), pipeline_mode=pl.Buffered(1)),
                  pl.BlockSpec((None, tq, qi_w), lambda b, i: (b, i, 0)),
                  pl.BlockSpec((None, seq, LANES), lambda b, i: (b, 0, 0)),
                  pl.BlockSpec((None, tq, LANES), lambda b, i: (b, i, 0))],
        out_specs=pl.BlockSpec((None, tq, dsa_w), lambda b, i: (b, i, 0)),
        out_shape=jax.ShapeDtypeStruct((batch, seq, dsa_w), BF16),
        scratch_shapes=[pltpu.VMEM((seq, tq), I32), pltpu.VMEM((seq, tq), F32), pltpu.VMEM((32, tq), I32),
                        pltpu.VMEM((hp, 1, tq), F32), pltpu.VMEM((hp, DSA_HEAD_DIM + SUM_ROWS, tq), F32)],
        compiler_params=_params(("parallel", "arbitrary")),
    )(qk4, qk4, vt_d, q_idx.reshape(batch, seq, qi_w), k_idx.reshape(batch, seq, LANES),
      w_idx_s.reshape(batch, seq, LANES))

    xw = X_HEADS * X_HEAD_DIM
    n_mem = mem.shape[1]
    k_mem, v_mem = pl.pallas_call(
        _mem_kv_kernel, name="mem_kv", grid=(batch,),
        in_specs=[pl.BlockSpec((None, n_mem, d), lambda b: (b, 0, 0)), _const((1, d)), _const((d, xw)),
                  _const((d, xw))],
        out_specs=[pl.BlockSpec((None, n_mem, xw), lambda b: (b, 0, 0))] * 2,
        out_shape=[jax.ShapeDtypeStruct((batch, n_mem, xw), BF16)] * 2,
        compiler_params=_params(("parallel",)),
    )(mem, row(g_mem), wk_c, wv_c)

    tp = 256
    per_b = seq // tp
    mem_spec = pl.BlockSpec((None, n_mem, xw), lambda i: (i // per_b, 0, 0))
    x2, hm = pl.pallas_call(
        _post_kernel, name="post", grid=(t // tp,),
        in_specs=[pl.BlockSpec((tp, mla_w), lambda i: (i, 0)), pl.BlockSpec((tp, dsa_w), lambda i: (i, 0)),
                  pl.BlockSpec((tp, d), lambda i: (i, 0)), _const((mla_w, d)), _const((dsa_w, d)), _const((1, d)),
                  _const((d, xw)), mem_spec, mem_spec, _const((xw, d)), _const((1, d))],
        out_specs=[pl.BlockSpec((tp, d), lambda i: (i, 0))] * 2,
        out_shape=[jax.ShapeDtypeStruct((t, d), F32), jax.ShapeDtypeStruct((t, d), BF16)],
        compiler_params=_params(("parallel",)),
    )(a_mla.reshape(t, mla_w), b_dsa.reshape(t, dsa_w), x2d, w_out_a, w_out_b, row(g_cross), wq_c,
      k_mem, v_mem, wo_c, row(g_mlp))

    tmm, tf = 512, 1024
    y = pl.pallas_call(
        _mlp_kernel, name="mlp", grid=(t // tmm, D_FF // tf),
        in_specs=[pl.BlockSpec((tmm, d), lambda i, f: (i, 0)), pl.BlockSpec((tmm, d), lambda i, f: (i, 0)),
                  pl.BlockSpec((d, tf), lambda i, f: (0, f)), pl.BlockSpec((tf, d), lambda i, f: (f, 0)),
                  _const((1, d))],
        out_specs=pl.BlockSpec((tmm, d), lambda i, f: (i, 0)),
        out_shape=jax.ShapeDtypeStruct((t, d), F32),
        compiler_params=_params(("parallel", "arbitrary")),
    )(hm, x2, wu, wd, row(g_final))
    return y.reshape(batch, seq, d)
```

```python
import functools

import jax
import jax.numpy as jnp
from jax import lax
from jax.experimental import pallas as pl
from jax.experimental.pallas import tpu as pltpu

F32 = jnp.float32
BF16 = jnp.bfloat16
I32 = jnp.int32
I16 = jnp.int16

D_MODEL = 2048
CHUNK = 64
ROPE_THETA = 500000.0
N_MEM = 256
EPS = 1e-6
MLA_HEADS = 8
MLA_NOPE = 128
MLA_ROPE = 64
MLA_V = 128
MLA_Q_LORA = 512
MLA_KV_LORA = 256
DSA_HEADS = 8
DSA_HEAD_DIM = 128
DSA_ROT = DSA_HEAD_DIM // 4
IDX_HEADS = 16
IDX_DIM = 64
IDX_ROT = IDX_DIM // 4
TOPK_MAX = 256
X_HEADS = 4
X_HEAD_DIM = 128
D_FF = 4 * D_MODEL

LANES = 128
MLA_QK_PAD = 256
NEG = -1e30
HALF_BIAS = 2 ** 15
LOG2E = 1.4426950408889634
VMEM_LIMIT = 56 * 1024 * 1024


def _params(sem, vmem=VMEM_LIMIT):
    return pltpu.CompilerParams(dimension_semantics=sem, vmem_limit_bytes=vmem)


def _rms(xf, g):
    return xf * lax.rsqrt(jnp.mean(xf * xf, axis=-1, keepdims=True) + EPS) * g


def _apply_rope(x, coeffs, half):
    c, s_lo, s_hi = coeffs
    return x * c + pltpu.roll(x, LANES - half, 1) * s_lo + pltpu.roll(x, half, 1) * s_hi


def _dot(a, b):
    return jnp.dot(a, b, preferred_element_type=F32)


def _dot_nt(a, b):
    return lax.dot_general(a, b, (((1,), (1,)), ((), ())), preferred_element_type=F32)


ROPE_VARIANTS = ((MLA_ROPE // 2, LANES), (DSA_ROT // 2, LANES), (IDX_ROT // 2, IDX_DIM))


def _rope_lane_plan():
    offs, o = [], 0
    for half, _ in ROPE_VARIANTS:
        offs.append(o)
        o += 2 * half
    assert o <= LANES
    return offs


def _prep_kernel(x_ref, pos_ref, g_ref, invf_ref, h_ref, rope_ref):
    h_ref[...] = _rms(x_ref[...], g_ref[...]).astype(BF16)
    ang = pos_ref[...].astype(F32) * invf_ref[...]
    cos, sin = jnp.cos(ang), jnp.sin(ang)
    lane = lax.broadcasted_iota(I32, (1, LANES), 1)
    for v, ((half, period), off) in enumerate(zip(ROPE_VARIANTS, _rope_lane_plan())):
        c = jnp.ones(cos.shape, F32)
        s_lo = jnp.zeros(cos.shape, F32)
        s_hi = jnp.zeros(cos.shape, F32)
        for base in range(0, LANES, period):
            shift = (base - off) % LANES
            cs = cos if shift == 0 else pltpu.roll(cos, shift, 1)
            sn = sin if shift == 0 else pltpu.roll(sin, shift, 1)
            in_lo = (lane >= base) & (lane < base + half)
            in_hi = (lane >= base + half) & (lane < base + 2 * half)
            c = jnp.where(in_lo | in_hi, cs, c)
            s_lo = jnp.where(in_lo, -sn, s_lo)
            s_hi = jnp.where(in_hi, sn, s_hi)
        rope_ref[:, (3 * v) * LANES:(3 * v + 1) * LANES] = c
        rope_ref[:, (3 * v + 1) * LANES:(3 * v + 2) * LANES] = s_lo
        rope_ref[:, (3 * v + 2) * LANES:(3 * v + 3) * LANES] = s_hi


def _rope_tiles(rope_ref):
    return rope_ref[:, :LANES], rope_ref[:, LANES:2 * LANES], rope_ref[:, 2 * LANES:]


def _proj_q_kernel(h_ref, rope_ref, wcq_ref, gcq_ref, wqb_ref, q_ref):
    cq = _dot(h_ref[...], wcq_ref[...])
    q = _dot(_rms(cq, gcq_ref[...]).astype(BF16), wqb_ref[...])
    coeffs = _rope_tiles(rope_ref)
    scale = (MLA_NOPE + MLA_ROPE) ** -0.5 * LOG2E
    for hh in range(MLA_HEADS):
        b0 = hh * MLA_QK_PAD
        q_ref[:, b0:b0 + LANES] = (q[:, b0:b0 + LANES] * scale).astype(BF16)
        r = _apply_rope(q[:, b0 + LANES:b0 + 2 * LANES], coeffs, MLA_ROPE // 2)
        q_ref[:, b0 + LANES:b0 + 2 * LANES] = (r * scale).astype(BF16)


def _proj_kv_kernel(h_ref, rope_ref, w_ref, gckv_ref, wk_ref, wvt_ref, k_ref, vt_ref):
    r = _dot(h_ref[...], w_ref[...])
    ckv = _rms(r[:, :MLA_KV_LORA], gckv_ref[...]).astype(BF16)
    kn = _dot(ckv, wk_ref[...])
    coeffs = _rope_tiles(rope_ref)
    kr = _apply_rope(r[:, MLA_KV_LORA:], coeffs, MLA_ROPE // 2).astype(BF16)
    for hh in range(MLA_HEADS):
        b0 = hh * MLA_QK_PAD
        k_ref[:, b0:b0 + LANES] = kn[:, hh * LANES:(hh + 1) * LANES].astype(BF16)
        k_ref[:, b0 + LANES:b0 + 2 * LANES] = kr
    vt_ref[...] = _dot_nt(wvt_ref[...], ckv).astype(BF16)


def _proj_dsa_kernel(h_ref, rope_ref, w_ref, wvt_ref, qk_ref, vt_ref):
    j = pl.program_id(1)

    @pl.when(j == 2)
    def _():
        vt_ref[...] = _dot_nt(wvt_ref[...], h_ref[...]).astype(BF16)

    @pl.when(j < 2)
    def _():
        r = _dot(h_ref[...], w_ref[...])
        coeffs = _rope_tiles(rope_ref)
        scale = jnp.where(j == 0, DSA_HEAD_DIM ** -0.5 * LOG2E, 1.0).astype(F32)
        for hh in range(DSA_HEADS):
            sl = slice(hh * LANES, (hh + 1) * LANES)
            qk_ref[:, sl] = (_apply_rope(r[:, sl], coeffs, DSA_ROT // 2) * scale).astype(BF16)


def _proj_idx_kernel(h_ref, rope_ref, w_ref, qi_ref, ki_ref, wi_ref):
    r = _dot(h_ref[...], w_ref[...])
    coeffs = _rope_tiles(rope_ref)
    lane = lax.broadcasted_iota(I32, (1, LANES), 1)
    nq = IDX_HEADS * IDX_DIM
    for p in range(IDX_HEADS // 2):
        t = _apply_rope(r[:, p * LANES:(p + 1) * LANES], coeffs, IDX_ROT // 2) * (IDX_DIM ** -0.5)
        qi_ref[:, (2 * p) * LANES:(2 * p + 1) * LANES] = jnp.where(lane < IDX_DIM, t, 0.0).astype(BF16)
        qi_ref[:, (2 * p + 1) * LANES:(2 * p + 2) * LANES] = jnp.where(lane >= IDX_DIM, t, 0.0).astype(BF16)
    ki_ref[...] = _apply_rope(r[:, nq:nq + LANES], coeffs, IDX_ROT // 2).astype(BF16)
    wi_ref[...] = r[:, nq + LANES:nq + 2 * LANES] * (IDX_HEADS ** -0.5)


SUM_ROWS = 16


def _flash_sweep_t(n_tiles, tk, heads, qk, vt, adjust, adjust_last, m_scr, acc_scr, pipelined):
    m_scr[...] = jnp.full(m_scr.shape, NEG, F32)
    acc_scr[...] = jnp.zeros(acc_scr.shape, F32)
    ones = jnp.ones((SUM_ROWS, tk), BF16)

    def finish(h, s_t, off, adj):
        s_t = adj(s_t, off)
        m_prev = m_scr[h]
        m_new = jnp.maximum(m_prev, jnp.max(s_t, axis=0, keepdims=True))
        p = jnp.exp2(s_t - m_new).astype(BF16)
        pv = _dot(jnp.concatenate([vt(h, off), ones], axis=0), p)
        acc_scr[h] = jnp.exp2(m_prev - m_new) * acc_scr[h] + pv
        m_scr[h] = m_new

    off_last = pl.multiple_of((n_tiles - 1) * tk, tk)
    if pipelined:
        def body(j, s_cur):
            off = pl.multiple_of(j * tk, tk)
            off_n = pl.multiple_of((j + 1) * tk, tk)
            s_new = [qk(h, off_n) for h in range(min(2, heads))]
            for h in range(heads):
                finish(h, s_cur[h], off, adjust)
                if h + 2 < heads:
                    s_new.append(qk(h + 2, off_n))
            return tuple(s_new)

        s_last = lax.fori_loop(0, n_tiles - 1, body, tuple(qk(h, 0) for h in range(heads)))
        for h in range(heads):
            finish(h, s_last[h], off_last, adjust_last)
    else:
        def tile(off, adj):
            s_ts = [qk(h, off) for h in range(heads)]
            for h in range(heads):
                finish(h, s_ts[h], off, adj)

        def body(j, c):
            tile(pl.multiple_of(j * tk, tk), adjust)
            return c

        lax.fori_loop(0, n_tiles - 1, body, 0)
        tile(off_last, adjust_last)


def _attn_out_t(acc, dv):
    return (acc[:dv] / acc[dv:dv + 1]).T.astype(BF16)


def _mla_attn_kernel(q_ref, k_ref, vt_ref, o_ref, m_scr, acc_scr, *, tq, tk, hp):
    i = pl.program_id(2)
    n_tiles = (i * tq) // tk + 1
    qry_chunk = (i * tq + lax.broadcasted_iota(I32, (1, tq), 1)) // CHUNK

    def qk(h, off):
        qs = slice(h * MLA_QK_PAD, (h + 1) * MLA_QK_PAD)
        return _dot_nt(k_ref[pl.ds(off, tk), qs], q_ref[:, qs])

    def vt(h, off):
        return vt_ref[h * MLA_V:(h + 1) * MLA_V, pl.ds(off, tk)]

    def causal(s_t, off):
        key_chunk = (off + lax.broadcasted_iota(I32, (tk, 1), 0)) // CHUNK
        return jnp.where(key_chunk <= qry_chunk, s_t, NEG)

    _flash_sweep_t(n_tiles, tk, hp, qk, vt, lambda s_t, off: s_t, causal, m_scr, acc_scr, pipelined=False)
    for h in range(hp):
        o_ref[:, h * MLA_V:(h + 1) * MLA_V] = _attn_out_t(acc_scr[h], MLA_V)


def _dsa_kernel(q_ref, k_ref, vt_ref, qi_ref, ki_ref, wi_ref, o_ref,
                hi_scr, lo_scr, bias_scr, cnt_scr, m_scr, acc_scr, *, tq, tk, n_sel, seq):
    i = pl.program_id(1)
    n_valid = ((i + 1) * tq + tk - 1) // tk
    qry_chunk = (i * tq + lax.broadcasted_iota(I32, (1, tq), 1)) // CHUNK

    def key_ids(off):
        return off + lax.broadcasted_iota(I32, (tk, 1), 0)

    wi_t = wi_ref[...].T

    def score_tile(j, c):
        off = pl.multiple_of(j * tk, tk)
        kj = ki_ref[pl.ds(off, tk), :]
        sc = jnp.zeros((tk, tq), F32)
        for hh in range(IDX_HEADS):
            lg = _dot_nt(kj, qi_ref[:, hh * LANES:(hh + 1) * LANES])
            sc = sc + jnp.maximum(lg, 0.0) * wi_t[hh:hh + 1, :]
        sc = jnp.where(sc == 0.0, 0.0, sc)
        sc = jnp.where(key_ids(off) // CHUNK <= qry_chunk, sc, -jnp.inf)
        bits = pltpu.bitcast(sc, I32)
        key = bits ^ ((bits >> 31) & I32(0x7FFFFFFF))
        hi_scr[pl.ds(off, tk), :] = (key >> 16).astype(I16)
        lo_scr[pl.ds(off, tk), :] = ((key & I32(0xFFFF)) - HALF_BIAS).astype(I16)
        return c

    lax.fori_loop(0, n_valid, score_tile, 0)

    sub = cnt_scr.shape[0]

    def count(pred):
        cnt_scr[...] = jnp.zeros(cnt_scr.shape, I16)

        def body(j, c):
            off = pl.multiple_of(j * tk, tk)
            hit = jnp.where(pred(lambda: hi_scr[pl.ds(off, tk), :], lambda: lo_scr[pl.ds(off, tk), :], off),
                            I16(1), I16(0))
            part = hit[:sub]
            for cc in range(1, tk // sub):
                part = part + hit[cc * sub:(cc + 1) * sub]
            cnt_scr[...] += part
            return c

        lax.fori_loop(0, n_valid, body, 0)
        return jnp.sum(cnt_scr[...].astype(I32), axis=0, keepdims=True)

    def half_search(count_ge):
        def bit(it, tb):
            cand_b = tb | (I32(1) << (15 - it))
            cnt = count_ge((cand_b - HALF_BIAS).astype(I16))
            return jnp.where(cnt >= n_sel, cand_b, tb)
        return lax.fori_loop(0, 16, bit, jnp.zeros((1, tq), I32)) - HALF_BIAS

    thr_hi32 = half_search(lambda cand: count(lambda hi, lo, off: hi() >= cand))
    thr_hi = thr_hi32.astype(I16)
    c_gt_hi = count(lambda hi, lo, off: hi() > thr_hi)
    c_ge_hi = count(lambda hi, lo, off: hi() >= thr_hi)

    def mask_lo_tile(j, c):
        off = pl.multiple_of(j * tk, tk)
        lo_scr[pl.ds(off, tk), :] = jnp.where(hi_scr[pl.ds(off, tk), :] == thr_hi, lo_scr[pl.ds(off, tk), :],
                                              I16(-HALF_BIAS))
        return c

    lax.fori_loop(0, n_valid, mask_lo_tile, 0)
    thr_lo32 = half_search(lambda cand: c_gt_hi + count(lambda hi, lo, off: lo() >= cand))
    thr_lo = thr_lo32.astype(I16)

    cnt_gt = c_gt_hi + count(lambda hi, lo, off: lo() > thr_lo)
    cnt_ge = jnp.where(thr_lo32 == -HALF_BIAS, c_ge_hi,
                       c_gt_hi + count(lambda hi, lo, off: lo() >= thr_lo))
    need = n_sel - cnt_gt
    neg_inf_key = 0x807FFFFF
    is_neg_inf = ((thr_hi32 == (neg_inf_key >> 16) - 2 * HALF_BIAS)
                  & (thr_lo32 == (neg_inf_key & 0xFFFF) - HALF_BIAS))
    tied = (cnt_ge > n_sel) & jnp.logical_not(is_neg_inf)
    any_tied = jnp.max(tied.astype(I32)) > 0
    idx_bits = (2 * seq - 1).bit_length()
    assert idx_bits < 16, "key indices must fit int16"

    def is_thr(hi, lo):
        return (hi() == thr_hi) & (lo() == thr_lo)

    def tie_cut():
        def cut_bit(it, jc):
            cand = jc | (I32(1) << (idx_bits - 1 - it))
            cand16 = cand.astype(I16)
            cnt = count(lambda hi, lo, off: is_thr(hi, lo) & (key_ids(off).astype(I16) < cand16))
            return jnp.where(cnt <= need, cand, jc)
        return lax.fori_loop(0, idx_bits, cut_bit, jnp.zeros((1, tq), I32))

    jcut = lax.cond(any_tied, tie_cut, lambda: jnp.full((1, tq), 2 ** idx_bits - 1, I32)).astype(I16)

    def bias_tile(j, c):
        off = pl.multiple_of(j * tk, tk)
        hi = lambda: hi_scr[pl.ds(off, tk), :]
        lo = lambda: lo_scr[pl.ds(off, tk), :]
        kid = key_ids(off)
        sel = (hi() > thr_hi) | (lo() > thr_lo) | (is_thr(hi, lo) & (kid.astype(I16) < jcut))
        sel = (jnp.where(sel, I16(1), I16(0)).astype(I32) > 0) & (kid // CHUNK <= qry_chunk)
        bias_scr[pl.ds(off, tk), :] = jnp.where(sel, 0.0, NEG).astype(F32)
        return c

    lax.fori_loop(0, n_valid, bias_tile, 0)

    hp = m_scr.shape[0]

    def biased(s_t, off):
        return s_t + bias_scr[pl.ds(off, tk), :]

    for g in range(DSA_HEADS // hp):
        def head_cols(h, g=g):
            return slice((g * hp + h) * DSA_HEAD_DIM, (g * hp + h + 1) * DSA_HEAD_DIM)

        def qk(h, off):
            return _dot_nt(k_ref[pl.ds(off, tk), head_cols(h)], q_ref[:, head_cols(h)])

        def vt(h, off):
            return vt_ref[head_cols(h), pl.ds(off, tk)]

        _flash_sweep_t(n_valid, tk, hp, qk, vt, biased, biased, m_scr, acc_scr, pipelined=True)
        for h in range(hp):
            o_ref[:, head_cols(h)] = _attn_out_t(acc_scr[h], DSA_HEAD_DIM)


def _mem_kv_kernel(mem_ref, g_ref, wk_ref, wv_ref, k_ref, v_ref):
    mn = _rms(mem_ref[...], g_ref[...]).astype(BF16)
    k_ref[...] = _dot(mn, wk_ref[...]).astype(BF16)
    v_ref[...] = _dot(mn, wv_ref[...]).astype(BF16)


def _post_kernel(a_ref, b_ref, x_ref, woa_ref, wob_ref, gc_ref, wq_ref, km_ref, vm_ref, wo_ref, gm_ref,
                 x2_ref, hm_ref):
    x1 = x_ref[...] + _dot(a_ref[...], woa_ref[...]) + _dot(b_ref[...], wob_ref[...])
    hc = _rms(x1, gc_ref[...]).astype(BF16)
    qc = (_dot(hc, wq_ref[...]) * (X_HEAD_DIM ** -0.5)).astype(BF16)
    outs = []
    for hh in range(X_HEADS):
        sl = slice(hh * X_HEAD_DIM, (hh + 1) * X_HEAD_DIM)
        s = _dot_nt(qc[:, sl], km_ref[:, sl])
        p = jnp.exp(s - jnp.max(s, axis=-1, keepdims=True))
        o = _dot(p.astype(BF16), vm_ref[:, sl]) / jnp.sum(p, axis=-1, keepdims=True)
        outs.append(o.astype(BF16))
    x2 = x1 + _dot(jnp.concatenate(outs, axis=-1), wo_ref[...])
    x2_ref[...] = x2
    hm_ref[...] = _rms(x2, gm_ref[...]).astype(BF16)


def _mlp_kernel(hm_ref, x2_ref, wu_ref, wd_ref, gf_ref, y_ref):
    f = pl.program_id(1)

    @pl.when(f == 0)
    def _():
        y_ref[...] = x2_ref[...]

    u = jnp.maximum(_dot(hm_ref[...], wu_ref[...]), 0.0)
    y_ref[...] += _dot((u * u).astype(BF16), wd_ref[...])

    @pl.when(f == pl.num_programs(1) - 1)
    def _():
        y_ref[...] = _rms(y_ref[...], gf_ref[...])


def _const(shape):
    return pl.BlockSpec(shape, lambda *_: (0,) * len(shape))


def kernel(x, mem, positions, g_mix, w_in, g_cq, g_ckv, w_qb, w_kvb, w_out, g_cross, g_mem,
           w_q_cross, w_k_cross, w_v_cross, w_o_cross, g_mlp, w_up, w_down, g_final):
    assert w_in.shape[0] == 1, "one layer"
    batch, seq, d = x.shape
    t = batch * seq
    n_sel = min(TOPK_MAX, seq // 4)
    x2d = x.reshape(t, d)
    pos = positions.reshape(t, 1)
    row = lambda g: g.reshape(1, -1).astype(F32)

    w = w_in[0]
    o = [0]
    for n in (MLA_Q_LORA, MLA_KV_LORA, MLA_ROPE, DSA_HEADS * DSA_HEAD_DIM, DSA_HEADS * DSA_HEAD_DIM,
              DSA_HEADS * DSA_HEAD_DIM, IDX_HEADS * IDX_DIM, IDX_DIM, IDX_HEADS):
        o.append(o[-1] + n)
    zeros = lambda n: jnp.zeros((d, n), w.dtype)
    w_cq = w[:, o[0]:o[1]].astype(BF16)
    w_ckv = jnp.concatenate([w[:, o[1]:o[3]], zeros(LANES - MLA_ROPE)], axis=1).astype(BF16)
    w_dsa_qk = w[:, o[3]:o[5]].astype(BF16)
    w_dsa_vt = w[:, o[5]:o[6]].T.astype(BF16)
    w_idx = jnp.concatenate([w[:, o[6]:o[7]], w[:, o[7]:o[8]], w[:, o[7]:o[8]], w[:, o[8]:o[9]],
                             zeros(LANES - IDX_HEADS)], axis=1).astype(BF16)
    wqb = w_qb[0].reshape(MLA_Q_LORA, MLA_HEADS, MLA_NOPE + MLA_ROPE)
    wqb = jnp.pad(wqb, ((0, 0), (0, 0), (0, MLA_QK_PAD - MLA_NOPE - MLA_ROPE)))
    wqb = wqb.reshape(MLA_Q_LORA, MLA_HEADS * MLA_QK_PAD).astype(BF16)
    wkvb = w_kvb[0].reshape(MLA_KV_LORA, MLA_HEADS, MLA_NOPE + MLA_V)
    wkvb_k = wkvb[:, :, :MLA_NOPE].reshape(MLA_KV_LORA, -1).astype(BF16)
    wkvb_vt = wkvb[:, :, MLA_NOPE:].reshape(MLA_KV_LORA, -1).T.astype(BF16)
    mla_w = MLA_HEADS * MLA_V
    w_out_a = w_out[0, :mla_w].astype(BF16)
    w_out_b = w_out[0, mla_w:].astype(BF16)
    wq_c, wk_c, wv_c, wo_c = (a[0].astype(BF16) for a in (w_q_cross, w_k_cross, w_v_cross, w_o_cross))
    wu, wd = w_up[0].astype(BF16), w_down[0].astype(BF16)
    invf = jnp.zeros((LANES,), F32)
    for (half, _), off in zip(ROPE_VARIANTS, _rope_lane_plan()):
        inv_freq = ROPE_THETA ** (-jnp.arange(half, dtype=F32) / half)
        invf = invf.at[off:off + 2 * half].set(jnp.concatenate([inv_freq, inv_freq]))
    invf = invf.reshape(1, LANES)

    tm = 512
    nt = t // tm
    qk_w = MLA_HEADS * MLA_QK_PAD
    n_rope = 3 * LANES

    h_mix, rope = pl.pallas_call(
        _prep_kernel, name="prep", grid=(nt,),
        in_specs=[pl.BlockSpec((tm, d), lambda i: (i, 0)), pl.BlockSpec((tm, 1), lambda i: (i, 0)),
                  _const((1, d)), _const((1, LANES))],
        out_specs=[pl.BlockSpec((tm, d), lambda i: (i, 0)),
                   pl.BlockSpec((tm, len(ROPE_VARIANTS) * n_rope), lambda i: (i, 0))],
        out_shape=[jax.ShapeDtypeStruct((t, d), BF16),
                   jax.ShapeDtypeStruct((t, len(ROPE_VARIANTS) * n_rope), F32)],
        compiler_params=_params(("parallel",)),
    )(x2d, pos, row(g_mix), invf)
    h_spec = pl.BlockSpec((tm, d), lambda i, *_: (i, 0))
    rope_spec = lambda v: pl.BlockSpec((tm, n_rope), lambda i, *_: (i, v))

    q_mla = pl.pallas_call(
        _proj_q_kernel, name="proj_q", grid=(nt,),
        in_specs=[h_spec, rope_spec(0), _const((d, MLA_Q_LORA)), _const((1, MLA_Q_LORA)),
                  _const((MLA_Q_LORA, qk_w))],
        out_specs=pl.BlockSpec((tm, qk_w), lambda i: (i, 0)),
        out_shape=jax.ShapeDtypeStruct((t, qk_w), BF16),
        compiler_params=_params(("parallel",)),
    )(h_mix, rope, w_cq, row(g_cq), wqb)

    k_mla, vt_mla = pl.pallas_call(
        _proj_kv_kernel, name="proj_kv", grid=(nt,),
        in_specs=[h_spec, rope_spec(0), _const((d, MLA_KV_LORA + LANES)), _const((1, MLA_KV_LORA)),
                  _const((MLA_KV_LORA, mla_w)), _const((mla_w, MLA_KV_LORA))],
        out_specs=[pl.BlockSpec((tm, qk_w), lambda i: (i, 0)), pl.BlockSpec((mla_w, tm), lambda i: (0, i))],
        out_shape=[jax.ShapeDtypeStruct((t, qk_w), BF16), jax.ShapeDtypeStruct((mla_w, t), BF16)],
        compiler_params=_params(("parallel",)),
    )(h_mix, rope, w_ckv, row(g_ckv), wkvb_k, wkvb_vt)

    dsa_w = DSA_HEADS * DSA_HEAD_DIM
    qk_d, vt_d = pl.pallas_call(
        _proj_dsa_kernel, name="proj_dsa", grid=(nt, 3),
        in_specs=[h_spec, rope_spec(1),
                  pl.BlockSpec((d, dsa_w), lambda i, j: (0, jnp.minimum(j, 1))),
                  _const((dsa_w, d))],
        out_specs=[pl.BlockSpec((None, tm, dsa_w), lambda i, j: (jnp.minimum(j, 1), i, 0)),
                   pl.BlockSpec((dsa_w, tm), lambda i, j: (0, i))],
        out_shape=[jax.ShapeDtypeStruct((2, t, dsa_w), BF16), jax.ShapeDtypeStruct((dsa_w, t), BF16)],
        compiler_params=_params(("parallel", "arbitrary")),
    )(h_mix, rope, w_dsa_qk, w_dsa_vt)

    qi_w = IDX_HEADS * LANES
    q_idx, k_idx, w_idx_s = pl.pallas_call(
        _proj_idx_kernel, name="proj_idx", grid=(nt,),
        in_specs=[h_spec, rope_spec(2), _const((d, IDX_HEADS * IDX_DIM + 2 * LANES))],
        out_specs=[pl.BlockSpec((tm, qi_w), lambda i: (i, 0)), pl.BlockSpec((tm, LANES), lambda i: (i, 0)),
                   pl.BlockSpec((tm, LANES), lambda i: (i, 0))],
        out_shape=[jax.ShapeDtypeStruct((t, qi_w), BF16), jax.ShapeDtypeStruct((t, LANES), BF16),
                   jax.ShapeDtypeStruct((t, LANES), F32)],
        compiler_params=_params(("parallel",)),
    )(h_mix, rope, w_idx)

    tq, tk, hp = 256, 512, 4
    a_mla = pl.pallas_call(
        functools.partial(_mla_attn_kernel, tq=tq, tk=tk, hp=hp), name="mla_attn",
        grid=(batch, MLA_HEADS // hp, seq // tq),
        in_specs=[pl.BlockSpec((None, tq, hp * MLA_QK_PAD), lambda b, h, i: (b, i, h)),
                  pl.BlockSpec((None, seq, hp * MLA_QK_PAD), lambda b, h, i: (b, 0, h)),
                  pl.BlockSpec((hp * MLA_V, seq), lambda b, h, i: (h, b))],
        out_specs=pl.BlockSpec((None, tq, hp * MLA_V), lambda b, h, i: (b, i, h)),
        out_shape=jax.ShapeDtypeStruct((batch, seq, mla_w), BF16),
        scratch_shapes=[pltpu.VMEM((hp, 1, tq), F32), pltpu.VMEM((hp, MLA_V + SUM_ROWS, tq), F32)],
        compiler_params=_params(("parallel", "parallel", "arbitrary")),
    )(q_mla.reshape(batch, seq, qk_w), k_mla.reshape(batch, seq, qk_w), vt_mla)

    qk4 = qk_d.reshape(2, batch, seq, dsa_w)
    tq, tk, hp = 256, 512, 4
    b_dsa = pl.pallas_call(
        functools.partial(_dsa_kernel, tq=tq, tk=tk, n_sel=n_sel, seq=seq), name="dsa",
        grid=(batch, seq // tq),
        in_specs=[pl.BlockSpec((None, None, tq, dsa_w), lambda b, i: (0, b, i, 0)),
                  pl.BlockSpec((None, None, seq, dsa_w), lambda b, i: (1, b, 0, 0), pipeline_mode=pl.Buffered(1)),
                  pl.BlockSpec((dsa_w, seq), lambda b, i: (0, b), pipeline_mode=pl.Buffered(1)),
                  pl.BlockSpec((None, tq, qi_w), lambda b, i: (b, i, 0)),
                  pl.BlockSpec((None, seq, LANES), lambda b, i: (b, 0, 0)),
                  pl.BlockSpec((None, tq, LANES), lambda b, i: (b, i, 0))],
        out_specs=pl.BlockSpec((None, tq, dsa_w), lambda b, i: (b, i, 0)),
        out_shape=jax.ShapeDtypeStruct((batch, seq, dsa_w), BF16),
        scratch_shapes=[pltpu.VMEM((seq, tq), I16), pltpu.VMEM((seq, tq), I16), pltpu.VMEM((seq, tq), F32),
                        pltpu.VMEM((32, tq), I16),
                        pltpu.VMEM((hp, 1, tq), F32), pltpu.VMEM((hp, DSA_HEAD_DIM + SUM_ROWS, tq), F32)],
        compiler_params=_params(("parallel", "arbitrary")),
    )(qk4, qk4, vt_d, q_idx.reshape(batch, seq, qi_w), k_idx.reshape(batch, seq, LANES),
      w_idx_s.reshape(batch, seq, LANES))

    xw = X_HEADS * X_HEAD_DIM
    n_mem = mem.shape[1]
    k_mem, v_mem = pl.pallas_call(
        _mem_kv_kernel, name="mem_kv", grid=(batch,),
        in_specs=[pl.BlockSpec((None, n_mem, d), lambda b: (b, 0, 0)), _const((1, d)), _const((d, xw)),
                  _const((d, xw))],
        out_specs=[pl.BlockSpec((None, n_mem, xw), lambda b: (b, 0, 0))] * 2,
        out_shape=[jax.ShapeDtypeStruct((batch, n_mem, xw), BF16)] * 2,
        compiler_params=_params(("parallel",)),
    )(mem, row(g_mem), wk_c, wv_c)

    tp = 256
    per_b = seq // tp
    mem_spec = pl.BlockSpec((None, n_mem, xw), lambda i: (i // per_b, 0, 0))
    x2, hm = pl.pallas_call(
        _post_kernel, name="post", grid=(t // tp,),
        in_specs=[pl.BlockSpec((tp, mla_w), lambda i: (i, 0)), pl.BlockSpec((tp, dsa_w), lambda i: (i, 0)),
                  pl.BlockSpec((tp, d), lambda i: (i, 0)), _const((mla_w, d)), _const((dsa_w, d)), _const((1, d)),
                  _const((d, xw)), mem_spec, mem_spec, _const((xw, d)), _const((1, d))],
        out_specs=[pl.BlockSpec((tp, d), lambda i: (i, 0))] * 2,
        out_shape=[jax.ShapeDtypeStruct((t, d), F32), jax.ShapeDtypeStruct((t, d), BF16)],
        compiler_params=_params(("parallel",)),
    )(a_mla.reshape(t, mla_w), b_dsa.reshape(t, dsa_w), x2d, w_out_a, w_out_b, row(g_cross), wq_c,
      k_mem, v_mem, wo_c, row(g_mlp))

    tmm, tf = 512, 1024
    y = pl.pallas_call(
        _mlp_kernel, name="mlp", grid=(t // tmm, D_FF // tf),
        in_specs=[pl.BlockSpec((tmm, d), lambda i, f: (i, 0)), pl.BlockSpec((tmm, d), lambda i, f: (i, 0)),
                  pl.BlockSpec((d, tf), lambda i, f: (0, f)), pl.BlockSpec((tf, d), lambda i, f: (f, 0)),
                  _const((1, d))],
        out_specs=pl.BlockSpec((tmm, d), lambda i, f: (i, 0)),
        out_shape=jax.ShapeDtypeStruct((t, d), F32),
        compiler_params=_params(("parallel", "arbitrary")),
    )(hm, x2, wu, wd, row(g_final))
    return y.reshape(batch, seq, d)
```

```python
import functools

import jax
import jax.numpy as jnp
from jax import lax
from jax.experimental import pallas as pl
from jax.experimental.pallas import tpu as pltpu

F32 = jnp.float32
BF16 = jnp.bfloat16
I32 = jnp.int32
I16 = jnp.int16

D_MODEL = 2048
CHUNK = 64
ROPE_THETA = 500000.0
N_MEM = 256
EPS = 1e-6
MLA_HEADS = 8
MLA_NOPE = 128
MLA_ROPE = 64
MLA_V = 128
MLA_Q_LORA = 512
MLA_KV_LORA = 256
DSA_HEADS = 8
DSA_HEAD_DIM = 128
DSA_ROT = DSA_HEAD_DIM // 4
IDX_HEADS = 16
IDX_DIM = 64
IDX_ROT = IDX_DIM // 4
TOPK_MAX = 256
X_HEADS = 4
X_HEAD_DIM = 128
D_FF = 4 * D_MODEL

LANES = 128
MLA_QK_PAD = 256
NEG = -1e30
HALF_BIAS = 2 ** 15
LOG2E = 1.4426950408889634
VMEM_LIMIT = 56 * 1024 * 1024


def _params(sem, vmem=VMEM_LIMIT):
    return pltpu.CompilerParams(dimension_semantics=sem, vmem_limit_bytes=vmem)


def _rms(xf, g):
    return xf * lax.rsqrt(jnp.mean(xf * xf, axis=-1, keepdims=True) + EPS) * g


def _apply_rope(x, coeffs, half):
    c, s_lo, s_hi = coeffs
    return x * c + pltpu.roll(x, LANES - half, 1) * s_lo + pltpu.roll(x, half, 1) * s_hi


def _dot(a, b):
    return jnp.dot(a, b, preferred_element_type=F32)


def _dot_nt(a, b):
    return lax.dot_general(a, b, (((1,), (1,)), ((), ())), preferred_element_type=F32)


ROPE_VARIANTS = ((MLA_ROPE // 2, LANES), (DSA_ROT // 2, LANES), (IDX_ROT // 2, IDX_DIM))


def _rope_lane_plan():
    offs, o = [], 0
    for half, _ in ROPE_VARIANTS:
        offs.append(o)
        o += 2 * half
    assert o <= LANES
    return offs


def _prep_kernel(x_ref, pos_ref, g_ref, invf_ref, h_ref, rope_ref):
    h_ref[...] = _rms(x_ref[...], g_ref[...]).astype(BF16)
    ang = pos_ref[...].astype(F32) * invf_ref[...]
    cos, sin = jnp.cos(ang), jnp.sin(ang)
    lane = lax.broadcasted_iota(I32, (1, LANES), 1)
    for v, ((half, period), off) in enumerate(zip(ROPE_VARIANTS, _rope_lane_plan())):
        c = jnp.ones(cos.shape, F32)
        s_lo = jnp.zeros(cos.shape, F32)
        s_hi = jnp.zeros(cos.shape, F32)
        for base in range(0, LANES, period):
            shift = (base - off) % LANES
            cs = cos if shift == 0 else pltpu.roll(cos, shift, 1)
            sn = sin if shift == 0 else pltpu.roll(sin, shift, 1)
            in_lo = (lane >= base) & (lane < base + half)
            in_hi = (lane >= base + half) & (lane < base + 2 * half)
            c = jnp.where(in_lo | in_hi, cs, c)
            s_lo = jnp.where(in_lo, -sn, s_lo)
            s_hi = jnp.where(in_hi, sn, s_hi)
        rope_ref[:, (3 * v) * LANES:(3 * v + 1) * LANES] = c
        rope_ref[:, (3 * v + 1) * LANES:(3 * v + 2) * LANES] = s_lo
        rope_ref[:, (3 * v + 2) * LANES:(3 * v + 3) * LANES] = s_hi


def _rope_tiles(rope_ref):
    return rope_ref[:, :LANES], rope_ref[:, LANES:2 * LANES], rope_ref[:, 2 * LANES:]


def _proj_q_kernel(h_ref, rope_ref, wcq_ref, gcq_ref, wqb_ref, q_ref):
    cq = _dot(h_ref[...], wcq_ref[...])
    q = _dot(_rms(cq, gcq_ref[...]).astype(BF16), wqb_ref[...])
    coeffs = _rope_tiles(rope_ref)
    scale = (MLA_NOPE + MLA_ROPE) ** -0.5 * LOG2E
    for hh in range(MLA_HEADS):
        b0 = hh * MLA_QK_PAD
        q_ref[:, b0:b0 + LANES] = (q[:, b0:b0 + LANES] * scale).astype(BF16)
        r = _apply_rope(q[:, b0 + LANES:b0 + 2 * LANES], coeffs, MLA_ROPE // 2)
        q_ref[:, b0 + LANES:b0 + 2 * LANES] = (r * scale).astype(BF16)


def _proj_kv_kernel(h_ref, rope_ref, w_ref, gckv_ref, wk_ref, wvt_ref, k_ref, vt_ref):
    r = _dot(h_ref[...], w_ref[...])
    ckv = _rms(r[:, :MLA_KV_LORA], gckv_ref[...]).astype(BF16)
    kn = _dot(ckv, wk_ref[...])
    coeffs = _rope_tiles(rope_ref)
    kr = _apply_rope(r[:, MLA_KV_LORA:], coeffs, MLA_ROPE // 2).astype(BF16)
    for hh in range(MLA_HEADS):
        b0 = hh * MLA_QK_PAD
        k_ref[:, b0:b0 + LANES] = kn[:, hh * LANES:(hh + 1) * LANES].astype(BF16)
        k_ref[:, b0 + LANES:b0 + 2 * LANES] = kr
    vt_ref[...] = _dot_nt(wvt_ref[...], ckv).astype(BF16)


def _proj_dsa_kernel(h_ref, rope_ref, w_ref, wvt_ref, qk_ref, vt_ref):
    j = pl.program_id(1)

    @pl.when(j == 2)
    def _():
        vt_ref[...] = _dot_nt(wvt_ref[...], h_ref[...]).astype(BF16)

    @pl.when(j < 2)
    def _():
        r = _dot(h_ref[...], w_ref[...])
        coeffs = _rope_tiles(rope_ref)
        scale = jnp.where(j == 0, DSA_HEAD_DIM ** -0.5 * LOG2E, 1.0).astype(F32)
        for hh in range(DSA_HEADS):
            sl = slice(hh * LANES, (hh + 1) * LANES)
            qk_ref[:, sl] = (_apply_rope(r[:, sl], coeffs, DSA_ROT // 2) * scale).astype(BF16)


def _proj_idx_kernel(h_ref, rope_ref, w_ref, qi_ref, ki_ref, wi_ref):
    r = _dot(h_ref[...], w_ref[...])
    coeffs = _rope_tiles(rope_ref)
    lane = lax.broadcasted_iota(I32, (1, LANES), 1)
    nq = IDX_HEADS * IDX_DIM
    for p in range(IDX_HEADS // 2):
        t = _apply_rope(r[:, p * LANES:(p + 1) * LANES], coeffs, IDX_ROT // 2) * (IDX_DIM ** -0.5)
        qi_ref[:, (2 * p) * LANES:(2 * p + 1) * LANES] = jnp.where(lane < IDX_DIM, t, 0.0).astype(BF16)
        qi_ref[:, (2 * p + 1) * LANES:(2 * p + 2) * LANES] = jnp.where(lane >= IDX_DIM, t, 0.0).astype(BF16)
    ki_ref[...] = _apply_rope(r[:, nq:nq + LANES], coeffs, IDX_ROT // 2).astype(BF16)
    wi_ref[...] = r[:, nq + LANES:nq + 2 * LANES] * (IDX_HEADS ** -0.5)


SUM_ROWS = 16


def _flash_sweep_t(n_tiles, tk, heads, qk, vt, adjust, adjust_last, m_scr, acc_scr, pipelined):
    m_scr[...] = jnp.full(m_scr.shape, NEG, F32)
    acc_scr[...] = jnp.zeros(acc_scr.shape, F32)
    ones = jnp.ones((SUM_ROWS, tk), BF16)

    def finish(h, s_t, off, adj):
        s_t = adj(s_t, off)
        m_prev = m_scr[h]
        m_new = jnp.maximum(m_prev, jnp.max(s_t, axis=0, keepdims=True))
        p = jnp.exp2(s_t - m_new).astype(BF16)
        pv = _dot(jnp.concatenate([vt(h, off), ones], axis=0), p)
        acc_scr[h] = jnp.exp2(m_prev - m_new) * acc_scr[h] + pv
        m_scr[h] = m_new

    off_last = pl.multiple_of((n_tiles - 1) * tk, tk)
    if pipelined:
        def body(j, s_cur):
            off = pl.multiple_of(j * tk, tk)
            off_n = pl.multiple_of((j + 1) * tk, tk)
            s_new = [qk(h, off_n) for h in range(min(2, heads))]
            for h in range(heads):
                finish(h, s_cur[h], off, adjust)
                if h + 2 < heads:
                    s_new.append(qk(h + 2, off_n))
            return tuple(s_new)

        s_last = lax.fori_loop(0, n_tiles - 1, body, tuple(qk(h, 0) for h in range(heads)))
        for h in range(heads):
            finish(h, s_last[h], off_last, adjust_last)
    else:
        def tile(off, adj):
            s_ts = [qk(h, off) for h in range(heads)]
            for h in range(heads):
                finish(h, s_ts[h], off, adj)

        def body(j, c):
            tile(pl.multiple_of(j * tk, tk), adjust)
            return c

        lax.fori_loop(0, n_tiles - 1, body, 0)
        tile(off_last, adjust_last)


def _attn_out_t(acc, dv):
    return (acc[:dv] / acc[dv:dv + 1]).T.astype(BF16)


def _mla_attn_kernel(q_ref, k_ref, vt_ref, o_ref, m_scr, acc_scr, *, tq, tk, hp):
    i = pl.program_id(2)
    n_tiles = (i * tq) // tk + 1
    qry_chunk = (i * tq + lax.broadcasted_iota(I32, (1, tq), 1)) // CHUNK

    def qk(h, off):
        qs = slice(h * MLA_QK_PAD, (h + 1) * MLA_QK_PAD)
        return _dot_nt(k_ref[pl.ds(off, tk), qs], q_ref[:, qs])

    def vt(h, off):
        return vt_ref[h * MLA_V:(h + 1) * MLA_V, pl.ds(off, tk)]

    def causal(s_t, off):
        key_chunk = (off + lax.broadcasted_iota(I32, (tk, 1), 0)) // CHUNK
        return jnp.where(key_chunk <= qry_chunk, s_t, NEG)

    _flash_sweep_t(n_tiles, tk, hp, qk, vt, lambda s_t, off: s_t, causal, m_scr, acc_scr, pipelined=False)
    for h in range(hp):
        o_ref[:, h * MLA_V:(h + 1) * MLA_V] = _attn_out_t(acc_scr[h], MLA_V)


def _dsa_kernel(q_ref, k_ref, vt_ref, qi_ref, ki_ref, wi_ref, o_ref,
                hi_scr, lo_scr, bias_scr, cnt_scr, m_scr, acc_scr, *, tq, tk, n_sel, seq):
    i = pl.program_id(1)
    n_valid = ((i + 1) * tq + tk - 1) // tk
    qry_chunk = (i * tq + lax.broadcasted_iota(I32, (1, tq), 1)) // CHUNK

    def key_ids(off):
        return off + lax.broadcasted_iota(I32, (tk, 1), 0)

    wi_t = wi_ref[...].T

    def score_tile(j, c):
        off = pl.multiple_of(j * tk, tk)
        kj = ki_ref[pl.ds(off, tk), :]
        sc = jnp.zeros((tk, tq), F32)
        for hh in range(IDX_HEADS):
            lg = _dot_nt(kj, qi_ref[:, hh * LANES:(hh + 1) * LANES])
            sc = sc + jnp.maximum(lg, 0.0) * wi_t[hh:hh + 1, :]
        sc = jnp.where(sc == 0.0, 0.0, sc)
        sc = jnp.where(key_ids(off) // CHUNK <= qry_chunk, sc, -jnp.inf)
        bits = pltpu.bitcast(sc, I32)
        key = bits ^ ((bits >> 31) & I32(0x7FFFFFFF))
        hi_scr[pl.ds(off, tk), :] = (key >> 16).astype(I16)
        lo_scr[pl.ds(off, tk), :] = ((key & I32(0xFFFF)) - HALF_BIAS).astype(I16)
        return c

    lax.fori_loop(0, n_valid, score_tile, 0)

    sub = cnt_scr.shape[0]

    def count(pred):
        cnt_scr[...] = jnp.zeros(cnt_scr.shape, I16)

        def body(j, c):
            off = pl.multiple_of(j * tk, tk)
            hit = jnp.where(pred(lambda: hi_scr[pl.ds(off, tk), :], lambda: lo_scr[pl.ds(off, tk), :], off),
                            I16(1), I16(0))
            part = hit[:sub]
            for cc in range(1, tk // sub):
                part = part + hit[cc * sub:(cc + 1) * sub]
            cnt_scr[...] += part
            return c

        lax.fori_loop(0, n_valid, body, 0)
        return jnp.sum(cnt_scr[...].astype(I32), axis=0, keepdims=True)

    def half_search(count_ge):
        def bit(it, tb):
            cand_b = tb | (I32(1) << (15 - it))
            cnt = count_ge((cand_b - HALF_BIAS).astype(I16))
            return jnp.where(cnt >= n_sel, cand_b, tb)
        return lax.fori_loop(0, 16, bit, jnp.zeros((1, tq), I32)) - HALF_BIAS

    thr_hi32 = half_search(lambda cand: count(lambda hi, lo, off: hi() >= cand))
    thr_hi = thr_hi32.astype(I16)
    c_gt_hi = count(lambda hi, lo, off: hi() > thr_hi)
    c_ge_hi = count(lambda hi, lo, off: hi() >= thr_hi)

    lo_max, lo_min = HALF_BIAS - 1, -HALF_BIAS

    def mask_lo_tile(j, c):
        off = pl.multiple_of(j * tk, tk)
        hi = hi_scr[pl.ds(off, tk), :]
        outside = jnp.where(hi > thr_hi, I16(lo_max), I16(lo_min))
        lo_scr[pl.ds(off, tk), :] = jnp.where(hi == thr_hi, lo_scr[pl.ds(off, tk), :], outside)
        return c

    lax.fori_loop(0, n_valid, mask_lo_tile, 0)
    thr_lo32 = half_search(lambda cand: count(lambda hi, lo, off: lo() >= cand))
    thr_lo = thr_lo32.astype(I16)

    cnt_gt = count(lambda hi, lo, off: lo() > thr_lo) + jnp.where(thr_lo32 == lo_max, c_gt_hi, 0)
    cnt_ge = jnp.where(thr_lo32 == lo_min, c_ge_hi, count(lambda hi, lo, off: lo() >= thr_lo))
    need = n_sel - cnt_gt
    neg_inf_key = 0x807FFFFF
    is_neg_inf = ((thr_hi32 == (neg_inf_key >> 16) - 2 * HALF_BIAS)
                  & (thr_lo32 == (neg_inf_key & 0xFFFF) - HALF_BIAS))
    tied = (cnt_ge > n_sel) & jnp.logical_not(is_neg_inf)
    any_tied = jnp.max(tied.astype(I32)) > 0
    idx_bits = (2 * seq - 1).bit_length()
    assert idx_bits < 16, "key indices must fit int16"

    def is_thr(hi, lo):
        return (hi() == thr_hi) & (lo() == thr_lo)

    def tie_cut():
        def cut_bit(it, jc):
            cand = jc | (I32(1) << (idx_bits - 1 - it))
            cand16 = cand.astype(I16)
            cnt = count(lambda hi, lo, off: is_thr(hi, lo) & (key_ids(off).astype(I16) < cand16))
            return jnp.where(cnt <= need, cand, jc)
        return lax.fori_loop(0, idx_bits, cut_bit, jnp.zeros((1, tq), I32))

    jcut = lax.cond(any_tied, tie_cut, lambda: jnp.full((1, tq), 2 ** idx_bits - 1, I32)).astype(I16)

    def write_bias(off, sel16, causal):
        sel = jnp.where(sel16, I16(1), I16(0)).astype(I32) > 0
        if causal:
            sel = sel & (key_ids(off) // CHUNK <= qry_chunk)
        bias_scr[pl.ds(off, tk), :] = jnp.where(sel, 0.0, NEG).astype(F32)

    def sel_plain(off):
        return lo_scr[pl.ds(off, tk), :] >= thr_lo

    def sel_general(off):
        hi = lambda: hi_scr[pl.ds(off, tk), :]
        lo = lambda: lo_scr[pl.ds(off, tk), :]
        gt = (hi() > thr_hi) | ((hi() == thr_hi) & (lo() > thr_lo))
        return gt | (is_thr(hi, lo) & (key_ids(off).astype(I16) < jcut))

    general = any_tied | (jnp.max((thr_lo32 == lo_min).astype(I32)) > 0)
    off_last = pl.multiple_of((n_valid - 1) * tk, tk)
    for pred, sel_fn in ((general, sel_general), (jnp.logical_not(general), sel_plain)):
        @pl.when(pred)
        def _(sel_fn=sel_fn):
            def bias_tile(j, c):
                off = pl.multiple_of(j * tk, tk)
                write_bias(off, sel_fn(off), causal=False)
                return c

            lax.fori_loop(0, n_valid - 1, bias_tile, 0)
            write_bias(off_last, sel_fn(off_last), causal=True)

    hp = m_scr.shape[0]

    def biased(s_t, off):
        return s_t + bias_scr[pl.ds(off, tk), :]

    for g in range(DSA_HEADS // hp):
        def head_cols(h, g=g):
            return slice((g * hp + h) * DSA_HEAD_DIM, (g * hp + h + 1) * DSA_HEAD_DIM)

        def qk(h, off):
            return _dot_nt(k_ref[pl.ds(off, tk), head_cols(h)], q_ref[:, head_cols(h)])

        def vt(h, off):
            return vt_ref[head_cols(h), pl.ds(off, tk)]

        _flash_sweep_t(n_valid, tk, hp, qk, vt, biased, biased, m_scr, acc_scr, pipelined=True)
        for h in range(hp):
            o_ref[:, head_cols(h)] = _attn_out_t(acc_scr[h], DSA_HEAD_DIM)


def _mem_kv_kernel(mem_ref, g_ref, wk_ref, wv_ref, k_ref, v_ref):
    mn = _rms(mem_ref[...], g_ref[...]).astype(BF16)
    k_ref[...] = _dot(mn, wk_ref[...]).astype(BF16)
    v_ref[...] = _dot(mn, wv_ref[...]).astype(BF16)


def _post_kernel(a_ref, b_ref, x_ref, woa_ref, wob_ref, gc_ref, wq_ref, km_ref, vm_ref, wo_ref, gm_ref,
                 x2_ref, hm_ref):
    x1 = x_ref[...] + _dot(a_ref[...], woa_ref[...]) + _dot(b_ref[...], wob_ref[...])
    hc = _rms(x1, gc_ref[...]).astype(BF16)
    qc = (_dot(hc, wq_ref[...]) * (X_HEAD_DIM ** -0.5)).astype(BF16)
    outs = []
    for hh in range(X_HEADS):
        sl = slice(hh * X_HEAD_DIM, (hh + 1) * X_HEAD_DIM)
        s = _dot_nt(qc[:, sl], km_ref[:, sl])
        p = jnp.exp(s - jnp.max(s, axis=-1, keepdims=True))
        o = _dot(p.astype(BF16), vm_ref[:, sl]) / jnp.sum(p, axis=-1, keepdims=True)
        outs.append(o.astype(BF16))
    x2 = x1 + _dot(jnp.concatenate(outs, axis=-1), wo_ref[...])
    x2_ref[...] = x2
    hm_ref[...] = _rms(x2, gm_ref[...]).astype(BF16)


def _mlp_kernel(hm_ref, x2_ref, wu_ref, wd_ref, gf_ref, y_ref):
    f = pl.program_id(1)

    @pl.when(f == 0)
    def _():
        y_ref[...] = x2_ref[...]

    u = jnp.maximum(_dot(hm_ref[...], wu_ref[...]), 0.0)
    y_ref[...] += _dot((u * u).astype(BF16), wd_ref[...])

    @pl.when(f == pl.num_programs(1) - 1)
    def _():
        y_ref[...] = _rms(y_ref[...], gf_ref[...])


def _const(shape):
    return pl.BlockSpec(shape, lambda *_: (0,) * len(shape))


def kernel(x, mem, positions, g_mix, w_in, g_cq, g_ckv, w_qb, w_kvb, w_out, g_cross, g_mem,
           w_q_cross, w_k_cross, w_v_cross, w_o_cross, g_mlp, w_up, w_down, g_final):
    assert w_in.shape[0] == 1, "one layer"
    batch, seq, d = x.shape
    t = batch * seq
    n_sel = min(TOPK_MAX, seq // 4)
    x2d = x.reshape(t, d)
    pos = positions.reshape(t, 1)
    row = lambda g: g.reshape(1, -1).astype(F32)

    w = w_in[0]
    o = [0]
    for n in (MLA_Q_LORA, MLA_KV_LORA, MLA_ROPE, DSA_HEADS * DSA_HEAD_DIM, DSA_HEADS * DSA_HEAD_DIM,
              DSA_HEADS * DSA_HEAD_DIM, IDX_HEADS * IDX_DIM, IDX_DIM, IDX_HEADS):
        o.append(o[-1] + n)
    zeros = lambda n: jnp.zeros((d, n), w.dtype)
    w_cq = w[:, o[0]:o[1]].astype(BF16)
    w_ckv = jnp.concatenate([w[:, o[1]:o[3]], zeros(LANES - MLA_ROPE)], axis=1).astype(BF16)
    w_dsa_qk = w[:, o[3]:o[5]].astype(BF16)
    w_dsa_vt = w[:, o[5]:o[6]].T.astype(BF16)
    w_idx = jnp.concatenate([w[:, o[6]:o[7]], w[:, o[7]:o[8]], w[:, o[7]:o[8]], w[:, o[8]:o[9]],
                             zeros(LANES - IDX_HEADS)], axis=1).astype(BF16)
    wqb = w_qb[0].reshape(MLA_Q_LORA, MLA_HEADS, MLA_NOPE + MLA_ROPE)
    wqb = jnp.pad(wqb, ((0, 0), (0, 0), (0, MLA_QK_PAD - MLA_NOPE - MLA_ROPE)))
    wqb = wqb.reshape(MLA_Q_LORA, MLA_HEADS * MLA_QK_PAD).astype(BF16)
    wkvb = w_kvb[0].reshape(MLA_KV_LORA, MLA_HEADS, MLA_NOPE + MLA_V)
    wkvb_k = wkvb[:, :, :MLA_NOPE].reshape(MLA_KV_LORA, -1).astype(BF16)
    wkvb_vt = wkvb[:, :, MLA_NOPE:].reshape(MLA_KV_LORA, -1).T.astype(BF16)
    mla_w = MLA_HEADS * MLA_V
    w_out_a = w_out[0, :mla_w].astype(BF16)
    w_out_b = w_out[0, mla_w:].astype(BF16)
    wq_c, wk_c, wv_c, wo_c = (a[0].astype(BF16) for a in (w_q_cross, w_k_cross, w_v_cross, w_o_cross))
    wu, wd = w_up[0].astype(BF16), w_down[0].astype(BF16)
    invf = jnp.zeros((LANES,), F32)
    for (half, _), off in zip(ROPE_VARIANTS, _rope_lane_plan()):
        inv_freq = ROPE_THETA ** (-jnp.arange(half, dtype=F32) / half)
        invf = invf.at[off:off + 2 * half].set(jnp.concatenate([inv_freq, inv_freq]))
    invf = invf.reshape(1, LANES)

    tm = 512
    nt = t // tm
    qk_w = MLA_HEADS * MLA_QK_PAD
    n_rope = 3 * LANES

    h_mix, rope = pl.pallas_call(
        _prep_kernel, name="prep", grid=(nt,),
        in_specs=[pl.BlockSpec((tm, d), lambda i: (i, 0)), pl.BlockSpec((tm, 1), lambda i: (i, 0)),
                  _const((1, d)), _const((1, LANES))],
        out_specs=[pl.BlockSpec((tm, d), lambda i: (i, 0)),
                   pl.BlockSpec((tm, len(ROPE_VARIANTS) * n_rope), lambda i: (i, 0))],
        out_shape=[jax.ShapeDtypeStruct((t, d), BF16),
                   jax.ShapeDtypeStruct((t, len(ROPE_VARIANTS) * n_rope), F32)],
        compiler_params=_params(("parallel",)),
    )(x2d, pos, row(g_mix), invf)
    h_spec = pl.BlockSpec((tm, d), lambda i, *_: (i, 0))
    rope_spec = lambda v: pl.BlockSpec((tm, n_rope), lambda i, *_: (i, v))

    q_mla = pl.pallas_call(
        _proj_q_kernel, name="proj_q", grid=(nt,),
        in_specs=[h_spec, rope_spec(0), _const((d, MLA_Q_LORA)), _const((1, MLA_Q_LORA)),
                  _const((MLA_Q_LORA, qk_w))],
        out_specs=pl.BlockSpec((tm, qk_w), lambda i: (i, 0)),
        out_shape=jax.ShapeDtypeStruct((t, qk_w), BF16),
        compiler_params=_params(("parallel",)),
    )(h_mix, rope, w_cq, row(g_cq), wqb)

    k_mla, vt_mla = pl.pallas_call(
        _proj_kv_kernel, name="proj_kv", grid=(nt,),
        in_specs=[h_spec, rope_spec(0), _const((d, MLA_KV_LORA + LANES)), _const((1, MLA_KV_LORA)),
                  _const((MLA_KV_LORA, mla_w)), _const((mla_w, MLA_KV_LORA))],
        out_specs=[pl.BlockSpec((tm, qk_w), lambda i: (i, 0)), pl.BlockSpec((mla_w, tm), lambda i: (0, i))],
        out_shape=[jax.ShapeDtypeStruct((t, qk_w), BF16), jax.ShapeDtypeStruct((mla_w, t), BF16)],
        compiler_params=_params(("parallel",)),
    )(h_mix, rope, w_ckv, row(g_ckv), wkvb_k, wkvb_vt)

    dsa_w = DSA_HEADS * DSA_HEAD_DIM
    qk_d, vt_d = pl.pallas_call(
        _proj_dsa_kernel, name="proj_dsa", grid=(nt, 3),
        in_specs=[h_spec, rope_spec(1),
                  pl.BlockSpec((d, dsa_w), lambda i, j: (0, jnp.minimum(j, 1))),
                  _const((dsa_w, d))],
        out_specs=[pl.BlockSpec((None, tm, dsa_w), lambda i, j: (jnp.minimum(j, 1), i, 0)),
                   pl.BlockSpec((dsa_w, tm), lambda i, j: (0, i))],
        out_shape=[jax.ShapeDtypeStruct((2, t, dsa_w), BF16), jax.ShapeDtypeStruct((dsa_w, t), BF16)],
        compiler_params=_params(("parallel", "arbitrary")),
    )(h_mix, rope, w_dsa_qk, w_dsa_vt)

    qi_w = IDX_HEADS * LANES
    q_idx, k_idx, w_idx_s = pl.pallas_call(
        _proj_idx_kernel, name="proj_idx", grid=(nt,),
        in_specs=[h_spec, rope_spec(2), _const((d, IDX_HEADS * IDX_DIM + 2 * LANES))],
        out_specs=[pl.BlockSpec((tm, qi_w), lambda i: (i, 0)), pl.BlockSpec((tm, LANES), lambda i: (i, 0)),
                   pl.BlockSpec((tm, LANES), lambda i: (i, 0))],
        out_shape=[jax.ShapeDtypeStruct((t, qi_w), BF16), jax.ShapeDtypeStruct((t, LANES), BF16),
                   jax.ShapeDtypeStruct((t, LANES), F32)],
        compiler_params=_params(("parallel",)),
    )(h_mix, rope, w_idx)

    tq, tk, hp = 256, 512, 8
    a_mla = pl.pallas_call(
        functools.partial(_mla_attn_kernel, tq=tq, tk=tk, hp=hp), name="mla_attn",
        grid=(batch, MLA_HEADS // hp, seq // tq),
        in_specs=[pl.BlockSpec((None, tq, hp * MLA_QK_PAD), lambda b, h, i: (b, i, h)),
                  pl.BlockSpec((None, seq, hp * MLA_QK_PAD), lambda b, h, i: (b, 0, h),
                               pipeline_mode=pl.Buffered(1)),
                  pl.BlockSpec((hp * MLA_V, seq), lambda b, h, i: (h, b), pipeline_mode=pl.Buffered(1))],
        out_specs=pl.BlockSpec((None, tq, hp * MLA_V), lambda b, h, i: (b, i, h)),
        out_shape=jax.ShapeDtypeStruct((batch, seq, mla_w), BF16),
        scratch_shapes=[pltpu.VMEM((hp, 1, tq), F32), pltpu.VMEM((hp, MLA_V + SUM_ROWS, tq), F32)],
        compiler_params=_params(("parallel", "parallel", "arbitrary")),
    )(q_mla.reshape(batch, seq, qk_w), k_mla.reshape(batch, seq, qk_w), vt_mla)

    qk4 = qk_d.reshape(2, batch, seq, dsa_w)
    tq, tk, hp = 256, 512, 8
    b_dsa = pl.pallas_call(
        functools.partial(_dsa_kernel, tq=tq, tk=tk, n_sel=n_sel, seq=seq), name="dsa",
        grid=(batch, seq // tq),
        in_specs=[pl.BlockSpec((None, None, tq, dsa_w), lambda b, i: (0, b, i, 0)),
                  pl.BlockSpec((None, None, seq, dsa_w), lambda b, i: (1, b, 0, 0), pipeline_mode=pl.Buffered(1)),
                  pl.BlockSpec((dsa_w, seq), lambda b, i: (0, b), pipeline_mode=pl.Buffered(1)),
                  pl.BlockSpec((None, tq, qi_w), lambda b, i: (b, i, 0)),
                  pl.BlockSpec((None, seq, LANES), lambda b, i: (b, 0, 0)),
                  pl.BlockSpec((None, tq, LANES), lambda b, i: (b, i, 0))],
        out_specs=pl.BlockSpec((None, tq, dsa_w), lambda b, i: (b, i, 0)),
        out_shape=jax.ShapeDtypeStruct((batch, seq, dsa_w), BF16),
        scratch_shapes=[pltpu.VMEM((seq, tq), I16), pltpu.VMEM((seq, tq), I16), pltpu.VMEM((seq, tq), F32),
                        pltpu.VMEM((32, tq), I16),
                        pltpu.VMEM((hp, 1, tq), F32), pltpu.VMEM((hp, DSA_HEAD_DIM + SUM_ROWS, tq), F32)],
        compiler_params=_params(("parallel", "arbitrary")),
    )(qk4, qk4, vt_d, q_idx.reshape(batch, seq, qi_w), k_idx.reshape(batch, seq, LANES),
      w_idx_s.reshape(batch, seq, LANES))

    xw = X_HEADS * X_HEAD_DIM
    n_mem = mem.shape[1]
    k_mem, v_mem = pl.pallas_call(
        _mem_kv_kernel, name="mem_kv", grid=(batch,),
        in_specs=[pl.BlockSpec((None, n_mem, d), lambda b: (b, 0, 0)), _const((1, d)), _const((d, xw)),
                  _const((d, xw))],
        out_specs=[pl.BlockSpec((None, n_mem, xw), lambda b: (b, 0, 0))] * 2,
        out_shape=[jax.ShapeDtypeStruct((batch, n_mem, xw), BF16)] * 2,
        compiler_params=_params(("parallel",)),
    )(mem, row(g_mem), wk_c, wv_c)

    tp = 256
    per_b = seq // tp
    mem_spec = pl.BlockSpec((None, n_mem, xw), lambda i: (i // per_b, 0, 0))
    x2, hm = pl.pallas_call(
        _post_kernel, name="post", grid=(t // tp,),
        in_specs=[pl.BlockSpec((tp, mla_w), lambda i: (i, 0)), pl.BlockSpec((tp, dsa_w), lambda i: (i, 0)),
                  pl.BlockSpec((tp, d), lambda i: (i, 0)), _const((mla_w, d)), _const((dsa_w, d)), _const((1, d)),
                  _const((d, xw)), mem_spec, mem_spec, _const((xw, d)), _const((1, d))],
        out_specs=[pl.BlockSpec((tp, d), lambda i: (i, 0))] * 2,
        out_shape=[jax.ShapeDtypeStruct((t, d), F32), jax.ShapeDtypeStruct((t, d), BF16)],
        compiler_params=_params(("parallel",)),
    )(a_mla.reshape(t, mla_w), b_dsa.reshape(t, dsa_w), x2d, w_out_a, w_out_b, row(g_cross), wq_c,
      k_mem, v_mem, wo_c, row(g_mlp))

    tmm, tf = 512, 1024
    y = pl.pallas_call(
        _mlp_kernel, name="mlp", grid=(t // tmm, D_FF // tf),
        in_specs=[pl.BlockSpec((tmm, d), lambda i, f: (i, 0)), pl.BlockSpec((tmm, d), lambda i, f: (i, 0)),
                  pl.BlockSpec((d, tf), lambda i, f: (0, f)), pl.BlockSpec((tf, d), lambda i, f: (f, 0)),
                  _const((1, d))],
        out_specs=pl.BlockSpec((tmm, d), lambda i, f: (i, 0)),
        out_shape=jax.ShapeDtypeStruct((t, d), F32),
        compiler_params=_params(("parallel", "arbitrary")),
    )(hm, x2, wu, wd, row(g_final))
    return y.reshape(batch, seq, d)
```

```python
import functools

import jax
import jax.numpy as jnp
from jax import lax
from jax.experimental import pallas as pl
from jax.experimental.pallas import tpu as pltpu

F32 = jnp.float32
BF16 = jnp.bfloat16
I32 = jnp.int32
I16 = jnp.int16

D_MODEL = 2048
CHUNK = 64
ROPE_THETA = 500000.0
N_MEM = 256
EPS = 1e-6
MLA_HEADS = 8
MLA_NOPE = 128
MLA_ROPE = 64
MLA_V = 128
MLA_Q_LORA = 512
MLA_KV_LORA = 256
DSA_HEADS = 8
DSA_HEAD_DIM = 128
DSA_ROT = DSA_HEAD_DIM // 4
IDX_HEADS = 16
IDX_DIM = 64
IDX_ROT = IDX_DIM // 4
TOPK_MAX = 256
X_HEADS = 4
X_HEAD_DIM = 128
D_FF = 4 * D_MODEL

LANES = 128
MLA_QK_PAD = 256
NEG = -1e30
HALF_BIAS = 2 ** 15
LOG2E = 1.4426950408889634
VMEM_LIMIT = 56 * 1024 * 1024


def _params(sem, vmem=VMEM_LIMIT):
    return pltpu.CompilerParams(dimension_semantics=sem, vmem_limit_bytes=vmem)


def _rms(xf, g):
    return xf * lax.rsqrt(jnp.mean(xf * xf, axis=-1, keepdims=True) + EPS) * g


def _apply_rope(x, coeffs, half):
    c, s_lo, s_hi = coeffs
    return x * c + pltpu.roll(x, LANES - half, 1) * s_lo + pltpu.roll(x, half, 1) * s_hi


def _dot(a, b):
    return jnp.dot(a, b, preferred_element_type=F32)


def _dot_nt(a, b):
    return lax.dot_general(a, b, (((1,), (1,)), ((), ())), preferred_element_type=F32)


ROPE_VARIANTS = ((MLA_ROPE // 2, LANES), (DSA_ROT // 2, LANES), (IDX_ROT // 2, IDX_DIM))


def _rope_lane_plan():
    offs, o = [], 0
    for half, _ in ROPE_VARIANTS:
        offs.append(o)
        o += 2 * half
    assert o <= LANES
    return offs


def _prep_kernel(x_ref, pos_ref, g_ref, invf_ref, h_ref, rope_ref):
    h_ref[...] = _rms(x_ref[...], g_ref[...]).astype(BF16)
    ang = pos_ref[...].astype(F32) * invf_ref[...]
    cos, sin = jnp.cos(ang), jnp.sin(ang)
    lane = lax.broadcasted_iota(I32, (1, LANES), 1)
    for v, ((half, period), off) in enumerate(zip(ROPE_VARIANTS, _rope_lane_plan())):
        c = jnp.ones(cos.shape, F32)
        s_lo = jnp.zeros(cos.shape, F32)
        s_hi = jnp.zeros(cos.shape, F32)
        for base in range(0, LANES, period):
            shift = (base - off) % LANES
            cs = cos if shift == 0 else pltpu.roll(cos, shift, 1)
            sn = sin if shift == 0 else pltpu.roll(sin, shift, 1)
            in_lo = (lane >= base) & (lane < base + half)
            in_hi = (lane >= base + half) & (lane < base + 2 * half)
            c = jnp.where(in_lo | in_hi, cs, c)
            s_lo = jnp.where(in_lo, -sn, s_lo)
            s_hi = jnp.where(in_hi, sn, s_hi)
        rope_ref[:, (3 * v) * LANES:(3 * v + 1) * LANES] = c
        rope_ref[:, (3 * v + 1) * LANES:(3 * v + 2) * LANES] = s_lo
        rope_ref[:, (3 * v + 2) * LANES:(3 * v + 3) * LANES] = s_hi


def _rope_tiles(rope_ref):
    return rope_ref[:, :LANES], rope_ref[:, LANES:2 * LANES], rope_ref[:, 2 * LANES:]


def _proj_q_kernel(h_ref, rope_ref, wcq_ref, gcq_ref, wqb_ref, q_ref):
    cq = _dot(h_ref[...], wcq_ref[...])
    q = _dot(_rms(cq, gcq_ref[...]).astype(BF16), wqb_ref[...])
    coeffs = _rope_tiles(rope_ref)
    scale = (MLA_NOPE + MLA_ROPE) ** -0.5 * LOG2E
    for hh in range(MLA_HEADS):
        b0 = hh * MLA_QK_PAD
        q_ref[:, b0:b0 + LANES] = (q[:, b0:b0 + LANES] * scale).astype(BF16)
        r = _apply_rope(q[:, b0 + LANES:b0 + 2 * LANES], coeffs, MLA_ROPE // 2)
        q_ref[:, b0 + LANES:b0 + 2 * LANES] = (r * scale).astype(BF16)


def _proj_kv_kernel(h_ref, rope_ref, w_ref, gckv_ref, wk_ref, wvt_ref, k_ref, vt_ref):
    r = _dot(h_ref[...], w_ref[...])
    ckv = _rms(r[:, :MLA_KV_LORA], gckv_ref[...]).astype(BF16)
    kn = _dot(ckv, wk_ref[...])
    coeffs = _rope_tiles(rope_ref)
    kr = _apply_rope(r[:, MLA_KV_LORA:], coeffs, MLA_ROPE // 2).astype(BF16)
    for hh in range(MLA_HEADS):
        b0 = hh * MLA_QK_PAD
        k_ref[:, b0:b0 + LANES] = kn[:, hh * LANES:(hh + 1) * LANES].astype(BF16)
        k_ref[:, b0 + LANES:b0 + 2 * LANES] = kr
    vt_ref[...] = _dot_nt(wvt_ref[...], ckv).astype(BF16)


def _proj_dsa_kernel(h_ref, rope_ref, w_ref, wvt_ref, qk_ref, vt_ref):
    j = pl.program_id(1)

    @pl.when(j == 2)
    def _():
        vt_ref[...] = _dot_nt(wvt_ref[...], h_ref[...]).astype(BF16)

    @pl.when(j < 2)
    def _():
        r = _dot(h_ref[...], w_ref[...])
        coeffs = _rope_tiles(rope_ref)
        scale = jnp.where(j == 0, DSA_HEAD_DIM ** -0.5 * LOG2E, 1.0).astype(F32)
        for hh in range(DSA_HEADS):
            sl = slice(hh * LANES, (hh + 1) * LANES)
            qk_ref[:, sl] = (_apply_rope(r[:, sl], coeffs, DSA_ROT // 2) * scale).astype(BF16)


def _proj_idx_kernel(h_ref, rope_ref, w_ref, qi_ref, ki_ref, wi_ref):
    r = _dot(h_ref[...], w_ref[...])
    coeffs = _rope_tiles(rope_ref)
    lane = lax.broadcasted_iota(I32, (1, LANES), 1)
    nq = IDX_HEADS * IDX_DIM
    for p in range(IDX_HEADS // 2):
        t = _apply_rope(r[:, p * LANES:(p + 1) * LANES], coeffs, IDX_ROT // 2) * (IDX_DIM ** -0.5)
        qi_ref[:, (2 * p) * LANES:(2 * p + 1) * LANES] = jnp.where(lane < IDX_DIM, t, 0.0).astype(BF16)
        qi_ref[:, (2 * p + 1) * LANES:(2 * p + 2) * LANES] = jnp.where(lane >= IDX_DIM, t, 0.0).astype(BF16)
    ki_ref[...] = _apply_rope(r[:, nq:nq + LANES], coeffs, IDX_ROT // 2).astype(BF16)
    wi_ref[...] = r[:, nq + LANES:nq + 2 * LANES] * (IDX_HEADS ** -0.5)


SUM_ROWS = 16


def _flash_sweep_t(n_tiles, tk, heads, qk, vt, adjust, adjust_last, m_scr, acc_scr, pipelined):
    m_scr[...] = jnp.full(m_scr.shape, NEG, F32)
    acc_scr[...] = jnp.zeros(acc_scr.shape, F32)
    ones = jnp.ones((SUM_ROWS, tk), BF16)

    def scores(h, off, adj):
        s_t = adj(qk(h, off), off)
        return s_t, jnp.max(s_t, axis=0, keepdims=True)

    def finish(h, s_and_max, off):
        s_t, m_tile = s_and_max
        m_prev = m_scr[h]
        m_new = jnp.maximum(m_prev, m_tile)
        p = jnp.exp2(s_t - m_new).astype(BF16)
        pv = _dot(jnp.concatenate([vt(h, off), ones], axis=0), p)
        acc_scr[h] = jnp.exp2(m_prev - m_new) * acc_scr[h] + pv
        m_scr[h] = m_new

    off_last = pl.multiple_of((n_tiles - 1) * tk, tk)
    if pipelined:
        assert adjust is adjust_last, "scores are adjusted one tile ahead"

        def body(j, s_cur):
            off = pl.multiple_of(j * tk, tk)
            off_n = pl.multiple_of((j + 1) * tk, tk)
            s_new = [scores(h, off_n, adjust) for h in range(min(2, heads))]
            for h in range(heads):
                finish(h, s_cur[h], off)
                if h + 2 < heads:
                    s_new.append(scores(h + 2, off_n, adjust))
            return tuple(s_new)

        s_last = lax.fori_loop(0, n_tiles - 1, body, tuple(scores(h, 0, adjust) for h in range(heads)))
        for h in range(heads):
            finish(h, s_last[h], off_last)
    else:
        def tile(off, adj):
            s_ts = [scores(h, off, adj) for h in range(heads)]
            for h in range(heads):
                finish(h, s_ts[h], off)

        def body(j, c):
            tile(pl.multiple_of(j * tk, tk), adjust)
            return c

        lax.fori_loop(0, n_tiles - 1, body, 0)
        tile(off_last, adjust_last)


def _attn_out_t(acc, dv):
    return (acc[:dv] / acc[dv:dv + 1]).T.astype(BF16)


def _mla_attn_kernel(q_ref, k_ref, vt_ref, o_ref, m_scr, acc_scr, *, tq, tk, hp):
    i = pl.program_id(2)
    n_tiles = (i * tq) // tk + 1
    qry_chunk = (i * tq + lax.broadcasted_iota(I32, (1, tq), 1)) // CHUNK

    def qk(h, off):
        qs = slice(h * MLA_QK_PAD, (h + 1) * MLA_QK_PAD)
        return _dot_nt(k_ref[pl.ds(off, tk), qs], q_ref[:, qs])

    def vt(h, off):
        return vt_ref[h * MLA_V:(h + 1) * MLA_V, pl.ds(off, tk)]

    def causal(s_t, off):
        key_chunk = (off + lax.broadcasted_iota(I32, (tk, 1), 0)) // CHUNK
        return jnp.where(key_chunk <= qry_chunk, s_t, NEG)

    _flash_sweep_t(n_tiles, tk, hp, qk, vt, lambda s_t, off: s_t, causal, m_scr, acc_scr, pipelined=False)
    for h in range(hp):
        o_ref[:, h * MLA_V:(h + 1) * MLA_V] = _attn_out_t(acc_scr[h], MLA_V)


def _dsa_kernel(q_ref, k_ref, vt_ref, qi_ref, ki_ref, wi_ref, o_ref,
                hi_scr, lo_scr, bias_scr, cnt_scr, m_scr, acc_scr, *, tq, tk, n_sel, seq):
    i = pl.program_id(1)
    n_valid = ((i + 1) * tq + tk - 1) // tk
    qry_chunk = (i * tq + lax.broadcasted_iota(I32, (1, tq), 1)) // CHUNK

    def key_ids(off):
        return off + lax.broadcasted_iota(I32, (tk, 1), 0)

    wi_t = wi_ref[...].T

    def score_tile(j, c):
        off = pl.multiple_of(j * tk, tk)
        kj = ki_ref[pl.ds(off, tk), :]
        sc = jnp.zeros((tk, tq), F32)
        for hh in range(IDX_HEADS):
            lg = _dot_nt(kj, qi_ref[:, hh * LANES:(hh + 1) * LANES])
            sc = sc + jnp.maximum(lg, 0.0) * wi_t[hh:hh + 1, :]
        sc = jnp.where(sc == 0.0, 0.0, sc)
        sc = jnp.where(key_ids(off) // CHUNK <= qry_chunk, sc, -jnp.inf)
        bits = pltpu.bitcast(sc, I32)
        key = bits ^ ((bits >> 31) & I32(0x7FFFFFFF))
        hi_scr[pl.ds(off, tk), :] = (key >> 16).astype(I16)
        lo_scr[pl.ds(off, tk), :] = ((key & I32(0xFFFF)) - HALF_BIAS).astype(I16)
        return c

    lax.fori_loop(0, n_valid, score_tile, 0)

    sub = cnt_scr.shape[0]

    def count(pred):
        cnt_scr[...] = jnp.zeros(cnt_scr.shape, I16)

        def body(j, c):
            off = pl.multiple_of(j * tk, tk)
            hit = jnp.where(pred(lambda: hi_scr[pl.ds(off, tk), :], lambda: lo_scr[pl.ds(off, tk), :], off),
                            I16(1), I16(0))
            part = hit[:sub]
            for cc in range(1, tk // sub):
                part = part + hit[cc * sub:(cc + 1) * sub]
            cnt_scr[...] += part
            return c

        lax.fori_loop(0, n_valid, body, 0)
        return jnp.sum(cnt_scr[...].astype(I32), axis=0, keepdims=True)

    def half_search(count_ge):
        def bit(it, tb):
            cand_b = tb | (I32(1) << (15 - it))
            cnt = count_ge((cand_b - HALF_BIAS).astype(I16))
            return jnp.where(cnt >= n_sel, cand_b, tb)
        return lax.fori_loop(0, 16, bit, jnp.zeros((1, tq), I32)) - HALF_BIAS

    thr_hi32 = half_search(lambda cand: count(lambda hi, lo, off: hi() >= cand))
    thr_hi = thr_hi32.astype(I16)
    c_gt_hi = count(lambda hi, lo, off: hi() > thr_hi)
    c_ge_hi = count(lambda hi, lo, off: hi() >= thr_hi)

    lo_max, lo_min = HALF_BIAS - 1, -HALF_BIAS

    def mask_lo_tile(j, c):
        off = pl.multiple_of(j * tk, tk)
        hi = hi_scr[pl.ds(off, tk), :]
        outside = jnp.where(hi > thr_hi, I16(lo_max), I16(lo_min))
        lo_scr[pl.ds(off, tk), :] = jnp.where(hi == thr_hi, lo_scr[pl.ds(off, tk), :], outside)
        return c

    lax.fori_loop(0, n_valid, mask_lo_tile, 0)
    thr_lo32 = half_search(lambda cand: count(lambda hi, lo, off: lo() >= cand))
    thr_lo = thr_lo32.astype(I16)

    cnt_gt = count(lambda hi, lo, off: lo() > thr_lo) + jnp.where(thr_lo32 == lo_max, c_gt_hi, 0)
    cnt_ge = jnp.where(thr_lo32 == lo_min, c_ge_hi, count(lambda hi, lo, off: lo() >= thr_lo))
    need = n_sel - cnt_gt
    neg_inf_key = 0x807FFFFF
    is_neg_inf = ((thr_hi32 == (neg_inf_key >> 16) - 2 * HALF_BIAS)
                  & (thr_lo32 == (neg_inf_key & 0xFFFF) - HALF_BIAS))
    tied = (cnt_ge > n_sel) & jnp.logical_not(is_neg_inf)
    any_tied = jnp.max(tied.astype(I32)) > 0
    idx_bits = (2 * seq - 1).bit_length()
    assert idx_bits < 16, "key indices must fit int16"

    def is_thr(hi, lo):
        return (hi() == thr_hi) & (lo() == thr_lo)

    def tie_cut():
        def cut_bit(it, jc):
            cand = jc | (I32(1) << (idx_bits - 1 - it))
            cand16 = cand.astype(I16)
            cnt = count(lambda hi, lo, off: is_thr(hi, lo) & (key_ids(off).astype(I16) < cand16))
            return jnp.where(cnt <= need, cand, jc)
        return lax.fori_loop(0, idx_bits, cut_bit, jnp.zeros((1, tq), I32))

    jcut = lax.cond(any_tied, tie_cut, lambda: jnp.full((1, tq), 2 ** idx_bits - 1, I32)).astype(I16)

    def write_bias(off, sel16, causal):
        sel = jnp.where(sel16, I16(1), I16(0)).astype(I32) > 0
        if causal:
            sel = sel & (key_ids(off) // CHUNK <= qry_chunk)
        bias_scr[pl.ds(off, tk), :] = jnp.where(sel, 0.0, NEG).astype(F32)

    def sel_plain(off):
        return lo_scr[pl.ds(off, tk), :] >= thr_lo

    def sel_general(off):
        hi = lambda: hi_scr[pl.ds(off, tk), :]
        lo = lambda: lo_scr[pl.ds(off, tk), :]
        gt = (hi() > thr_hi) | ((hi() == thr_hi) & (lo() > thr_lo))
        return gt | (is_thr(hi, lo) & (key_ids(off).astype(I16) < jcut))

    general = any_tied | (jnp.max((thr_lo32 == lo_min).astype(I32)) > 0)
    off_last = pl.multiple_of((n_valid - 1) * tk, tk)
    for pred, sel_fn in ((general, sel_general), (jnp.logical_not(general), sel_plain)):
        @pl.when(pred)
        def _(sel_fn=sel_fn):
            def bias_tile(j, c):
                off = pl.multiple_of(j * tk, tk)
                write_bias(off, sel_fn(off), causal=False)
                return c

            lax.fori_loop(0, n_valid - 1, bias_tile, 0)
            write_bias(off_last, sel_fn(off_last), causal=True)

    hp = m_scr.shape[0]

    def biased(s_t, off):
        return s_t + bias_scr[pl.ds(off, tk), :]

    for g in range(DSA_HEADS // hp):
        def head_cols(h, g=g):
            return slice((g * hp + h) * DSA_HEAD_DIM, (g * hp + h + 1) * DSA_HEAD_DIM)

        def qk(h, off):
            return _dot_nt(k_ref[pl.ds(off, tk), head_cols(h)], q_ref[:, head_cols(h)])

        def vt(h, off):
            return vt_ref[head_cols(h), pl.ds(off, tk)]

        _flash_sweep_t(n_valid, tk, hp, qk, vt, biased, biased, m_scr, acc_scr, pipelined=True)
        for h in range(hp):
            o_ref[:, head_cols(h)] = _attn_out_t(acc_scr[h], DSA_HEAD_DIM)


def _mem_kv_kernel(mem_ref, g_ref, wk_ref, wv_ref, k_ref, v_ref):
    mn = _rms(mem_ref[...], g_ref[...]).astype(BF16)
    k_ref[...] = _dot(mn, wk_ref[...]).astype(BF16)
    v_ref[...] = _dot(mn, wv_ref[...]).astype(BF16)


def _post_kernel(a_ref, b_ref, x_ref, woa_ref, wob_ref, gc_ref, wq_ref, km_ref, vm_ref, wo_ref, gm_ref,
                 x2_ref, hm_ref):
    x1 = x_ref[...] + _dot(a_ref[...], woa_ref[...]) + _dot(b_ref[...], wob_ref[...])
    hc = _rms(x1, gc_ref[...]).astype(BF16)
    qc = (_dot(hc, wq_ref[...]) * (X_HEAD_DIM ** -0.5)).astype(BF16)
    outs = []
    for hh in range(X_HEADS):
        sl = slice(hh * X_HEAD_DIM, (hh + 1) * X_HEAD_DIM)
        s = _dot_nt(qc[:, sl], km_ref[:, sl])
        p = jnp.exp(s - jnp.max(s, axis=-1, keepdims=True))
        o = _dot(p.astype(BF16), vm_ref[:, sl]) / jnp.sum(p, axis=-1, keepdims=True)
        outs.append(o.astype(BF16))
    x2 = x1 + _dot(jnp.concatenate(outs, axis=-1), wo_ref[...])
    x2_ref[...] = x2
    hm_ref[...] = _rms(x2, gm_ref[...]).astype(BF16)


def _mlp_kernel(hm_ref, x2_ref, wu_ref, wd_ref, gf_ref, y_ref):
    f = pl.program_id(1)

    @pl.when(f == 0)
    def _():
        y_ref[...] = x2_ref[...]

    u = jnp.maximum(_dot(hm_ref[...], wu_ref[...]), 0.0)
    y_ref[...] += _dot((u * u).astype(BF16), wd_ref[...])

    @pl.when(f == pl.num_programs(1) - 1)
    def _():
        y_ref[...] = _rms(y_ref[...], gf_ref[...])


def _const(shape):
    return pl.BlockSpec(shape, lambda *_: (0,) * len(shape), pipeline_mode=pl.Buffered(1))


def kernel(x, mem, positions, g_mix, w_in, g_cq, g_ckv, w_qb, w_kvb, w_out, g_cross, g_mem,
           w_q_cross, w_k_cross, w_v_cross, w_o_cross, g_mlp, w_up, w_down, g_final):
    assert w_in.shape[0] == 1, "one layer"
    batch, seq, d = x.shape
    t = batch * seq
    n_sel = min(TOPK_MAX, seq // 4)
    x2d = x.reshape(t, d)
    pos = positions.reshape(t, 1)
    row = lambda g: g.reshape(1, -1).astype(F32)

    w = w_in[0]
    o = [0]
    for n in (MLA_Q_LORA, MLA_KV_LORA, MLA_ROPE, DSA_HEADS * DSA_HEAD_DIM, DSA_HEADS * DSA_HEAD_DIM,
              DSA_HEADS * DSA_HEAD_DIM, IDX_HEADS * IDX_DIM, IDX_DIM, IDX_HEADS):
        o.append(o[-1] + n)
    zeros = lambda n: jnp.zeros((d, n), w.dtype)
    w_cq = w[:, o[0]:o[1]].astype(BF16)
    w_ckv = jnp.concatenate([w[:, o[1]:o[3]], zeros(LANES - MLA_ROPE)], axis=1).astype(BF16)
    w_dsa_qk = w[:, o[3]:o[5]].astype(BF16)
    w_dsa_vt = w[:, o[5]:o[6]].T.astype(BF16)
    w_idx = jnp.concatenate([w[:, o[6]:o[7]], w[:, o[7]:o[8]], w[:, o[7]:o[8]], w[:, o[8]:o[9]],
                             zeros(LANES - IDX_HEADS)], axis=1).astype(BF16)
    wqb = w_qb[0].reshape(MLA_Q_LORA, MLA_HEADS, MLA_NOPE + MLA_ROPE)
    wqb = jnp.pad(wqb, ((0, 0), (0, 0), (0, MLA_QK_PAD - MLA_NOPE - MLA_ROPE)))
    wqb = wqb.reshape(MLA_Q_LORA, MLA_HEADS * MLA_QK_PAD).astype(BF16)
    wkvb = w_kvb[0].reshape(MLA_KV_LORA, MLA_HEADS, MLA_NOPE + MLA_V)
    wkvb_k = wkvb[:, :, :MLA_NOPE].reshape(MLA_KV_LORA, -1).astype(BF16)
    wkvb_vt = wkvb[:, :, MLA_NOPE:].reshape(MLA_KV_LORA, -1).T.astype(BF16)
    mla_w = MLA_HEADS * MLA_V
    w_out_a = w_out[0, :mla_w].astype(BF16)
    w_out_b = w_out[0, mla_w:].astype(BF16)
    wq_c, wk_c, wv_c, wo_c = (a[0].astype(BF16) for a in (w_q_cross, w_k_cross, w_v_cross, w_o_cross))
    wu, wd = w_up[0].astype(BF16), w_down[0].astype(BF16)
    invf = jnp.zeros((LANES,), F32)
    for (half, _), off in zip(ROPE_VARIANTS, _rope_lane_plan()):
        inv_freq = ROPE_THETA ** (-jnp.arange(half, dtype=F32) / half)
        invf = invf.at[off:off + 2 * half].set(jnp.concatenate([inv_freq, inv_freq]))
    invf = invf.reshape(1, LANES)

    tm = 512
    nt = t // tm
    qk_w = MLA_HEADS * MLA_QK_PAD
    n_rope = 3 * LANES

    h_mix, rope = pl.pallas_call(
        _prep_kernel, name="prep", grid=(nt,),
        in_specs=[pl.BlockSpec((tm, d), lambda i: (i, 0)), pl.BlockSpec((tm, 1), lambda i: (i, 0)),
                  _const((1, d)), _const((1, LANES))],
        out_specs=[pl.BlockSpec((tm, d), lambda i: (i, 0)),
                   pl.BlockSpec((tm, len(ROPE_VARIANTS) * n_rope), lambda i: (i, 0))],
        out_shape=[jax.ShapeDtypeStruct((t, d), BF16),
                   jax.ShapeDtypeStruct((t, len(ROPE_VARIANTS) * n_rope), F32)],
        compiler_params=_params(("parallel",)),
    )(x2d, pos, row(g_mix), invf)
    h_spec = pl.BlockSpec((tm, d), lambda i, *_: (i, 0))
    rope_spec = lambda v: pl.BlockSpec((tm, n_rope), lambda i, *_: (i, v))

    q_mla = pl.pallas_call(
        _proj_q_kernel, name="proj_q", grid=(nt,),
        in_specs=[h_spec, rope_spec(0), _const((d, MLA_Q_LORA)), _const((1, MLA_Q_LORA)),
                  _const((MLA_Q_LORA, qk_w))],
        out_specs=pl.BlockSpec((tm, qk_w), lambda i: (i, 0)),
        out_shape=jax.ShapeDtypeStruct((t, qk_w), BF16),
        compiler_params=_params(("parallel",)),
    )(h_mix, rope, w_cq, row(g_cq), wqb)

    k_mla, vt_mla = pl.pallas_call(
        _proj_kv_kernel, name="proj_kv", grid=(nt,),
        in_specs=[h_spec, rope_spec(0), _const((d, MLA_KV_LORA + LANES)), _const((1, MLA_KV_LORA)),
                  _const((MLA_KV_LORA, mla_w)), _const((mla_w, MLA_KV_LORA))],
        out_specs=[pl.BlockSpec((tm, qk_w), lambda i: (i, 0)), pl.BlockSpec((mla_w, tm), lambda i: (0, i))],
        out_shape=[jax.ShapeDtypeStruct((t, qk_w), BF16), jax.ShapeDtypeStruct((mla_w, t), BF16)],
        compiler_params=_params(("parallel",)),
    )(h_mix, rope, w_ckv, row(g_ckv), wkvb_k, wkvb_vt)

    dsa_w = DSA_HEADS * DSA_HEAD_DIM
    qk_d, vt_d = pl.pallas_call(
        _proj_dsa_kernel, name="proj_dsa", grid=(nt, 3),
        in_specs=[h_spec, rope_spec(1),
                  pl.BlockSpec((d, dsa_w), lambda i, j: (0, jnp.minimum(j, 1))),
                  _const((dsa_w, d))],
        out_specs=[pl.BlockSpec((None, tm, dsa_w), lambda i, j: (jnp.minimum(j, 1), i, 0)),
                   pl.BlockSpec((dsa_w, tm), lambda i, j: (0, i))],
        out_shape=[jax.ShapeDtypeStruct((2, t, dsa_w), BF16), jax.ShapeDtypeStruct((dsa_w, t), BF16)],
        compiler_params=_params(("parallel", "arbitrary")),
    )(h_mix, rope, w_dsa_qk, w_dsa_vt)

    qi_w = IDX_HEADS * LANES
    q_idx, k_idx, w_idx_s = pl.pallas_call(
        _proj_idx_kernel, name="proj_idx", grid=(nt,),
        in_specs=[h_spec, rope_spec(2), _const((d, IDX_HEADS * IDX_DIM + 2 * LANES))],
        out_specs=[pl.BlockSpec((tm, qi_w), lambda i: (i, 0)), pl.BlockSpec((tm, LANES), lambda i: (i, 0)),
                   pl.BlockSpec((tm, LANES), lambda i: (i, 0))],
        out_shape=[jax.ShapeDtypeStruct((t, qi_w), BF16), jax.ShapeDtypeStruct((t, LANES), BF16),
                   jax.ShapeDtypeStruct((t, LANES), F32)],
        compiler_params=_params(("parallel",)),
    )(h_mix, rope, w_idx)

    tq, tk, hp = 256, 512, 8
    a_mla = pl.pallas_call(
        functools.partial(_mla_attn_kernel, tq=tq, tk=tk, hp=hp), name="mla_attn",
        grid=(batch, MLA_HEADS // hp, seq // tq),
        in_specs=[pl.BlockSpec((None, tq, hp * MLA_QK_PAD), lambda b, h, i: (b, i, h)),
                  pl.BlockSpec((None, seq, hp * MLA_QK_PAD), lambda b, h, i: (b, 0, h),
                               pipeline_mode=pl.Buffered(1)),
                  pl.BlockSpec((hp * MLA_V, seq), lambda b, h, i: (h, b), pipeline_mode=pl.Buffered(1))],
        out_specs=pl.BlockSpec((None, tq, hp * MLA_V), lambda b, h, i: (b, i, h)),
        out_shape=jax.ShapeDtypeStruct((batch, seq, mla_w), BF16),
        scratch_shapes=[pltpu.VMEM((hp, 1, tq), F32), pltpu.VMEM((hp, MLA_V + SUM_ROWS, tq), F32)],
        compiler_params=_params(("parallel", "parallel", "arbitrary")),
    )(q_mla.reshape(batch, seq, qk_w), k_mla.reshape(batch, seq, qk_w), vt_mla)

    qk4 = qk_d.reshape(2, batch, seq, dsa_w)
    tq, tk, hp = 256, 512, 8
    b_dsa = pl.pallas_call(
        functools.partial(_dsa_kernel, tq=tq, tk=tk, n_sel=n_sel, seq=seq), name="dsa",
        grid=(batch, seq // tq),
        in_specs=[pl.BlockSpec((None, None, tq, dsa_w), lambda b, i: (0, b, i, 0)),
                  pl.BlockSpec((None, None, seq, dsa_w), lambda b, i: (1, b, 0, 0), pipeline_mode=pl.Buffered(1)),
                  pl.BlockSpec((dsa_w, seq), lambda b, i: (0, b), pipeline_mode=pl.Buffered(1)),
                  pl.BlockSpec((None, tq, qi_w), lambda b, i: (b, i, 0)),
                  pl.BlockSpec((None, seq, LANES), lambda b, i: (b, 0, 0)),
                  pl.BlockSpec((None, tq, LANES), lambda b, i: (b, i, 0))],
        out_specs=pl.BlockSpec((None, tq, dsa_w), lambda b, i: (b, i, 0)),
        out_shape=jax.ShapeDtypeStruct((batch, seq, dsa_w), BF16),
        scratch_shapes=[pltpu.VMEM((seq, tq), I16), pltpu.VMEM((seq, tq), I16), pltpu.VMEM((seq, tq), F32),
                        pltpu.VMEM((32, tq), I16),
                        pltpu.VMEM((hp, 1, tq), F32), pltpu.VMEM((hp, DSA_HEAD_DIM + SUM_ROWS, tq), F32)],
        compiler_params=_params(("parallel", "arbitrary")),
    )(qk4, qk4, vt_d, q_idx.reshape(batch, seq, qi_w), k_idx.reshape(batch, seq, LANES),
      w_idx_s.reshape(batch, seq, LANES))

    xw = X_HEADS * X_HEAD_DIM
    n_mem = mem.shape[1]
    k_mem, v_mem = pl.pallas_call(
        _mem_kv_kernel, name="mem_kv", grid=(batch,),
        in_specs=[pl.BlockSpec((None, n_mem, d), lambda b: (b, 0, 0)), _const((1, d)), _const((d, xw)),
                  _const((d, xw))],
        out_specs=[pl.BlockSpec((None, n_mem, xw), lambda b: (b, 0, 0))] * 2,
        out_shape=[jax.ShapeDtypeStruct((batch, n_mem, xw), BF16)] * 2,
        compiler_params=_params(("parallel",)),
    )(mem, row(g_mem), wk_c, wv_c)

    tp = 512
    per_b = seq // tp
    mem_spec = pl.BlockSpec((None, n_mem, xw), lambda i: (i // per_b, 0, 0))
    x2, hm = pl.pallas_call(
        _post_kernel, name="post", grid=(t // tp,),
        in_specs=[pl.BlockSpec((tp, mla_w), lambda i: (i, 0)), pl.BlockSpec((tp, dsa_w), lambda i: (i, 0)),
                  pl.BlockSpec((tp, d), lambda i: (i, 0)), _const((mla_w, d)), _const((dsa_w, d)), _const((1, d)),
                  _const((d, xw)), mem_spec, mem_spec, _const((xw, d)), _const((1, d))],
        out_specs=[pl.BlockSpec((tp, d), lambda i: (i, 0))] * 2,
        out_shape=[jax.ShapeDtypeStruct((t, d), F32), jax.ShapeDtypeStruct((t, d), BF16)],
        compiler_params=_params(("parallel",)),
    )(a_mla.reshape(t, mla_w), b_dsa.reshape(t, dsa_w), x2d, w_out_a, w_out_b, row(g_cross), wq_c,
      k_mem, v_mem, wo_c, row(g_mlp))

    tmm, tf = 1024, 1024
    y = pl.pallas_call(
        _mlp_kernel, name="mlp", grid=(t // tmm, D_FF // tf),
        in_specs=[pl.BlockSpec((tmm, d), lambda i, f: (i, 0)),
                  pl.BlockSpec((tmm, d), lambda i, f: (i, 0), pipeline_mode=pl.Buffered(1)),
                  pl.BlockSpec((d, tf), lambda i, f: (0, f)), pl.BlockSpec((tf, d), lambda i, f: (f, 0)),
                  _const((1, d))],
        out_specs=pl.BlockSpec((tmm, d), lambda i, f: (i, 0)),
        out_shape=jax.ShapeDtypeStruct((t, d), F32),
        compiler_params=_params(("parallel", "arbitrary")),
    )(hm, x2, wu, wd, row(g_final))
    return y.reshape(batch, seq, d)
```

```python
import functools

import jax
import jax.numpy as jnp
from jax import lax
from jax.experimental import pallas as pl
from jax.experimental.pallas import tpu as pltpu

F32 = jnp.float32
BF16 = jnp.bfloat16
I32 = jnp.int32
I16 = jnp.int16

D_MODEL = 2048
CHUNK = 64
ROPE_THETA = 500000.0
N_MEM = 256
EPS = 1e-6
MLA_HEADS = 8
MLA_NOPE = 128
MLA_ROPE = 64
MLA_V = 128
MLA_Q_LORA = 512
MLA_KV_LORA = 256
DSA_HEADS = 8
DSA_HEAD_DIM = 128
DSA_ROT = DSA_HEAD_DIM // 4
IDX_HEADS = 16
IDX_DIM = 64
IDX_ROT = IDX_DIM // 4
TOPK_MAX = 256
X_HEADS = 4
X_HEAD_DIM = 128
D_FF = 4 * D_MODEL

LANES = 128
MLA_QK_PAD = 256
NEG = -1e30
HALF_BIAS = 2 ** 15
LOG2E = 1.4426950408889634
VMEM_LIMIT = 56 * 1024 * 1024


def _params(sem, vmem=VMEM_LIMIT):
    return pltpu.CompilerParams(dimension_semantics=sem, vmem_limit_bytes=vmem)


def _rms(xf, g):
    return xf * lax.rsqrt(jnp.mean(xf * xf, axis=-1, keepdims=True) + EPS) * g


def _apply_rope(x, coeffs, half):
    c, s_lo, s_hi = coeffs
    return x * c + pltpu.roll(x, LANES - half, 1) * s_lo + pltpu.roll(x, half, 1) * s_hi


def _dot(a, b):
    return jnp.dot(a, b, preferred_element_type=F32)


def _dot_nt(a, b):
    return lax.dot_general(a, b, (((1,), (1,)), ((), ())), preferred_element_type=F32)


ROPE_VARIANTS = ((MLA_ROPE // 2, LANES), (DSA_ROT // 2, LANES), (IDX_ROT // 2, IDX_DIM))


def _rope_lane_plan():
    offs, o = [], 0
    for half, _ in ROPE_VARIANTS:
        offs.append(o)
        o += 2 * half
    assert o <= LANES
    return offs


def _prep_kernel(x_ref, pos_ref, g_ref, invf_ref, h_ref, rope_ref):
    h_ref[...] = _rms(x_ref[...], g_ref[...]).astype(BF16)
    ang = pos_ref[...].astype(F32) * invf_ref[...]
    cos, sin = jnp.cos(ang), jnp.sin(ang)
    lane = lax.broadcasted_iota(I32, (1, LANES), 1)
    for v, ((half, period), off) in enumerate(zip(ROPE_VARIANTS, _rope_lane_plan())):
        c = jnp.ones(cos.shape, F32)
        s_lo = jnp.zeros(cos.shape, F32)
        s_hi = jnp.zeros(cos.shape, F32)
        for base in range(0, LANES, period):
            shift = (base - off) % LANES
            cs = cos if shift == 0 else pltpu.roll(cos, shift, 1)
            sn = sin if shift == 0 else pltpu.roll(sin, shift, 1)
            in_lo = (lane >= base) & (lane < base + half)
            in_hi = (lane >= base + half) & (lane < base + 2 * half)
            c = jnp.where(in_lo | in_hi, cs, c)
            s_lo = jnp.where(in_lo, -sn, s_lo)
            s_hi = jnp.where(in_hi, sn, s_hi)
        rope_ref[:, (3 * v) * LANES:(3 * v + 1) * LANES] = c
        rope_ref[:, (3 * v + 1) * LANES:(3 * v + 2) * LANES] = s_lo
        rope_ref[:, (3 * v + 2) * LANES:(3 * v + 3) * LANES] = s_hi


def _proj_kernel(h_ref, rope_ref, wcq_ref, gcq_ref, wqb_ref, wckv_ref, gckv_ref, wk_ref, wvt_ref,
                 wdsa_ref, wdsa_vt_ref, widx_ref,
                 q_ref, k_ref, vt_ref, qk_d_ref, vt_d_ref, qi_ref, ki_ref, wi_ref):
    h = h_ref[...]
    n_rope = 3 * LANES
    rope = [tuple(rope_ref[:, v * n_rope + c * LANES:v * n_rope + (c + 1) * LANES] for c in range(3))
            for v in range(len(ROPE_VARIANTS))]
    lane = lax.broadcasted_iota(I32, (1, LANES), 1)

    cq = _dot(h, wcq_ref[...])
    r = _dot(h, wckv_ref[...])
    q = _dot(_rms(cq, gcq_ref[...]).astype(BF16), wqb_ref[...])
    ckv = _rms(r[:, :MLA_KV_LORA], gckv_ref[...]).astype(BF16)
    kn = _dot(ckv, wk_ref[...])
    vt_ref[...] = _dot_nt(wvt_ref[...], ckv).astype(BF16)
    scale = (MLA_NOPE + MLA_ROPE) ** -0.5 * LOG2E
    kr = _apply_rope(r[:, MLA_KV_LORA:], rope[0], MLA_ROPE // 2).astype(BF16)
    for hh in range(MLA_HEADS):
        b0 = hh * MLA_QK_PAD
        q_ref[:, b0:b0 + LANES] = (q[:, b0:b0 + LANES] * scale).astype(BF16)
        qr = _apply_rope(q[:, b0 + LANES:b0 + 2 * LANES], rope[0], MLA_ROPE // 2)
        q_ref[:, b0 + LANES:b0 + 2 * LANES] = (qr * scale).astype(BF16)
        k_ref[:, b0:b0 + LANES] = kn[:, hh * LANES:(hh + 1) * LANES].astype(BF16)
        k_ref[:, b0 + LANES:b0 + 2 * LANES] = kr

    dsa_w = DSA_HEADS * DSA_HEAD_DIM
    for g, scale in enumerate((DSA_HEAD_DIM ** -0.5 * LOG2E, 1.0)):
        rd = _dot(h, wdsa_ref[:, g * dsa_w:(g + 1) * dsa_w])
        for hh in range(DSA_HEADS):
            sl = slice(hh * LANES, (hh + 1) * LANES)
            qk_d_ref[g, :, sl] = (_apply_rope(rd[:, sl], rope[1], DSA_ROT // 2) * scale).astype(BF16)
    vt_d_ref[...] = _dot_nt(wdsa_vt_ref[...], h).astype(BF16)

    ri = _dot(h, widx_ref[...])
    nq = IDX_HEADS * IDX_DIM
    for p in range(IDX_HEADS // 2):
        t = _apply_rope(ri[:, p * LANES:(p + 1) * LANES], rope[2], IDX_ROT // 2) * (IDX_DIM ** -0.5)
        qi_ref[:, (2 * p) * LANES:(2 * p + 1) * LANES] = jnp.where(lane < IDX_DIM, t, 0.0).astype(BF16)
        qi_ref[:, (2 * p + 1) * LANES:(2 * p + 2) * LANES] = jnp.where(lane >= IDX_DIM, t, 0.0).astype(BF16)
    ki_ref[...] = _apply_rope(ri[:, nq:nq + LANES], rope[2], IDX_ROT // 2).astype(BF16)
    wi_ref[...] = ri[:, nq + LANES:nq + 2 * LANES] * (IDX_HEADS ** -0.5)


SUM_ROWS = 16


def _flash_sweep_t(n_tiles, tk, heads, qk, vt, adjust, adjust_last, m_scr, acc_scr, pipelined):
    m_scr[...] = jnp.full(m_scr.shape, NEG, F32)
    acc_scr[...] = jnp.zeros(acc_scr.shape, F32)
    ones = jnp.ones((SUM_ROWS, tk), BF16)

    def scores(h, off, adj):
        s_t = adj(qk(h, off), off)
        return s_t, jnp.max(s_t, axis=0, keepdims=True)

    def finish(h, s_and_max, off):
        s_t, m_tile = s_and_max
        m_prev = m_scr[h]
        m_new = jnp.maximum(m_prev, m_tile)
        p = jnp.exp2(s_t - m_new).astype(BF16)
        pv = _dot(jnp.concatenate([vt(h, off), ones], axis=0), p)
        acc_scr[h] = jnp.exp2(m_prev - m_new) * acc_scr[h] + pv
        m_scr[h] = m_new

    off_last = pl.multiple_of((n_tiles - 1) * tk, tk)
    if pipelined:
        assert adjust is adjust_last, "scores are adjusted one tile ahead"

        def body(j, s_cur):
            off = pl.multiple_of(j * tk, tk)
            off_n = pl.multiple_of((j + 1) * tk, tk)
            s_new = [scores(h, off_n, adjust) for h in range(min(2, heads))]
            for h in range(heads):
                finish(h, s_cur[h], off)
                if h + 2 < heads:
                    s_new.append(scores(h + 2, off_n, adjust))
            return tuple(s_new)

        s_last = lax.fori_loop(0, n_tiles - 1, body, tuple(scores(h, 0, adjust) for h in range(heads)))
        for h in range(heads):
            finish(h, s_last[h], off_last)
    else:
        def tile(off, adj):
            s_ts = [scores(h, off, adj) for h in range(heads)]
            for h in range(heads):
                finish(h, s_ts[h], off)

        def body(j, c):
            tile(pl.multiple_of(j * tk, tk), adjust)
            return c

        lax.fori_loop(0, n_tiles - 1, body, 0)
        tile(off_last, adjust_last)


def _attn_out_t(acc, dv):
    return (acc[:dv] / acc[dv:dv + 1]).T.astype(BF16)


def _mla_attn_kernel(q_ref, k_ref, vt_ref, o_ref, m_scr, acc_scr, *, tq, tk, hp):
    i = pl.program_id(2)
    n_tiles = (i * tq) // tk + 1
    qry_chunk = (i * tq + lax.broadcasted_iota(I32, (1, tq), 1)) // CHUNK

    def qk(h, off):
        qs = slice(h * MLA_QK_PAD, (h + 1) * MLA_QK_PAD)
        return _dot_nt(k_ref[pl.ds(off, tk), qs], q_ref[:, qs])

    def vt(h, off):
        return vt_ref[h * MLA_V:(h + 1) * MLA_V, pl.ds(off, tk)]

    def causal(s_t, off):
        key_chunk = (off + lax.broadcasted_iota(I32, (tk, 1), 0)) // CHUNK
        return jnp.where(key_chunk <= qry_chunk, s_t, NEG)

    _flash_sweep_t(n_tiles, tk, hp, qk, vt, lambda s_t, off: s_t, causal, m_scr, acc_scr, pipelined=False)
    for h in range(hp):
        o_ref[:, h * MLA_V:(h + 1) * MLA_V] = _attn_out_t(acc_scr[h], MLA_V)


def _dsa_kernel(q_ref, k_ref, vt_ref, qi_ref, ki_ref, wi_ref, o_ref,
                hi_scr, lo_scr, bias_scr, cnt_scr, m_scr, acc_scr, *, tq, tk, n_sel, seq):
    i = pl.program_id(1)
    n_valid = ((i + 1) * tq + tk - 1) // tk
    qry_chunk = (i * tq + lax.broadcasted_iota(I32, (1, tq), 1)) // CHUNK

    def key_ids(off):
        return off + lax.broadcasted_iota(I32, (tk, 1), 0)

    wi_t = wi_ref[...].T

    def score_tile(j, c):
        off = pl.multiple_of(j * tk, tk)
        kj = ki_ref[pl.ds(off, tk), :]
        sc = jnp.zeros((tk, tq), F32)
        for hh in range(IDX_HEADS):
            lg = _dot_nt(kj, qi_ref[:, hh * LANES:(hh + 1) * LANES])
            sc = sc + jnp.maximum(lg, 0.0) * wi_t[hh:hh + 1, :]
        sc = jnp.where(sc == 0.0, 0.0, sc)
        sc = jnp.where(key_ids(off) // CHUNK <= qry_chunk, sc, -jnp.inf)
        bits = pltpu.bitcast(sc, I32)
        key = bits ^ ((bits >> 31) & I32(0x7FFFFFFF))
        hi_scr[pl.ds(off, tk), :] = (key >> 16).astype(I16)
        lo_scr[pl.ds(off, tk), :] = ((key & I32(0xFFFF)) - HALF_BIAS).astype(I16)
        return c

    lax.fori_loop(0, n_valid, score_tile, 0)

    sub = cnt_scr.shape[0]

    def count(pred):
        cnt_scr[...] = jnp.zeros(cnt_scr.shape, I16)

        def body(j, c):
            off = pl.multiple_of(j * tk, tk)
            hit = jnp.where(pred(lambda: hi_scr[pl.ds(off, tk), :], lambda: lo_scr[pl.ds(off, tk), :], off),
                            I16(1), I16(0))
            part = hit[:sub]
            for cc in range(1, tk // sub):
                part = part + hit[cc * sub:(cc + 1) * sub]
            cnt_scr[...] += part
            return c

        lax.fori_loop(0, n_valid, body, 0)
        return jnp.sum(cnt_scr[...].astype(I32), axis=0, keepdims=True)

    def half_search(count_ge):
        def bit(it, tb):
            cand_b = tb | (I32(1) << (15 - it))
            cnt = count_ge((cand_b - HALF_BIAS).astype(I16))
            return jnp.where(cnt >= n_sel, cand_b, tb)
        return lax.fori_loop(0, 16, bit, jnp.zeros((1, tq), I32)) - HALF_BIAS

    thr_hi32 = half_search(lambda cand: count(lambda hi, lo, off: hi() >= cand))
    thr_hi = thr_hi32.astype(I16)
    c_gt_hi = count(lambda hi, lo, off: hi() > thr_hi)
    c_ge_hi = count(lambda hi, lo, off: hi() >= thr_hi)

    lo_max, lo_min = HALF_BIAS - 1, -HALF_BIAS

    def mask_lo_tile(j, c):
        off = pl.multiple_of(j * tk, tk)
        hi = hi_scr[pl.ds(off, tk), :]
        outside = jnp.where(hi > thr_hi, I16(lo_max), I16(lo_min))
        lo_scr[pl.ds(off, tk), :] = jnp.where(hi == thr_hi, lo_scr[pl.ds(off, tk), :], outside)
        return c

    lax.fori_loop(0, n_valid, mask_lo_tile, 0)
    thr_lo32 = half_search(lambda cand: count(lambda hi, lo, off: lo() >= cand))
    thr_lo = thr_lo32.astype(I16)

    cnt_gt = count(lambda hi, lo, off: lo() > thr_lo) + jnp.where(thr_lo32 == lo_max, c_gt_hi, 0)
    cnt_ge = jnp.where(thr_lo32 == lo_min, c_ge_hi, count(lambda hi, lo, off: lo() >= thr_lo))
    need = n_sel - cnt_gt
    neg_inf_key = 0x807FFFFF
    is_neg_inf = ((thr_hi32 == (neg_inf_key >> 16) - 2 * HALF_BIAS)
                  & (thr_lo32 == (neg_inf_key & 0xFFFF) - HALF_BIAS))
    tied = (cnt_ge > n_sel) & jnp.logical_not(is_neg_inf)
    any_tied = jnp.max(tied.astype(I32)) > 0
    idx_bits = (2 * seq - 1).bit_length()
    assert idx_bits < 16, "key indices must fit int16"

    def is_thr(hi, lo):
        return (hi() == thr_hi) & (lo() == thr_lo)

    def tie_cut():
        def cut_bit(it, jc):
            cand = jc | (I32(1) << (idx_bits - 1 - it))
            cand16 = cand.astype(I16)
            cnt = count(lambda hi, lo, off: is_thr(hi, lo) & (key_ids(off).astype(I16) < cand16))
            return jnp.where(cnt <= need, cand, jc)
        return lax.fori_loop(0, idx_bits, cut_bit, jnp.zeros((1, tq), I32))

    jcut = lax.cond(any_tied, tie_cut, lambda: jnp.full((1, tq), 2 ** idx_bits - 1, I32)).astype(I16)

    def write_bias(off, sel16, causal):
        sel = jnp.where(sel16, I16(1), I16(0)).astype(I32) > 0
        if causal:
            sel = sel & (key_ids(off) // CHUNK <= qry_chunk)
        bias_scr[pl.ds(off, tk), :] = jnp.where(sel, 0.0, NEG).astype(F32)

    def sel_plain(off):
        return lo_scr[pl.ds(off, tk), :] >= thr_lo

    def sel_general(off):
        hi = lambda: hi_scr[pl.ds(off, tk), :]
        lo = lambda: lo_scr[pl.ds(off, tk), :]
        gt = (hi() > thr_hi) | ((hi() == thr_hi) & (lo() > thr_lo))
        return gt | (is_thr(hi, lo) & (key_ids(off).astype(I16) < jcut))

    general = any_tied | (jnp.max((thr_lo32 == lo_min).astype(I32)) > 0)
    off_last = pl.multiple_of((n_valid - 1) * tk, tk)
    for pred, sel_fn in ((general, sel_general), (jnp.logical_not(general), sel_plain)):
        @pl.when(pred)
        def _(sel_fn=sel_fn):
            def bias_tile(j, c):
                off = pl.multiple_of(j * tk, tk)
                write_bias(off, sel_fn(off), causal=False)
                return c

            lax.fori_loop(0, n_valid - 1, bias_tile, 0)
            write_bias(off_last, sel_fn(off_last), causal=True)

    hp = m_scr.shape[0]

    def biased(s_t, off):
        return s_t + bias_scr[pl.ds(off, tk), :]

    for g in range(DSA_HEADS // hp):
        def head_cols(h, g=g):
            return slice((g * hp + h) * DSA_HEAD_DIM, (g * hp + h + 1) * DSA_HEAD_DIM)

        def qk(h, off):
            return _dot_nt(k_ref[pl.ds(off, tk), head_cols(h)], q_ref[:, head_cols(h)])

        def vt(h, off):
            return vt_ref[head_cols(h), pl.ds(off, tk)]

        _flash_sweep_t(n_valid, tk, hp, qk, vt, biased, biased, m_scr, acc_scr, pipelined=True)
        for h in range(hp):
            o_ref[:, head_cols(h)] = _attn_out_t(acc_scr[h], DSA_HEAD_DIM)


def _mem_kv_kernel(mem_ref, g_ref, wk_ref, wv_ref, k_ref, v_ref):
    mn = _rms(mem_ref[...], g_ref[...]).astype(BF16)
    k_ref[...] = _dot(mn, wk_ref[...]).astype(BF16)
    v_ref[...] = _dot(mn, wv_ref[...]).astype(BF16)


def _post_kernel(a_ref, b_ref, x_ref, woa_ref, wob_ref, gc_ref, wq_ref, km_ref, vm_ref, wo_ref, gm_ref,
                 x2_ref, hm_ref):
    x1 = x_ref[...] + _dot(a_ref[...], woa_ref[...]) + _dot(b_ref[...], wob_ref[...])
    hc = _rms(x1, gc_ref[...]).astype(BF16)
    qc = (_dot(hc, wq_ref[...]) * (X_HEAD_DIM ** -0.5)).astype(BF16)
    outs = []
    for hh in range(X_HEADS):
        sl = slice(hh * X_HEAD_DIM, (hh + 1) * X_HEAD_DIM)
        s = _dot_nt(qc[:, sl], km_ref[:, sl])
        p = jnp.exp(s - jnp.max(s, axis=-1, keepdims=True))
        o = _dot(p.astype(BF16), vm_ref[:, sl]) / jnp.sum(p, axis=-1, keepdims=True)
        outs.append(o.astype(BF16))
    x2 = x1 + _dot(jnp.concatenate(outs, axis=-1), wo_ref[...])
    x2_ref[...] = x2
    hm_ref[...] = _rms(x2, gm_ref[...]).astype(BF16)


def _mlp_kernel(hm_ref, x2_ref, wu_ref, wd_ref, gf_ref, y_ref):
    f = pl.program_id(1)

    @pl.when(f == 0)
    def _():
        y_ref[...] = x2_ref[...]

    u = jnp.maximum(_dot(hm_ref[...], wu_ref[...]), 0.0)
    y_ref[...] += _dot((u * u).astype(BF16), wd_ref[...])

    @pl.when(f == pl.num_programs(1) - 1)
    def _():
        y_ref[...] = _rms(y_ref[...], gf_ref[...])


def _const(shape):
    return pl.BlockSpec(shape, lambda *_: (0,) * len(shape), pipeline_mode=pl.Buffered(1))


def kernel(x, mem, positions, g_mix, w_in, g_cq, g_ckv, w_qb, w_kvb, w_out, g_cross, g_mem,
           w_q_cross, w_k_cross, w_v_cross, w_o_cross, g_mlp, w_up, w_down, g_final):
    assert w_in.shape[0] == 1, "one layer"
    batch, seq, d = x.shape
    t = batch * seq
    n_sel = min(TOPK_MAX, seq // 4)
    x2d = x.reshape(t, d)
    pos = positions.reshape(t, 1)
    row = lambda g: g.reshape(1, -1).astype(F32)

    w = w_in[0]
    o = [0]
    for n in (MLA_Q_LORA, MLA_KV_LORA, MLA_ROPE, DSA_HEADS * DSA_HEAD_DIM, DSA_HEADS * DSA_HEAD_DIM,
              DSA_HEADS * DSA_HEAD_DIM, IDX_HEADS * IDX_DIM, IDX_DIM, IDX_HEADS):
        o.append(o[-1] + n)
    zeros = lambda n: jnp.zeros((d, n), w.dtype)
    w_cq = w[:, o[0]:o[1]].astype(BF16)
    w_ckv = jnp.concatenate([w[:, o[1]:o[3]], zeros(LANES - MLA_ROPE)], axis=1).astype(BF16)
    w_dsa_qk = w[:, o[3]:o[5]].astype(BF16)
    w_dsa_vt = w[:, o[5]:o[6]].T.astype(BF16)
    w_idx = jnp.concatenate([w[:, o[6]:o[7]], w[:, o[7]:o[8]], w[:, o[7]:o[8]], w[:, o[8]:o[9]],
                             zeros(LANES - IDX_HEADS)], axis=1).astype(BF16)
    wqb = w_qb[0].reshape(MLA_Q_LORA, MLA_HEADS, MLA_NOPE + MLA_ROPE)
    wqb = jnp.pad(wqb, ((0, 0), (0, 0), (0, MLA_QK_PAD - MLA_NOPE - MLA_ROPE)))
    wqb = wqb.reshape(MLA_Q_LORA, MLA_HEADS * MLA_QK_PAD).astype(BF16)
    wkvb = w_kvb[0].reshape(MLA_KV_LORA, MLA_HEADS, MLA_NOPE + MLA_V)
    wkvb_k = wkvb[:, :, :MLA_NOPE].reshape(MLA_KV_LORA, -1).astype(BF16)
    wkvb_vt = wkvb[:, :, MLA_NOPE:].reshape(MLA_KV_LORA, -1).T.astype(BF16)
    mla_w = MLA_HEADS * MLA_V
    w_out_a = w_out[0, :mla_w].astype(BF16)
    w_out_b = w_out[0, mla_w:].astype(BF16)
    wq_c, wk_c, wv_c, wo_c = (a[0].astype(BF16) for a in (w_q_cross, w_k_cross, w_v_cross, w_o_cross))
    wu, wd = w_up[0].astype(BF16), w_down[0].astype(BF16)
    invf = jnp.zeros((LANES,), F32)
    for (half, _), off in zip(ROPE_VARIANTS, _rope_lane_plan()):
        inv_freq = ROPE_THETA ** (-jnp.arange(half, dtype=F32) / half)
        invf = invf.at[off:off + 2 * half].set(jnp.concatenate([inv_freq, inv_freq]))
    invf = invf.reshape(1, LANES)

    tm = 512
    nt = t // tm
    qk_w = MLA_HEADS * MLA_QK_PAD
    n_rope = 3 * LANES

    h_mix, rope = pl.pallas_call(
        _prep_kernel, name="prep", grid=(nt,),
        in_specs=[pl.BlockSpec((tm, d), lambda i: (i, 0)), pl.BlockSpec((tm, 1), lambda i: (i, 0)),
                  _const((1, d)), _const((1, LANES))],
        out_specs=[pl.BlockSpec((tm, d), lambda i: (i, 0)),
                   pl.BlockSpec((tm, len(ROPE_VARIANTS) * n_rope), lambda i: (i, 0))],
        out_shape=[jax.ShapeDtypeStruct((t, d), BF16),
                   jax.ShapeDtypeStruct((t, len(ROPE_VARIANTS) * n_rope), F32)],
        compiler_params=_params(("parallel",)),
    )(x2d, pos, row(g_mix), invf)
    dsa_w = DSA_HEADS * DSA_HEAD_DIM
    qi_w = IDX_HEADS * LANES
    tpj = 256
    rows = lambda w: pl.BlockSpec((tpj, w), lambda i: (i, 0))
    cols = lambda w: pl.BlockSpec((w, tpj), lambda i: (0, i))
    q_mla, k_mla, vt_mla, qk_d, vt_d, q_idx, k_idx, w_idx_s = pl.pallas_call(
        _proj_kernel, name="proj", grid=(t // tpj,),
        in_specs=[rows(d), rows(len(ROPE_VARIANTS) * n_rope),
                  _const((d, MLA_Q_LORA)), _const((1, MLA_Q_LORA)), _const((MLA_Q_LORA, qk_w)),
                  _const((d, MLA_KV_LORA + LANES)), _const((1, MLA_KV_LORA)), _const((MLA_KV_LORA, mla_w)),
                  _const((mla_w, MLA_KV_LORA)),
                  _const((d, 2 * dsa_w)), _const((dsa_w, d)), _const((d, IDX_HEADS * IDX_DIM + 2 * LANES))],
        out_specs=[rows(qk_w), rows(qk_w), cols(mla_w),
                   pl.BlockSpec((2, tpj, dsa_w), lambda i: (0, i, 0)), cols(dsa_w),
                   rows(qi_w), rows(LANES), rows(LANES)],
        out_shape=[jax.ShapeDtypeStruct((t, qk_w), BF16), jax.ShapeDtypeStruct((t, qk_w), BF16),
                   jax.ShapeDtypeStruct((mla_w, t), BF16),
                   jax.ShapeDtypeStruct((2, t, dsa_w), BF16), jax.ShapeDtypeStruct((dsa_w, t), BF16),
                   jax.ShapeDtypeStruct((t, qi_w), BF16), jax.ShapeDtypeStruct((t, LANES), BF16),
                   jax.ShapeDtypeStruct((t, LANES), F32)],
        compiler_params=_params(("parallel",)),
    )(h_mix, rope, w_cq, row(g_cq), wqb, w_ckv, row(g_ckv), wkvb_k, wkvb_vt, w_dsa_qk, w_dsa_vt, w_idx)

    tq, tk, hp = 256, 512, 8
    a_mla = pl.pallas_call(
        functools.partial(_mla_attn_kernel, tq=tq, tk=tk, hp=hp), name="mla_attn",
        grid=(batch, MLA_HEADS // hp, seq // tq),
        in_specs=[pl.BlockSpec((None, tq, hp * MLA_QK_PAD), lambda b, h, i: (b, i, h)),
                  pl.BlockSpec((None, seq, hp * MLA_QK_PAD), lambda b, h, i: (b, 0, h),
                               pipeline_mode=pl.Buffered(1)),
                  pl.BlockSpec((hp * MLA_V, seq), lambda b, h, i: (h, b), pipeline_mode=pl.Buffered(1))],
        out_specs=pl.BlockSpec((None, tq, hp * MLA_V), lambda b, h, i: (b, i, h)),
        out_shape=jax.ShapeDtypeStruct((batch, seq, mla_w), BF16),
        scratch_shapes=[pltpu.VMEM((hp, 1, tq), F32), pltpu.VMEM((hp, MLA_V + SUM_ROWS, tq), F32)],
        compiler_params=_params(("parallel", "parallel", "arbitrary")),
    )(q_mla.reshape(batch, seq, qk_w), k_mla.reshape(batch, seq, qk_w), vt_mla)

    qk4 = qk_d.reshape(2, batch, seq, dsa_w)
    tq, tk, hp = 256, 512, 8
    b_dsa = pl.pallas_call(
        functools.partial(_dsa_kernel, tq=tq, tk=tk, n_sel=n_sel, seq=seq), name="dsa",
        grid=(batch, seq // tq),
        in_specs=[pl.BlockSpec((None, None, tq, dsa_w), lambda b, i: (0, b, i, 0)),
                  pl.BlockSpec((None, None, seq, dsa_w), lambda b, i: (1, b, 0, 0), pipeline_mode=pl.Buffered(1)),
                  pl.BlockSpec((dsa_w, seq), lambda b, i: (0, b), pipeline_mode=pl.Buffered(1)),
                  pl.BlockSpec((None, tq, qi_w), lambda b, i: (b, i, 0)),
                  pl.BlockSpec((None, seq, LANES), lambda b, i: (b, 0, 0)),
                  pl.BlockSpec((None, tq, LANES), lambda b, i: (b, i, 0))],
        out_specs=pl.BlockSpec((None, tq, dsa_w), lambda b, i: (b, i, 0)),
        out_shape=jax.ShapeDtypeStruct((batch, seq, dsa_w), BF16),
        scratch_shapes=[pltpu.VMEM((seq, tq), I16), pltpu.VMEM((seq, tq), I16), pltpu.VMEM((seq, tq), F32),
                        pltpu.VMEM((32, tq), I16),
                        pltpu.VMEM((hp, 1, tq), F32), pltpu.VMEM((hp, DSA_HEAD_DIM + SUM_ROWS, tq), F32)],
        compiler_params=_params(("parallel", "arbitrary")),
    )(qk4, qk4, vt_d, q_idx.reshape(batch, seq, qi_w), k_idx.reshape(batch, seq, LANES),
      w_idx_s.reshape(batch, seq, LANES))

    xw = X_HEADS * X_HEAD_DIM
    n_mem = mem.shape[1]
    k_mem, v_mem = pl.pallas_call(
        _mem_kv_kernel, name="mem_kv", grid=(batch,),
        in_specs=[pl.BlockSpec((None, n_mem, d), lambda b: (b, 0, 0)), _const((1, d)), _const((d, xw)),
                  _const((d, xw))],
        out_specs=[pl.BlockSpec((None, n_mem, xw), lambda b: (b, 0, 0))] * 2,
        out_shape=[jax.ShapeDtypeStruct((batch, n_mem, xw), BF16)] * 2,
        compiler_params=_params(("parallel",)),
    )(mem, row(g_mem), wk_c, wv_c)

    tp = 512
    per_b = seq // tp
    mem_spec = pl.BlockSpec((None, n_mem, xw), lambda i: (i // per_b, 0, 0))
    x2, hm = pl.pallas_call(
        _post_kernel, name="post", grid=(t // tp,),
        in_specs=[pl.BlockSpec((tp, mla_w), lambda i: (i, 0)), pl.BlockSpec((tp, dsa_w), lambda i: (i, 0)),
                  pl.BlockSpec((tp, d), lambda i: (i, 0)), _const((mla_w, d)), _const((dsa_w, d)), _const((1, d)),
                  _const((d, xw)), mem_spec, mem_spec, _const((xw, d)), _const((1, d))],
        out_specs=[pl.BlockSpec((tp, d), lambda i: (i, 0))] * 2,
        out_shape=[jax.ShapeDtypeStruct((t, d), F32), jax.ShapeDtypeStruct((t, d), BF16)],
        compiler_params=_params(("parallel",)),
    )(a_mla.reshape(t, mla_w), b_dsa.reshape(t, dsa_w), x2d, w_out_a, w_out_b, row(g_cross), wq_c,
      k_mem, v_mem, wo_c, row(g_mlp))

    tmm, tf = 512, 1024
    y = pl.pallas_call(
        _mlp_kernel, name="mlp", grid=(t // tmm, D_FF // tf),
        in_specs=[pl.BlockSpec((tmm, d), lambda i, f: (i, 0)), pl.BlockSpec((tmm, d), lambda i, f: (i, 0)),
                  pl.BlockSpec((d, tf), lambda i, f: (0, f)), pl.BlockSpec((tf, d), lambda i, f: (f, 0)),
                  _const((1, d))],
        out_specs=pl.BlockSpec((tmm, d), lambda i, f: (i, 0)),
        out_shape=jax.ShapeDtypeStruct((t, d), F32),
        compiler_params=_params(("parallel", "arbitrary")),
    )(hm, x2, wu, wd, row(g_final))
    return y.reshape(batch, seq, d)
```

```python
import functools

import jax
import jax.numpy as jnp
from jax import lax
from jax.experimental import pallas as pl
from jax.experimental.pallas import tpu as pltpu

F32 = jnp.float32
BF16 = jnp.bfloat16
I32 = jnp.int32

D_MODEL = 2048
CHUNK = 64
ROPE_THETA = 500000.0
N_MEM = 256
EPS = 1e-6
MLA_HEADS = 8
MLA_NOPE = 128
MLA_ROPE = 64
MLA_V = 128
MLA_Q_LORA = 512
MLA_KV_LORA = 256
DSA_HEADS = 8
DSA_HEAD_DIM = 128
DSA_ROT = DSA_HEAD_DIM // 4
IDX_HEADS = 16
IDX_DIM = 64
IDX_ROT = IDX_DIM // 4
TOPK_MAX = 256
X_HEADS = 4
X_HEAD_DIM = 128
D_FF = 4 * D_MODEL

LANES = 128
MLA_QK_PAD = 256
NEG = -1e30
INT_MIN = -2 ** 31
LOG2E = 1.4426950408889634
VMEM_LIMIT = 56 * 1024 * 1024


def _params(sem, vmem=VMEM_LIMIT):
    return pltpu.CompilerParams(dimension_semantics=sem, vmem_limit_bytes=vmem)


def _rms(xf, g):
    return xf * lax.rsqrt(jnp.mean(xf * xf, axis=-1, keepdims=True) + EPS) * g


def _apply_rope(x, coeffs, half):
    c, s_lo, s_hi = coeffs
    return x * c + pltpu.roll(x, LANES - half, 1) * s_lo + pltpu.roll(x, half, 1) * s_hi


def _dot(a, b):
    return jnp.dot(a, b, preferred_element_type=F32)


def _dot_nt(a, b):
    return lax.dot_general(a, b, (((1,), (1,)), ((), ())), preferred_element_type=F32)


ROPE_VARIANTS = ((MLA_ROPE // 2, LANES), (DSA_ROT // 2, LANES), (IDX_ROT // 2, IDX_DIM))


def _rope_lane_plan():
    offs, o = [], 0
    for half, _ in ROPE_VARIANTS:
        offs.append(o)
        o += 2 * half
    assert o <= LANES
    return offs


def _rope_coeffs(pos, invf):
    ang = pos.astype(F32) * invf
    cos, sin = jnp.cos(ang), jnp.sin(ang)
    lane = lax.broadcasted_iota(I32, (1, LANES), 1)
    coeffs = []
    for (half, period), off in zip(ROPE_VARIANTS, _rope_lane_plan()):
        c = jnp.ones(cos.shape, F32)
        s_lo = jnp.zeros(cos.shape, F32)
        s_hi = jnp.zeros(cos.shape, F32)
        for base in range(0, LANES, period):
            shift = (base - off) % LANES
            cs = cos if shift == 0 else pltpu.roll(cos, shift, 1)
            sn = sin if shift == 0 else pltpu.roll(sin, shift, 1)
            in_lo = (lane >= base) & (lane < base + half)
            in_hi = (lane >= base + half) & (lane < base + 2 * half)
            c = jnp.where(in_lo | in_hi, cs, c)
            s_lo = jnp.where(in_lo, -sn, s_lo)
            s_hi = jnp.where(in_hi, sn, s_hi)
        coeffs.append((c, s_lo, s_hi))
    return coeffs


def _proj_kernel(x_ref, pos_ref, g_ref, invf_ref, wcq_ref, gcq_ref, wqb_ref, wckv_ref, gckv_ref, wk_ref,
                 wvt_ref, wdsa_ref, wdsa_vt_ref, widx_ref,
                 q_ref, k_ref, vt_ref, qk_d_ref, vt_d_ref, qi_ref, ki_ref, wi_ref):
    h = _rms(x_ref[...], g_ref[...]).astype(BF16)
    rope = _rope_coeffs(pos_ref[...], invf_ref[...])
    lane = lax.broadcasted_iota(I32, (1, LANES), 1)

    cq = _dot(h, wcq_ref[...])
    r = _dot(h, wckv_ref[...])
    q = _dot(_rms(cq, gcq_ref[...]).astype(BF16), wqb_ref[...])
    ckv = _rms(r[:, :MLA_KV_LORA], gckv_ref[...]).astype(BF16)
    kn = _dot(ckv, wk_ref[...])
    vt_ref[...] = _dot_nt(wvt_ref[...], ckv).astype(BF16)
    scale = (MLA_NOPE + MLA_ROPE) ** -0.5 * LOG2E
    kr = _apply_rope(r[:, MLA_KV_LORA:], rope[0], MLA_ROPE // 2).astype(BF16)
    for hh in range(MLA_HEADS):
        b0 = hh * MLA_QK_PAD
        q_ref[:, b0:b0 + LANES] = (q[:, b0:b0 + LANES] * scale).astype(BF16)
        qr = _apply_rope(q[:, b0 + LANES:b0 + 2 * LANES], rope[0], MLA_ROPE // 2)
        q_ref[:, b0 + LANES:b0 + 2 * LANES] = (qr * scale).astype(BF16)
        k_ref[:, b0:b0 + LANES] = kn[:, hh * LANES:(hh + 1) * LANES].astype(BF16)
        k_ref[:, b0 + LANES:b0 + 2 * LANES] = kr

    dsa_w = DSA_HEADS * DSA_HEAD_DIM
    for g, scale in enumerate((DSA_HEAD_DIM ** -0.5 * LOG2E, 1.0)):
        rd = _dot(h, wdsa_ref[:, g * dsa_w:(g + 1) * dsa_w])
        for hh in range(DSA_HEADS):
            sl = slice(hh * LANES, (hh + 1) * LANES)
            qk_d_ref[g, :, sl] = (_apply_rope(rd[:, sl], rope[1], DSA_ROT // 2) * scale).astype(BF16)
    vt_d_ref[...] = _dot_nt(wdsa_vt_ref[...], h).astype(BF16)

    ri = _dot(h, widx_ref[...])
    nq = IDX_HEADS * IDX_DIM
    for p in range(IDX_HEADS // 2):
        t = _apply_rope(ri[:, p * LANES:(p + 1) * LANES], rope[2], IDX_ROT // 2) * (IDX_DIM ** -0.5)
        qi_ref[:, (2 * p) * LANES:(2 * p + 1) * LANES] = jnp.where(lane < IDX_DIM, t, 0.0).astype(BF16)
        qi_ref[:, (2 * p + 1) * LANES:(2 * p + 2) * LANES] = jnp.where(lane >= IDX_DIM, t, 0.0).astype(BF16)
    ki_ref[...] = _apply_rope(ri[:, nq:nq + LANES], rope[2], IDX_ROT // 2).astype(BF16)
    wi_ref[...] = ri[:, nq + LANES:nq + 2 * LANES] * (IDX_HEADS ** -0.5)


SUM_ROWS = 16


def _flash_sweep_t(n_tiles, tk, heads, qk, vt, adjust, adjust_last, m_scr, acc_scr, pipelined):
    m_scr[...] = jnp.full(m_scr.shape, NEG, F32)
    acc_scr[...] = jnp.zeros(acc_scr.shape, F32)
    ones = jnp.ones((SUM_ROWS, tk), BF16)

    def scores(h, off, adj):
        s_t = adj(qk(h, off), off)
        return s_t, jnp.max(s_t, axis=0, keepdims=True)

    def finish(h, s_and_max, off):
        s_t, m_tile = s_and_max
        m_prev = m_scr[h]
        m_new = jnp.maximum(m_prev, m_tile)
        p = jnp.exp2(s_t - m_new).astype(BF16)
        pv = _dot(jnp.concatenate([vt(h, off), ones], axis=0), p)
        acc_scr[h] = jnp.exp2(m_prev - m_new) * acc_scr[h] + pv
        m_scr[h] = m_new

    off_last = pl.multiple_of((n_tiles - 1) * tk, tk)
    if pipelined:
        assert adjust is adjust_last, "scores are adjusted one tile ahead"

        def body(j, s_cur):
            off = pl.multiple_of(j * tk, tk)
            off_n = pl.multiple_of((j + 1) * tk, tk)
            s_new = [scores(h, off_n, adjust) for h in range(min(2, heads))]
            for h in range(heads):
                finish(h, s_cur[h], off)
                if h + 2 < heads:
                    s_new.append(scores(h + 2, off_n, adjust))
            return tuple(s_new)

        s_last = lax.fori_loop(0, n_tiles - 1, body, tuple(scores(h, 0, adjust) for h in range(heads)))
        for h in range(heads):
            finish(h, s_last[h], off_last)
    else:
        def tile(off, adj):
            s_ts = [scores(h, off, adj) for h in range(heads)]
            for h in range(heads):
                finish(h, s_ts[h], off)

        def body(j, c):
            tile(pl.multiple_of(j * tk, tk), adjust)
            return c

        lax.fori_loop(0, n_tiles - 1, body, 0)
        tile(off_last, adjust_last)


def _attn_out_t(acc, dv):
    return (acc[:dv] / acc[dv:dv + 1]).T.astype(BF16)


def _mla_attn_kernel(q_ref, k_ref, vt_ref, wu_ref, wd_ref, o_ref, wu_bf_ref, wd_bf_ref, m_scr, acc_scr,
                     *, tq, tk, hp):
    wu_bf_ref[...] = wu_ref[...].astype(BF16)
    wd_bf_ref[...] = wd_ref[...].astype(BF16)
    i = pl.program_id(2)
    n_tiles = (i * tq) // tk + 1
    qry_chunk = (i * tq + lax.broadcasted_iota(I32, (1, tq), 1)) // CHUNK

    def qk(h, off):
        qs = slice(h * MLA_QK_PAD, (h + 1) * MLA_QK_PAD)
        return _dot_nt(k_ref[pl.ds(off, tk), qs], q_ref[:, qs])

    def vt(h, off):
        return vt_ref[h * MLA_V:(h + 1) * MLA_V, pl.ds(off, tk)]

    def causal(s_t, off):
        key_chunk = (off + lax.broadcasted_iota(I32, (tk, 1), 0)) // CHUNK
        return jnp.where(key_chunk <= qry_chunk, s_t, NEG)

    _flash_sweep_t(n_tiles, tk, hp, qk, vt, lambda s_t, off: s_t, causal, m_scr, acc_scr, pipelined=False)
    for h in range(hp):
        o_ref[:, h * MLA_V:(h + 1) * MLA_V] = _attn_out_t(acc_scr[h], MLA_V)


def _dsa_kernel(q_ref, k_ref, vt_ref, qi_ref, ki_ref, wi_ref, o_ref,
                sc_scr, bias_scr, cnt_scr, m_scr, acc_scr, *, tq, tk, n_sel, seq):
    i = pl.program_id(1)
    n_valid = ((i + 1) * tq + tk - 1) // tk
    qry_chunk = (i * tq + lax.broadcasted_iota(I32, (1, tq), 1)) // CHUNK

    def key_ids(off):
        return off + lax.broadcasted_iota(I32, (tk, 1), 0)

    wi_t = wi_ref[...].T

    def score_tile(j, c):
        off = pl.multiple_of(j * tk, tk)
        kj = ki_ref[pl.ds(off, tk), :]
        sc = jnp.zeros((tk, tq), F32)
        for hh in range(IDX_HEADS):
            lg = _dot_nt(kj, qi_ref[:, hh * LANES:(hh + 1) * LANES])
            sc = sc + jnp.maximum(lg, 0.0) * wi_t[hh:hh + 1, :]
        sc_scr[pl.ds(off, tk), :] = jnp.where(key_ids(off) // CHUNK <= qry_chunk, sc, -jnp.inf)
        return c

    lax.fori_loop(0, n_valid, score_tile, 0)

    sub = cnt_scr.shape[0]

    def count(pred):
        cnt_scr[...] = jnp.zeros(cnt_scr.shape, I32)

        def body(j, c):
            off = pl.multiple_of(j * tk, tk)
            hit = jnp.where(pred(sc_scr[pl.ds(off, tk), :], off), 1, 0).astype(I32)
            part = hit[:sub]
            for cc in range(1, tk // sub):
                part = part + hit[cc * sub:(cc + 1) * sub]
            cnt_scr[...] += part
            return c

        lax.fori_loop(0, n_valid, body, 0)
        return jnp.sum(cnt_scr[...], axis=0, keepdims=True)

    def key_to_float(key):
        return pltpu.bitcast(key ^ ((key >> 31) & I32(0x7FFFFFFF)), F32)

    def thr_bit(it, tb):
        cand_b = tb | (I32(1) << (31 - it))
        cand = key_to_float(cand_b ^ I32(INT_MIN))
        cnt = count(lambda sc, off: sc >= cand)
        return jnp.where(cnt >= n_sel, cand_b, tb)

    thr_key = lax.fori_loop(0, 32, thr_bit, jnp.zeros((1, tq), I32)) ^ I32(INT_MIN)
    neg_inf_key = I32(0x807FFFFF - 2 ** 32)
    thr = key_to_float(jnp.maximum(thr_key, neg_inf_key))

    cnt_gt = count(lambda sc, off: sc > thr)
    cnt_ge = count(lambda sc, off: sc >= thr)
    need = n_sel - cnt_gt
    tied = (cnt_ge > n_sel) & (thr > -jnp.inf)
    any_tied = jnp.max(tied.astype(I32)) > 0
    idx_bits = (2 * seq - 1).bit_length()

    def tie_cut():
        def cut_bit(it, jc):
            cand = jc | (I32(1) << (idx_bits - 1 - it))
            cnt = count(lambda sc, off: (sc == thr) & (key_ids(off) < cand))
            return jnp.where(cnt <= need, cand, jc)
        return lax.fori_loop(0, idx_bits, cut_bit, jnp.zeros((1, tq), I32))

    jcut = lax.cond(any_tied, tie_cut, lambda: jnp.full((1, tq), 2 ** idx_bits - 1, I32))

    def write_bias(off, sel, causal):
        if causal:
            sel = sel & (key_ids(off) // CHUNK <= qry_chunk)
        bias_scr[pl.ds(off, tk), :] = jnp.where(sel, 0.0, NEG).astype(F32)

    def sel_plain(off):
        return sc_scr[pl.ds(off, tk), :] >= thr

    def sel_general(off):
        sc = sc_scr[pl.ds(off, tk), :]
        return (sc > thr) | ((sc == thr) & (key_ids(off) < jcut))

    off_last = pl.multiple_of((n_valid - 1) * tk, tk)
    for pred, sel_fn in ((any_tied, sel_general), (jnp.logical_not(any_tied), sel_plain)):
        @pl.when(pred)
        def _(sel_fn=sel_fn):
            def bias_tile(j, c):
                off = pl.multiple_of(j * tk, tk)
                write_bias(off, sel_fn(off), causal=False)
                return c

            lax.fori_loop(0, n_valid - 1, bias_tile, 0)
            write_bias(off_last, sel_fn(off_last), causal=True)

    hp = m_scr.shape[0]

    def biased(s_t, off):
        return s_t + bias_scr[pl.ds(off, tk), :]

    for g in range(DSA_HEADS // hp):
        def head_cols(h, g=g):
            return slice((g * hp + h) * DSA_HEAD_DIM, (g * hp + h + 1) * DSA_HEAD_DIM)

        def qk(h, off):
            return _dot_nt(k_ref[pl.ds(off, tk), head_cols(h)], q_ref[:, head_cols(h)])

        def vt(h, off):
            return vt_ref[head_cols(h), pl.ds(off, tk)]

        _flash_sweep_t(n_valid, tk, hp, qk, vt, biased, biased, m_scr, acc_scr, pipelined=True)
        for h in range(hp):
            o_ref[:, head_cols(h)] = _attn_out_t(acc_scr[h], DSA_HEAD_DIM)


def _mem_kv_kernel(mem_ref, g_ref, wk_ref, wv_ref, k_ref, v_ref):
    mn = _rms(mem_ref[...], g_ref[...]).astype(BF16)
    k_ref[...] = _dot(mn, wk_ref[...]).astype(BF16)
    v_ref[...] = _dot(mn, wv_ref[...]).astype(BF16)


def _post_kernel(a_ref, b_ref, x_ref, woa_ref, wob_ref, gc_ref, wq_ref, km_ref, vm_ref, wo_ref, gm_ref,
                 x2_ref, hm_ref):
    x1 = x_ref[...] + _dot(a_ref[...], woa_ref[...]) + _dot(b_ref[...], wob_ref[...])
    hc = _rms(x1, gc_ref[...]).astype(BF16)
    qc = (_dot(hc, wq_ref[...]) * (X_HEAD_DIM ** -0.5)).astype(BF16)
    outs = []
    for hh in range(X_HEADS):
        sl = slice(hh * X_HEAD_DIM, (hh + 1) * X_HEAD_DIM)
        s = _dot_nt(qc[:, sl], km_ref[:, sl])
        p = jnp.exp(s - jnp.max(s, axis=-1, keepdims=True))
        o = _dot(p.astype(BF16), vm_ref[:, sl]) / jnp.sum(p, axis=-1, keepdims=True)
        outs.append(o.astype(BF16))
    x2 = x1 + _dot(jnp.concatenate(outs, axis=-1), wo_ref[...])
    x2_ref[...] = x2
    hm_ref[...] = _rms(x2, gm_ref[...]).astype(BF16)


def _mlp_kernel(hm_ref, x2_ref, wu_ref, wd_ref, gf_ref, y_ref):
    f = pl.program_id(1)

    @pl.when(f == 0)
    def _():
        y_ref[...] = x2_ref[...]

    u = jnp.maximum(_dot(hm_ref[...], wu_ref[...]), 0.0)
    y_ref[...] += _dot((u * u).astype(BF16), wd_ref[...])

    @pl.when(f == pl.num_programs(1) - 1)
    def _():
        y_ref[...] = _rms(y_ref[...], gf_ref[...])


def _const(shape):
    return pl.BlockSpec(shape, lambda *_: (0,) * len(shape), pipeline_mode=pl.Buffered(1))


def kernel(x, mem, positions, g_mix, w_in, g_cq, g_ckv, w_qb, w_kvb, w_out, g_cross, g_mem,
           w_q_cross, w_k_cross, w_v_cross, w_o_cross, g_mlp, w_up, w_down, g_final):
    assert w_in.shape[0] == 1, "one layer"
    batch, seq, d = x.shape
    t = batch * seq
    n_sel = min(TOPK_MAX, seq // 4)
    x2d = x.reshape(t, d)
    pos = positions.reshape(t, 1)
    row = lambda g: g.reshape(1, -1).astype(F32)

    w = w_in[0]
    o = [0]
    for n in (MLA_Q_LORA, MLA_KV_LORA, MLA_ROPE, DSA_HEADS * DSA_HEAD_DIM, DSA_HEADS * DSA_HEAD_DIM,
              DSA_HEADS * DSA_HEAD_DIM, IDX_HEADS * IDX_DIM, IDX_DIM, IDX_HEADS):
        o.append(o[-1] + n)
    zeros = lambda n: jnp.zeros((d, n), w.dtype)
    w_cq = w[:, o[0]:o[1]].astype(BF16)
    w_ckv = jnp.concatenate([w[:, o[1]:o[3]], zeros(LANES - MLA_ROPE)], axis=1).astype(BF16)
    w_dsa_qk = w[:, o[3]:o[5]].astype(BF16)
    w_dsa_vt = w[:, o[5]:o[6]].T.astype(BF16)
    w_idx = jnp.concatenate([w[:, o[6]:o[7]], w[:, o[7]:o[8]], w[:, o[7]:o[8]], w[:, o[8]:o[9]],
                             zeros(LANES - IDX_HEADS)], axis=1).astype(BF16)
    wqb = w_qb[0].reshape(MLA_Q_LORA, MLA_HEADS, MLA_NOPE + MLA_ROPE)
    wqb = jnp.pad(wqb, ((0, 0), (0, 0), (0, MLA_QK_PAD - MLA_NOPE - MLA_ROPE)))
    wqb = wqb.reshape(MLA_Q_LORA, MLA_HEADS * MLA_QK_PAD).astype(BF16)
    wkvb = w_kvb[0].reshape(MLA_KV_LORA, MLA_HEADS, MLA_NOPE + MLA_V)
    wkvb_k = wkvb[:, :, :MLA_NOPE].reshape(MLA_KV_LORA, -1).astype(BF16)
    wkvb_vt = wkvb[:, :, MLA_NOPE:].reshape(MLA_KV_LORA, -1).T.astype(BF16)
    mla_w = MLA_HEADS * MLA_V
    w_out_a = w_out[0, :mla_w].astype(BF16)
    w_out_b = w_out[0, mla_w:].astype(BF16)
    wq_c, wk_c, wv_c, wo_c = (a[0].astype(BF16) for a in (w_q_cross, w_k_cross, w_v_cross, w_o_cross))
    invf = jnp.zeros((LANES,), F32)
    for (half, _), off in zip(ROPE_VARIANTS, _rope_lane_plan()):
        inv_freq = ROPE_THETA ** (-jnp.arange(half, dtype=F32) / half)
        invf = invf.at[off:off + 2 * half].set(jnp.concatenate([inv_freq, inv_freq]))
    invf = invf.reshape(1, LANES)

    qk_w = MLA_HEADS * MLA_QK_PAD
    dsa_w = DSA_HEADS * DSA_HEAD_DIM
    qi_w = IDX_HEADS * LANES
    tpj = 256
    rows = lambda w: pl.BlockSpec((tpj, w), lambda i: (i, 0))
    cols = lambda w: pl.BlockSpec((w, tpj), lambda i: (0, i))
    q_mla, k_mla, vt_mla, qk_d, vt_d, q_idx, k_idx, w_idx_s = pl.pallas_call(
        _proj_kernel, name="proj", grid=(t // tpj,),
        in_specs=[rows(d), rows(1), _const((1, d)), _const((1, LANES)),
                  _const((d, MLA_Q_LORA)), _const((1, MLA_Q_LORA)), _const((MLA_Q_LORA, qk_w)),
                  _const((d, MLA_KV_LORA + LANES)), _const((1, MLA_KV_LORA)), _const((MLA_KV_LORA, mla_w)),
                  _const((mla_w, MLA_KV_LORA)),
                  _const((d, 2 * dsa_w)), _const((dsa_w, d)), _const((d, IDX_HEADS * IDX_DIM + 2 * LANES))],
        out_specs=[rows(qk_w), rows(qk_w), cols(mla_w),
                   pl.BlockSpec((2, tpj, dsa_w), lambda i: (0, i, 0)), cols(dsa_w),
                   rows(qi_w), rows(LANES), rows(LANES)],
        out_shape=[jax.ShapeDtypeStruct((t, qk_w), BF16), jax.ShapeDtypeStruct((t, qk_w), BF16),
                   jax.ShapeDtypeStruct((mla_w, t), BF16),
                   jax.ShapeDtypeStruct((2, t, dsa_w), BF16), jax.ShapeDtypeStruct((dsa_w, t), BF16),
                   jax.ShapeDtypeStruct((t, qi_w), BF16), jax.ShapeDtypeStruct((t, LANES), BF16),
                   jax.ShapeDtypeStruct((t, LANES), F32)],
        compiler_params=_params(("parallel",)),
    )(x2d, pos, row(g_mix), invf, w_cq, row(g_cq), wqb, w_ckv, row(g_ckv), wkvb_k, wkvb_vt, w_dsa_qk, w_dsa_vt,
      w_idx)

    tq, tk, hp = 256, 512, 8
    mla_grid = (batch, MLA_HEADS // hp, seq // tq)
    n_steps = mla_grid[0] * mla_grid[1] * mla_grid[2]
    step = lambda b, h, i: (b * mla_grid[1] + h) * mla_grid[2] + i
    wu_rows, wd_rows = d // n_steps, D_FF // n_steps
    a_mla, wu, wd = pl.pallas_call(
        functools.partial(_mla_attn_kernel, tq=tq, tk=tk, hp=hp), name="mla_attn", grid=mla_grid,
        in_specs=[pl.BlockSpec((None, tq, hp * MLA_QK_PAD), lambda b, h, i: (b, i, h)),
                  pl.BlockSpec((None, seq, hp * MLA_QK_PAD), lambda b, h, i: (b, 0, h),
                               pipeline_mode=pl.Buffered(1)),
                  pl.BlockSpec((hp * MLA_V, seq), lambda b, h, i: (h, b), pipeline_mode=pl.Buffered(1)),
                  pl.BlockSpec((None, wu_rows, D_FF), lambda b, h, i: (0, step(b, h, i), 0)),
                  pl.BlockSpec((None, wd_rows, d), lambda b, h, i: (0, step(b, h, i), 0))],
        out_specs=[pl.BlockSpec((None, tq, hp * MLA_V), lambda b, h, i: (b, i, h)),
                   pl.BlockSpec((wu_rows, D_FF), lambda b, h, i: (step(b, h, i), 0)),
                   pl.BlockSpec((wd_rows, d), lambda b, h, i: (step(b, h, i), 0))],
        out_shape=[jax.ShapeDtypeStruct((batch, seq, mla_w), BF16),
                   jax.ShapeDtypeStruct((d, D_FF), BF16), jax.ShapeDtypeStruct((D_FF, d), BF16)],
        scratch_shapes=[pltpu.VMEM((hp, 1, tq), F32), pltpu.VMEM((hp, MLA_V + SUM_ROWS, tq), F32)],
        compiler_params=_params(("parallel", "parallel", "arbitrary")),
    )(q_mla.reshape(batch, seq, qk_w), k_mla.reshape(batch, seq, qk_w), vt_mla, w_up, w_down)

    qk4 = qk_d.reshape(2, batch, seq, dsa_w)
    tq, tk, hp = 256, 512, 8
    b_dsa = pl.pallas_call(
        functools.partial(_dsa_kernel, tq=tq, tk=tk, n_sel=n_sel, seq=seq), name="dsa",
        grid=(batch, seq // tq),
        in_specs=[pl.BlockSpec((None, None, tq, dsa_w), lambda b, i: (0, b, i, 0)),
                  pl.BlockSpec((None, None, seq, dsa_w), lambda b, i: (1, b, 0, 0), pipeline_mode=pl.Buffered(1)),
                  pl.BlockSpec((dsa_w, seq), lambda b, i: (0, b), pipeline_mode=pl.Buffered(1)),
                  pl.BlockSpec((None, tq, qi_w), lambda b, i: (b, i, 0)),
                  pl.BlockSpec((None, seq, LANES), lambda b, i: (b, 0, 0)),
                  pl.BlockSpec((None, tq, LANES), lambda b, i: (b, i, 0))],
        out_specs=pl.BlockSpec((None, tq, dsa_w), lambda b, i: (b, i, 0)),
        out_shape=jax.ShapeDtypeStruct((batch, seq, dsa_w), BF16),
        scratch_shapes=[pltpu.VMEM((seq, tq), F32), pltpu.VMEM((seq, tq), F32), pltpu.VMEM((32, tq), I32),
                        pltpu.VMEM((hp, 1, tq), F32), pltpu.VMEM((hp, DSA_HEAD_DIM + SUM_ROWS, tq), F32)],
        compiler_params=_params(("parallel", "arbitrary")),
    )(qk4, qk4, vt_d, q_idx.reshape(batch, seq, qi_w), k_idx.reshape(batch, seq, LANES),
      w_idx_s.reshape(batch, seq, LANES))

    xw = X_HEADS * X_HEAD_DIM
    n_mem = mem.shape[1]
    k_mem, v_mem = pl.pallas_call(
        _mem_kv_kernel, name="mem_kv", grid=(batch,),
        in_specs=[pl.BlockSpec((None, n_mem, d), lambda b: (b, 0, 0)), _const((1, d)), _const((d, xw)),
                  _const((d, xw))],
        out_specs=[pl.BlockSpec((None, n_mem, xw), lambda b: (b, 0, 0))] * 2,
        out_shape=[jax.ShapeDtypeStruct((batch, n_mem, xw), BF16)] * 2,
        compiler_params=_params(("parallel",)),
    )(mem, row(g_mem), wk_c, wv_c)

    tp = 512
    per_b = seq // tp
    mem_spec = pl.BlockSpec((None, n_mem, xw), lambda i: (i // per_b, 0, 0))
    x2, hm = pl.pallas_call(
        _post_kernel, name="post", grid=(t // tp,),
        in_specs=[pl.BlockSpec((tp, mla_w), lambda i: (i, 0)), pl.BlockSpec((tp, dsa_w), lambda i: (i, 0)),
                  pl.BlockSpec((tp, d), lambda i: (i, 0)), _const((mla_w, d)), _const((dsa_w, d)), _const((1, d)),
                  _const((d, xw)), mem_spec, mem_spec, _const((xw, d)), _const((1, d))],
        out_specs=[pl.BlockSpec((tp, d), lambda i: (i, 0))] * 2,
        out_shape=[jax.ShapeDtypeStruct((t, d), F32), jax.ShapeDtypeStruct((t, d), BF16)],
        compiler_params=_params(("parallel",)),
    )(a_mla.reshape(t, mla_w), b_dsa.reshape(t, dsa_w), x2d, w_out_a, w_out_b, row(g_cross), wq_c,
      k_mem, v_mem, wo_c, row(g_mlp))

    tmm, tf = 512, 1024
    y = pl.pallas_call(
        _mlp_kernel, name="mlp", grid=(t // tmm, D_FF // tf),
        in_specs=[pl.BlockSpec((tmm, d), lambda i, f: (i, 0)), pl.BlockSpec((tmm, d), lambda i, f: (i, 0)),
                  pl.BlockSpec((d, tf), lambda i, f: (0, f)), pl.BlockSpec((tf, d), lambda i, f: (f, 0)),
                  _const((1, d))],
        out_specs=pl.BlockSpec((tmm, d), lambda i, f: (i, 0)),
        out_shape=jax.ShapeDtypeStruct((t, d), F32),
        compiler_params=_params(("parallel", "arbitrary")),
    )(hm, x2, wu, wd, row(g_final))
    return y.reshape(batch, seq, d)
```

```python
import functools

import jax
import jax.numpy as jnp
from jax import lax
from jax.experimental import pallas as pl
from jax.experimental.pallas import tpu as pltpu

F32 = jnp.float32
BF16 = jnp.bfloat16
I32 = jnp.int32

D_MODEL = 2048
CHUNK = 64
ROPE_THETA = 500000.0
N_MEM = 256
EPS = 1e-6
MLA_HEADS = 8
MLA_NOPE = 128
MLA_ROPE = 64
MLA_V = 128
MLA_Q_LORA = 512
MLA_KV_LORA = 256
DSA_HEADS = 8
DSA_HEAD_DIM = 128
DSA_ROT = DSA_HEAD_DIM // 4
IDX_HEADS = 16
IDX_DIM = 64
IDX_ROT = IDX_DIM // 4
TOPK_MAX = 256
X_HEADS = 4
X_HEAD_DIM = 128
D_FF = 4 * D_MODEL

LANES = 128
MLA_QK_PAD = 256
NEG = -1e30
INT_MIN = -2 ** 31
LOG2E = 1.4426950408889634
VMEM_LIMIT = 56 * 1024 * 1024


def _params(sem, vmem=VMEM_LIMIT):
    return pltpu.CompilerParams(dimension_semantics=sem, vmem_limit_bytes=vmem)


def _rms(xf, g):
    return xf * lax.rsqrt(jnp.mean(xf * xf, axis=-1, keepdims=True) + EPS) * g


def _apply_rope(x, coeffs, half):
    c, s_lo, s_hi = coeffs
    return x * c + pltpu.roll(x, LANES - half, 1) * s_lo + pltpu.roll(x, half, 1) * s_hi


def _dot(a, b):
    return jnp.dot(a, b, preferred_element_type=F32)


def _dot_nt(a, b):
    return lax.dot_general(a, b, (((1,), (1,)), ((), ())), preferred_element_type=F32)


ROPE_VARIANTS = ((MLA_ROPE // 2, LANES), (DSA_ROT // 2, LANES), (IDX_ROT // 2, IDX_DIM))


def _rope_lane_plan():
    offs, o = [], 0
    for half, _ in ROPE_VARIANTS:
        offs.append(o)
        o += 2 * half
    assert o <= LANES
    return offs


def _rope_coeffs(pos, invf):
    ang = pos.astype(F32) * invf
    cos, sin = jnp.cos(ang), jnp.sin(ang)
    lane = lax.broadcasted_iota(I32, (1, LANES), 1)
    coeffs = []
    for (half, period), off in zip(ROPE_VARIANTS, _rope_lane_plan()):
        c = jnp.ones(cos.shape, F32)
        s_lo = jnp.zeros(cos.shape, F32)
        s_hi = jnp.zeros(cos.shape, F32)
        for base in range(0, LANES, period):
            shift = (base - off) % LANES
            cs = cos if shift == 0 else pltpu.roll(cos, shift, 1)
            sn = sin if shift == 0 else pltpu.roll(sin, shift, 1)
            in_lo = (lane >= base) & (lane < base + half)
            in_hi = (lane >= base + half) & (lane < base + 2 * half)
            c = jnp.where(in_lo | in_hi, cs, c)
            s_lo = jnp.where(in_lo, -sn, s_lo)
            s_hi = jnp.where(in_hi, sn, s_hi)
        coeffs.append((c, s_lo, s_hi))
    return coeffs


def _proj_kernel(x_ref, pos_ref, g_ref, invf_ref, wcq_ref, gcq_ref, wqb_ref, wckv_ref, gckv_ref, wk_ref,
                 wvt_ref, wdsa_ref, wdsa_vt_ref, widx_ref,
                 q_ref, k_ref, vt_ref, qk_d_ref, vt_d_ref, qi_ref, ki_ref, wi_ref):
    h = _rms(x_ref[...], g_ref[...]).astype(BF16)
    rope = _rope_coeffs(pos_ref[...], invf_ref[...])
    lane = lax.broadcasted_iota(I32, (1, LANES), 1)

    cq = _dot(h, wcq_ref[...])
    r = _dot(h, wckv_ref[...])
    q = _dot(_rms(cq, gcq_ref[...]).astype(BF16), wqb_ref[...])
    ckv = _rms(r[:, :MLA_KV_LORA], gckv_ref[...]).astype(BF16)
    kn = _dot(ckv, wk_ref[...])
    vt_ref[...] = _dot_nt(wvt_ref[...], ckv).astype(BF16)
    scale = (MLA_NOPE + MLA_ROPE) ** -0.5 * LOG2E
    kr = _apply_rope(r[:, MLA_KV_LORA:], rope[0], MLA_ROPE // 2).astype(BF16)
    for hh in range(MLA_HEADS):
        b0 = hh * MLA_QK_PAD
        q_ref[:, b0:b0 + LANES] = (q[:, b0:b0 + LANES] * scale).astype(BF16)
        qr = _apply_rope(q[:, b0 + LANES:b0 + 2 * LANES], rope[0], MLA_ROPE // 2)
        q_ref[:, b0 + LANES:b0 + 2 * LANES] = (qr * scale).astype(BF16)
        k_ref[:, b0:b0 + LANES] = kn[:, hh * LANES:(hh + 1) * LANES].astype(BF16)
        k_ref[:, b0 + LANES:b0 + 2 * LANES] = kr

    dsa_w = DSA_HEADS * DSA_HEAD_DIM
    for g, scale in enumerate((DSA_HEAD_DIM ** -0.5 * LOG2E, 1.0)):
        rd = _dot(h, wdsa_ref[:, g * dsa_w:(g + 1) * dsa_w])
        for hh in range(DSA_HEADS):
            sl = slice(hh * LANES, (hh + 1) * LANES)
            qk_d_ref[g, :, sl] = (_apply_rope(rd[:, sl], rope[1], DSA_ROT // 2) * scale).astype(BF16)
    vt_d_ref[...] = _dot_nt(wdsa_vt_ref[...], h).astype(BF16)

    ri = _dot(h, widx_ref[...])
    nq = IDX_HEADS * IDX_DIM
    for p in range(IDX_HEADS // 2):
        t = _apply_rope(ri[:, p * LANES:(p + 1) * LANES], rope[2], IDX_ROT // 2) * (IDX_DIM ** -0.5)
        qi_ref[:, (2 * p) * LANES:(2 * p + 1) * LANES] = jnp.where(lane < IDX_DIM, t, 0.0).astype(BF16)
        qi_ref[:, (2 * p + 1) * LANES:(2 * p + 2) * LANES] = jnp.where(lane >= IDX_DIM, t, 0.0).astype(BF16)
    ki_ref[...] = _apply_rope(ri[:, nq:nq + LANES], rope[2], IDX_ROT // 2).astype(BF16)
    wi_ref[...] = ri[:, nq + LANES:nq + 2 * LANES] * (IDX_HEADS ** -0.5)


SUM_ROWS = 16


def _flash_sweep_t(n_tiles, tk, heads, qk, vt, adjust, adjust_last, m_scr, acc_scr, pipelined):
    m_scr[...] = jnp.full(m_scr.shape, NEG, F32)
    acc_scr[...] = jnp.zeros(acc_scr.shape, F32)
    ones = jnp.ones((SUM_ROWS, tk), BF16)

    def scores(h, off, adj):
        s_t = adj(qk(h, off), off)
        return s_t, jnp.max(s_t, axis=0, keepdims=True)

    def finish(h, s_and_max, off):
        s_t, m_tile = s_and_max
        m_prev = m_scr[h]
        m_new = jnp.maximum(m_prev, m_tile)
        p = jnp.exp2(s_t - m_new).astype(BF16)
        pv = _dot(jnp.concatenate([vt(h, off), ones], axis=0), p)
        acc_scr[h] = jnp.exp2(m_prev - m_new) * acc_scr[h] + pv
        m_scr[h] = m_new

    off_last = pl.multiple_of((n_tiles - 1) * tk, tk)
    if pipelined:
        assert adjust is adjust_last, "scores are adjusted one tile ahead"

        def body(j, s_cur):
            off = pl.multiple_of(j * tk, tk)
            off_n = pl.multiple_of((j + 1) * tk, tk)
            s_new = [scores(h, off_n, adjust) for h in range(min(2, heads))]
            for h in range(heads):
                finish(h, s_cur[h], off)
                if h + 2 < heads:
                    s_new.append(scores(h + 2, off_n, adjust))
            return tuple(s_new)

        s_last = lax.fori_loop(0, n_tiles - 1, body, tuple(scores(h, 0, adjust) for h in range(heads)))
        for h in range(heads):
            finish(h, s_last[h], off_last)
    else:
        def tile(off, adj):
            s_ts = [scores(h, off, adj) for h in range(heads)]
            for h in range(heads):
                finish(h, s_ts[h], off)

        def body(j, c):
            tile(pl.multiple_of(j * tk, tk), adjust)
            return c

        lax.fori_loop(0, n_tiles - 1, body, 0)
        tile(off_last, adjust_last)


def _attn_out_t(acc, dv):
    return (acc[:dv] / acc[dv:dv + 1]).T.astype(BF16)


def _mla_attn_kernel(q_ref, k_ref, vt_ref, wu_ref, wd_ref, o_ref, wu_bf_ref, wd_bf_ref, m_scr, acc_scr,
                     *, tq, tk, hp):
    wu_bf_ref[...] = wu_ref[...].astype(BF16)
    wd_bf_ref[...] = wd_ref[...].astype(BF16)
    i = pl.program_id(2)
    n_tiles = (i * tq) // tk + 1
    qry_chunk = (i * tq + lax.broadcasted_iota(I32, (1, tq), 1)) // CHUNK

    def qk(h, off):
        qs = slice(h * MLA_QK_PAD, (h + 1) * MLA_QK_PAD)
        return _dot_nt(k_ref[pl.ds(off, tk), qs], q_ref[:, qs])

    def vt(h, off):
        return vt_ref[h * MLA_V:(h + 1) * MLA_V, pl.ds(off, tk)]

    def causal(s_t, off):
        key_chunk = (off + lax.broadcasted_iota(I32, (tk, 1), 0)) // CHUNK
        return jnp.where(key_chunk <= qry_chunk, s_t, NEG)

    _flash_sweep_t(n_tiles, tk, hp, qk, vt, lambda s_t, off: s_t, causal, m_scr, acc_scr, pipelined=False)
    for h in range(hp):
        o_ref[:, h * MLA_V:(h + 1) * MLA_V] = _attn_out_t(acc_scr[h], MLA_V)


def _dsa_kernel(q_ref, k_ref, vt_ref, qi_ref, ki_ref, wi_ref, o_ref,
                sc_scr, sb_scr, bias_scr, cnt_scr, cnt16_scr, m_scr, acc_scr, *, tq, tk, n_sel, seq):
    i = pl.program_id(1)
    n_valid = ((i + 1) * tq + tk - 1) // tk
    qry_chunk = (i * tq + lax.broadcasted_iota(I32, (1, tq), 1)) // CHUNK

    def key_ids(off):
        return off + lax.broadcasted_iota(I32, (tk, 1), 0)

    wi_t = wi_ref[...].T

    def score_tile(j, c):
        off = pl.multiple_of(j * tk, tk)
        kj = ki_ref[pl.ds(off, tk), :]
        sc = jnp.zeros((tk, tq), F32)
        for hh in range(IDX_HEADS):
            lg = _dot_nt(kj, qi_ref[:, hh * LANES:(hh + 1) * LANES])
            sc = sc + jnp.maximum(lg, 0.0) * wi_t[hh:hh + 1, :]
        sc = jnp.where(key_ids(off) // CHUNK <= qry_chunk, sc, -jnp.inf)
        sc_scr[pl.ds(off, tk), :] = sc
        sb_scr[pl.ds(off, tk), :] = sc.astype(BF16)
        return c

    lax.fori_loop(0, n_valid, score_tile, 0)

    sub = cnt_scr.shape[0]

    def count_bf16(cand):
        cnt16_scr[...] = jnp.zeros(cnt16_scr.shape, jnp.int16)

        def body(j, c):
            off = pl.multiple_of(j * tk, tk)
            hit = jnp.where(sb_scr[pl.ds(off, tk), :] >= cand, jnp.int16(1), jnp.int16(0))
            part = hit[:sub]
            for cc in range(1, tk // sub):
                part = part + hit[cc * sub:(cc + 1) * sub]
            cnt16_scr[...] += part
            return c

        lax.fori_loop(0, n_valid, body, 0)
        return jnp.sum(cnt16_scr[...].astype(I32), axis=0, keepdims=True)

    def count(pred):
        cnt_scr[...] = jnp.zeros(cnt_scr.shape, I32)

        def body(j, c):
            off = pl.multiple_of(j * tk, tk)
            hit = jnp.where(pred(sc_scr[pl.ds(off, tk), :], off), 1, 0).astype(I32)
            part = hit[:sub]
            for cc in range(1, tk // sub):
                part = part + hit[cc * sub:(cc + 1) * sub]
            cnt_scr[...] += part
            return c

        lax.fori_loop(0, n_valid, body, 0)
        return jnp.sum(cnt_scr[...], axis=0, keepdims=True)

    def key_to_float(key):
        return pltpu.bitcast(key ^ ((key >> 31) & I32(0x7FFFFFFF)), F32)

    half = 2 ** 15
    neg_inf_key = 0x807FFFFF - 2 ** 32

    def coarse_key(key16):
        return (key16 << 16) + ((key16 >> 31) & I32(0xFFFF))

    def coarse_bit(it, tb):
        cand_b = tb | (I32(1) << (15 - it))
        cand = key_to_float(coarse_key(cand_b - half)).astype(BF16)
        cnt = count_bf16(cand)
        return jnp.where(cnt >= n_sel, cand_b, tb)

    v_key16 = jnp.maximum(lax.fori_loop(0, 16, coarse_bit, jnp.zeros((1, tq), I32)) - half, neg_inf_key >> 16)
    lo_key = coarse_key(v_key16) - half

    def fine_bit(it, off):
        cand_off = off | (I32(1) << (16 - it))
        cand = key_to_float(lo_key + cand_off)
        cnt = count(lambda sc, o: sc >= cand)
        return jnp.where(cnt >= n_sel, cand_off, off)

    thr_key = lo_key + lax.fori_loop(0, 17, fine_bit, jnp.zeros((1, tq), I32))
    thr = key_to_float(jnp.maximum(thr_key, neg_inf_key))

    cnt_gt = count(lambda sc, off: sc > thr)
    cnt_ge = count(lambda sc, off: sc >= thr)
    need = n_sel - cnt_gt
    tied = (cnt_ge > n_sel) & (thr > -jnp.inf)
    any_tied = jnp.max(tied.astype(I32)) > 0
    idx_bits = (2 * seq - 1).bit_length()

    def tie_cut():
        def cut_bit(it, jc):
            cand = jc | (I32(1) << (idx_bits - 1 - it))
            cnt = count(lambda sc, off: (sc == thr) & (key_ids(off) < cand))
            return jnp.where(cnt <= need, cand, jc)
        return lax.fori_loop(0, idx_bits, cut_bit, jnp.zeros((1, tq), I32))

    jcut = lax.cond(any_tied, tie_cut, lambda: jnp.full((1, tq), 2 ** idx_bits - 1, I32))

    def write_bias(off, sel, causal):
        if causal:
            sel = sel & (key_ids(off) // CHUNK <= qry_chunk)
        bias_scr[pl.ds(off, tk), :] = jnp.where(sel, 0.0, NEG).astype(F32)

    def sel_plain(off):
        return sc_scr[pl.ds(off, tk), :] >= thr

    def sel_general(off):
        sc = sc_scr[pl.ds(off, tk), :]
        return (sc > thr) | ((sc == thr) & (key_ids(off) < jcut))

    off_last = pl.multiple_of((n_valid - 1) * tk, tk)
    for pred, sel_fn in ((any_tied, sel_general), (jnp.logical_not(any_tied), sel_plain)):
        @pl.when(pred)
        def _(sel_fn=sel_fn):
            def bias_tile(j, c):
                off = pl.multiple_of(j * tk, tk)
                write_bias(off, sel_fn(off), causal=False)
                return c

            lax.fori_loop(0, n_valid - 1, bias_tile, 0)
            write_bias(off_last, sel_fn(off_last), causal=True)

    hp = m_scr.shape[0]

    def biased(s_t, off):
        return s_t + bias_scr[pl.ds(off, tk), :]

    for g in range(DSA_HEADS // hp):
        def head_cols(h, g=g):
            return slice((g * hp + h) * DSA_HEAD_DIM, (g * hp + h + 1) * DSA_HEAD_DIM)

        def qk(h, off):
            return _dot_nt(k_ref[pl.ds(off, tk), head_cols(h)], q_ref[:, head_cols(h)])

        def vt(h, off):
            return vt_ref[head_cols(h), pl.ds(off, tk)]

        _flash_sweep_t(n_valid, tk, hp, qk, vt, biased, biased, m_scr, acc_scr, pipelined=True)
        for h in range(hp):
            o_ref[:, head_cols(h)] = _attn_out_t(acc_scr[h], DSA_HEAD_DIM)


def _mem_kv_kernel(mem_ref, g_ref, wk_ref, wv_ref, k_ref, v_ref):
    mn = _rms(mem_ref[...], g_ref[...]).astype(BF16)
    k_ref[...] = _dot(mn, wk_ref[...]).astype(BF16)
    v_ref[...] = _dot(mn, wv_ref[...]).astype(BF16)


def _post_kernel(a_ref, b_ref, x_ref, woa_ref, wob_ref, gc_ref, wq_ref, km_ref, vm_ref, wo_ref, gm_ref,
                 x2_ref, hm_ref):
    x1 = x_ref[...] + _dot(a_ref[...], woa_ref[...]) + _dot(b_ref[...], wob_ref[...])
    hc = _rms(x1, gc_ref[...]).astype(BF16)
    qc = (_dot(hc, wq_ref[...]) * (X_HEAD_DIM ** -0.5)).astype(BF16)
    outs = []
    for hh in range(X_HEADS):
        sl = slice(hh * X_HEAD_DIM, (hh + 1) * X_HEAD_DIM)
        s = _dot_nt(qc[:, sl], km_ref[:, sl])
        p = jnp.exp(s - jnp.max(s, axis=-1, keepdims=True))
        o = _dot(p.astype(BF16), vm_ref[:, sl]) / jnp.sum(p, axis=-1, keepdims=True)
        outs.append(o.astype(BF16))
    x2 = x1 + _dot(jnp.concatenate(outs, axis=-1), wo_ref[...])
    x2_ref[...] = x2
    hm_ref[...] = _rms(x2, gm_ref[...]).astype(BF16)


def _mlp_kernel(hm_ref, x2_ref, wu_ref, wd_ref, gf_ref, y_ref):
    f = pl.program_id(1)

    @pl.when(f == 0)
    def _():
        y_ref[...] = x2_ref[...]

    u = jnp.maximum(_dot(hm_ref[...], wu_ref[...]), 0.0)
    y_ref[...] += _dot((u * u).astype(BF16), wd_ref[...])

    @pl.when(f == pl.num_programs(1) - 1)
    def _():
        y_ref[...] = _rms(y_ref[...], gf_ref[...])


def _const(shape):
    return pl.BlockSpec(shape, lambda *_: (0,) * len(shape), pipeline_mode=pl.Buffered(1))


def kernel(x, mem, positions, g_mix, w_in, g_cq, g_ckv, w_qb, w_kvb, w_out, g_cross, g_mem,
           w_q_cross, w_k_cross, w_v_cross, w_o_cross, g_mlp, w_up, w_down, g_final):
    assert w_in.shape[0] == 1, "one layer"
    batch, seq, d = x.shape
    t = batch * seq
    n_sel = min(TOPK_MAX, seq // 4)
    x2d = x.reshape(t, d)
    pos = positions.reshape(t, 1)
    row = lambda g: g.reshape(1, -1).astype(F32)

    w = w_in[0]
    o = [0]
    for n in (MLA_Q_LORA, MLA_KV_LORA, MLA_ROPE, DSA_HEADS * DSA_HEAD_DIM, DSA_HEADS * DSA_HEAD_DIM,
              DSA_HEADS * DSA_HEAD_DIM, IDX_HEADS * IDX_DIM, IDX_DIM, IDX_HEADS):
        o.append(o[-1] + n)
    zeros = lambda n: jnp.zeros((d, n), w.dtype)
    w_cq = w[:, o[0]:o[1]].astype(BF16)
    w_ckv = jnp.concatenate([w[:, o[1]:o[3]], zeros(LANES - MLA_ROPE)], axis=1).astype(BF16)
    w_dsa_qk = w[:, o[3]:o[5]].astype(BF16)
    w_dsa_vt = w[:, o[5]:o[6]].T.astype(BF16)
    w_idx = jnp.concatenate([w[:, o[6]:o[7]], w[:, o[7]:o[8]], w[:, o[7]:o[8]], w[:, o[8]:o[9]],
                             zeros(LANES - IDX_HEADS)], axis=1).astype(BF16)
    wqb = w_qb[0].reshape(MLA_Q_LORA, MLA_HEADS, MLA_NOPE + MLA_ROPE)
    wqb = jnp.pad(wqb, ((0, 0), (0, 0), (0, MLA_QK_PAD - MLA_NOPE - MLA_ROPE)))
    wqb = wqb.reshape(MLA_Q_LORA, MLA_HEADS * MLA_QK_PAD).astype(BF16)
    wkvb = w_kvb[0].reshape(MLA_KV_LORA, MLA_HEADS, MLA_NOPE + MLA_V)
    wkvb_k = wkvb[:, :, :MLA_NOPE].reshape(MLA_KV_LORA, -1).astype(BF16)
    wkvb_vt = wkvb[:, :, MLA_NOPE:].reshape(MLA_KV_LORA, -1).T.astype(BF16)
    mla_w = MLA_HEADS * MLA_V
    w_out_a = w_out[0, :mla_w].astype(BF16)
    w_out_b = w_out[0, mla_w:].astype(BF16)
    wq_c, wk_c, wv_c, wo_c = (a[0].astype(BF16) for a in (w_q_cross, w_k_cross, w_v_cross, w_o_cross))
    invf = jnp.zeros((LANES,), F32)
    for (half, _), off in zip(ROPE_VARIANTS, _rope_lane_plan()):
        inv_freq = ROPE_THETA ** (-jnp.arange(half, dtype=F32) / half)
        invf = invf.at[off:off + 2 * half].set(jnp.concatenate([inv_freq, inv_freq]))
    invf = invf.reshape(1, LANES)

    qk_w = MLA_HEADS * MLA_QK_PAD
    dsa_w = DSA_HEADS * DSA_HEAD_DIM
    qi_w = IDX_HEADS * LANES
    tpj = 256
    rows = lambda w: pl.BlockSpec((tpj, w), lambda i: (i, 0))
    cols = lambda w: pl.BlockSpec((w, tpj), lambda i: (0, i))
    q_mla, k_mla, vt_mla, qk_d, vt_d, q_idx, k_idx, w_idx_s = pl.pallas_call(
        _proj_kernel, name="proj", grid=(t // tpj,),
        in_specs=[rows(d), rows(1), _const((1, d)), _const((1, LANES)),
                  _const((d, MLA_Q_LORA)), _const((1, MLA_Q_LORA)), _const((MLA_Q_LORA, qk_w)),
                  _const((d, MLA_KV_LORA + LANES)), _const((1, MLA_KV_LORA)), _const((MLA_KV_LORA, mla_w)),
                  _const((mla_w, MLA_KV_LORA)),
                  _const((d, 2 * dsa_w)), _const((dsa_w, d)), _const((d, IDX_HEADS * IDX_DIM + 2 * LANES))],
        out_specs=[rows(qk_w), rows(qk_w), cols(mla_w),
                   pl.BlockSpec((2, tpj, dsa_w), lambda i: (0, i, 0)), cols(dsa_w),
                   rows(qi_w), rows(LANES), rows(LANES)],
        out_shape=[jax.ShapeDtypeStruct((t, qk_w), BF16), jax.ShapeDtypeStruct((t, qk_w), BF16),
                   jax.ShapeDtypeStruct((mla_w, t), BF16),
                   jax.ShapeDtypeStruct((2, t, dsa_w), BF16), jax.ShapeDtypeStruct((dsa_w, t), BF16),
                   jax.ShapeDtypeStruct((t, qi_w), BF16), jax.ShapeDtypeStruct((t, LANES), BF16),
                   jax.ShapeDtypeStruct((t, LANES), F32)],
        compiler_params=_params(("parallel",)),
    )(x2d, pos, row(g_mix), invf, w_cq, row(g_cq), wqb, w_ckv, row(g_ckv), wkvb_k, wkvb_vt, w_dsa_qk, w_dsa_vt,
      w_idx)

    tq, tk, hp = 256, 512, 8
    mla_grid = (batch, MLA_HEADS // hp, seq // tq)
    n_steps = mla_grid[0] * mla_grid[1] * mla_grid[2]
    step = lambda b, h, i: (b * mla_grid[1] + h) * mla_grid[2] + i
    wu_rows, wd_rows = d // n_steps, D_FF // n_steps
    a_mla, wu, wd = pl.pallas_call(
        functools.partial(_mla_attn_kernel, tq=tq, tk=tk, hp=hp), name="mla_attn", grid=mla_grid,
        in_specs=[pl.BlockSpec((None, tq, hp * MLA_QK_PAD), lambda b, h, i: (b, i, h)),
                  pl.BlockSpec((None, seq, hp * MLA_QK_PAD), lambda b, h, i: (b, 0, h),
                               pipeline_mode=pl.Buffered(1)),
                  pl.BlockSpec((hp * MLA_V, seq), lambda b, h, i: (h, b), pipeline_mode=pl.Buffered(1)),
                  pl.BlockSpec((None, wu_rows, D_FF), lambda b, h, i: (0, step(b, h, i), 0)),
                  pl.BlockSpec((None, wd_rows, d), lambda b, h, i: (0, step(b, h, i), 0))],
        out_specs=[pl.BlockSpec((None, tq, hp * MLA_V), lambda b, h, i: (b, i, h)),
                   pl.BlockSpec((wu_rows, D_FF), lambda b, h, i: (step(b, h, i), 0)),
                   pl.BlockSpec((wd_rows, d), lambda b, h, i: (step(b, h, i), 0))],
        out_shape=[jax.ShapeDtypeStruct((batch, seq, mla_w), BF16),
                   jax.ShapeDtypeStruct((d, D_FF), BF16), jax.ShapeDtypeStruct((D_FF, d), BF16)],
        scratch_shapes=[pltpu.VMEM((hp, 1, tq), F32), pltpu.VMEM((hp, MLA_V + SUM_ROWS, tq), F32)],
        compiler_params=_params(("parallel", "parallel", "arbitrary")),
    )(q_mla.reshape(batch, seq, qk_w), k_mla.reshape(batch, seq, qk_w), vt_mla, w_up, w_down)

    qk4 = qk_d.reshape(2, batch, seq, dsa_w)
    tq, tk, hp = 256, 512, 8
    b_dsa = pl.pallas_call(
        functools.partial(_dsa_kernel, tq=tq, tk=tk, n_sel=n_sel, seq=seq), name="dsa",
        grid=(batch, seq // tq),
        in_specs=[pl.BlockSpec((None, None, tq, dsa_w), lambda b, i: (0, b, i, 0)),
                  pl.BlockSpec((None, None, seq, dsa_w), lambda b, i: (1, b, 0, 0), pipeline_mode=pl.Buffered(1)),
                  pl.BlockSpec((dsa_w, seq), lambda b, i: (0, b), pipeline_mode=pl.Buffered(1)),
                  pl.BlockSpec((None, tq, qi_w), lambda b, i: (b, i, 0)),
                  pl.BlockSpec((None, seq, LANES), lambda b, i: (b, 0, 0)),
                  pl.BlockSpec((None, tq, LANES), lambda b, i: (b, i, 0))],
        out_specs=pl.BlockSpec((None, tq, dsa_w), lambda b, i: (b, i, 0)),
        out_shape=jax.ShapeDtypeStruct((batch, seq, dsa_w), BF16),
        scratch_shapes=[pltpu.VMEM((seq, tq), F32), pltpu.VMEM((seq, tq), BF16), pltpu.VMEM((seq, tq), F32),
                        pltpu.VMEM((32, tq), I32), pltpu.VMEM((32, tq), jnp.int16),
                        pltpu.VMEM((hp, 1, tq), F32), pltpu.VMEM((hp, DSA_HEAD_DIM + SUM_ROWS, tq), F32)],
        compiler_params=_params(("parallel", "arbitrary")),
    )(qk4, qk4, vt_d, q_idx.reshape(batch, seq, qi_w), k_idx.reshape(batch, seq, LANES),
      w_idx_s.reshape(batch, seq, LANES))

    xw = X_HEADS * X_HEAD_DIM
    n_mem = mem.shape[1]
    k_mem, v_mem = pl.pallas_call(
        _mem_kv_kernel, name="mem_kv", grid=(batch,),
        in_specs=[pl.BlockSpec((None, n_mem, d), lambda b: (b, 0, 0)), _const((1, d)), _const((d, xw)),
                  _const((d, xw))],
        out_specs=[pl.BlockSpec((None, n_mem, xw), lambda b: (b, 0, 0))] * 2,
        out_shape=[jax.ShapeDtypeStruct((batch, n_mem, xw), BF16)] * 2,
        compiler_params=_params(("parallel",)),
    )(mem, row(g_mem), wk_c, wv_c)

    tp = 512
    per_b = seq // tp
    mem_spec = pl.BlockSpec((None, n_mem, xw), lambda i: (i // per_b, 0, 0))
    x2, hm = pl.pallas_call(
        _post_kernel, name="post", grid=(t // tp,),
        in_specs=[pl.BlockSpec((tp, mla_w), lambda i: (i, 0)), pl.BlockSpec((tp, dsa_w), lambda i: (i, 0)),
                  pl.BlockSpec((tp, d), lambda i: (i, 0)), _const((mla_w, d)), _const((dsa_w, d)), _const((1, d)),
                  _const((d, xw)), mem_spec, mem_spec, _const((xw, d)), _const((1, d))],
        out_specs=[pl.BlockSpec((tp, d), lambda i: (i, 0))] * 2,
        out_shape=[jax.ShapeDtypeStruct((t, d), F32), jax.ShapeDtypeStruct((t, d), BF16)],
        compiler_params=_params(("parallel",)),
    )(a_mla.reshape(t, mla_w), b_dsa.reshape(t, dsa_w), x2d, w_out_a, w_out_b, row(g_cross), wq_c,
      k_mem, v_mem, wo_c, row(g_mlp))

    tmm, tf = 512, 1024
    y = pl.pallas_call(
        _mlp_kernel, name="mlp", grid=(t // tmm, D_FF // tf),
        in_specs=[pl.BlockSpec((tmm, d), lambda i, f: (i, 0)), pl.BlockSpec((tmm, d), lambda i, f: (i, 0)),
                  pl.BlockSpec((d, tf), lambda i, f: (0, f)), pl.BlockSpec((tf, d), lambda i, f: (f, 0)),
                  _const((1, d))],
        out_specs=pl.BlockSpec((tmm, d), lambda i, f: (i, 0)),
        out_shape=jax.ShapeDtypeStruct((t, d), F32),
        compiler_params=_params(("parallel", "arbitrary")),
    )(hm, x2, wu, wd, row(g_final))
    return y.reshape(batch, seq, d)
```

```python
import functools

import jax
import jax.numpy as jnp
from jax import lax
from jax.experimental import pallas as pl
from jax.experimental.pallas import tpu as pltpu

F32 = jnp.float32
BF16 = jnp.bfloat16
I32 = jnp.int32

D_MODEL = 2048
CHUNK = 64
ROPE_THETA = 500000.0
N_MEM = 256
EPS = 1e-6
MLA_HEADS = 8
MLA_NOPE = 128
MLA_ROPE = 64
MLA_V = 128
MLA_Q_LORA = 512
MLA_KV_LORA = 256
DSA_HEADS = 8
DSA_HEAD_DIM = 128
DSA_ROT = DSA_HEAD_DIM // 4
IDX_HEADS = 16
IDX_DIM = 64
IDX_ROT = IDX_DIM // 4
TOPK_MAX = 256
X_HEADS = 4
X_HEAD_DIM = 128
D_FF = 4 * D_MODEL

LANES = 128
MLA_QK_PAD = 256
NEG = -1e30
INT_MIN = -2 ** 31
LOG2E = 1.4426950408889634
VMEM_LIMIT = 56 * 1024 * 1024


def _params(sem, vmem=VMEM_LIMIT):
    return pltpu.CompilerParams(dimension_semantics=sem, vmem_limit_bytes=vmem)


def _rms(xf, g):
    return xf * lax.rsqrt(jnp.mean(xf * xf, axis=-1, keepdims=True) + EPS) * g


def _apply_rope(x, coeffs, half):
    c, s_lo, s_hi = coeffs
    return x * c + pltpu.roll(x, LANES - half, 1) * s_lo + pltpu.roll(x, half, 1) * s_hi


def _dot(a, b):
    return jnp.dot(a, b, preferred_element_type=F32)


def _dot_nt(a, b):
    return lax.dot_general(a, b, (((1,), (1,)), ((), ())), preferred_element_type=F32)


ROPE_VARIANTS = ((MLA_ROPE // 2, LANES), (DSA_ROT // 2, LANES), (IDX_ROT // 2, IDX_DIM))


def _rope_lane_plan():
    offs, o = [], 0
    for half, _ in ROPE_VARIANTS:
        offs.append(o)
        o += 2 * half
    assert o <= LANES
    return offs


def _rope_coeffs(pos, invf):
    ang = pos.astype(F32) * invf
    cos, sin = jnp.cos(ang), jnp.sin(ang)
    lane = lax.broadcasted_iota(I32, (1, LANES), 1)
    coeffs = []
    for (half, period), off in zip(ROPE_VARIANTS, _rope_lane_plan()):
        c = jnp.ones(cos.shape, F32)
        s_lo = jnp.zeros(cos.shape, F32)
        s_hi = jnp.zeros(cos.shape, F32)
        for base in range(0, LANES, period):
            shift = (base - off) % LANES
            cs = cos if shift == 0 else pltpu.roll(cos, shift, 1)
            sn = sin if shift == 0 else pltpu.roll(sin, shift, 1)
            in_lo = (lane >= base) & (lane < base + half)
            in_hi = (lane >= base + half) & (lane < base + 2 * half)
            c = jnp.where(in_lo | in_hi, cs, c)
            s_lo = jnp.where(in_lo, -sn, s_lo)
            s_hi = jnp.where(in_hi, sn, s_hi)
        coeffs.append((c, s_lo, s_hi))
    return coeffs


def _proj_kernel(x_ref, pos_ref, g_ref, invf_ref, wcq_ref, gcq_ref, wqb_ref, wckv_ref, gckv_ref, wk_ref,
                 wvt_ref, wdsa_ref, wdsa_vt_ref, widx_ref,
                 q_ref, k_ref, vt_ref, qk_d_ref, vt_d_ref, qi_ref, ki_ref, wi_ref):
    h = _rms(x_ref[...], g_ref[...]).astype(BF16)
    rope = _rope_coeffs(pos_ref[...], invf_ref[...])
    lane = lax.broadcasted_iota(I32, (1, LANES), 1)

    cq = _dot(h, wcq_ref[...])
    r = _dot(h, wckv_ref[...])
    q = _dot(_rms(cq, gcq_ref[...]).astype(BF16), wqb_ref[...])
    ckv = _rms(r[:, :MLA_KV_LORA], gckv_ref[...]).astype(BF16)
    kn = _dot(ckv, wk_ref[...])
    vt_ref[...] = _dot_nt(wvt_ref[...], ckv).astype(BF16)
    scale = (MLA_NOPE + MLA_ROPE) ** -0.5 * LOG2E
    kr = _apply_rope(r[:, MLA_KV_LORA:], rope[0], MLA_ROPE // 2).astype(BF16)
    for hh in range(MLA_HEADS):
        b0 = hh * MLA_QK_PAD
        q_ref[:, b0:b0 + LANES] = (q[:, b0:b0 + LANES] * scale).astype(BF16)
        qr = _apply_rope(q[:, b0 + LANES:b0 + 2 * LANES], rope[0], MLA_ROPE // 2)
        q_ref[:, b0 + LANES:b0 + 2 * LANES] = (qr * scale).astype(BF16)
        k_ref[:, b0:b0 + LANES] = kn[:, hh * LANES:(hh + 1) * LANES].astype(BF16)
        k_ref[:, b0 + LANES:b0 + 2 * LANES] = kr

    dsa_w = DSA_HEADS * DSA_HEAD_DIM
    for g, scale in enumerate((DSA_HEAD_DIM ** -0.5 * LOG2E, 1.0)):
        rd = _dot(h, wdsa_ref[:, g * dsa_w:(g + 1) * dsa_w])
        for hh in range(DSA_HEADS):
            sl = slice(hh * LANES, (hh + 1) * LANES)
            qk_d_ref[g, :, sl] = (_apply_rope(rd[:, sl], rope[1], DSA_ROT // 2) * scale).astype(BF16)
    vt_d_ref[...] = _dot_nt(wdsa_vt_ref[...], h).astype(BF16)

    ri = _dot(h, widx_ref[...])
    nq = IDX_HEADS * IDX_DIM
    for p in range(IDX_HEADS // 2):
        t = _apply_rope(ri[:, p * LANES:(p + 1) * LANES], rope[2], IDX_ROT // 2) * (IDX_DIM ** -0.5)
        qi_ref[:, (2 * p) * LANES:(2 * p + 1) * LANES] = jnp.where(lane < IDX_DIM, t, 0.0).astype(BF16)
        qi_ref[:, (2 * p + 1) * LANES:(2 * p + 2) * LANES] = jnp.where(lane >= IDX_DIM, t, 0.0).astype(BF16)
    ki_ref[...] = _apply_rope(ri[:, nq:nq + LANES], rope[2], IDX_ROT // 2).astype(BF16)
    wi_ref[...] = ri[:, nq + LANES:nq + 2 * LANES] * (IDX_HEADS ** -0.5)


SUM_ROWS = 16


def _flash_sweep_t(n_tiles, tk, heads, qk, vt, adjust, adjust_last, m_scr, acc_scr):
    m_scr[...] = jnp.full(m_scr.shape, NEG, F32)
    acc_scr[...] = jnp.zeros(acc_scr.shape, F32)
    ones = jnp.ones((SUM_ROWS, tk), BF16)

    def scores(h, off, adj):
        s_t = adj(qk(h, off))
        return s_t, jnp.max(s_t, axis=0, keepdims=True)

    def finish(h, s_and_max, off):
        s_t, m_tile = s_and_max
        m_prev = m_scr[h]
        m_new = jnp.maximum(m_prev, m_tile)
        p = jnp.exp2(s_t - m_new).astype(BF16)
        pv = _dot(jnp.concatenate([vt(h, off), ones], axis=0), p)
        acc_scr[h] = jnp.exp2(m_prev - m_new) * acc_scr[h] + pv
        m_scr[h] = m_new

    def tile(off, adjust_tile):
        adj = adjust_tile(off)
        s_ts = [scores(h, off, adj) for h in range(heads)]
        for h in range(heads):
            finish(h, s_ts[h], off)

    def body(j, c):
        tile(pl.multiple_of(j * tk, tk), adjust)
        return c

    lax.fori_loop(0, n_tiles - 1, body, 0)
    tile(pl.multiple_of((n_tiles - 1) * tk, tk), adjust_last)


def _attn_out_t(acc, dv):
    return (acc[:dv] / acc[dv:dv + 1]).T.astype(BF16)


def _mla_attn_kernel(q_ref, k_ref, vt_ref, wu_ref, wd_ref, o_ref, wu_bf_ref, wd_bf_ref, m_scr, acc_scr,
                     *, tq, tk, hp):
    wu_bf_ref[...] = wu_ref[...].astype(BF16)
    wd_bf_ref[...] = wd_ref[...].astype(BF16)
    i = pl.program_id(2)
    n_tiles = (i * tq) // tk + 1
    qry_chunk = (i * tq + lax.broadcasted_iota(I32, (1, tq), 1)) // CHUNK

    def qk(h, off):
        qs = slice(h * MLA_QK_PAD, (h + 1) * MLA_QK_PAD)
        return _dot_nt(k_ref[pl.ds(off, tk), qs], q_ref[:, qs])

    def vt(h, off):
        return vt_ref[h * MLA_V:(h + 1) * MLA_V, pl.ds(off, tk)]

    def causal(off):
        key_chunk = (off + lax.broadcasted_iota(I32, (tk, 1), 0)) // CHUNK
        return lambda s_t: jnp.where(key_chunk <= qry_chunk, s_t, NEG)

    _flash_sweep_t(n_tiles, tk, hp, qk, vt, lambda off: (lambda s_t: s_t), causal, m_scr, acc_scr)
    for h in range(hp):
        o_ref[:, h * MLA_V:(h + 1) * MLA_V] = _attn_out_t(acc_scr[h], MLA_V)


def _dsa_kernel(q_ref, k_ref, vt_ref, qi_ref, ki_ref, wi_ref, o_ref,
                sc_scr, sb_scr, bias_scr, cnt_scr, cnt16_scr, m_scr, acc_scr, *, tq, tk, n_sel, seq):
    i = pl.program_id(1)
    n_valid = ((i + 1) * tq + tk - 1) // tk
    qry_chunk = (i * tq + lax.broadcasted_iota(I32, (1, tq), 1)) // CHUNK

    def key_ids(off):
        return off + lax.broadcasted_iota(I32, (tk, 1), 0)

    wi_t = wi_ref[...].T

    def score_tile(j, c):
        off = pl.multiple_of(j * tk, tk)
        kj = ki_ref[pl.ds(off, tk), :]
        sc = jnp.zeros((tk, tq), F32)
        for hh in range(IDX_HEADS):
            lg = _dot_nt(kj, qi_ref[:, hh * LANES:(hh + 1) * LANES])
            sc = sc + jnp.maximum(lg, 0.0) * wi_t[hh:hh + 1, :]
        sc = jnp.where(key_ids(off) // CHUNK <= qry_chunk, sc, -jnp.inf)
        sc_scr[pl.ds(off, tk), :] = sc
        sb_scr[pl.ds(off, tk), :] = sc.astype(BF16)
        return c

    lax.fori_loop(0, n_valid, score_tile, 0)

    sub = cnt_scr.shape[0]

    def count_bf16(cand):
        cnt16_scr[...] = jnp.zeros(cnt16_scr.shape, jnp.int16)

        def body(j, c):
            off = pl.multiple_of(j * tk, tk)
            hit = jnp.where(sb_scr[pl.ds(off, tk), :] >= cand, jnp.int16(1), jnp.int16(0))
            part = hit[:sub]
            for cc in range(1, tk // sub):
                part = part + hit[cc * sub:(cc + 1) * sub]
            cnt16_scr[...] += part
            return c

        lax.fori_loop(0, n_valid, body, 0)
        return jnp.sum(cnt16_scr[...].astype(I32), axis=0, keepdims=True)

    def count(pred):
        cnt_scr[...] = jnp.zeros(cnt_scr.shape, I32)

        def body(j, c):
            off = pl.multiple_of(j * tk, tk)
            hit = jnp.where(pred(sc_scr[pl.ds(off, tk), :], off), 1, 0).astype(I32)
            part = hit[:sub]
            for cc in range(1, tk // sub):
                part = part + hit[cc * sub:(cc + 1) * sub]
            cnt_scr[...] += part
            return c

        lax.fori_loop(0, n_valid, body, 0)
        return jnp.sum(cnt_scr[...], axis=0, keepdims=True)

    def key_to_float(key):
        return pltpu.bitcast(key ^ ((key >> 31) & I32(0x7FFFFFFF)), F32)

    half = 2 ** 15
    neg_inf_key = 0x807FFFFF - 2 ** 32

    def coarse_key(key16):
        return (key16 << 16) + ((key16 >> 31) & I32(0xFFFF))

    def coarse_bit(it, tb):
        cand_b = tb | (I32(1) << (15 - it))
        cand = key_to_float(coarse_key(cand_b - half)).astype(BF16)
        cnt = count_bf16(cand)
        return jnp.where(cnt >= n_sel, cand_b, tb)

    v_key16 = jnp.maximum(lax.fori_loop(0, 16, coarse_bit, jnp.zeros((1, tq), I32)) - half, neg_inf_key >> 16)
    lo_key = coarse_key(v_key16) - half

    def fine_bit(it, off):
        cand_off = off | (I32(1) << (16 - it))
        cand = key_to_float(lo_key + cand_off)
        cnt = count(lambda sc, o: sc >= cand)
        return jnp.where(cnt >= n_sel, cand_off, off)

    thr_key = lo_key + lax.fori_loop(0, 17, fine_bit, jnp.zeros((1, tq), I32))
    thr = key_to_float(jnp.maximum(thr_key, neg_inf_key))

    cnt_gt = count(lambda sc, off: sc > thr)
    cnt_ge = count(lambda sc, off: sc >= thr)
    need = n_sel - cnt_gt
    tied = (cnt_ge > n_sel) & (thr > -jnp.inf)
    any_tied = jnp.max(tied.astype(I32)) > 0
    idx_bits = (2 * seq - 1).bit_length()

    def tie_cut():
        def cut_bit(it, jc):
            cand = jc | (I32(1) << (idx_bits - 1 - it))
            cnt = count(lambda sc, off: (sc == thr) & (key_ids(off) < cand))
            return jnp.where(cnt <= need, cand, jc)
        return lax.fori_loop(0, idx_bits, cut_bit, jnp.zeros((1, tq), I32))

    jcut = lax.cond(any_tied, tie_cut, lambda: jnp.full((1, tq), 2 ** idx_bits - 1, I32))

    def write_bias(off, sel, causal):
        if causal:
            sel = sel & (key_ids(off) // CHUNK <= qry_chunk)
        bias_scr[pl.ds(off, tk), :] = jnp.where(sel, 0.0, NEG).astype(F32)

    def sel_plain(off):
        return sc_scr[pl.ds(off, tk), :] >= thr

    def sel_general(off):
        sc = sc_scr[pl.ds(off, tk), :]
        return (sc > thr) | ((sc == thr) & (key_ids(off) < jcut))

    off_last = pl.multiple_of((n_valid - 1) * tk, tk)
    for pred, sel_fn in ((any_tied, sel_general), (jnp.logical_not(any_tied), sel_plain)):
        @pl.when(pred)
        def _(sel_fn=sel_fn):
            def bias_tile(j, c):
                off = pl.multiple_of(j * tk, tk)
                write_bias(off, sel_fn(off), causal=False)
                return c

            lax.fori_loop(0, n_valid - 1, bias_tile, 0)
            write_bias(off_last, sel_fn(off_last), causal=True)

    hp = m_scr.shape[0]

    def biased(off):
        bias = bias_scr[pl.ds(off, tk), :]
        return lambda s_t: s_t + bias

    for g in range(DSA_HEADS // hp):
        def head_cols(h, g=g):
            return slice((g * hp + h) * DSA_HEAD_DIM, (g * hp + h + 1) * DSA_HEAD_DIM)

        def qk(h, off):
            return _dot_nt(k_ref[pl.ds(off, tk), head_cols(h)], q_ref[:, head_cols(h)])

        def vt(h, off):
            return vt_ref[head_cols(h), pl.ds(off, tk)]

        _flash_sweep_t(n_valid, tk, hp, qk, vt, biased, biased, m_scr, acc_scr)
        for h in range(hp):
            o_ref[:, head_cols(h)] = _attn_out_t(acc_scr[h], DSA_HEAD_DIM)


def _mem_kv_kernel(mem_ref, g_ref, wk_ref, wv_ref, k_ref, v_ref):
    mn = _rms(mem_ref[...], g_ref[...]).astype(BF16)
    k_ref[...] = _dot(mn, wk_ref[...]).astype(BF16)
    v_ref[...] = _dot(mn, wv_ref[...]).astype(BF16)


def _post_kernel(a_ref, b_ref, x_ref, woa_ref, wob_ref, gc_ref, wq_ref, km_ref, vm_ref, wo_ref, gm_ref,
                 x2_ref, hm_ref):
    x1 = x_ref[...] + _dot(a_ref[...], woa_ref[...]) + _dot(b_ref[...], wob_ref[...])
    hc = _rms(x1, gc_ref[...]).astype(BF16)
    qc = (_dot(hc, wq_ref[...]) * (X_HEAD_DIM ** -0.5)).astype(BF16)
    outs = []
    for hh in range(X_HEADS):
        sl = slice(hh * X_HEAD_DIM, (hh + 1) * X_HEAD_DIM)
        s = _dot_nt(qc[:, sl], km_ref[:, sl])
        p = jnp.exp(s - jnp.max(s, axis=-1, keepdims=True))
        o = _dot(p.astype(BF16), vm_ref[:, sl]) / jnp.sum(p, axis=-1, keepdims=True)
        outs.append(o.astype(BF16))
    x2 = x1 + _dot(jnp.concatenate(outs, axis=-1), wo_ref[...])
    x2_ref[...] = x2
    hm_ref[...] = _rms(x2, gm_ref[...]).astype(BF16)


def _mlp_kernel(hm_ref, x2_ref, wu_ref, wd_ref, gf_ref, y_ref):
    f = pl.program_id(1)

    @pl.when(f == 0)
    def _():
        y_ref[...] = x2_ref[...]

    u = jnp.maximum(_dot(hm_ref[...], wu_ref[...]), 0.0)
    y_ref[...] += _dot((u * u).astype(BF16), wd_ref[...])

    @pl.when(f == pl.num_programs(1) - 1)
    def _():
        y_ref[...] = _rms(y_ref[...], gf_ref[...])


def _const(shape):
    return pl.BlockSpec(shape, lambda *_: (0,) * len(shape), pipeline_mode=pl.Buffered(1))


def kernel(x, mem, positions, g_mix, w_in, g_cq, g_ckv, w_qb, w_kvb, w_out, g_cross, g_mem,
           w_q_cross, w_k_cross, w_v_cross, w_o_cross, g_mlp, w_up, w_down, g_final):
    assert w_in.shape[0] == 1, "one layer"
    batch, seq, d = x.shape
    t = batch * seq
    n_sel = min(TOPK_MAX, seq // 4)
    x2d = x.reshape(t, d)
    pos = positions.reshape(t, 1)
    row = lambda g: g.reshape(1, -1).astype(F32)

    w = w_in[0]
    o = [0]
    for n in (MLA_Q_LORA, MLA_KV_LORA, MLA_ROPE, DSA_HEADS * DSA_HEAD_DIM, DSA_HEADS * DSA_HEAD_DIM,
              DSA_HEADS * DSA_HEAD_DIM, IDX_HEADS * IDX_DIM, IDX_DIM, IDX_HEADS):
        o.append(o[-1] + n)
    zeros = lambda n: jnp.zeros((d, n), w.dtype)
    w_cq = w[:, o[0]:o[1]].astype(BF16)
    w_ckv = jnp.concatenate([w[:, o[1]:o[3]], zeros(LANES - MLA_ROPE)], axis=1).astype(BF16)
    w_dsa_qk = w[:, o[3]:o[5]].astype(BF16)
    w_dsa_vt = w[:, o[5]:o[6]].T.astype(BF16)
    w_idx = jnp.concatenate([w[:, o[6]:o[7]], w[:, o[7]:o[8]], w[:, o[7]:o[8]], w[:, o[8]:o[9]],
                             zeros(LANES - IDX_HEADS)], axis=1).astype(BF16)
    wqb = w_qb[0].reshape(MLA_Q_LORA, MLA_HEADS, MLA_NOPE + MLA_ROPE)
    wqb = jnp.pad(wqb, ((0, 0), (0, 0), (0, MLA_QK_PAD - MLA_NOPE - MLA_ROPE)))
    wqb = wqb.reshape(MLA_Q_LORA, MLA_HEADS * MLA_QK_PAD).astype(BF16)
    wkvb = w_kvb[0].reshape(MLA_KV_LORA, MLA_HEADS, MLA_NOPE + MLA_V)
    wkvb_k = wkvb[:, :, :MLA_NOPE].reshape(MLA_KV_LORA, -1).astype(BF16)
    wkvb_vt = wkvb[:, :, MLA_NOPE:].reshape(MLA_KV_LORA, -1).T.astype(BF16)
    mla_w = MLA_HEADS * MLA_V
    w_out_a = w_out[0, :mla_w].astype(BF16)
    w_out_b = w_out[0, mla_w:].astype(BF16)
    wq_c, wk_c, wv_c, wo_c = (a[0].astype(BF16) for a in (w_q_cross, w_k_cross, w_v_cross, w_o_cross))
    invf = jnp.zeros((LANES,), F32)
    for (half, _), off in zip(ROPE_VARIANTS, _rope_lane_plan()):
        inv_freq = ROPE_THETA ** (-jnp.arange(half, dtype=F32) / half)
        invf = invf.at[off:off + 2 * half].set(jnp.concatenate([inv_freq, inv_freq]))
    invf = invf.reshape(1, LANES)

    qk_w = MLA_HEADS * MLA_QK_PAD
    dsa_w = DSA_HEADS * DSA_HEAD_DIM
    qi_w = IDX_HEADS * LANES
    tpj = 256
    rows = lambda w: pl.BlockSpec((tpj, w), lambda i: (i, 0))
    cols = lambda w: pl.BlockSpec((w, tpj), lambda i: (0, i))
    q_mla, k_mla, vt_mla, qk_d, vt_d, q_idx, k_idx, w_idx_s = pl.pallas_call(
        _proj_kernel, name="proj", grid=(t // tpj,),
        in_specs=[rows(d), rows(1), _const((1, d)), _const((1, LANES)),
                  _const((d, MLA_Q_LORA)), _const((1, MLA_Q_LORA)), _const((MLA_Q_LORA, qk_w)),
                  _const((d, MLA_KV_LORA + LANES)), _const((1, MLA_KV_LORA)), _const((MLA_KV_LORA, mla_w)),
                  _const((mla_w, MLA_KV_LORA)),
                  _const((d, 2 * dsa_w)), _const((dsa_w, d)), _const((d, IDX_HEADS * IDX_DIM + 2 * LANES))],
        out_specs=[rows(qk_w), rows(qk_w), cols(mla_w),
                   pl.BlockSpec((2, tpj, dsa_w), lambda i: (0, i, 0)), cols(dsa_w),
                   rows(qi_w), rows(LANES), rows(LANES)],
        out_shape=[jax.ShapeDtypeStruct((t, qk_w), BF16), jax.ShapeDtypeStruct((t, qk_w), BF16),
                   jax.ShapeDtypeStruct((mla_w, t), BF16),
                   jax.ShapeDtypeStruct((2, t, dsa_w), BF16), jax.ShapeDtypeStruct((dsa_w, t), BF16),
                   jax.ShapeDtypeStruct((t, qi_w), BF16), jax.ShapeDtypeStruct((t, LANES), BF16),
                   jax.ShapeDtypeStruct((t, LANES), F32)],
        compiler_params=_params(("parallel",)),
    )(x2d, pos, row(g_mix), invf, w_cq, row(g_cq), wqb, w_ckv, row(g_ckv), wkvb_k, wkvb_vt, w_dsa_qk, w_dsa_vt,
      w_idx)

    tq, tk, hp = 256, 512, 8
    mla_grid = (batch, MLA_HEADS // hp, seq // tq)
    n_steps = mla_grid[0] * mla_grid[1] * mla_grid[2]
    step = lambda b, h, i: (b * mla_grid[1] + h) * mla_grid[2] + i
    wu_rows, wd_rows = d // n_steps, D_FF // n_steps
    a_mla, wu, wd = pl.pallas_call(
        functools.partial(_mla_attn_kernel, tq=tq, tk=tk, hp=hp), name="mla_attn", grid=mla_grid,
        in_specs=[pl.BlockSpec((None, tq, hp * MLA_QK_PAD), lambda b, h, i: (b, i, h)),
                  pl.BlockSpec((None, seq, hp * MLA_QK_PAD), lambda b, h, i: (b, 0, h),
                               pipeline_mode=pl.Buffered(1)),
                  pl.BlockSpec((hp * MLA_V, seq), lambda b, h, i: (h, b), pipeline_mode=pl.Buffered(1)),
                  pl.BlockSpec((None, wu_rows, D_FF), lambda b, h, i: (0, step(b, h, i), 0)),
                  pl.BlockSpec((None, wd_rows, d), lambda b, h, i: (0, step(b, h, i), 0))],
        out_specs=[pl.BlockSpec((None, tq, hp * MLA_V), lambda b, h, i: (b, i, h)),
                   pl.BlockSpec((wu_rows, D_FF), lambda b, h, i: (step(b, h, i), 0)),
                   pl.BlockSpec((wd_rows, d), lambda b, h, i: (step(b, h, i), 0))],
        out_shape=[jax.ShapeDtypeStruct((batch, seq, mla_w), BF16),
                   jax.ShapeDtypeStruct((d, D_FF), BF16), jax.ShapeDtypeStruct((D_FF, d), BF16)],
        scratch_shapes=[pltpu.VMEM((hp, 1, tq), F32), pltpu.VMEM((hp, MLA_V + SUM_ROWS, tq), F32)],
        compiler_params=_params(("parallel", "parallel", "arbitrary")),
    )(q_mla.reshape(batch, seq, qk_w), k_mla.reshape(batch, seq, qk_w), vt_mla, w_up, w_down)

    qk4 = qk_d.reshape(2, batch, seq, dsa_w)
    tq, tk, hp = 256, 512, 8
    b_dsa = pl.pallas_call(
        functools.partial(_dsa_kernel, tq=tq, tk=tk, n_sel=n_sel, seq=seq), name="dsa",
        grid=(batch, seq // tq),
        in_specs=[pl.BlockSpec((None, None, tq, dsa_w), lambda b, i: (0, b, i, 0)),
                  pl.BlockSpec((None, None, seq, dsa_w), lambda b, i: (1, b, 0, 0), pipeline_mode=pl.Buffered(1)),
                  pl.BlockSpec((dsa_w, seq), lambda b, i: (0, b), pipeline_mode=pl.Buffered(1)),
                  pl.BlockSpec((None, tq, qi_w), lambda b, i: (b, i, 0)),
                  pl.BlockSpec((None, seq, LANES), lambda b, i: (b, 0, 0)),
                  pl.BlockSpec((None, tq, LANES), lambda b, i: (b, i, 0))],
        out_specs=pl.BlockSpec((None, tq, dsa_w), lambda b, i: (b, i, 0)),
        out_shape=jax.ShapeDtypeStruct((batch, seq, dsa_w), BF16),
        scratch_shapes=[pltpu.VMEM((seq, tq), F32), pltpu.VMEM((seq, tq), BF16), pltpu.VMEM((seq, tq), F32),
                        pltpu.VMEM((32, tq), I32), pltpu.VMEM((32, tq), jnp.int16),
                        pltpu.VMEM((hp, 1, tq), F32), pltpu.VMEM((hp, DSA_HEAD_DIM + SUM_ROWS, tq), F32)],
        compiler_params=_params(("parallel", "arbitrary")),
    )(qk4, qk4, vt_d, q_idx.reshape(batch, seq, qi_w), k_idx.reshape(batch, seq, LANES),
      w_idx_s.reshape(batch, seq, LANES))

    xw = X_HEADS * X_HEAD_DIM
    n_mem = mem.shape[1]
    k_mem, v_mem = pl.pallas_call(
        _mem_kv_kernel, name="mem_kv", grid=(batch,),
        in_specs=[pl.BlockSpec((None, n_mem, d), lambda b: (b, 0, 0)), _const((1, d)), _const((d, xw)),
                  _const((d, xw))],
        out_specs=[pl.BlockSpec((None, n_mem, xw), lambda b: (b, 0, 0))] * 2,
        out_shape=[jax.ShapeDtypeStruct((batch, n_mem, xw), BF16)] * 2,
        compiler_params=_params(("parallel",)),
    )(mem, row(g_mem), wk_c, wv_c)

    tp = 512
    per_b = seq // tp
    mem_spec = pl.BlockSpec((None, n_mem, xw), lambda i: (i // per_b, 0, 0))
    x2, hm = pl.pallas_call(
        _post_kernel, name="post", grid=(t // tp,),
        in_specs=[pl.BlockSpec((tp, mla_w), lambda i: (i, 0)), pl.BlockSpec((tp, dsa_w), lambda i: (i, 0)),
                  pl.BlockSpec((tp, d), lambda i: (i, 0)), _const((mla_w, d)), _const((dsa_w, d)), _const((1, d)),
                  _const((d, xw)), mem_spec, mem_spec, _const((xw, d)), _const((1, d))],
        out_specs=[pl.BlockSpec((tp, d), lambda i: (i, 0))] * 2,
        out_shape=[jax.ShapeDtypeStruct((t, d), F32), jax.ShapeDtypeStruct((t, d), BF16)],
        compiler_params=_params(("parallel",)),
    )(a_mla.reshape(t, mla_w), b_dsa.reshape(t, dsa_w), x2d, w_out_a, w_out_b, row(g_cross), wq_c,
      k_mem, v_mem, wo_c, row(g_mlp))

    tmm, tf = 512, 1024
    y = pl.pallas_call(
        _mlp_kernel, name="mlp", grid=(t // tmm, D_FF // tf),
        in_specs=[pl.BlockSpec((tmm, d), lambda i, f: (i, 0)), pl.BlockSpec((tmm, d), lambda i, f: (i, 0)),
                  pl.BlockSpec((d, tf), lambda i, f: (0, f)), pl.BlockSpec((tf, d), lambda i, f: (f, 0)),
                  _const((1, d))],
        out_specs=pl.BlockSpec((tmm, d), lambda i, f: (i, 0)),
        out_shape=jax.ShapeDtypeStruct((t, d), F32),
        compiler_params=_params(("parallel", "arbitrary")),
    )(hm, x2, wu, wd, row(g_final))
    return y.reshape(batch, seq, d)
```

```python
import functools

import jax
import jax.numpy as jnp
from jax import lax
from jax.experimental import pallas as pl
from jax.experimental.pallas import tpu as pltpu

F32 = jnp.float32
BF16 = jnp.bfloat16
I32 = jnp.int32

D_MODEL = 2048
CHUNK = 64
ROPE_THETA = 500000.0
N_MEM = 256
EPS = 1e-6
MLA_HEADS = 8
MLA_NOPE = 128
MLA_ROPE = 64
MLA_V = 128
MLA_Q_LORA = 512
MLA_KV_LORA = 256
DSA_HEADS = 8
DSA_HEAD_DIM = 128
DSA_ROT = DSA_HEAD_DIM // 4
IDX_HEADS = 16
IDX_DIM = 64
IDX_ROT = IDX_DIM // 4
TOPK_MAX = 256
X_HEADS = 4
X_HEAD_DIM = 128
D_FF = 4 * D_MODEL

LANES = 128
W_CQ = 0
W_CKV = W_CQ + MLA_Q_LORA
W_DSA = W_CKV + MLA_KV_LORA + LANES
W_IDX = W_DSA + 3 * DSA_HEADS * DSA_HEAD_DIM
W_END = W_IDX + IDX_HEADS * IDX_DIM + LANES
MLA_QK_PAD = 256
NEG = -1e30
INT_MIN = -2 ** 31
LOG2E = 1.4426950408889634
VMEM_LIMIT = 56 * 1024 * 1024


def _params(sem, vmem=VMEM_LIMIT):
    return pltpu.CompilerParams(dimension_semantics=sem, vmem_limit_bytes=vmem)


def _rms(xf, g):
    return xf * lax.rsqrt(jnp.mean(xf * xf, axis=-1, keepdims=True) + EPS) * g


def _apply_rope(x, coeffs, half):
    c, s_lo, s_hi = coeffs
    return x * c + pltpu.roll(x, LANES - half, 1) * s_lo + pltpu.roll(x, half, 1) * s_hi


def _dot(a, b):
    return jnp.dot(a, b, preferred_element_type=F32)


def _dot_nt(a, b):
    return lax.dot_general(a, b, (((1,), (1,)), ((), ())), preferred_element_type=F32)


ROPE_VARIANTS = ((MLA_ROPE // 2, LANES), (DSA_ROT // 2, LANES), (IDX_ROT // 2, IDX_DIM))


def _rope_lane_plan():
    offs, o = [], 0
    for half, _ in ROPE_VARIANTS:
        offs.append(o)
        o += 2 * half
    assert o <= LANES
    return offs


def _rope_coeffs(pos, invf):
    ang = pos.astype(F32) * invf
    cos, sin = jnp.cos(ang), jnp.sin(ang)
    lane = lax.broadcasted_iota(I32, (1, LANES), 1)
    coeffs = []
    for (half, period), off in zip(ROPE_VARIANTS, _rope_lane_plan()):
        c = jnp.ones(cos.shape, F32)
        s_lo = jnp.zeros(cos.shape, F32)
        s_hi = jnp.zeros(cos.shape, F32)
        for base in range(0, LANES, period):
            shift = (base - off) % LANES
            cs = cos if shift == 0 else pltpu.roll(cos, shift, 1)
            sn = sin if shift == 0 else pltpu.roll(sin, shift, 1)
            in_lo = (lane >= base) & (lane < base + half)
            in_hi = (lane >= base + half) & (lane < base + 2 * half)
            c = jnp.where(in_lo | in_hi, cs, c)
            s_lo = jnp.where(in_lo, -sn, s_lo)
            s_hi = jnp.where(in_hi, sn, s_hi)
        coeffs.append((c, s_lo, s_hi))
    return coeffs


def _proj_kernel(x_ref, pos_ref, g_ref, invf_ref, w_ref, gcq_ref, wqb_ref, gckv_ref, wk_ref, wvt_ref, wdsa_vt_ref,
                 q_ref, k_ref, vt_ref, qk_d_ref, vt_d_ref, qi_ref, ki_ref, wi_ref):
    h = _rms(x_ref[...], g_ref[...]).astype(BF16)
    rope = _rope_coeffs(pos_ref[...], invf_ref[...])
    lane = lax.broadcasted_iota(I32, (1, LANES), 1)

    cq = _dot(h, w_ref[:, W_CQ:W_CKV])
    r = _dot(h, w_ref[:, W_CKV:W_DSA])
    q = _dot(_rms(cq, gcq_ref[...]).astype(BF16), wqb_ref[...])
    ckv = _rms(r[:, :MLA_KV_LORA], gckv_ref[...]).astype(BF16)
    kn = _dot(ckv, wk_ref[...])
    vt_ref[...] = _dot_nt(wvt_ref[...], ckv).astype(BF16)
    scale = (MLA_NOPE + MLA_ROPE) ** -0.5 * LOG2E
    kr = _apply_rope(r[:, MLA_KV_LORA:], rope[0], MLA_ROPE // 2).astype(BF16)
    for hh in range(MLA_HEADS):
        b0 = hh * MLA_QK_PAD
        q_ref[:, b0:b0 + LANES] = (q[:, b0:b0 + LANES] * scale).astype(BF16)
        qr = _apply_rope(q[:, b0 + LANES:b0 + 2 * LANES], rope[0], MLA_ROPE // 2)
        q_ref[:, b0 + LANES:b0 + 2 * LANES] = (qr * scale).astype(BF16)
        k_ref[:, b0:b0 + LANES] = kn[:, hh * LANES:(hh + 1) * LANES].astype(BF16)
        k_ref[:, b0 + LANES:b0 + 2 * LANES] = kr

    dsa_w = DSA_HEADS * DSA_HEAD_DIM
    for g, scale in enumerate((DSA_HEAD_DIM ** -0.5 * LOG2E, 1.0)):
        rd = _dot(h, w_ref[:, W_DSA + g * dsa_w:W_DSA + (g + 1) * dsa_w])
        for hh in range(DSA_HEADS):
            sl = slice(hh * LANES, (hh + 1) * LANES)
            qk_d_ref[g, :, sl] = (_apply_rope(rd[:, sl], rope[1], DSA_ROT // 2) * scale).astype(BF16)
    vt_d_ref[...] = _dot_nt(wdsa_vt_ref[...], h).astype(BF16)

    ri = _dot(h, w_ref[:, W_IDX:W_END])
    nq = IDX_HEADS * IDX_DIM
    tail = ri[:, nq:nq + LANES]
    tail_hi = pltpu.roll(tail, LANES - IDX_DIM, 1)
    for p in range(IDX_HEADS // 2):
        t = _apply_rope(ri[:, p * LANES:(p + 1) * LANES], rope[2], IDX_ROT // 2) * (IDX_DIM ** -0.5)
        qi_ref[:, (2 * p) * LANES:(2 * p + 1) * LANES] = jnp.where(lane < IDX_DIM, t, 0.0).astype(BF16)
        qi_ref[:, (2 * p + 1) * LANES:(2 * p + 2) * LANES] = jnp.where(lane >= IDX_DIM, t, 0.0).astype(BF16)
    k_dup = jnp.where(lane < IDX_DIM, tail, pltpu.roll(tail, IDX_DIM, 1))
    ki_ref[...] = _apply_rope(k_dup, rope[2], IDX_ROT // 2).astype(BF16)
    wi_ref[...] = jnp.where(lane < IDX_HEADS, tail_hi, 0.0) * (IDX_HEADS ** -0.5)


SUM_ROWS = 16


def _flash_sweep_t(n_tiles, tk, heads, qk, vt, adjust, adjust_last, m_scr, acc_scr):
    m_scr[...] = jnp.full(m_scr.shape, NEG, F32)
    acc_scr[...] = jnp.zeros(acc_scr.shape, F32)
    ones = jnp.ones((SUM_ROWS, tk), BF16)

    def scores(h, off, adj):
        s_t = adj(qk(h, off))
        return s_t, jnp.max(s_t, axis=0, keepdims=True)

    def finish(h, s_and_max, off):
        s_t, m_tile = s_and_max
        m_prev = m_scr[h]
        m_new = jnp.maximum(m_prev, m_tile)
        p = jnp.exp2(s_t - m_new).astype(BF16)
        pv = _dot(jnp.concatenate([vt(h, off), ones], axis=0), p)
        acc_scr[h] = jnp.exp2(m_prev - m_new) * acc_scr[h] + pv
        m_scr[h] = m_new

    def tile(off, adjust_tile):
        adj = adjust_tile(off)
        s_ts = [scores(h, off, adj) for h in range(heads)]
        for h in range(heads):
            finish(h, s_ts[h], off)

    def body(j, c):
        tile(pl.multiple_of(j * tk, tk), adjust)
        return c

    lax.fori_loop(0, n_tiles - 1, body, 0)
    tile(pl.multiple_of((n_tiles - 1) * tk, tk), adjust_last)


def _attn_out_t(acc, dv):
    return (acc[:dv] / acc[dv:dv + 1]).T.astype(BF16)


def _mla_attn_kernel(q_ref, k_ref, vt_ref, wu_ref, wd_ref, o_ref, wu_bf_ref, wd_bf_ref, m_scr, acc_scr,
                     *, tq, tk, hp):
    wu_bf_ref[...] = wu_ref[...].astype(BF16)
    wd_bf_ref[...] = wd_ref[...].astype(BF16)
    i = pl.program_id(2)
    n_tiles = (i * tq) // tk + 1
    qry_chunk = (i * tq + lax.broadcasted_iota(I32, (1, tq), 1)) // CHUNK

    def qk(h, off):
        qs = slice(h * MLA_QK_PAD, (h + 1) * MLA_QK_PAD)
        return _dot_nt(k_ref[pl.ds(off, tk), qs], q_ref[:, qs])

    def vt(h, off):
        return vt_ref[h * MLA_V:(h + 1) * MLA_V, pl.ds(off, tk)]

    def causal(off):
        key_chunk = (off + lax.broadcasted_iota(I32, (tk, 1), 0)) // CHUNK
        return lambda s_t: jnp.where(key_chunk <= qry_chunk, s_t, NEG)

    _flash_sweep_t(n_tiles, tk, hp, qk, vt, lambda off: (lambda s_t: s_t), causal, m_scr, acc_scr)
    for h in range(hp):
        o_ref[:, h * MLA_V:(h + 1) * MLA_V] = _attn_out_t(acc_scr[h], MLA_V)


def _dsa_kernel(q_ref, k_ref, vt_ref, qi_ref, ki_ref, wi_ref, o_ref,
                sc_scr, sb_scr, bias_scr, cnt_scr, cnt16_scr, m_scr, acc_scr, *, tq, tk, n_sel, seq):
    i = pl.program_id(1)
    n_valid = ((i + 1) * tq + tk - 1) // tk
    qry_chunk = (i * tq + lax.broadcasted_iota(I32, (1, tq), 1)) // CHUNK

    def key_ids(off):
        return off + lax.broadcasted_iota(I32, (tk, 1), 0)

    wi_t = wi_ref[...].T

    def score_tile(j, c):
        off = pl.multiple_of(j * tk, tk)
        kj = ki_ref[pl.ds(off, tk), :]
        sc = jnp.zeros((tk, tq), F32)
        for hh in range(IDX_HEADS):
            lg = _dot_nt(kj, qi_ref[:, hh * LANES:(hh + 1) * LANES])
            sc = sc + jnp.maximum(lg, 0.0) * wi_t[hh:hh + 1, :]
        sc = jnp.where(key_ids(off) // CHUNK <= qry_chunk, sc, -jnp.inf)
        sc_scr[pl.ds(off, tk), :] = sc
        sb_scr[pl.ds(off, tk), :] = sc.astype(BF16)
        return c

    lax.fori_loop(0, n_valid, score_tile, 0)

    sub = cnt_scr.shape[0]

    def count_bf16(cand):
        cnt16_scr[...] = jnp.zeros(cnt16_scr.shape, jnp.int16)

        def body(j, c):
            off = pl.multiple_of(j * tk, tk)
            hit = jnp.where(sb_scr[pl.ds(off, tk), :] >= cand, jnp.int16(1), jnp.int16(0))
            part = hit[:sub]
            for cc in range(1, tk // sub):
                part = part + hit[cc * sub:(cc + 1) * sub]
            cnt16_scr[...] += part
            return c

        lax.fori_loop(0, n_valid, body, 0)
        return jnp.sum(cnt16_scr[...].astype(I32), axis=0, keepdims=True)

    def count(pred):
        cnt_scr[...] = jnp.zeros(cnt_scr.shape, I32)

        def body(j, c):
            off = pl.multiple_of(j * tk, tk)
            hit = jnp.where(pred(sc_scr[pl.ds(off, tk), :], off), 1, 0).astype(I32)
            part = hit[:sub]
            for cc in range(1, tk // sub):
                part = part + hit[cc * sub:(cc + 1) * sub]
            cnt_scr[...] += part
            return c

        lax.fori_loop(0, n_valid, body, 0)
        return jnp.sum(cnt_scr[...], axis=0, keepdims=True)

    def key_to_float(key):
        return pltpu.bitcast(key ^ ((key >> 31) & I32(0x7FFFFFFF)), F32)

    half = 2 ** 15
    neg_inf_key = 0x807FFFFF - 2 ** 32

    def coarse_key(key16):
        return (key16 << 16) + ((key16 >> 31) & I32(0xFFFF))

    def coarse_bit(it, tb):
        cand_b = tb | (I32(1) << (15 - it))
        cand = key_to_float(coarse_key(cand_b - half)).astype(BF16)
        cnt = count_bf16(cand)
        return jnp.where(cnt >= n_sel, cand_b, tb)

    v_key16 = jnp.maximum(lax.fori_loop(0, 16, coarse_bit, jnp.zeros((1, tq), I32)) - half, neg_inf_key >> 16)
    lo_key = coarse_key(v_key16) - half

    def fine_bit(it, off):
        cand_off = off | (I32(1) << (16 - it))
        cand = key_to_float(lo_key + cand_off)
        cnt = count(lambda sc, o: sc >= cand)
        return jnp.where(cnt >= n_sel, cand_off, off)

    thr_key = lo_key + lax.fori_loop(0, 17, fine_bit, jnp.zeros((1, tq), I32))
    thr = key_to_float(jnp.maximum(thr_key, neg_inf_key))

    cnt_gt = count(lambda sc, off: sc > thr)
    cnt_ge = count(lambda sc, off: sc >= thr)
    need = n_sel - cnt_gt
    tied = (cnt_ge > n_sel) & (thr > -jnp.inf)
    any_tied = jnp.max(tied.astype(I32)) > 0
    idx_bits = (2 * seq - 1).bit_length()

    def tie_cut():
        def cut_bit(it, jc):
            cand = jc | (I32(1) << (idx_bits - 1 - it))
            cnt = count(lambda sc, off: (sc == thr) & (key_ids(off) < cand))
            return jnp.where(cnt <= need, cand, jc)
        return lax.fori_loop(0, idx_bits, cut_bit, jnp.zeros((1, tq), I32))

    jcut = lax.cond(any_tied, tie_cut, lambda: jnp.full((1, tq), 2 ** idx_bits - 1, I32))

    def write_bias(off, sel, causal):
        if causal:
            sel = sel & (key_ids(off) // CHUNK <= qry_chunk)
        bias_scr[pl.ds(off, tk), :] = jnp.where(sel, 0.0, NEG).astype(F32)

    def sel_plain(off):
        return sc_scr[pl.ds(off, tk), :] >= thr

    def sel_general(off):
        sc = sc_scr[pl.ds(off, tk), :]
        return (sc > thr) | ((sc == thr) & (key_ids(off) < jcut))

    off_last = pl.multiple_of((n_valid - 1) * tk, tk)
    for pred, sel_fn in ((any_tied, sel_general), (jnp.logical_not(any_tied), sel_plain)):
        @pl.when(pred)
        def _(sel_fn=sel_fn):
            def bias_tile(j, c):
                off = pl.multiple_of(j * tk, tk)
                write_bias(off, sel_fn(off), causal=False)
                return c

            lax.fori_loop(0, n_valid - 1, bias_tile, 0)
            write_bias(off_last, sel_fn(off_last), causal=True)

    hp = m_scr.shape[0]

    def biased(off):
        bias = bias_scr[pl.ds(off, tk), :]
        return lambda s_t: s_t + bias

    for g in range(DSA_HEADS // hp):
        def head_cols(h, g=g):
            return slice((g * hp + h) * DSA_HEAD_DIM, (g * hp + h + 1) * DSA_HEAD_DIM)

        def qk(h, off):
            return _dot_nt(k_ref[pl.ds(off, tk), head_cols(h)], q_ref[:, head_cols(h)])

        def vt(h, off):
            return vt_ref[head_cols(h), pl.ds(off, tk)]

        _flash_sweep_t(n_valid, tk, hp, qk, vt, biased, biased, m_scr, acc_scr)
        for h in range(hp):
            o_ref[:, head_cols(h)] = _attn_out_t(acc_scr[h], DSA_HEAD_DIM)


def _mem_kv_kernel(mem_ref, g_ref, wk_ref, wv_ref, k_ref, v_ref):
    mn = _rms(mem_ref[...], g_ref[...]).astype(BF16)
    k_ref[...] = _dot(mn, wk_ref[...]).astype(BF16)
    v_ref[...] = _dot(mn, wv_ref[...]).astype(BF16)


def _post_kernel(a_ref, b_ref, x_ref, wout_ref, gc_ref, wq_ref, km_ref, vm_ref, wo_ref, gm_ref,
                 x2_ref, hm_ref):
    na = a_ref.shape[1]
    x1 = x_ref[...] + _dot(a_ref[...], wout_ref[:na]) + _dot(b_ref[...], wout_ref[na:])
    hc = _rms(x1, gc_ref[...]).astype(BF16)
    qc = (_dot(hc, wq_ref[...]) * (X_HEAD_DIM ** -0.5)).astype(BF16)
    outs = []
    for hh in range(X_HEADS):
        sl = slice(hh * X_HEAD_DIM, (hh + 1) * X_HEAD_DIM)
        s = _dot_nt(qc[:, sl], km_ref[:, sl])
        p = jnp.exp(s - jnp.max(s, axis=-1, keepdims=True))
        o = _dot(p.astype(BF16), vm_ref[:, sl]) / jnp.sum(p, axis=-1, keepdims=True)
        outs.append(o.astype(BF16))
    x2 = x1 + _dot(jnp.concatenate(outs, axis=-1), wo_ref[...])
    x2_ref[...] = x2
    hm_ref[...] = _rms(x2, gm_ref[...]).astype(BF16)


def _mlp_kernel(hm_ref, x2_ref, wu_ref, wd_ref, gf_ref, y_ref):
    f = pl.program_id(1)

    @pl.when(f == 0)
    def _():
        y_ref[...] = x2_ref[...]

    u = jnp.maximum(_dot(hm_ref[...], wu_ref[...]), 0.0)
    y_ref[...] += _dot((u * u).astype(BF16), wd_ref[...])

    @pl.when(f == pl.num_programs(1) - 1)
    def _():
        y_ref[...] = _rms(y_ref[...], gf_ref[...])


def _const(shape):
    return pl.BlockSpec(shape, lambda *_: (0,) * len(shape), pipeline_mode=pl.Buffered(1))


def kernel(x, mem, positions, g_mix, w_in, g_cq, g_ckv, w_qb, w_kvb, w_out, g_cross, g_mem,
           w_q_cross, w_k_cross, w_v_cross, w_o_cross, g_mlp, w_up, w_down, g_final):
    assert w_in.shape[0] == 1, "one layer"
    batch, seq, d = x.shape
    t = batch * seq
    n_sel = min(TOPK_MAX, seq // 4)
    x2d = x.reshape(t, d)
    pos = positions.reshape(t, 1)
    row = lambda g: g.reshape(1, -1).astype(F32)

    w = w_in[0]
    o = [0]
    for n in (MLA_Q_LORA, MLA_KV_LORA, MLA_ROPE, DSA_HEADS * DSA_HEAD_DIM, DSA_HEADS * DSA_HEAD_DIM,
              DSA_HEADS * DSA_HEAD_DIM, IDX_HEADS * IDX_DIM, IDX_DIM, IDX_HEADS):
        o.append(o[-1] + n)
    zeros = lambda n: jnp.zeros((d, n), w.dtype)
    w_pad = jnp.concatenate([w[:, :o[3]], zeros(LANES - MLA_ROPE), w[:, o[3]:],
                             zeros(W_END - o[9] - (LANES - MLA_ROPE))], axis=1).astype(BF16)
    w_dsa_vt = w[:, o[5]:o[6]].T.astype(BF16)
    wqb = w_qb[0].reshape(MLA_Q_LORA, MLA_HEADS, MLA_NOPE + MLA_ROPE)
    wqb = jnp.pad(wqb, ((0, 0), (0, 0), (0, MLA_QK_PAD - MLA_NOPE - MLA_ROPE)))
    wqb = wqb.reshape(MLA_Q_LORA, MLA_HEADS * MLA_QK_PAD).astype(BF16)
    wkvb = w_kvb[0].reshape(MLA_KV_LORA, MLA_HEADS, MLA_NOPE + MLA_V)
    wkvb_k = wkvb[:, :, :MLA_NOPE].reshape(MLA_KV_LORA, -1).astype(BF16)
    wkvb_vt = wkvb[:, :, MLA_NOPE:].reshape(MLA_KV_LORA, -1).T.astype(BF16)
    mla_w = MLA_HEADS * MLA_V
    wq_c, wk_c, wv_c, wo_c = (a[0].astype(BF16) for a in (w_q_cross, w_k_cross, w_v_cross, w_o_cross))
    invf = jnp.zeros((LANES,), F32)
    for (half, _), off in zip(ROPE_VARIANTS, _rope_lane_plan()):
        inv_freq = ROPE_THETA ** (-jnp.arange(half, dtype=F32) / half)
        invf = invf.at[off:off + 2 * half].set(jnp.concatenate([inv_freq, inv_freq]))
    invf = invf.reshape(1, LANES)

    qk_w = MLA_HEADS * MLA_QK_PAD
    dsa_w = DSA_HEADS * DSA_HEAD_DIM
    qi_w = IDX_HEADS * LANES
    tpj = 256
    rows = lambda w: pl.BlockSpec((tpj, w), lambda i: (i, 0))
    cols = lambda w: pl.BlockSpec((w, tpj), lambda i: (0, i))
    q_mla, k_mla, vt_mla, qk_d, vt_d, q_idx, k_idx, w_idx_s = pl.pallas_call(
        _proj_kernel, name="proj", grid=(t // tpj,),
        in_specs=[rows(d), rows(1), _const((1, d)), _const((1, LANES)),
                  _const((d, W_END)), _const((1, MLA_Q_LORA)), _const((MLA_Q_LORA, qk_w)),
                  _const((1, MLA_KV_LORA)), _const((MLA_KV_LORA, mla_w)), _const((mla_w, MLA_KV_LORA)),
                  _const((dsa_w, d))],
        out_specs=[rows(qk_w), rows(qk_w), cols(mla_w),
                   pl.BlockSpec((2, tpj, dsa_w), lambda i: (0, i, 0)), cols(dsa_w),
                   rows(qi_w), rows(LANES), rows(LANES)],
        out_shape=[jax.ShapeDtypeStruct((t, qk_w), BF16), jax.ShapeDtypeStruct((t, qk_w), BF16),
                   jax.ShapeDtypeStruct((mla_w, t), BF16),
                   jax.ShapeDtypeStruct((2, t, dsa_w), BF16), jax.ShapeDtypeStruct((dsa_w, t), BF16),
                   jax.ShapeDtypeStruct((t, qi_w), BF16), jax.ShapeDtypeStruct((t, LANES), BF16),
                   jax.ShapeDtypeStruct((t, LANES), F32)],
        compiler_params=_params(("parallel",)),
    )(x2d, pos, row(g_mix), invf, w_pad, row(g_cq), wqb, row(g_ckv), wkvb_k, wkvb_vt, w_dsa_vt)

    tq, tk, hp = 256, 512, 8
    mla_grid = (batch, MLA_HEADS // hp, seq // tq)
    n_steps = mla_grid[0] * mla_grid[1] * mla_grid[2]
    step = lambda b, h, i: (b * mla_grid[1] + h) * mla_grid[2] + i
    wu_rows, wd_rows = d // n_steps, D_FF // n_steps
    a_mla, wu, wd = pl.pallas_call(
        functools.partial(_mla_attn_kernel, tq=tq, tk=tk, hp=hp), name="mla_attn", grid=mla_grid,
        in_specs=[pl.BlockSpec((None, tq, hp * MLA_QK_PAD), lambda b, h, i: (b, i, h)),
                  pl.BlockSpec((None, seq, hp * MLA_QK_PAD), lambda b, h, i: (b, 0, h),
                               pipeline_mode=pl.Buffered(1)),
                  pl.BlockSpec((hp * MLA_V, seq), lambda b, h, i: (h, b), pipeline_mode=pl.Buffered(1)),
                  pl.BlockSpec((None, wu_rows, D_FF), lambda b, h, i: (0, step(b, h, i), 0)),
                  pl.BlockSpec((None, wd_rows, d), lambda b, h, i: (0, step(b, h, i), 0))],
        out_specs=[pl.BlockSpec((None, tq, hp * MLA_V), lambda b, h, i: (b, i, h)),
                   pl.BlockSpec((wu_rows, D_FF), lambda b, h, i: (step(b, h, i), 0)),
                   pl.BlockSpec((wd_rows, d), lambda b, h, i: (step(b, h, i), 0))],
        out_shape=[jax.ShapeDtypeStruct((batch, seq, mla_w), BF16),
                   jax.ShapeDtypeStruct((d, D_FF), BF16), jax.ShapeDtypeStruct((D_FF, d), BF16)],
        scratch_shapes=[pltpu.VMEM((hp, 1, tq), F32), pltpu.VMEM((hp, MLA_V + SUM_ROWS, tq), F32)],
        compiler_params=_params(("parallel", "parallel", "arbitrary")),
    )(q_mla.reshape(batch, seq, qk_w), k_mla.reshape(batch, seq, qk_w), vt_mla, w_up, w_down)

    qk4 = qk_d.reshape(2, batch, seq, dsa_w)
    tq, tk, hp = 256, 512, 8
    b_dsa = pl.pallas_call(
        functools.partial(_dsa_kernel, tq=tq, tk=tk, n_sel=n_sel, seq=seq), name="dsa",
        grid=(batch, seq // tq),
        in_specs=[pl.BlockSpec((None, None, tq, dsa_w), lambda b, i: (0, b, i, 0)),
                  pl.BlockSpec((None, None, seq, dsa_w), lambda b, i: (1, b, 0, 0), pipeline_mode=pl.Buffered(1)),
                  pl.BlockSpec((dsa_w, seq), lambda b, i: (0, b), pipeline_mode=pl.Buffered(1)),
                  pl.BlockSpec((None, tq, qi_w), lambda b, i: (b, i, 0)),
                  pl.BlockSpec((None, seq, LANES), lambda b, i: (b, 0, 0)),
                  pl.BlockSpec((None, tq, LANES), lambda b, i: (b, i, 0))],
        out_specs=pl.BlockSpec((None, tq, dsa_w), lambda b, i: (b, i, 0)),
        out_shape=jax.ShapeDtypeStruct((batch, seq, dsa_w), BF16),
        scratch_shapes=[pltpu.VMEM((seq, tq), F32), pltpu.VMEM((seq, tq), BF16), pltpu.VMEM((seq, tq), F32),
                        pltpu.VMEM((32, tq), I32), pltpu.VMEM((32, tq), jnp.int16),
                        pltpu.VMEM((hp, 1, tq), F32), pltpu.VMEM((hp, DSA_HEAD_DIM + SUM_ROWS, tq), F32)],
        compiler_params=_params(("parallel", "arbitrary")),
    )(qk4, qk4, vt_d, q_idx.reshape(batch, seq, qi_w), k_idx.reshape(batch, seq, LANES),
      w_idx_s.reshape(batch, seq, LANES))

    xw = X_HEADS * X_HEAD_DIM
    n_mem = mem.shape[1]
    k_mem, v_mem = pl.pallas_call(
        _mem_kv_kernel, name="mem_kv", grid=(batch,),
        in_specs=[pl.BlockSpec((None, n_mem, d), lambda b: (b, 0, 0)), _const((1, d)), _const((d, xw)),
                  _const((d, xw))],
        out_specs=[pl.BlockSpec((None, n_mem, xw), lambda b: (b, 0, 0))] * 2,
        out_shape=[jax.ShapeDtypeStruct((batch, n_mem, xw), BF16)] * 2,
        compiler_params=_params(("parallel",)),
    )(mem, row(g_mem), wk_c, wv_c)

    tp = 512
    per_b = seq // tp
    mem_spec = pl.BlockSpec((None, n_mem, xw), lambda i: (i // per_b, 0, 0))
    x2, hm = pl.pallas_call(
        _post_kernel, name="post", grid=(t // tp,),
        in_specs=[pl.BlockSpec((tp, mla_w), lambda i: (i, 0)), pl.BlockSpec((tp, dsa_w), lambda i: (i, 0)),
                  pl.BlockSpec((tp, d), lambda i: (i, 0)), _const((mla_w + dsa_w, d)), _const((1, d)),
                  _const((d, xw)), mem_spec, mem_spec, _const((xw, d)), _const((1, d))],
        out_specs=[pl.BlockSpec((tp, d), lambda i: (i, 0))] * 2,
        out_shape=[jax.ShapeDtypeStruct((t, d), F32), jax.ShapeDtypeStruct((t, d), BF16)],
        compiler_params=_params(("parallel",)),
    )(a_mla.reshape(t, mla_w), b_dsa.reshape(t, dsa_w), x2d, w_out[0].astype(BF16), row(g_cross), wq_c,
      k_mem, v_mem, wo_c, row(g_mlp))

    tmm, tf = 512, 1024
    y = pl.pallas_call(
        _mlp_kernel, name="mlp", grid=(t // tmm, D_FF // tf),
        in_specs=[pl.BlockSpec((tmm, d), lambda i, f: (i, 0)), pl.BlockSpec((tmm, d), lambda i, f: (i, 0)),
                  pl.BlockSpec((d, tf), lambda i, f: (0, f)), pl.BlockSpec((tf, d), lambda i, f: (f, 0)),
                  _const((1, d))],
        out_specs=pl.BlockSpec((tmm, d), lambda i, f: (i, 0)),
        out_shape=jax.ShapeDtypeStruct((t, d), F32),
        compiler_params=_params(("parallel", "arbitrary")),
    )(hm, x2, wu, wd, row(g_final))
    return y.reshape(batch, seq, d)
```

```python
import functools

import jax
import jax.numpy as jnp
from jax import lax
from jax.experimental import pallas as pl
from jax.experimental.pallas import tpu as pltpu

F32 = jnp.float32
BF16 = jnp.bfloat16
I32 = jnp.int32

D_MODEL = 2048
CHUNK = 64
ROPE_THETA = 500000.0
N_MEM = 256
EPS = 1e-6
MLA_HEADS = 8
MLA_NOPE = 128
MLA_ROPE = 64
MLA_V = 128
MLA_Q_LORA = 512
MLA_KV_LORA = 256
DSA_HEADS = 8
DSA_HEAD_DIM = 128
DSA_ROT = DSA_HEAD_DIM // 4
IDX_HEADS = 16
IDX_DIM = 64
IDX_ROT = IDX_DIM // 4
TOPK_MAX = 256
X_HEADS = 4
X_HEAD_DIM = 128
D_FF = 4 * D_MODEL

LANES = 128
W_CQ = 0
W_CKV = W_CQ + MLA_Q_LORA
W_DSA = W_CKV + MLA_KV_LORA + LANES
W_IDX = W_DSA + 3 * DSA_HEADS * DSA_HEAD_DIM
W_END = W_IDX + IDX_HEADS * IDX_DIM + LANES
MLA_QK_PAD = 256
NEG = -1e30
INT_MIN = -2 ** 31
LOG2E = 1.4426950408889634
VMEM_LIMIT = 56 * 1024 * 1024


def _params(sem, vmem=VMEM_LIMIT):
    return pltpu.CompilerParams(dimension_semantics=sem, vmem_limit_bytes=vmem)


def _rms(xf, g):
    return xf * lax.rsqrt(jnp.mean(xf * xf, axis=-1, keepdims=True) + EPS) * g


def _apply_rope(x, coeffs, half):
    c, s_lo, s_hi = coeffs
    return x * c + pltpu.roll(x, LANES - half, 1) * s_lo + pltpu.roll(x, half, 1) * s_hi


def _dot(a, b):
    return jnp.dot(a, b, preferred_element_type=F32)


def _dot_nt(a, b):
    return lax.dot_general(a, b, (((1,), (1,)), ((), ())), preferred_element_type=F32)


ROPE_VARIANTS = ((MLA_ROPE // 2, LANES), (DSA_ROT // 2, LANES), (IDX_ROT // 2, IDX_DIM))


def _rope_lane_plan():
    offs, o = [], 0
    for half, _ in ROPE_VARIANTS:
        offs.append(o)
        o += 2 * half
    assert o <= LANES
    return offs


def _rope_coeffs(pos, invf):
    ang = pos.astype(F32) * invf
    cos, sin = jnp.cos(ang), jnp.sin(ang)
    lane = lax.broadcasted_iota(I32, (1, LANES), 1)
    coeffs = []
    for (half, period), off in zip(ROPE_VARIANTS, _rope_lane_plan()):
        c = jnp.ones(cos.shape, F32)
        s_lo = jnp.zeros(cos.shape, F32)
        s_hi = jnp.zeros(cos.shape, F32)
        for base in range(0, LANES, period):
            shift = (base - off) % LANES
            cs = cos if shift == 0 else pltpu.roll(cos, shift, 1)
            sn = sin if shift == 0 else pltpu.roll(sin, shift, 1)
            in_lo = (lane >= base) & (lane < base + half)
            in_hi = (lane >= base + half) & (lane < base + 2 * half)
            c = jnp.where(in_lo | in_hi, cs, c)
            s_lo = jnp.where(in_lo, -sn, s_lo)
            s_hi = jnp.where(in_hi, sn, s_hi)
        coeffs.append((c, s_lo, s_hi))
    return coeffs


def _proj_kernel(x_ref, pos_ref, g_ref, invf_ref, w_ref, gcq_ref, wqb_ref, gckv_ref, wk_ref, wvt_ref, wdsa_vt_ref,
                 q_ref, k_ref, vt_ref, qk_d_ref, vt_d_ref, qi_ref, ki_ref, wi_ref):
    h = _rms(x_ref[...], g_ref[...]).astype(BF16)
    rope = _rope_coeffs(pos_ref[...], invf_ref[...])
    lane = lax.broadcasted_iota(I32, (1, LANES), 1)

    cq = _dot(h, w_ref[:, W_CQ:W_CKV])
    r = _dot(h, w_ref[:, W_CKV:W_DSA])
    q = _dot(_rms(cq, gcq_ref[...]).astype(BF16), wqb_ref[...])
    ckv = _rms(r[:, :MLA_KV_LORA], gckv_ref[...]).astype(BF16)
    kn = _dot(ckv, wk_ref[...])
    vt_ref[...] = _dot_nt(wvt_ref[...], ckv).astype(BF16)
    scale = (MLA_NOPE + MLA_ROPE) ** -0.5 * LOG2E
    kr = _apply_rope(r[:, MLA_KV_LORA:], rope[0], MLA_ROPE // 2).astype(BF16)
    for hh in range(MLA_HEADS):
        b0 = hh * MLA_QK_PAD
        q_ref[:, b0:b0 + LANES] = (q[:, b0:b0 + LANES] * scale).astype(BF16)
        qr = _apply_rope(q[:, b0 + LANES:b0 + 2 * LANES], rope[0], MLA_ROPE // 2)
        q_ref[:, b0 + LANES:b0 + 2 * LANES] = (qr * scale).astype(BF16)
        k_ref[:, b0:b0 + LANES] = kn[:, hh * LANES:(hh + 1) * LANES].astype(BF16)
        k_ref[:, b0 + LANES:b0 + 2 * LANES] = kr

    dsa_w = DSA_HEADS * DSA_HEAD_DIM
    for g, scale in enumerate((DSA_HEAD_DIM ** -0.5 * LOG2E, 1.0)):
        rd = _dot(h, w_ref[:, W_DSA + g * dsa_w:W_DSA + (g + 1) * dsa_w])
        for hh in range(DSA_HEADS):
            sl = slice(hh * LANES, (hh + 1) * LANES)
            qk_d_ref[g, :, sl] = (_apply_rope(rd[:, sl], rope[1], DSA_ROT // 2) * scale).astype(BF16)
    vt_d_ref[...] = _dot_nt(wdsa_vt_ref[...], h).astype(BF16)

    ri = _dot(h, w_ref[:, W_IDX:W_END])
    nq = IDX_HEADS * IDX_DIM
    tail = ri[:, nq:nq + LANES]
    tail_hi = pltpu.roll(tail, LANES - IDX_DIM, 1)
    for p in range(IDX_HEADS // 2):
        t = _apply_rope(ri[:, p * LANES:(p + 1) * LANES], rope[2], IDX_ROT // 2) * (IDX_DIM ** -0.5)
        qi_ref[:, (2 * p) * LANES:(2 * p + 1) * LANES] = jnp.where(lane < IDX_DIM, t, 0.0).astype(BF16)
        qi_ref[:, (2 * p + 1) * LANES:(2 * p + 2) * LANES] = jnp.where(lane >= IDX_DIM, t, 0.0).astype(BF16)
    k_dup = jnp.where(lane < IDX_DIM, tail, pltpu.roll(tail, IDX_DIM, 1))
    ki_ref[...] = _apply_rope(k_dup, rope[2], IDX_ROT // 2).astype(BF16)
    wi_ref[...] = jnp.where(lane < IDX_HEADS, tail_hi, 0.0) * (IDX_HEADS ** -0.5)


SUM_ROWS = 16


def _flash_sweep_t(n_tiles, tk, heads, lead, qk, vt, adjust, adjust_last, m_scr, acc_scr):
    m_scr[...] = jnp.full(m_scr.shape, NEG, F32)
    acc_scr[...] = jnp.zeros(acc_scr.shape, F32)
    ones = jnp.ones((SUM_ROWS, tk), BF16)

    def scores(h, off, adj):
        s_t = adj(qk(h, off))
        return s_t, jnp.max(s_t, axis=0, keepdims=True)

    def finish(h, s_and_max, off):
        s_t, m_tile = s_and_max
        m_prev = m_scr[h]
        m_new = jnp.maximum(m_prev, m_tile)
        p = jnp.exp2(s_t - m_new).astype(BF16)
        pv = _dot(jnp.concatenate([vt(h, off), ones], axis=0), p)
        acc_scr[h] = jnp.exp2(m_prev - m_new) * acc_scr[h] + pv
        m_scr[h] = m_new

    def tile(off, adjust_tile):
        adj = adjust_tile(off)
        s_ts = [scores(h, off, adj) for h in range(min(lead, heads))]
        for h in range(heads):
            finish(h, s_ts[h], off)
            if h + lead < heads:
                s_ts.append(scores(h + lead, off, adj))

    def body(j, c):
        tile(pl.multiple_of(j * tk, tk), adjust)
        return c

    lax.fori_loop(0, n_tiles - 1, body, 0)
    tile(pl.multiple_of((n_tiles - 1) * tk, tk), adjust_last)


def _attn_out_t(acc, dv):
    return (acc[:dv] / acc[dv:dv + 1]).T.astype(BF16)


def _mla_attn_kernel(q_ref, k_ref, vt_ref, wu_ref, wd_ref, o_ref, wu_bf_ref, wd_bf_ref, m_scr, acc_scr,
                     *, tq, tk, hp):
    wu_bf_ref[...] = wu_ref[...].astype(BF16)
    wd_bf_ref[...] = wd_ref[...].astype(BF16)
    i = pl.program_id(2)
    n_tiles = (i * tq) // tk + 1
    qry_chunk = (i * tq + lax.broadcasted_iota(I32, (1, tq), 1)) // CHUNK

    def qk(h, off):
        qs = slice(h * MLA_QK_PAD, (h + 1) * MLA_QK_PAD)
        return _dot_nt(k_ref[pl.ds(off, tk), qs], q_ref[:, qs])

    def vt(h, off):
        return vt_ref[h * MLA_V:(h + 1) * MLA_V, pl.ds(off, tk)]

    def causal(off):
        key_chunk = (off + lax.broadcasted_iota(I32, (tk, 1), 0)) // CHUNK
        return lambda s_t: jnp.where(key_chunk <= qry_chunk, s_t, NEG)

    _flash_sweep_t(n_tiles, tk, hp, hp, qk, vt, lambda off: (lambda s_t: s_t), causal, m_scr, acc_scr)
    for h in range(hp):
        o_ref[:, h * MLA_V:(h + 1) * MLA_V] = _attn_out_t(acc_scr[h], MLA_V)


def _dsa_kernel(q_ref, k_ref, vt_ref, qi_ref, ki_ref, wi_ref, o_ref,
                sc_scr, sb_scr, bias_scr, cnt_scr, cnt16_scr, m_scr, acc_scr, *, tq, tk, n_sel, seq):
    i = pl.program_id(1)
    n_valid = ((i + 1) * tq + tk - 1) // tk
    qry_chunk = (i * tq + lax.broadcasted_iota(I32, (1, tq), 1)) // CHUNK

    def key_ids(off):
        return off + lax.broadcasted_iota(I32, (tk, 1), 0)

    wi_t = wi_ref[...].T

    def score_tile(j, c):
        off = pl.multiple_of(j * tk, tk)
        kj = ki_ref[pl.ds(off, tk), :]
        sc = jnp.zeros((tk, tq), F32)
        for hh in range(IDX_HEADS):
            lg = _dot_nt(kj, qi_ref[:, hh * LANES:(hh + 1) * LANES])
            sc = sc + jnp.maximum(lg, 0.0) * wi_t[hh:hh + 1, :]
        sc = jnp.where(key_ids(off) // CHUNK <= qry_chunk, sc, -jnp.inf)
        sc_scr[pl.ds(off, tk), :] = sc
        sb_scr[pl.ds(off, tk), :] = sc.astype(BF16)
        return c

    lax.fori_loop(0, n_valid, score_tile, 0)

    sub = cnt_scr.shape[0]

    def count_bf16(cand):
        cnt16_scr[...] = jnp.zeros(cnt16_scr.shape, jnp.int16)

        def body(j, c):
            off = pl.multiple_of(j * tk, tk)
            hit = jnp.where(sb_scr[pl.ds(off, tk), :] >= cand, jnp.int16(1), jnp.int16(0))
            part = hit[:sub]
            for cc in range(1, tk // sub):
                part = part + hit[cc * sub:(cc + 1) * sub]
            cnt16_scr[...] += part
            return c

        lax.fori_loop(0, n_valid, body, 0)
        return jnp.sum(cnt16_scr[...].astype(I32), axis=0, keepdims=True)

    def count(pred):
        cnt_scr[...] = jnp.zeros(cnt_scr.shape, I32)

        def body(j, c):
            off = pl.multiple_of(j * tk, tk)
            hit = jnp.where(pred(sc_scr[pl.ds(off, tk), :], off), 1, 0).astype(I32)
            part = hit[:sub]
            for cc in range(1, tk // sub):
                part = part + hit[cc * sub:(cc + 1) * sub]
            cnt_scr[...] += part
            return c

        lax.fori_loop(0, n_valid, body, 0)
        return jnp.sum(cnt_scr[...], axis=0, keepdims=True)

    def key_to_float(key):
        return pltpu.bitcast(key ^ ((key >> 31) & I32(0x7FFFFFFF)), F32)

    half = 2 ** 15
    neg_inf_key = 0x807FFFFF - 2 ** 32

    def coarse_key(key16):
        return (key16 << 16) + ((key16 >> 31) & I32(0xFFFF))

    def coarse_bit(it, tb):
        cand_b = tb | (I32(1) << (15 - it))
        cand = key_to_float(coarse_key(cand_b - half)).astype(BF16)
        cnt = count_bf16(cand)
        return jnp.where(cnt >= n_sel, cand_b, tb)

    v_key16 = jnp.maximum(lax.fori_loop(0, 16, coarse_bit, jnp.zeros((1, tq), I32)) - half, neg_inf_key >> 16)
    lo_key = coarse_key(v_key16) - half

    def fine_bit(it, off):
        cand_off = off | (I32(1) << (16 - it))
        cand = key_to_float(lo_key + cand_off)
        cnt = count(lambda sc, o: sc >= cand)
        return jnp.where(cnt >= n_sel, cand_off, off)

    thr_key = lo_key + lax.fori_loop(0, 17, fine_bit, jnp.zeros((1, tq), I32))
    thr = key_to_float(jnp.maximum(thr_key, neg_inf_key))

    cnt_gt = count(lambda sc, off: sc > thr)
    cnt_ge = count(lambda sc, off: sc >= thr)
    need = n_sel - cnt_gt
    tied = (cnt_ge > n_sel) & (thr > -jnp.inf)
    any_tied = jnp.max(tied.astype(I32)) > 0
    idx_bits = (2 * seq - 1).bit_length()

    def tie_cut():
        def cut_bit(it, jc):
            cand = jc | (I32(1) << (idx_bits - 1 - it))
            cnt = count(lambda sc, off: (sc == thr) & (key_ids(off) < cand))
            return jnp.where(cnt <= need, cand, jc)
        return lax.fori_loop(0, idx_bits, cut_bit, jnp.zeros((1, tq), I32))

    jcut = lax.cond(any_tied, tie_cut, lambda: jnp.full((1, tq), 2 ** idx_bits - 1, I32))

    def write_bias(off, sel, causal):
        if causal:
            sel = sel & (key_ids(off) // CHUNK <= qry_chunk)
        bias_scr[pl.ds(off, tk), :] = jnp.where(sel, 0.0, NEG).astype(F32)

    def sel_plain(off):
        return sc_scr[pl.ds(off, tk), :] >= thr

    def sel_general(off):
        sc = sc_scr[pl.ds(off, tk), :]
        return (sc > thr) | ((sc == thr) & (key_ids(off) < jcut))

    off_last = pl.multiple_of((n_valid - 1) * tk, tk)
    for pred, sel_fn in ((any_tied, sel_general), (jnp.logical_not(any_tied), sel_plain)):
        @pl.when(pred)
        def _(sel_fn=sel_fn):
            def bias_tile(j, c):
                off = pl.multiple_of(j * tk, tk)
                write_bias(off, sel_fn(off), causal=False)
                return c

            lax.fori_loop(0, n_valid - 1, bias_tile, 0)
            write_bias(off_last, sel_fn(off_last), causal=True)

    hp = m_scr.shape[0]

    def biased(off):
        bias = bias_scr[pl.ds(off, tk), :]
        return lambda s_t: s_t + bias

    for g in range(DSA_HEADS // hp):
        def head_cols(h, g=g):
            return slice((g * hp + h) * DSA_HEAD_DIM, (g * hp + h + 1) * DSA_HEAD_DIM)

        def qk(h, off):
            return _dot_nt(k_ref[pl.ds(off, tk), head_cols(h)], q_ref[:, head_cols(h)])

        def vt(h, off):
            return vt_ref[head_cols(h), pl.ds(off, tk)]

        _flash_sweep_t(n_valid, tk, hp, hp // 2, qk, vt, biased, biased, m_scr, acc_scr)
        for h in range(hp):
            o_ref[:, head_cols(h)] = _attn_out_t(acc_scr[h], DSA_HEAD_DIM)


def _mem_kv_kernel(mem_ref, g_ref, wk_ref, wv_ref, k_ref, v_ref):
    mn = _rms(mem_ref[...], g_ref[...]).astype(BF16)
    k_ref[...] = _dot(mn, wk_ref[...]).astype(BF16)
    v_ref[...] = _dot(mn, wv_ref[...]).astype(BF16)


def _post_kernel(a_ref, b_ref, x_ref, wout_ref, gc_ref, wq_ref, km_ref, vm_ref, wo_ref, gm_ref,
                 x2_ref, hm_ref):
    na = a_ref.shape[1]
    x1 = x_ref[...] + _dot(a_ref[...], wout_ref[:na]) + _dot(b_ref[...], wout_ref[na:])
    hc = _rms(x1, gc_ref[...]).astype(BF16)
    qc = (_dot(hc, wq_ref[...]) * (X_HEAD_DIM ** -0.5)).astype(BF16)
    outs = []
    for hh in range(X_HEADS):
        sl = slice(hh * X_HEAD_DIM, (hh + 1) * X_HEAD_DIM)
        s = _dot_nt(qc[:, sl], km_ref[:, sl])
        p = jnp.exp(s - jnp.max(s, axis=-1, keepdims=True))
        o = _dot(p.astype(BF16), vm_ref[:, sl]) / jnp.sum(p, axis=-1, keepdims=True)
        outs.append(o.astype(BF16))
    x2 = x1 + _dot(jnp.concatenate(outs, axis=-1), wo_ref[...])
    x2_ref[...] = x2
    hm_ref[...] = _rms(x2, gm_ref[...]).astype(BF16)


def _mlp_kernel(hm_ref, x2_ref, wu_ref, wd_ref, gf_ref, y_ref):
    f = pl.program_id(1)

    @pl.when(f == 0)
    def _():
        y_ref[...] = x2_ref[...]

    u = jnp.maximum(_dot(hm_ref[...], wu_ref[...]), 0.0)
    y_ref[...] += _dot((u * u).astype(BF16), wd_ref[...])

    @pl.when(f == pl.num_programs(1) - 1)
    def _():
        y_ref[...] = _rms(y_ref[...], gf_ref[...])


def _const(shape):
    return pl.BlockSpec(shape, lambda *_: (0,) * len(shape), pipeline_mode=pl.Buffered(1))


def kernel(x, mem, positions, g_mix, w_in, g_cq, g_ckv, w_qb, w_kvb, w_out, g_cross, g_mem,
           w_q_cross, w_k_cross, w_v_cross, w_o_cross, g_mlp, w_up, w_down, g_final):
    assert w_in.shape[0] == 1, "one layer"
    batch, seq, d = x.shape
    t = batch * seq
    n_sel = min(TOPK_MAX, seq // 4)
    x2d = x.reshape(t, d)
    pos = positions.reshape(t, 1)
    row = lambda g: g.reshape(1, -1).astype(F32)

    w = w_in[0]
    o = [0]
    for n in (MLA_Q_LORA, MLA_KV_LORA, MLA_ROPE, DSA_HEADS * DSA_HEAD_DIM, DSA_HEADS * DSA_HEAD_DIM,
              DSA_HEADS * DSA_HEAD_DIM, IDX_HEADS * IDX_DIM, IDX_DIM, IDX_HEADS):
        o.append(o[-1] + n)
    zeros = lambda n: jnp.zeros((d, n), w.dtype)
    w_pad = jnp.concatenate([w[:, :o[3]], zeros(LANES - MLA_ROPE), w[:, o[3]:],
                             zeros(W_END - o[9] - (LANES - MLA_ROPE))], axis=1).astype(BF16)
    w_dsa_vt = w[:, o[5]:o[6]].T.astype(BF16)
    wqb = w_qb[0].reshape(MLA_Q_LORA, MLA_HEADS, MLA_NOPE + MLA_ROPE)
    wqb = jnp.pad(wqb, ((0, 0), (0, 0), (0, MLA_QK_PAD - MLA_NOPE - MLA_ROPE)))
    wqb = wqb.reshape(MLA_Q_LORA, MLA_HEADS * MLA_QK_PAD).astype(BF16)
    wkvb = w_kvb[0].reshape(MLA_KV_LORA, MLA_HEADS, MLA_NOPE + MLA_V)
    wkvb_k = wkvb[:, :, :MLA_NOPE].reshape(MLA_KV_LORA, -1).astype(BF16)
    wkvb_vt = wkvb[:, :, MLA_NOPE:].reshape(MLA_KV_LORA, -1).T.astype(BF16)
    mla_w = MLA_HEADS * MLA_V
    wq_c, wk_c, wv_c, wo_c = (a[0].astype(BF16) for a in (w_q_cross, w_k_cross, w_v_cross, w_o_cross))
    invf = jnp.zeros((LANES,), F32)
    for (half, _), off in zip(ROPE_VARIANTS, _rope_lane_plan()):
        inv_freq = ROPE_THETA ** (-jnp.arange(half, dtype=F32) / half)
        invf = invf.at[off:off + 2 * half].set(jnp.concatenate([inv_freq, inv_freq]))
    invf = invf.reshape(1, LANES)

    qk_w = MLA_HEADS * MLA_QK_PAD
    dsa_w = DSA_HEADS * DSA_HEAD_DIM
    qi_w = IDX_HEADS * LANES
    tpj = 256
    rows = lambda w: pl.BlockSpec((tpj, w), lambda i: (i, 0))
    cols = lambda w: pl.BlockSpec((w, tpj), lambda i: (0, i))
    q_mla, k_mla, vt_mla, qk_d, vt_d, q_idx, k_idx, w_idx_s = pl.pallas_call(
        _proj_kernel, name="proj", grid=(t // tpj,),
        in_specs=[rows(d), rows(1), _const((1, d)), _const((1, LANES)),
                  _const((d, W_END)), _const((1, MLA_Q_LORA)), _const((MLA_Q_LORA, qk_w)),
                  _const((1, MLA_KV_LORA)), _const((MLA_KV_LORA, mla_w)), _const((mla_w, MLA_KV_LORA)),
                  _const((dsa_w, d))],
        out_specs=[rows(qk_w), rows(qk_w), cols(mla_w),
                   pl.BlockSpec((2, tpj, dsa_w), lambda i: (0, i, 0)), cols(dsa_w),
                   rows(qi_w), rows(LANES), rows(LANES)],
        out_shape=[jax.ShapeDtypeStruct((t, qk_w), BF16), jax.ShapeDtypeStruct((t, qk_w), BF16),
                   jax.ShapeDtypeStruct((mla_w, t), BF16),
                   jax.ShapeDtypeStruct((2, t, dsa_w), BF16), jax.ShapeDtypeStruct((dsa_w, t), BF16),
                   jax.ShapeDtypeStruct((t, qi_w), BF16), jax.ShapeDtypeStruct((t, LANES), BF16),
                   jax.ShapeDtypeStruct((t, LANES), F32)],
        compiler_params=_params(("parallel",)),
    )(x2d, pos, row(g_mix), invf, w_pad, row(g_cq), wqb, row(g_ckv), wkvb_k, wkvb_vt, w_dsa_vt)

    tq, tk, hp = 256, 512, 8
    mla_grid = (batch, MLA_HEADS // hp, seq // tq)
    n_steps = mla_grid[0] * mla_grid[1] * mla_grid[2]
    step = lambda b, h, i: (b * mla_grid[1] + h) * mla_grid[2] + i
    wu_rows, wd_rows = d // n_steps, D_FF // n_steps
    a_mla, wu, wd = pl.pallas_call(
        functools.partial(_mla_attn_kernel, tq=tq, tk=tk, hp=hp), name="mla_attn", grid=mla_grid,
        in_specs=[pl.BlockSpec((None, tq, hp * MLA_QK_PAD), lambda b, h, i: (b, i, h)),
                  pl.BlockSpec((None, seq, hp * MLA_QK_PAD), lambda b, h, i: (b, 0, h),
                               pipeline_mode=pl.Buffered(1)),
                  pl.BlockSpec((hp * MLA_V, seq), lambda b, h, i: (h, b), pipeline_mode=pl.Buffered(1)),
                  pl.BlockSpec((None, wu_rows, D_FF), lambda b, h, i: (0, step(b, h, i), 0)),
                  pl.BlockSpec((None, wd_rows, d), lambda b, h, i: (0, step(b, h, i), 0))],
        out_specs=[pl.BlockSpec((None, tq, hp * MLA_V), lambda b, h, i: (b, i, h)),
                   pl.BlockSpec((wu_rows, D_FF), lambda b, h, i: (step(b, h, i), 0)),
                   pl.BlockSpec((wd_rows, d), lambda b, h, i: (step(b, h, i), 0))],
        out_shape=[jax.ShapeDtypeStruct((batch, seq, mla_w), BF16),
                   jax.ShapeDtypeStruct((d, D_FF), BF16), jax.ShapeDtypeStruct((D_FF, d), BF16)],
        scratch_shapes=[pltpu.VMEM((hp, 1, tq), F32), pltpu.VMEM((hp, MLA_V + SUM_ROWS, tq), F32)],
        compiler_params=_params(("parallel", "parallel", "arbitrary")),
    )(q_mla.reshape(batch, seq, qk_w), k_mla.reshape(batch, seq, qk_w), vt_mla, w_up, w_down)

    qk4 = qk_d.reshape(2, batch, seq, dsa_w)
    tq, tk, hp = 256, 512, 8
    b_dsa = pl.pallas_call(
        functools.partial(_dsa_kernel, tq=tq, tk=tk, n_sel=n_sel, seq=seq), name="dsa",
        grid=(batch, seq // tq),
        in_specs=[pl.BlockSpec((None, None, tq, dsa_w), lambda b, i: (0, b, i, 0)),
                  pl.BlockSpec((None, None, seq, dsa_w), lambda b, i: (1, b, 0, 0), pipeline_mode=pl.Buffered(1)),
                  pl.BlockSpec((dsa_w, seq), lambda b, i: (0, b), pipeline_mode=pl.Buffered(1)),
                  pl.BlockSpec((None, tq, qi_w), lambda b, i: (b, i, 0)),
                  pl.BlockSpec((None, seq, LANES), lambda b, i: (b, 0, 0)),
                  pl.BlockSpec((None, tq, LANES), lambda b, i: (b, i, 0))],
        out_specs=pl.BlockSpec((None, tq, dsa_w), lambda b, i: (b, i, 0)),
        out_shape=jax.ShapeDtypeStruct((batch, seq, dsa_w), BF16),
        scratch_shapes=[pltpu.VMEM((seq, tq), F32), pltpu.VMEM((seq, tq), BF16), pltpu.VMEM((seq, tq), F32),
                        pltpu.VMEM((32, tq), I32), pltpu.VMEM((32, tq), jnp.int16),
                        pltpu.VMEM((hp, 1, tq), F32), pltpu.VMEM((hp, DSA_HEAD_DIM + SUM_ROWS, tq), F32)],
        compiler_params=_params(("parallel", "arbitrary")),
    )(qk4, qk4, vt_d, q_idx.reshape(batch, seq, qi_w), k_idx.reshape(batch, seq, LANES),
      w_idx_s.reshape(batch, seq, LANES))

    xw = X_HEADS * X_HEAD_DIM
    n_mem = mem.shape[1]
    k_mem, v_mem = pl.pallas_call(
        _mem_kv_kernel, name="mem_kv", grid=(batch,),
        in_specs=[pl.BlockSpec((None, n_mem, d), lambda b: (b, 0, 0)), _const((1, d)), _const((d, xw)),
                  _const((d, xw))],
        out_specs=[pl.BlockSpec((None, n_mem, xw), lambda b: (b, 0, 0))] * 2,
        out_shape=[jax.ShapeDtypeStruct((batch, n_mem, xw), BF16)] * 2,
        compiler_params=_params(("parallel",)),
    )(mem, row(g_mem), wk_c, wv_c)

    tp = 512
    per_b = seq // tp
    mem_spec = pl.BlockSpec((None, n_mem, xw), lambda i: (i // per_b, 0, 0))
    x2, hm = pl.pallas_call(
        _post_kernel, name="post", grid=(t // tp,),
        in_specs=[pl.BlockSpec((tp, mla_w), lambda i: (i, 0)), pl.BlockSpec((tp, dsa_w), lambda i: (i, 0)),
                  pl.BlockSpec((tp, d), lambda i: (i, 0)), _const((mla_w + dsa_w, d)), _const((1, d)),
                  _const((d, xw)), mem_spec, mem_spec, _const((xw, d)), _const((1, d))],
        out_specs=[pl.BlockSpec((tp, d), lambda i: (i, 0))] * 2,
        out_shape=[jax.ShapeDtypeStruct((t, d), F32), jax.ShapeDtypeStruct((t, d), BF16)],
        compiler_params=_params(("parallel",)),
    )(a_mla.reshape(t, mla_w), b_dsa.reshape(t, dsa_w), x2d, w_out[0].astype(BF16), row(g_cross), wq_c,
      k_mem, v_mem, wo_c, row(g_mlp))

    tmm, tf = 512, 1024
    y = pl.pallas_call(
        _mlp_kernel, name="mlp", grid=(t // tmm, D_FF // tf),
        in_specs=[pl.BlockSpec((tmm, d), lambda i, f: (i, 0)), pl.BlockSpec((tmm, d), lambda i, f: (i, 0)),
                  pl.BlockSpec((d, tf), lambda i, f: (0, f)), pl.BlockSpec((tf, d), lambda i, f: (f, 0)),
                  _const((1, d))],
        out_specs=pl.BlockSpec((tmm, d), lambda i, f: (i, 0)),
        out_shape=jax.ShapeDtypeStruct((t, d), F32),
        compiler_params=_params(("parallel", "arbitrary")),
    )(hm, x2, wu, wd, row(g_final))
    return y.reshape(batch, seq, d)
```

```python
import functools

import jax
import jax.numpy as jnp
from jax import lax
from jax.experimental import pallas as pl
from jax.experimental.pallas import tpu as pltpu

F32 = jnp.float32
BF16 = jnp.bfloat16
I32 = jnp.int32

D_MODEL = 2048
CHUNK = 64
ROPE_THETA = 500000.0
N_MEM = 256
EPS = 1e-6
MLA_HEADS = 8
MLA_NOPE = 128
MLA_ROPE = 64
MLA_V = 128
MLA_Q_LORA = 512
MLA_KV_LORA = 256
DSA_HEADS = 8
DSA_HEAD_DIM = 128
DSA_ROT = DSA_HEAD_DIM // 4
IDX_HEADS = 16
IDX_DIM = 64
IDX_ROT = IDX_DIM // 4
TOPK_MAX = 256
X_HEADS = 4
X_HEAD_DIM = 128
D_FF = 4 * D_MODEL

LANES = 128
W_CQ = 0
W_CKV = W_CQ + MLA_Q_LORA
W_DSA = W_CKV + MLA_KV_LORA + LANES
W_IDX = W_DSA + 3 * DSA_HEADS * DSA_HEAD_DIM
W_END = W_IDX + IDX_HEADS * IDX_DIM + LANES
MLA_QK_PAD = 256
NEG = -1e30
INT_MIN = -2 ** 31
LOG2E = 1.4426950408889634
VMEM_LIMIT = 56 * 1024 * 1024


def _params(sem, vmem=VMEM_LIMIT):
    return pltpu.CompilerParams(dimension_semantics=sem, vmem_limit_bytes=vmem)


def _rms(xf, g):
    return xf * lax.rsqrt(jnp.mean(xf * xf, axis=-1, keepdims=True) + EPS) * g


def _apply_rope(x, coeffs, half):
    c, s_lo, s_hi = coeffs
    return x * c + pltpu.roll(x, LANES - half, 1) * s_lo + pltpu.roll(x, half, 1) * s_hi


def _dot(a, b):
    return jnp.dot(a, b, preferred_element_type=F32)


def _dot_nt(a, b):
    return lax.dot_general(a, b, (((1,), (1,)), ((), ())), preferred_element_type=F32)


ROPE_VARIANTS = ((MLA_ROPE // 2, LANES), (DSA_ROT // 2, LANES), (IDX_ROT // 2, IDX_DIM))


def _rope_lane_plan():
    offs, o = [], 0
    for half, _ in ROPE_VARIANTS:
        offs.append(o)
        o += 2 * half
    assert o <= LANES
    return offs


def _rope_coeffs(pos, invf):
    ang = pos.astype(F32) * invf
    cos, sin = jnp.cos(ang), jnp.sin(ang)
    lane = lax.broadcasted_iota(I32, (1, LANES), 1)
    coeffs = []
    for (half, period), off in zip(ROPE_VARIANTS, _rope_lane_plan()):
        c = jnp.ones(cos.shape, F32)
        s_lo = jnp.zeros(cos.shape, F32)
        s_hi = jnp.zeros(cos.shape, F32)
        for base in range(0, LANES, period):
            shift = (base - off) % LANES
            cs = cos if shift == 0 else pltpu.roll(cos, shift, 1)
            sn = sin if shift == 0 else pltpu.roll(sin, shift, 1)
            in_lo = (lane >= base) & (lane < base + half)
            in_hi = (lane >= base + half) & (lane < base + 2 * half)
            c = jnp.where(in_lo | in_hi, cs, c)
            s_lo = jnp.where(in_lo, -sn, s_lo)
            s_hi = jnp.where(in_hi, sn, s_hi)
        coeffs.append((c, s_lo, s_hi))
    return coeffs


def _proj_kernel(x_ref, pos_ref, g_ref, invf_ref, w_ref, gcq_ref, wqb_ref, gckv_ref, wk_ref, wvt_ref, wdsa_vt_ref,
                 q_ref, k_ref, vt_ref, qk_d_ref, vt_d_ref, qi_ref, ki_ref, wi_ref):
    h = _rms(x_ref[...], g_ref[...]).astype(BF16)
    rope = _rope_coeffs(pos_ref[...], invf_ref[...])
    lane = lax.broadcasted_iota(I32, (1, LANES), 1)

    cq = _dot(h, w_ref[:, W_CQ:W_CKV])
    r = _dot(h, w_ref[:, W_CKV:W_DSA])
    q = _dot(_rms(cq, gcq_ref[...]).astype(BF16), wqb_ref[...])
    ckv = _rms(r[:, :MLA_KV_LORA], gckv_ref[...]).astype(BF16)
    kn = _dot(ckv, wk_ref[...])
    vt_ref[...] = _dot_nt(wvt_ref[...], ckv).astype(BF16)
    scale = (MLA_NOPE + MLA_ROPE) ** -0.5 * LOG2E
    kr = _apply_rope(r[:, MLA_KV_LORA:], rope[0], MLA_ROPE // 2).astype(BF16)
    for hh in range(MLA_HEADS):
        b0 = hh * MLA_QK_PAD
        q_ref[:, b0:b0 + LANES] = (q[:, b0:b0 + LANES] * scale).astype(BF16)
        qr = _apply_rope(q[:, b0 + LANES:b0 + 2 * LANES], rope[0], MLA_ROPE // 2)
        q_ref[:, b0 + LANES:b0 + 2 * LANES] = (qr * scale).astype(BF16)
        k_ref[:, b0:b0 + LANES] = kn[:, hh * LANES:(hh + 1) * LANES].astype(BF16)
        k_ref[:, b0 + LANES:b0 + 2 * LANES] = kr

    dsa_w = DSA_HEADS * DSA_HEAD_DIM
    for g, scale in enumerate((DSA_HEAD_DIM ** -0.5 * LOG2E, 1.0)):
        rd = _dot(h, w_ref[:, W_DSA + g * dsa_w:W_DSA + (g + 1) * dsa_w])
        for hh in range(DSA_HEADS):
            sl = slice(hh * LANES, (hh + 1) * LANES)
            qk_d_ref[g, :, sl] = (_apply_rope(rd[:, sl], rope[1], DSA_ROT // 2) * scale).astype(BF16)
    vt_d_ref[...] = _dot_nt(wdsa_vt_ref[...], h).astype(BF16)

    ri = _dot(h, w_ref[:, W_IDX:W_END])
    nq = IDX_HEADS * IDX_DIM
    tail = ri[:, nq:nq + LANES]
    tail_hi = pltpu.roll(tail, LANES - IDX_DIM, 1)
    for p in range(IDX_HEADS // 2):
        t = _apply_rope(ri[:, p * LANES:(p + 1) * LANES], rope[2], IDX_ROT // 2) * (IDX_DIM ** -0.5)
        qi_ref[:, (2 * p) * LANES:(2 * p + 1) * LANES] = jnp.where(lane < IDX_DIM, t, 0.0).astype(BF16)
        qi_ref[:, (2 * p + 1) * LANES:(2 * p + 2) * LANES] = jnp.where(lane >= IDX_DIM, t, 0.0).astype(BF16)
    k_dup = jnp.where(lane < IDX_DIM, tail, pltpu.roll(tail, IDX_DIM, 1))
    ki_ref[...] = _apply_rope(k_dup, rope[2], IDX_ROT // 2).astype(BF16)
    wi_ref[...] = jnp.where(lane < IDX_HEADS, tail_hi, 0.0) * (IDX_HEADS ** -0.5)


SUM_ROWS = 16


def _flash_sweep_t(n_tiles, tk, heads, lead, qk, vt, adjust, adjust_last, m_scr, acc_scr):
    m_scr[...] = jnp.full(m_scr.shape, NEG, F32)
    acc_scr[...] = jnp.zeros(acc_scr.shape, F32)
    ones = jnp.ones((SUM_ROWS, tk), BF16)

    def scores(h, off, adj):
        s_t = adj(qk(h, off))
        return s_t, jnp.max(s_t, axis=0, keepdims=True)

    def finish(h, s_and_max, off):
        s_t, m_tile = s_and_max
        m_prev = m_scr[h]
        m_new = jnp.maximum(m_prev, m_tile)
        p = jnp.exp2(s_t - m_new).astype(BF16)
        pv = _dot(jnp.concatenate([vt(h, off), ones], axis=0), p)
        acc_scr[h] = jnp.exp2(m_prev - m_new) * acc_scr[h] + pv
        m_scr[h] = m_new

    def tile(off, adjust_tile):
        adj = adjust_tile(off)
        s_ts = [scores(h, off, adj) for h in range(min(lead, heads))]
        for h in range(heads):
            finish(h, s_ts[h], off)
            if h + lead < heads:
                s_ts.append(scores(h + lead, off, adj))

    def body(j, c):
        tile(pl.multiple_of(j * tk, tk), adjust)
        return c

    lax.fori_loop(0, n_tiles - 1, body, 0)
    tile(pl.multiple_of((n_tiles - 1) * tk, tk), adjust_last)


def _attn_out_t(acc, dv):
    return (acc[:dv] / acc[dv:dv + 1]).T.astype(BF16)


def _mla_attn_kernel(q_ref, k_ref, vt_ref, wu_ref, wd_ref, o_ref, wu_bf_ref, wd_bf_ref, m_scr, acc_scr,
                     *, tq, tk, hp):
    wu_bf_ref[...] = wu_ref[...].astype(BF16)
    wd_bf_ref[...] = wd_ref[...].astype(BF16)
    i = pl.program_id(2)
    n_tiles = (i * tq) // tk + 1
    qry_chunk = (i * tq + lax.broadcasted_iota(I32, (1, tq), 1)) // CHUNK

    def qk(h, off):
        qs = slice(h * MLA_QK_PAD, (h + 1) * MLA_QK_PAD)
        return _dot_nt(k_ref[pl.ds(off, tk), qs], q_ref[:, qs])

    def vt(h, off):
        return vt_ref[h * MLA_V:(h + 1) * MLA_V, pl.ds(off, tk)]

    def causal(off):
        key_chunk = (off + lax.broadcasted_iota(I32, (tk, 1), 0)) // CHUNK
        return lambda s_t: jnp.where(key_chunk <= qry_chunk, s_t, NEG)

    _flash_sweep_t(n_tiles, tk, hp, hp, qk, vt, lambda off: (lambda s_t: s_t), causal, m_scr, acc_scr)
    for h in range(hp):
        o_ref[:, h * MLA_V:(h + 1) * MLA_V] = _attn_out_t(acc_scr[h], MLA_V)


def _dsa_kernel(q_ref, k_ref, vt_ref, qi_ref, ki_ref, wi_ref, o_ref,
                sc_scr, sb_scr, bias_scr, cnt_scr, cnt16_scr, m_scr, acc_scr, *, tq, tk, n_sel, seq):
    i = pl.program_id(1)
    n_valid = ((i + 1) * tq + tk - 1) // tk
    qry_chunk = (i * tq + lax.broadcasted_iota(I32, (1, tq), 1)) // CHUNK

    def key_ids(off):
        return off + lax.broadcasted_iota(I32, (tk, 1), 0)

    wi_t = wi_ref[...].T

    def score_tile(j):
        off = pl.multiple_of(j * tk, tk)
        kj = ki_ref[pl.ds(off, tk), :]
        sc = jnp.zeros((tk, tq), F32)
        for hh in range(IDX_HEADS):
            lg = _dot_nt(kj, qi_ref[:, hh * LANES:(hh + 1) * LANES])
            sc = sc + jnp.maximum(lg, 0.0) * wi_t[hh:hh + 1, :]
        sc = jnp.where(key_ids(off) // CHUNK <= qry_chunk, sc, -jnp.inf)
        sc_scr[pl.ds(off, tk), :] = sc
        sb_scr[pl.ds(off, tk), :] = sc.astype(BF16)

    def score_pair(jj, c):
        score_tile(2 * jj)
        score_tile(2 * jj + 1)
        return c

    lax.fori_loop(0, n_valid // 2, score_pair, 0)

    @pl.when(n_valid % 2 == 1)
    def _():
        score_tile(n_valid - 1)

    sub = cnt_scr.shape[0]

    def count_bf16(cand):
        cnt16_scr[...] = jnp.zeros(cnt16_scr.shape, jnp.int16)

        def body(j, c):
            off = pl.multiple_of(j * tk, tk)
            hit = jnp.where(sb_scr[pl.ds(off, tk), :] >= cand, jnp.int16(1), jnp.int16(0))
            part = hit[:sub]
            for cc in range(1, tk // sub):
                part = part + hit[cc * sub:(cc + 1) * sub]
            cnt16_scr[...] += part
            return c

        lax.fori_loop(0, n_valid, body, 0)
        return jnp.sum(cnt16_scr[...].astype(I32), axis=0, keepdims=True)

    def count(pred):
        cnt_scr[...] = jnp.zeros(cnt_scr.shape, I32)

        def body(j, c):
            off = pl.multiple_of(j * tk, tk)
            hit = jnp.where(pred(sc_scr[pl.ds(off, tk), :], off), 1, 0).astype(I32)
            part = hit[:sub]
            for cc in range(1, tk // sub):
                part = part + hit[cc * sub:(cc + 1) * sub]
            cnt_scr[...] += part
            return c

        lax.fori_loop(0, n_valid, body, 0)
        return jnp.sum(cnt_scr[...], axis=0, keepdims=True)

    def key_to_float(key):
        return pltpu.bitcast(key ^ ((key >> 31) & I32(0x7FFFFFFF)), F32)

    half = 2 ** 15
    neg_inf_key = 0x807FFFFF - 2 ** 32

    def coarse_key(key16):
        return (key16 << 16) + ((key16 >> 31) & I32(0xFFFF))

    def coarse_bit(it, tb):
        cand_b = tb | (I32(1) << (15 - it))
        cand = key_to_float(coarse_key(cand_b - half)).astype(BF16)
        cnt = count_bf16(cand)
        return jnp.where(cnt >= n_sel, cand_b, tb)

    v_key16 = jnp.maximum(lax.fori_loop(0, 16, coarse_bit, jnp.zeros((1, tq), I32)) - half, neg_inf_key >> 16)
    lo_key = coarse_key(v_key16) - half

    def fine_bit(it, off):
        cand_off = off | (I32(1) << (16 - it))
        cand = key_to_float(lo_key + cand_off)
        cnt = count(lambda sc, o: sc >= cand)
        return jnp.where(cnt >= n_sel, cand_off, off)

    thr_key = lo_key + lax.fori_loop(0, 17, fine_bit, jnp.zeros((1, tq), I32))
    thr = key_to_float(jnp.maximum(thr_key, neg_inf_key))

    cnt_gt = count(lambda sc, off: sc > thr)
    cnt_ge = count(lambda sc, off: sc >= thr)
    need = n_sel - cnt_gt
    tied = (cnt_ge > n_sel) & (thr > -jnp.inf)
    any_tied = jnp.max(tied.astype(I32)) > 0
    idx_bits = (2 * seq - 1).bit_length()

    def tie_cut():
        def cut_bit(it, jc):
            cand = jc | (I32(1) << (idx_bits - 1 - it))
            cnt = count(lambda sc, off: (sc == thr) & (key_ids(off) < cand))
            return jnp.where(cnt <= need, cand, jc)
        return lax.fori_loop(0, idx_bits, cut_bit, jnp.zeros((1, tq), I32))

    jcut = lax.cond(any_tied, tie_cut, lambda: jnp.full((1, tq), 2 ** idx_bits - 1, I32))

    def write_bias(off, sel, causal):
        if causal:
            sel = sel & (key_ids(off) // CHUNK <= qry_chunk)
        bias_scr[pl.ds(off, tk), :] = jnp.where(sel, 0.0, NEG).astype(F32)

    def sel_plain(off):
        return sc_scr[pl.ds(off, tk), :] >= thr

    def sel_general(off):
        sc = sc_scr[pl.ds(off, tk), :]
        return (sc > thr) | ((sc == thr) & (key_ids(off) < jcut))

    off_last = pl.multiple_of((n_valid - 1) * tk, tk)
    for pred, sel_fn in ((any_tied, sel_general), (jnp.logical_not(any_tied), sel_plain)):
        @pl.when(pred)
        def _(sel_fn=sel_fn):
            def bias_tile(j, c):
                off = pl.multiple_of(j * tk, tk)
                write_bias(off, sel_fn(off), causal=False)
                return c

            lax.fori_loop(0, n_valid - 1, bias_tile, 0)
            write_bias(off_last, sel_fn(off_last), causal=True)

    hp = m_scr.shape[0]

    def biased(off):
        bias = bias_scr[pl.ds(off, tk), :]
        return lambda s_t: s_t + bias

    for g in range(DSA_HEADS // hp):
        def head_cols(h, g=g):
            return slice((g * hp + h) * DSA_HEAD_DIM, (g * hp + h + 1) * DSA_HEAD_DIM)

        def qk(h, off):
            return _dot_nt(k_ref[pl.ds(off, tk), head_cols(h)], q_ref[:, head_cols(h)])

        def vt(h, off):
            return vt_ref[head_cols(h), pl.ds(off, tk)]

        _flash_sweep_t(n_valid, tk, hp, hp // 2, qk, vt, biased, biased, m_scr, acc_scr)
        for h in range(hp):
            o_ref[:, head_cols(h)] = _attn_out_t(acc_scr[h], DSA_HEAD_DIM)


def _mem_kv_kernel(mem_ref, g_ref, wk_ref, wv_ref, k_ref, v_ref):
    mn = _rms(mem_ref[...], g_ref[...]).astype(BF16)
    k_ref[...] = _dot(mn, wk_ref[...]).astype(BF16)
    v_ref[...] = _dot(mn, wv_ref[...]).astype(BF16)


def _post_kernel(a_ref, b_ref, x_ref, wout_ref, gc_ref, wq_ref, km_ref, vm_ref, wo_ref, gm_ref,
                 x2_ref, hm_ref):
    na = a_ref.shape[1]
    x1 = x_ref[...] + _dot(a_ref[...], wout_ref[:na]) + _dot(b_ref[...], wout_ref[na:])
    hc = _rms(x1, gc_ref[...]).astype(BF16)
    qc = (_dot(hc, wq_ref[...]) * (X_HEAD_DIM ** -0.5)).astype(BF16)
    outs = []
    for hh in range(X_HEADS):
        sl = slice(hh * X_HEAD_DIM, (hh + 1) * X_HEAD_DIM)
        s = _dot_nt(qc[:, sl], km_ref[:, sl])
        p = jnp.exp(s - jnp.max(s, axis=-1, keepdims=True))
        o = _dot(p.astype(BF16), vm_ref[:, sl]) / jnp.sum(p, axis=-1, keepdims=True)
        outs.append(o.astype(BF16))
    x2 = x1 + _dot(jnp.concatenate(outs, axis=-1), wo_ref[...])
    x2_ref[...] = x2
    hm_ref[...] = _rms(x2, gm_ref[...]).astype(BF16)


def _mlp_kernel(hm_ref, x2_ref, wu_ref, wd_ref, gf_ref, y_ref):
    f = pl.program_id(1)

    @pl.when(f == 0)
    def _():
        y_ref[...] = x2_ref[...]

    u = jnp.maximum(_dot(hm_ref[...], wu_ref[...]), 0.0)
    y_ref[...] += _dot((u * u).astype(BF16), wd_ref[...])

    @pl.when(f == pl.num_programs(1) - 1)
    def _():
        y_ref[...] = _rms(y_ref[...], gf_ref[...])


def _const(shape):
    return pl.BlockSpec(shape, lambda *_: (0,) * len(shape), pipeline_mode=pl.Buffered(1))


def kernel(x, mem, positions, g_mix, w_in, g_cq, g_ckv, w_qb, w_kvb, w_out, g_cross, g_mem,
           w_q_cross, w_k_cross, w_v_cross, w_o_cross, g_mlp, w_up, w_down, g_final):
    assert w_in.shape[0] == 1, "one layer"
    batch, seq, d = x.shape
    t = batch * seq
    n_sel = min(TOPK_MAX, seq // 4)
    x2d = x.reshape(t, d)
    pos = positions.reshape(t, 1)
    row = lambda g: g.reshape(1, -1).astype(F32)

    w = w_in[0]
    o = [0]
    for n in (MLA_Q_LORA, MLA_KV_LORA, MLA_ROPE, DSA_HEADS * DSA_HEAD_DIM, DSA_HEADS * DSA_HEAD_DIM,
              DSA_HEADS * DSA_HEAD_DIM, IDX_HEADS * IDX_DIM, IDX_DIM, IDX_HEADS):
        o.append(o[-1] + n)
    zeros = lambda n: jnp.zeros((d, n), w.dtype)
    w_pad = jnp.concatenate([w[:, :o[3]], zeros(LANES - MLA_ROPE), w[:, o[3]:],
                             zeros(W_END - o[9] - (LANES - MLA_ROPE))], axis=1).astype(BF16)
    w_dsa_vt = w[:, o[5]:o[6]].T.astype(BF16)
    wqb = w_qb[0].reshape(MLA_Q_LORA, MLA_HEADS, MLA_NOPE + MLA_ROPE)
    wqb = jnp.pad(wqb, ((0, 0), (0, 0), (0, MLA_QK_PAD - MLA_NOPE - MLA_ROPE)))
    wqb = wqb.reshape(MLA_Q_LORA, MLA_HEADS * MLA_QK_PAD).astype(BF16)
    wkvb = w_kvb[0].reshape(MLA_KV_LORA, MLA_HEADS, MLA_NOPE + MLA_V)
    wkvb_k = wkvb[:, :, :MLA_NOPE].reshape(MLA_KV_LORA, -1).astype(BF16)
    wkvb_vt = wkvb[:, :, MLA_NOPE:].reshape(MLA_KV_LORA, -1).T.astype(BF16)
    mla_w = MLA_HEADS * MLA_V
    wq_c, wk_c, wv_c, wo_c = (a[0].astype(BF16) for a in (w_q_cross, w_k_cross, w_v_cross, w_o_cross))
    invf = jnp.zeros((LANES,), F32)
    for (half, _), off in zip(ROPE_VARIANTS, _rope_lane_plan()):
        inv_freq = ROPE_THETA ** (-jnp.arange(half, dtype=F32) / half)
        invf = invf.at[off:off + 2 * half].set(jnp.concatenate([inv_freq, inv_freq]))
    invf = invf.reshape(1, LANES)

    qk_w = MLA_HEADS * MLA_QK_PAD
    dsa_w = DSA_HEADS * DSA_HEAD_DIM
    qi_w = IDX_HEADS * LANES
    tpj = 256
    rows = lambda w: pl.BlockSpec((tpj, w), lambda i: (i, 0))
    cols = lambda w: pl.BlockSpec((w, tpj), lambda i: (0, i))
    q_mla, k_mla, vt_mla, qk_d, vt_d, q_idx, k_idx, w_idx_s = pl.pallas_call(
        _proj_kernel, name="proj", grid=(t // tpj,),
        in_specs=[rows(d), rows(1), _const((1, d)), _const((1, LANES)),
                  _const((d, W_END)), _const((1, MLA_Q_LORA)), _const((MLA_Q_LORA, qk_w)),
                  _const((1, MLA_KV_LORA)), _const((MLA_KV_LORA, mla_w)), _const((mla_w, MLA_KV_LORA)),
                  _const((dsa_w, d))],
        out_specs=[rows(qk_w), rows(qk_w), cols(mla_w),
                   pl.BlockSpec((2, tpj, dsa_w), lambda i: (0, i, 0)), cols(dsa_w),
                   rows(qi_w), rows(LANES), rows(LANES)],
        out_shape=[jax.ShapeDtypeStruct((t, qk_w), BF16), jax.ShapeDtypeStruct((t, qk_w), BF16),
                   jax.ShapeDtypeStruct((mla_w, t), BF16),
                   jax.ShapeDtypeStruct((2, t, dsa_w), BF16), jax.ShapeDtypeStruct((dsa_w, t), BF16),
                   jax.ShapeDtypeStruct((t, qi_w), BF16), jax.ShapeDtypeStruct((t, LANES), BF16),
                   jax.ShapeDtypeStruct((t, LANES), F32)],
        compiler_params=_params(("parallel",)),
    )(x2d, pos, row(g_mix), invf, w_pad, row(g_cq), wqb, row(g_ckv), wkvb_k, wkvb_vt, w_dsa_vt)

    tq, tk, hp = 256, 512, 8
    mla_grid = (batch, MLA_HEADS // hp, seq // tq)
    n_steps = mla_grid[0] * mla_grid[1] * mla_grid[2]
    step = lambda b, h, i: (b * mla_grid[1] + h) * mla_grid[2] + i
    wu_rows, wd_rows = d // n_steps, D_FF // n_steps
    a_mla, wu, wd = pl.pallas_call(
        functools.partial(_mla_attn_kernel, tq=tq, tk=tk, hp=hp), name="mla_attn", grid=mla_grid,
        in_specs=[pl.BlockSpec((None, tq, hp * MLA_QK_PAD), lambda b, h, i: (b, i, h)),
                  pl.BlockSpec((None, seq, hp * MLA_QK_PAD), lambda b, h, i: (b, 0, h),
                               pipeline_mode=pl.Buffered(1)),
                  pl.BlockSpec((hp * MLA_V, seq), lambda b, h, i: (h, b), pipeline_mode=pl.Buffered(1)),
                  pl.BlockSpec((None, wu_rows, D_FF), lambda b, h, i: (0, step(b, h, i), 0)),
                  pl.BlockSpec((None, wd_rows, d), lambda b, h, i: (0, step(b, h, i), 0))],
        out_specs=[pl.BlockSpec((None, tq, hp * MLA_V), lambda b, h, i: (b, i, h)),
                   pl.BlockSpec((wu_rows, D_FF), lambda b, h, i: (step(b, h, i), 0)),
                   pl.BlockSpec((wd_rows, d), lambda b, h, i: (step(b, h, i), 0))],
        out_shape=[jax.ShapeDtypeStruct((batch, seq, mla_w), BF16),
                   jax.ShapeDtypeStruct((d, D_FF), BF16), jax.ShapeDtypeStruct((D_FF, d), BF16)],
        scratch_shapes=[pltpu.VMEM((hp, 1, tq), F32), pltpu.VMEM((hp, MLA_V + SUM_ROWS, tq), F32)],
        compiler_params=_params(("parallel", "parallel", "arbitrary")),
    )(q_mla.reshape(batch, seq, qk_w), k_mla.reshape(batch, seq, qk_w), vt_mla, w_up, w_down)

    qk4 = qk_d.reshape(2, batch, seq, dsa_w)
    tq, tk, hp = 256, 512, 8
    b_dsa = pl.pallas_call(
        functools.partial(_dsa_kernel, tq=tq, tk=tk, n_sel=n_sel, seq=seq), name="dsa",
        grid=(batch, seq // tq),
        in_specs=[pl.BlockSpec((None, None, tq, dsa_w), lambda b, i: (0, b, i, 0)),
                  pl.BlockSpec((None, None, seq, dsa_w), lambda b, i: (1, b, 0, 0), pipeline_mode=pl.Buffered(1)),
                  pl.BlockSpec((dsa_w, seq), lambda b, i: (0, b), pipeline_mode=pl.Buffered(1)),
                  pl.BlockSpec((None, tq, qi_w), lambda b, i: (b, i, 0)),
                  pl.BlockSpec((None, seq, LANES), lambda b, i: (b, 0, 0)),
                  pl.BlockSpec((None, tq, LANES), lambda b, i: (b, i, 0))],
        out_specs=pl.BlockSpec((None, tq, dsa_w), lambda b, i: (b, i, 0)),
        out_shape=jax.ShapeDtypeStruct((batch, seq, dsa_w), BF16),
        scratch_shapes=[pltpu.VMEM((seq, tq), F32), pltpu.VMEM((seq, tq), BF16), pltpu.VMEM((seq, tq), F32),
                        pltpu.VMEM((32, tq), I32), pltpu.VMEM((32, tq), jnp.int16),
                        pltpu.VMEM((hp, 1, tq), F32), pltpu.VMEM((hp, DSA_HEAD_DIM + SUM_ROWS, tq), F32)],
        compiler_params=_params(("parallel", "arbitrary")),
    )(qk4, qk4, vt_d, q_idx.reshape(batch, seq, qi_w), k_idx.reshape(batch, seq, LANES),
      w_idx_s.reshape(batch, seq, LANES))

    xw = X_HEADS * X_HEAD_DIM
    n_mem = mem.shape[1]
    k_mem, v_mem = pl.pallas_call(
        _mem_kv_kernel, name="mem_kv", grid=(batch,),
        in_specs=[pl.BlockSpec((None, n_mem, d), lambda b: (b, 0, 0)), _const((1, d)), _const((d, xw)),
                  _const((d, xw))],
        out_specs=[pl.BlockSpec((None, n_mem, xw), lambda b: (b, 0, 0))] * 2,
        out_shape=[jax.ShapeDtypeStruct((batch, n_mem, xw), BF16)] * 2,
        compiler_params=_params(("parallel",)),
    )(mem, row(g_mem), wk_c, wv_c)

    tp = 512
    per_b = seq // tp
    mem_spec = pl.BlockSpec((None, n_mem, xw), lambda i: (i // per_b, 0, 0))
    x2, hm = pl.pallas_call(
        _post_kernel, name="post", grid=(t // tp,),
        in_specs=[pl.BlockSpec((tp, mla_w), lambda i: (i, 0)), pl.BlockSpec((tp, dsa_w), lambda i: (i, 0)),
                  pl.BlockSpec((tp, d), lambda i: (i, 0)), _const((mla_w + dsa_w, d)), _const((1, d)),
                  _const((d, xw)), mem_spec, mem_spec, _const((xw, d)), _const((1, d))],
        out_specs=[pl.BlockSpec((tp, d), lambda i: (i, 0))] * 2,
        out_shape=[jax.ShapeDtypeStruct((t, d), F32), jax.ShapeDtypeStruct((t, d), BF16)],
        compiler_params=_params(("parallel",)),
    )(a_mla.reshape(t, mla_w), b_dsa.reshape(t, dsa_w), x2d, w_out[0].astype(BF16), row(g_cross), wq_c,
      k_mem, v_mem, wo_c, row(g_mlp))

    tmm, tf = 512, 1024
    y = pl.pallas_call(
        _mlp_kernel, name="mlp", grid=(t // tmm, D_FF // tf),
        in_specs=[pl.BlockSpec((tmm, d), lambda i, f: (i, 0)), pl.BlockSpec((tmm, d), lambda i, f: (i, 0)),
                  pl.BlockSpec((d, tf), lambda i, f: (0, f)), pl.BlockSpec((tf, d), lambda i, f: (f, 0)),
                  _const((1, d))],
        out_specs=pl.BlockSpec((tmm, d), lambda i, f: (i, 0)),
        out_shape=jax.ShapeDtypeStruct((t, d), F32),
        compiler_params=_params(("parallel", "arbitrary")),
    )(hm, x2, wu, wd, row(g_final))
    return y.reshape(batch, seq, d)
```

```python
import functools

import jax
import jax.numpy as jnp
from jax import lax
from jax.experimental import pallas as pl
from jax.experimental.pallas import tpu as pltpu

F32 = jnp.float32
BF16 = jnp.bfloat16
I32 = jnp.int32

D_MODEL = 2048
CHUNK = 64
ROPE_THETA = 500000.0
N_MEM = 256
EPS = 1e-6
MLA_HEADS = 8
MLA_NOPE = 128
MLA_ROPE = 64
MLA_V = 128
MLA_Q_LORA = 512
MLA_KV_LORA = 256
DSA_HEADS = 8
DSA_HEAD_DIM = 128
DSA_ROT = DSA_HEAD_DIM // 4
IDX_HEADS = 16
IDX_DIM = 64
IDX_ROT = IDX_DIM // 4
TOPK_MAX = 256
X_HEADS = 4
X_HEAD_DIM = 128
D_FF = 4 * D_MODEL

LANES = 128
W_CQ = 0
W_CKV = W_CQ + MLA_Q_LORA
W_DSA = W_CKV + MLA_KV_LORA + LANES
W_IDX = W_DSA + 3 * DSA_HEADS * DSA_HEAD_DIM
W_END = W_IDX + IDX_HEADS * IDX_DIM + LANES
MLA_QK_PAD = 256
NEG = -1e30
INT_MIN = -2 ** 31
LOG2E = 1.4426950408889634
VMEM_LIMIT = 56 * 1024 * 1024


def _params(sem, vmem=VMEM_LIMIT):
    return pltpu.CompilerParams(dimension_semantics=sem, vmem_limit_bytes=vmem)


def _rms(xf, g):
    return xf * lax.rsqrt(jnp.mean(xf * xf, axis=-1, keepdims=True) + EPS) * g


def _apply_rope(x, coeffs, half):
    c, s_lo, s_hi = coeffs
    return x * c + pltpu.roll(x, LANES - half, 1) * s_lo + pltpu.roll(x, half, 1) * s_hi


def _dot(a, b):
    return jnp.dot(a, b, preferred_element_type=F32)


def _dot_nt(a, b):
    return lax.dot_general(a, b, (((1,), (1,)), ((), ())), preferred_element_type=F32)


def _for_tiles(n, fn):
    def pair(jj, c):
        fn(2 * jj)
        fn(2 * jj + 1)
        return c

    lax.fori_loop(0, n // 2, pair, 0)

    @pl.when(n % 2 == 1)
    def _():
        fn(n - 1)


ROPE_VARIANTS = ((MLA_ROPE // 2, LANES), (DSA_ROT // 2, LANES), (IDX_ROT // 2, IDX_DIM))


def _rope_lane_plan():
    offs, o = [], 0
    for half, _ in ROPE_VARIANTS:
        offs.append(o)
        o += 2 * half
    assert o <= LANES
    return offs


def _rope_coeffs(pos, invf):
    ang = pos.astype(F32) * invf
    cos, sin = jnp.cos(ang), jnp.sin(ang)
    lane = lax.broadcasted_iota(I32, (1, LANES), 1)
    coeffs = []
    for (half, period), off in zip(ROPE_VARIANTS, _rope_lane_plan()):
        c = jnp.ones(cos.shape, F32)
        s_lo = jnp.zeros(cos.shape, F32)
        s_hi = jnp.zeros(cos.shape, F32)
        for base in range(0, LANES, period):
            shift = (base - off) % LANES
            cs = cos if shift == 0 else pltpu.roll(cos, shift, 1)
            sn = sin if shift == 0 else pltpu.roll(sin, shift, 1)
            in_lo = (lane >= base) & (lane < base + half)
            in_hi = (lane >= base + half) & (lane < base + 2 * half)
            c = jnp.where(in_lo | in_hi, cs, c)
            s_lo = jnp.where(in_lo, -sn, s_lo)
            s_hi = jnp.where(in_hi, sn, s_hi)
        coeffs.append((c, s_lo, s_hi))
    return coeffs


def _proj_kernel(x_ref, pos_ref, g_ref, invf_ref, w_ref, gcq_ref, wqb_ref, gckv_ref, wk_ref, wvt_ref, wdsa_vt_ref,
                 q_ref, k_ref, vt_ref, qk_d_ref, vt_d_ref, qi_ref, ki_ref, wi_ref):
    h = _rms(x_ref[...], g_ref[...]).astype(BF16)
    rope = _rope_coeffs(pos_ref[...], invf_ref[...])
    lane = lax.broadcasted_iota(I32, (1, LANES), 1)

    cq = _dot(h, w_ref[:, W_CQ:W_CKV])
    r = _dot(h, w_ref[:, W_CKV:W_DSA])
    q = _dot(_rms(cq, gcq_ref[...]).astype(BF16), wqb_ref[...])
    ckv = _rms(r[:, :MLA_KV_LORA], gckv_ref[...]).astype(BF16)
    kn = _dot(ckv, wk_ref[...])
    vt_ref[...] = _dot_nt(wvt_ref[...], ckv).astype(BF16)
    scale = (MLA_NOPE + MLA_ROPE) ** -0.5 * LOG2E
    kr = _apply_rope(r[:, MLA_KV_LORA:], rope[0], MLA_ROPE // 2).astype(BF16)
    for hh in range(MLA_HEADS):
        b0 = hh * MLA_QK_PAD
        q_ref[:, b0:b0 + LANES] = (q[:, b0:b0 + LANES] * scale).astype(BF16)
        qr = _apply_rope(q[:, b0 + LANES:b0 + 2 * LANES], rope[0], MLA_ROPE // 2)
        q_ref[:, b0 + LANES:b0 + 2 * LANES] = (qr * scale).astype(BF16)
        k_ref[:, b0:b0 + LANES] = kn[:, hh * LANES:(hh + 1) * LANES].astype(BF16)
        k_ref[:, b0 + LANES:b0 + 2 * LANES] = kr

    dsa_w = DSA_HEADS * DSA_HEAD_DIM
    for g, scale in enumerate((DSA_HEAD_DIM ** -0.5 * LOG2E, 1.0)):
        rd = _dot(h, w_ref[:, W_DSA + g * dsa_w:W_DSA + (g + 1) * dsa_w])
        for hh in range(DSA_HEADS):
            sl = slice(hh * LANES, (hh + 1) * LANES)
            qk_d_ref[g, :, sl] = (_apply_rope(rd[:, sl], rope[1], DSA_ROT // 2) * scale).astype(BF16)
    vt_d_ref[...] = _dot_nt(wdsa_vt_ref[...], h).astype(BF16)

    ri = _dot(h, w_ref[:, W_IDX:W_END])
    nq = IDX_HEADS * IDX_DIM
    tail = ri[:, nq:nq + LANES]
    tail_hi = pltpu.roll(tail, LANES - IDX_DIM, 1)
    for p in range(IDX_HEADS // 2):
        t = _apply_rope(ri[:, p * LANES:(p + 1) * LANES], rope[2], IDX_ROT // 2) * (IDX_DIM ** -0.5)
        qi_ref[:, (2 * p) * LANES:(2 * p + 1) * LANES] = jnp.where(lane < IDX_DIM, t, 0.0).astype(BF16)
        qi_ref[:, (2 * p + 1) * LANES:(2 * p + 2) * LANES] = jnp.where(lane >= IDX_DIM, t, 0.0).astype(BF16)
    k_dup = jnp.where(lane < IDX_DIM, tail, pltpu.roll(tail, IDX_DIM, 1))
    ki_ref[...] = _apply_rope(k_dup, rope[2], IDX_ROT // 2).astype(BF16)
    wi_ref[...] = jnp.where(lane < IDX_HEADS, tail_hi, 0.0) * (IDX_HEADS ** -0.5)


SUM_ROWS = 16


def _flash_sweep_t(n_tiles, tk, heads, lead, qk, vt, adjust, adjust_last, m_scr, acc_scr):
    m_scr[...] = jnp.full(m_scr.shape, NEG, F32)
    acc_scr[...] = jnp.zeros(acc_scr.shape, F32)
    ones = jnp.ones((SUM_ROWS, tk), BF16)

    def scores(h, off, adj):
        s_t = adj(qk(h, off))
        return s_t, jnp.max(s_t, axis=0, keepdims=True)

    def finish(h, s_and_max, off):
        s_t, m_tile = s_and_max
        m_prev = m_scr[h]
        m_new = jnp.maximum(m_prev, m_tile)
        p = jnp.exp2(s_t - m_new).astype(BF16)
        pv = _dot(jnp.concatenate([vt(h, off), ones], axis=0), p)
        acc_scr[h] = jnp.exp2(m_prev - m_new) * acc_scr[h] + pv
        m_scr[h] = m_new

    def tile(off, adjust_tile):
        adj = adjust_tile(off)
        s_ts = [scores(h, off, adj) for h in range(min(lead, heads))]
        for h in range(heads):
            finish(h, s_ts[h], off)
            if h + lead < heads:
                s_ts.append(scores(h + lead, off, adj))

    _for_tiles(n_tiles - 1, lambda j: tile(pl.multiple_of(j * tk, tk), adjust))
    tile(pl.multiple_of((n_tiles - 1) * tk, tk), adjust_last)


def _attn_out_t(acc, dv):
    return (acc[:dv] / acc[dv:dv + 1]).T.astype(BF16)


def _mla_attn_kernel(q_ref, k_ref, vt_ref, wu_ref, wd_ref, o_ref, wu_bf_ref, wd_bf_ref, m_scr, acc_scr,
                     *, tq, tk, hp):
    wu_bf_ref[...] = wu_ref[...].astype(BF16)
    wd_bf_ref[...] = wd_ref[...].astype(BF16)
    i = pl.program_id(2)
    n_tiles = (i * tq) // tk + 1
    qry_chunk = (i * tq + lax.broadcasted_iota(I32, (1, tq), 1)) // CHUNK

    def qk(h, off):
        qs = slice(h * MLA_QK_PAD, (h + 1) * MLA_QK_PAD)
        return _dot_nt(k_ref[pl.ds(off, tk), qs], q_ref[:, qs])

    def vt(h, off):
        return vt_ref[h * MLA_V:(h + 1) * MLA_V, pl.ds(off, tk)]

    def causal(off):
        key_chunk = (off + lax.broadcasted_iota(I32, (tk, 1), 0)) // CHUNK
        return lambda s_t: jnp.where(key_chunk <= qry_chunk, s_t, NEG)

    _flash_sweep_t(n_tiles, tk, hp, hp, qk, vt, lambda off: (lambda s_t: s_t), causal, m_scr, acc_scr)
    for h in range(hp):
        o_ref[:, h * MLA_V:(h + 1) * MLA_V] = _attn_out_t(acc_scr[h], MLA_V)


def _dsa_kernel(q_ref, k_ref, vt_ref, qi_ref, ki_ref, wi_ref, o_ref,
                sc_scr, sb_scr, bias_scr, cnt_scr, cnt16_scr, m_scr, acc_scr, *, tq, tk, n_sel, seq):
    i = pl.program_id(1)
    n_valid = ((i + 1) * tq + tk - 1) // tk
    qry_chunk = (i * tq + lax.broadcasted_iota(I32, (1, tq), 1)) // CHUNK

    def key_ids(off):
        return off + lax.broadcasted_iota(I32, (tk, 1), 0)

    wi_t = wi_ref[...].T

    def score_tile(j):
        off = pl.multiple_of(j * tk, tk)
        kj = ki_ref[pl.ds(off, tk), :]
        sc = jnp.zeros((tk, tq), F32)
        for hh in range(IDX_HEADS):
            lg = _dot_nt(kj, qi_ref[:, hh * LANES:(hh + 1) * LANES])
            sc = sc + jnp.maximum(lg, 0.0) * wi_t[hh:hh + 1, :]
        sc = jnp.where(key_ids(off) // CHUNK <= qry_chunk, sc, -jnp.inf)
        sc_scr[pl.ds(off, tk), :] = sc
        sb_scr[pl.ds(off, tk), :] = sc.astype(BF16)

    _for_tiles(n_valid, score_tile)

    sub = cnt_scr.shape[0]

    def count_bf16(cand):
        cnt16_scr[...] = jnp.zeros(cnt16_scr.shape, jnp.int16)

        def body(j):
            off = pl.multiple_of(j * tk, tk)
            hit = jnp.where(sb_scr[pl.ds(off, tk), :] >= cand, jnp.int16(1), jnp.int16(0))
            part = hit[:sub]
            for cc in range(1, tk // sub):
                part = part + hit[cc * sub:(cc + 1) * sub]
            cnt16_scr[...] += part

        _for_tiles(n_valid, body)
        return jnp.sum(cnt16_scr[...].astype(I32), axis=0, keepdims=True)

    def count(pred):
        cnt_scr[...] = jnp.zeros(cnt_scr.shape, I32)

        def body(j):
            off = pl.multiple_of(j * tk, tk)
            hit = jnp.where(pred(sc_scr[pl.ds(off, tk), :], off), 1, 0).astype(I32)
            part = hit[:sub]
            for cc in range(1, tk // sub):
                part = part + hit[cc * sub:(cc + 1) * sub]
            cnt_scr[...] += part

        _for_tiles(n_valid, body)
        return jnp.sum(cnt_scr[...], axis=0, keepdims=True)

    def key_to_float(key):
        return pltpu.bitcast(key ^ ((key >> 31) & I32(0x7FFFFFFF)), F32)

    half = 2 ** 15
    neg_inf_key = 0x807FFFFF - 2 ** 32

    def coarse_key(key16):
        return (key16 << 16) + ((key16 >> 31) & I32(0xFFFF))

    def coarse_bit(it, tb):
        cand_b = tb | (I32(1) << (15 - it))
        cand = key_to_float(coarse_key(cand_b - half)).astype(BF16)
        cnt = count_bf16(cand)
        return jnp.where(cnt >= n_sel, cand_b, tb)

    v_key16 = jnp.maximum(lax.fori_loop(0, 16, coarse_bit, jnp.zeros((1, tq), I32)) - half, neg_inf_key >> 16)
    lo_key = coarse_key(v_key16) - half

    def fine_bit(it, off):
        cand_off = off | (I32(1) << (16 - it))
        cand = key_to_float(lo_key + cand_off)
        cnt = count(lambda sc, o: sc >= cand)
        return jnp.where(cnt >= n_sel, cand_off, off)

    thr_key = lo_key + lax.fori_loop(0, 17, fine_bit, jnp.zeros((1, tq), I32))
    thr = key_to_float(jnp.maximum(thr_key, neg_inf_key))

    cnt_gt = count(lambda sc, off: sc > thr)
    cnt_ge = count(lambda sc, off: sc >= thr)
    need = n_sel - cnt_gt
    tied = (cnt_ge > n_sel) & (thr > -jnp.inf)
    any_tied = jnp.max(tied.astype(I32)) > 0
    idx_bits = (2 * seq - 1).bit_length()

    def tie_cut():
        def cut_bit(it, jc):
            cand = jc | (I32(1) << (idx_bits - 1 - it))
            cnt = count(lambda sc, off: (sc == thr) & (key_ids(off) < cand))
            return jnp.where(cnt <= need, cand, jc)
        return lax.fori_loop(0, idx_bits, cut_bit, jnp.zeros((1, tq), I32))

    jcut = lax.cond(any_tied, tie_cut, lambda: jnp.full((1, tq), 2 ** idx_bits - 1, I32))

    def write_bias(off, sel, causal):
        if causal:
            sel = sel & (key_ids(off) // CHUNK <= qry_chunk)
        bias_scr[pl.ds(off, tk), :] = jnp.where(sel, 0.0, NEG).astype(F32)

    def sel_plain(off):
        return sc_scr[pl.ds(off, tk), :] >= thr

    def sel_general(off):
        sc = sc_scr[pl.ds(off, tk), :]
        return (sc > thr) | ((sc == thr) & (key_ids(off) < jcut))

    off_last = pl.multiple_of((n_valid - 1) * tk, tk)
    for pred, sel_fn in ((any_tied, sel_general), (jnp.logical_not(any_tied), sel_plain)):
        @pl.when(pred)
        def _(sel_fn=sel_fn):
            def bias_tile(j, c):
                off = pl.multiple_of(j * tk, tk)
                write_bias(off, sel_fn(off), causal=False)
                return c

            lax.fori_loop(0, n_valid - 1, bias_tile, 0)
            write_bias(off_last, sel_fn(off_last), causal=True)

    hp = m_scr.shape[0]

    def biased(off):
        bias = bias_scr[pl.ds(off, tk), :]
        return lambda s_t: s_t + bias

    for g in range(DSA_HEADS // hp):
        def head_cols(h, g=g):
            return slice((g * hp + h) * DSA_HEAD_DIM, (g * hp + h + 1) * DSA_HEAD_DIM)

        def qk(h, off):
            return _dot_nt(k_ref[pl.ds(off, tk), head_cols(h)], q_ref[:, head_cols(h)])

        def vt(h, off):
            return vt_ref[head_cols(h), pl.ds(off, tk)]

        _flash_sweep_t(n_valid, tk, hp, hp // 2, qk, vt, biased, biased, m_scr, acc_scr)
        for h in range(hp):
            o_ref[:, head_cols(h)] = _attn_out_t(acc_scr[h], DSA_HEAD_DIM)


def _mem_kv_kernel(mem_ref, g_ref, wk_ref, wv_ref, k_ref, v_ref):
    mn = _rms(mem_ref[...], g_ref[...]).astype(BF16)
    k_ref[...] = _dot(mn, wk_ref[...]).astype(BF16)
    v_ref[...] = _dot(mn, wv_ref[...]).astype(BF16)


def _post_kernel(a_ref, b_ref, x_ref, wout_ref, gc_ref, wq_ref, km_ref, vm_ref, wo_ref, gm_ref,
                 x2_ref, hm_ref):
    na = a_ref.shape[1]
    x1 = x_ref[...] + _dot(a_ref[...], wout_ref[:na]) + _dot(b_ref[...], wout_ref[na:])
    hc = _rms(x1, gc_ref[...]).astype(BF16)
    qc = (_dot(hc, wq_ref[...]) * (X_HEAD_DIM ** -0.5)).astype(BF16)
    outs = []
    for hh in range(X_HEADS):
        sl = slice(hh * X_HEAD_DIM, (hh + 1) * X_HEAD_DIM)
        s = _dot_nt(qc[:, sl], km_ref[:, sl])
        p = jnp.exp(s - jnp.max(s, axis=-1, keepdims=True))
        o = _dot(p.astype(BF16), vm_ref[:, sl]) / jnp.sum(p, axis=-1, keepdims=True)
        outs.append(o.astype(BF16))
    x2 = x1 + _dot(jnp.concatenate(outs, axis=-1), wo_ref[...])
    x2_ref[...] = x2
    hm_ref[...] = _rms(x2, gm_ref[...]).astype(BF16)


def _mlp_kernel(hm_ref, x2_ref, wu_ref, wd_ref, gf_ref, y_ref):
    f = pl.program_id(1)

    @pl.when(f == 0)
    def _():
        y_ref[...] = x2_ref[...]

    u = jnp.maximum(_dot(hm_ref[...], wu_ref[...]), 0.0)
    y_ref[...] += _dot((u * u).astype(BF16), wd_ref[...])

    @pl.when(f == pl.num_programs(1) - 1)
    def _():
        y_ref[...] = _rms(y_ref[...], gf_ref[...])


def _const(shape):
    return pl.BlockSpec(shape, lambda *_: (0,) * len(shape), pipeline_mode=pl.Buffered(1))


def kernel(x, mem, positions, g_mix, w_in, g_cq, g_ckv, w_qb, w_kvb, w_out, g_cross, g_mem,
           w_q_cross, w_k_cross, w_v_cross, w_o_cross, g_mlp, w_up, w_down, g_final):
    assert w_in.shape[0] == 1, "one layer"
    batch, seq, d = x.shape
    t = batch * seq
    n_sel = min(TOPK_MAX, seq // 4)
    x2d = x.reshape(t, d)
    pos = positions.reshape(t, 1)
    row = lambda g: g.reshape(1, -1).astype(F32)

    w = w_in[0]
    o = [0]
    for n in (MLA_Q_LORA, MLA_KV_LORA, MLA_ROPE, DSA_HEADS * DSA_HEAD_DIM, DSA_HEADS * DSA_HEAD_DIM,
              DSA_HEADS * DSA_HEAD_DIM, IDX_HEADS * IDX_DIM, IDX_DIM, IDX_HEADS):
        o.append(o[-1] + n)
    zeros = lambda n: jnp.zeros((d, n), w.dtype)
    w_pad = jnp.concatenate([w[:, :o[3]], zeros(LANES - MLA_ROPE), w[:, o[3]:],
                             zeros(W_END - o[9] - (LANES - MLA_ROPE))], axis=1).astype(BF16)
    w_dsa_vt = w[:, o[5]:o[6]].T.astype(BF16)
    wqb = w_qb[0].reshape(MLA_Q_LORA, MLA_HEADS, MLA_NOPE + MLA_ROPE)
    wqb = jnp.pad(wqb, ((0, 0), (0, 0), (0, MLA_QK_PAD - MLA_NOPE - MLA_ROPE)))
    wqb = wqb.reshape(MLA_Q_LORA, MLA_HEADS * MLA_QK_PAD).astype(BF16)
    wkvb = w_kvb[0].reshape(MLA_KV_LORA, MLA_HEADS, MLA_NOPE + MLA_V)
    wkvb_k = wkvb[:, :, :MLA_NOPE].reshape(MLA_KV_LORA, -1).astype(BF16)
    wkvb_vt = wkvb[:, :, MLA_NOPE:].reshape(MLA_KV_LORA, -1).T.astype(BF16)
    mla_w = MLA_HEADS * MLA_V
    wq_c, wk_c, wv_c, wo_c = (a[0].astype(BF16) for a in (w_q_cross, w_k_cross, w_v_cross, w_o_cross))
    invf = jnp.zeros((LANES,), F32)
    for (half, _), off in zip(ROPE_VARIANTS, _rope_lane_plan()):
        inv_freq = ROPE_THETA ** (-jnp.arange(half, dtype=F32) / half)
        invf = invf.at[off:off + 2 * half].set(jnp.concatenate([inv_freq, inv_freq]))
    invf = invf.reshape(1, LANES)

    qk_w = MLA_HEADS * MLA_QK_PAD
    dsa_w = DSA_HEADS * DSA_HEAD_DIM
    qi_w = IDX_HEADS * LANES
    tpj = 256
    rows = lambda w: pl.BlockSpec((tpj, w), lambda i: (i, 0))
    cols = lambda w: pl.BlockSpec((w, tpj), lambda i: (0, i))
    q_mla, k_mla, vt_mla, qk_d, vt_d, q_idx, k_idx, w_idx_s = pl.pallas_call(
        _proj_kernel, name="proj", grid=(t // tpj,),
        in_specs=[rows(d), rows(1), _const((1, d)), _const((1, LANES)),
                  _const((d, W_END)), _const((1, MLA_Q_LORA)), _const((MLA_Q_LORA, qk_w)),
                  _const((1, MLA_KV_LORA)), _const((MLA_KV_LORA, mla_w)), _const((mla_w, MLA_KV_LORA)),
                  _const((dsa_w, d))],
        out_specs=[rows(qk_w), rows(qk_w), cols(mla_w),
                   pl.BlockSpec((2, tpj, dsa_w), lambda i: (0, i, 0)), cols(dsa_w),
                   rows(qi_w), rows(LANES), rows(LANES)],
        out_shape=[jax.ShapeDtypeStruct((t, qk_w), BF16), jax.ShapeDtypeStruct((t, qk_w), BF16),
                   jax.ShapeDtypeStruct((mla_w, t), BF16),
                   jax.ShapeDtypeStruct((2, t, dsa_w), BF16), jax.ShapeDtypeStruct((dsa_w, t), BF16),
                   jax.ShapeDtypeStruct((t, qi_w), BF16), jax.ShapeDtypeStruct((t, LANES), BF16),
                   jax.ShapeDtypeStruct((t, LANES), F32)],
        compiler_params=_params(("parallel",)),
    )(x2d, pos, row(g_mix), invf, w_pad, row(g_cq), wqb, row(g_ckv), wkvb_k, wkvb_vt, w_dsa_vt)

    tq, tk, hp = 256, 512, 8
    mla_grid = (batch, MLA_HEADS // hp, seq // tq)
    n_steps = mla_grid[0] * mla_grid[1] * mla_grid[2]
    step = lambda b, h, i: (b * mla_grid[1] + h) * mla_grid[2] + i
    wu_rows, wd_rows = d // n_steps, D_FF // n_steps
    a_mla, wu, wd = pl.pallas_call(
        functools.partial(_mla_attn_kernel, tq=tq, tk=tk, hp=hp), name="mla_attn", grid=mla_grid,
        in_specs=[pl.BlockSpec((None, tq, hp * MLA_QK_PAD), lambda b, h, i: (b, i, h)),
                  pl.BlockSpec((None, seq, hp * MLA_QK_PAD), lambda b, h, i: (b, 0, h),
                               pipeline_mode=pl.Buffered(1)),
                  pl.BlockSpec((hp * MLA_V, seq), lambda b, h, i: (h, b), pipeline_mode=pl.Buffered(1)),
                  pl.BlockSpec((None, wu_rows, D_FF), lambda b, h, i: (0, step(b, h, i), 0)),
                  pl.BlockSpec((None, wd_rows, d), lambda b, h, i: (0, step(b, h, i), 0))],
        out_specs=[pl.BlockSpec((None, tq, hp * MLA_V), lambda b, h, i: (b, i, h)),
                   pl.BlockSpec((wu_rows, D_FF), lambda b, h, i: (step(b, h, i), 0)),
                   pl.BlockSpec((wd_rows, d), lambda b, h, i: (step(b, h, i), 0))],
        out_shape=[jax.ShapeDtypeStruct((batch, seq, mla_w), BF16),
                   jax.ShapeDtypeStruct((d, D_FF), BF16), jax.ShapeDtypeStruct((D_FF, d), BF16)],
        scratch_shapes=[pltpu.VMEM((hp, 1, tq), F32), pltpu.VMEM((hp, MLA_V + SUM_ROWS, tq), F32)],
        compiler_params=_params(("parallel", "parallel", "arbitrary")),
    )(q_mla.reshape(batch, seq, qk_w), k_mla.reshape(batch, seq, qk_w), vt_mla, w_up, w_down)

    qk4 = qk_d.reshape(2, batch, seq, dsa_w)
    tq, tk, hp = 256, 512, 8
    b_dsa = pl.pallas_call(
        functools.partial(_dsa_kernel, tq=tq, tk=tk, n_sel=n_sel, seq=seq), name="dsa",
        grid=(batch, seq // tq),
        in_specs=[pl.BlockSpec((None, None, tq, dsa_w), lambda b, i: (0, b, i, 0)),
                  pl.BlockSpec((None, None, seq, dsa_w), lambda b, i: (1, b, 0, 0), pipeline_mode=pl.Buffered(1)),
                  pl.BlockSpec((dsa_w, seq), lambda b, i: (0, b), pipeline_mode=pl.Buffered(1)),
                  pl.BlockSpec((None, tq, qi_w), lambda b, i: (b, i, 0)),
                  pl.BlockSpec((None, seq, LANES), lambda b, i: (b, 0, 0)),
                  pl.BlockSpec((None, tq, LANES), lambda b, i: (b, i, 0))],
        out_specs=pl.BlockSpec((None, tq, dsa_w), lambda b, i: (b, i, 0)),
        out_shape=jax.ShapeDtypeStruct((batch, seq, dsa_w), BF16),
        scratch_shapes=[pltpu.VMEM((seq, tq), F32), pltpu.VMEM((seq, tq), BF16), pltpu.VMEM((seq, tq), F32),
                        pltpu.VMEM((32, tq), I32), pltpu.VMEM((32, tq), jnp.int16),
                        pltpu.VMEM((hp, 1, tq), F32), pltpu.VMEM((hp, DSA_HEAD_DIM + SUM_ROWS, tq), F32)],
        compiler_params=_params(("parallel", "arbitrary")),
    )(qk4, qk4, vt_d, q_idx.reshape(batch, seq, qi_w), k_idx.reshape(batch, seq, LANES),
      w_idx_s.reshape(batch, seq, LANES))

    xw = X_HEADS * X_HEAD_DIM
    n_mem = mem.shape[1]
    k_mem, v_mem = pl.pallas_call(
        _mem_kv_kernel, name="mem_kv", grid=(batch,),
        in_specs=[pl.BlockSpec((None, n_mem, d), lambda b: (b, 0, 0)), _const((1, d)), _const((d, xw)),
                  _const((d, xw))],
        out_specs=[pl.BlockSpec((None, n_mem, xw), lambda b: (b, 0, 0))] * 2,
        out_shape=[jax.ShapeDtypeStruct((batch, n_mem, xw), BF16)] * 2,
        compiler_params=_params(("parallel",)),
    )(mem, row(g_mem), wk_c, wv_c)

    tp = 512
    per_b = seq // tp
    mem_spec = pl.BlockSpec((None, n_mem, xw), lambda i: (i // per_b, 0, 0))
    x2, hm = pl.pallas_call(
        _post_kernel, name="post", grid=(t // tp,),
        in_specs=[pl.BlockSpec((tp, mla_w), lambda i: (i, 0)), pl.BlockSpec((tp, dsa_w), lambda i: (i, 0)),
                  pl.BlockSpec((tp, d), lambda i: (i, 0)), _const((mla_w + dsa_w, d)), _const((1, d)),
                  _const((d, xw)), mem_spec, mem_spec, _const((xw, d)), _const((1, d))],
        out_specs=[pl.BlockSpec((tp, d), lambda i: (i, 0))] * 2,
        out_shape=[jax.ShapeDtypeStruct((t, d), F32), jax.ShapeDtypeStruct((t, d), BF16)],
        compiler_params=_params(("parallel",)),
    )(a_mla.reshape(t, mla_w), b_dsa.reshape(t, dsa_w), x2d, w_out[0].astype(BF16), row(g_cross), wq_c,
      k_mem, v_mem, wo_c, row(g_mlp))

    tmm, tf = 512, 1024
    y = pl.pallas_call(
        _mlp_kernel, name="mlp", grid=(t // tmm, D_FF // tf),
        in_specs=[pl.BlockSpec((tmm, d), lambda i, f: (i, 0)), pl.BlockSpec((tmm, d), lambda i, f: (i, 0)),
                  pl.BlockSpec((d, tf), lambda i, f: (0, f)), pl.BlockSpec((tf, d), lambda i, f: (f, 0)),
                  _const((1, d))],
        out_specs=pl.BlockSpec((tmm, d), lambda i, f: (i, 0)),
        out_shape=jax.ShapeDtypeStruct((t, d), F32),
        compiler_params=_params(("parallel", "arbitrary")),
    )(hm, x2, wu, wd, row(g_final))
    return y.reshape(batch, seq, d)
```

```python
import functools

import jax
import jax.numpy as jnp
from jax import lax
from jax.experimental import pallas as pl
from jax.experimental.pallas import tpu as pltpu

F32 = jnp.float32
BF16 = jnp.bfloat16
I32 = jnp.int32

D_MODEL = 2048
CHUNK = 64
ROPE_THETA = 500000.0
N_MEM = 256
EPS = 1e-6
MLA_HEADS = 8
MLA_NOPE = 128
MLA_ROPE = 64
MLA_V = 128
MLA_Q_LORA = 512
MLA_KV_LORA = 256
DSA_HEADS = 8
DSA_HEAD_DIM = 128
DSA_ROT = DSA_HEAD_DIM // 4
IDX_HEADS = 16
IDX_DIM = 64
IDX_ROT = IDX_DIM // 4
TOPK_MAX = 256
X_HEADS = 4
X_HEAD_DIM = 128
D_FF = 4 * D_MODEL

LANES = 128
W_CQ = 0
W_CKV = W_CQ + MLA_Q_LORA
W_DSA = W_CKV + MLA_KV_LORA + LANES
W_IDX = W_DSA + 3 * DSA_HEADS * DSA_HEAD_DIM
W_END = W_IDX + IDX_HEADS * IDX_DIM + LANES
MLA_QK_PAD = 256
NEG = -1e30
INT_MIN = -2 ** 31
LOG2E = 1.4426950408889634
VMEM_LIMIT = 56 * 1024 * 1024


def _params(sem, vmem=VMEM_LIMIT):
    return pltpu.CompilerParams(dimension_semantics=sem, vmem_limit_bytes=vmem)


def _rms(xf, g):
    return xf * lax.rsqrt(jnp.mean(xf * xf, axis=-1, keepdims=True) + EPS) * g


def _apply_rope(x, coeffs, half):
    c, s_lo, s_hi = coeffs
    return x * c + pltpu.roll(x, LANES - half, 1) * s_lo + pltpu.roll(x, half, 1) * s_hi


def _dot(a, b):
    return jnp.dot(a, b, preferred_element_type=F32)


def _dot_nt(a, b):
    return lax.dot_general(a, b, (((1,), (1,)), ((), ())), preferred_element_type=F32)


def _for_tiles(n, fn):
    def pair(jj, c):
        fn(2 * jj)
        fn(2 * jj + 1)
        return c

    lax.fori_loop(0, n // 2, pair, 0)

    @pl.when(n % 2 == 1)
    def _():
        fn(n - 1)


ROPE_VARIANTS = ((MLA_ROPE // 2, LANES), (DSA_ROT // 2, LANES), (IDX_ROT // 2, IDX_DIM))


def _rope_lane_plan():
    offs, o = [], 0
    for half, _ in ROPE_VARIANTS:
        offs.append(o)
        o += 2 * half
    assert o <= LANES
    return offs


def _rope_coeffs(pos, invf):
    ang = pos.astype(F32) * invf
    cos, sin = jnp.cos(ang), jnp.sin(ang)
    lane = lax.broadcasted_iota(I32, (1, LANES), 1)
    coeffs = []
    for (half, period), off in zip(ROPE_VARIANTS, _rope_lane_plan()):
        c = jnp.ones(cos.shape, F32)
        s_lo = jnp.zeros(cos.shape, F32)
        s_hi = jnp.zeros(cos.shape, F32)
        for base in range(0, LANES, period):
            shift = (base - off) % LANES
            cs = cos if shift == 0 else pltpu.roll(cos, shift, 1)
            sn = sin if shift == 0 else pltpu.roll(sin, shift, 1)
            in_lo = (lane >= base) & (lane < base + half)
            in_hi = (lane >= base + half) & (lane < base + 2 * half)
            c = jnp.where(in_lo | in_hi, cs, c)
            s_lo = jnp.where(in_lo, -sn, s_lo)
            s_hi = jnp.where(in_hi, sn, s_hi)
        coeffs.append((c, s_lo, s_hi))
    return coeffs


def _proj_kernel(x_ref, pos_ref, g_ref, invf_ref, w_ref, gcq_ref, wqb_ref, gckv_ref, wk_ref, wvt_ref, wdsa_vt_ref,
                 q_ref, k_ref, vt_ref, qk_d_ref, vt_d_ref, qi_ref, ki_ref, wi_ref):
    h = _rms(x_ref[...], g_ref[...]).astype(BF16)
    rope = _rope_coeffs(pos_ref[...], invf_ref[...])
    lane = lax.broadcasted_iota(I32, (1, LANES), 1)

    cq = _dot(h, w_ref[:, W_CQ:W_CKV])
    r = _dot(h, w_ref[:, W_CKV:W_DSA])
    q = _dot(_rms(cq, gcq_ref[...]).astype(BF16), wqb_ref[...])
    ckv = _rms(r[:, :MLA_KV_LORA], gckv_ref[...]).astype(BF16)
    kn = _dot(ckv, wk_ref[...])
    vt_ref[...] = _dot_nt(wvt_ref[...], ckv).astype(BF16)
    scale = (MLA_NOPE + MLA_ROPE) ** -0.5 * LOG2E
    kr = _apply_rope(r[:, MLA_KV_LORA:], rope[0], MLA_ROPE // 2).astype(BF16)
    for hh in range(MLA_HEADS):
        b0 = hh * MLA_QK_PAD
        q_ref[:, b0:b0 + LANES] = (q[:, b0:b0 + LANES] * scale).astype(BF16)
        qr = _apply_rope(q[:, b0 + LANES:b0 + 2 * LANES], rope[0], MLA_ROPE // 2)
        q_ref[:, b0 + LANES:b0 + 2 * LANES] = (qr * scale).astype(BF16)
        k_ref[:, b0:b0 + LANES] = kn[:, hh * LANES:(hh + 1) * LANES].astype(BF16)
        k_ref[:, b0 + LANES:b0 + 2 * LANES] = kr

    dsa_w = DSA_HEADS * DSA_HEAD_DIM
    for g, scale in enumerate((DSA_HEAD_DIM ** -0.5 * LOG2E, 1.0)):
        rd = _dot(h, w_ref[:, W_DSA + g * dsa_w:W_DSA + (g + 1) * dsa_w])
        for hh in range(DSA_HEADS):
            sl = slice(hh * LANES, (hh + 1) * LANES)
            qk_d_ref[g, :, sl] = (_apply_rope(rd[:, sl], rope[1], DSA_ROT // 2) * scale).astype(BF16)
    vt_d_ref[...] = _dot_nt(wdsa_vt_ref[...], h).astype(BF16)

    ri = _dot(h, w_ref[:, W_IDX:W_END])
    nq = IDX_HEADS * IDX_DIM
    tail = ri[:, nq:nq + LANES]
    tail_hi = pltpu.roll(tail, LANES - IDX_DIM, 1)
    for p in range(IDX_HEADS // 2):
        t = _apply_rope(ri[:, p * LANES:(p + 1) * LANES], rope[2], IDX_ROT // 2) * (IDX_DIM ** -0.5)
        qi_ref[:, (2 * p) * LANES:(2 * p + 1) * LANES] = jnp.where(lane < IDX_DIM, t, 0.0).astype(BF16)
        qi_ref[:, (2 * p + 1) * LANES:(2 * p + 2) * LANES] = jnp.where(lane >= IDX_DIM, t, 0.0).astype(BF16)
    k_dup = jnp.where(lane < IDX_DIM, tail, pltpu.roll(tail, IDX_DIM, 1))
    ki_ref[...] = _apply_rope(k_dup, rope[2], IDX_ROT // 2).astype(BF16)
    wi_ref[...] = jnp.where(lane < IDX_HEADS, tail_hi, 0.0) * (IDX_HEADS ** -0.5)


SUM_ROWS = 16


def _flash_sweep_t(n_tiles, tk, heads, lead, qk, vt, adjust, adjust_last, m_scr, acc_scr):
    m_scr[...] = jnp.full(m_scr.shape, NEG, F32)
    acc_scr[...] = jnp.zeros(acc_scr.shape, F32)
    ones = jnp.ones((SUM_ROWS, tk), BF16)

    def scores(h, off, adj):
        s_t = adj(qk(h, off))
        return s_t, jnp.max(s_t, axis=0, keepdims=True)

    def finish(h, s_and_max, off):
        s_t, m_tile = s_and_max
        m_prev = m_scr[h]
        m_new = jnp.maximum(m_prev, m_tile)
        p = jnp.exp2(s_t - m_new).astype(BF16)
        pv = _dot(jnp.concatenate([vt(h, off), ones], axis=0), p)
        acc_scr[h] = jnp.exp2(m_prev - m_new) * acc_scr[h] + pv
        m_scr[h] = m_new

    def tile(off, adjust_tile):
        adj = adjust_tile(off)
        s_ts = [scores(h, off, adj) for h in range(min(lead, heads))]
        for h in range(heads):
            finish(h, s_ts[h], off)
            if h + lead < heads:
                s_ts.append(scores(h + lead, off, adj))

    _for_tiles(n_tiles - 1, lambda j: tile(pl.multiple_of(j * tk, tk), adjust))
    tile(pl.multiple_of((n_tiles - 1) * tk, tk), adjust_last)


def _attn_out_t(acc, dv):
    return (acc[:dv] / acc[dv:dv + 1]).T.astype(BF16)


def _mla_attn_kernel(q_ref, k_ref, vt_ref, wu_ref, wd_ref, o_ref, wu_bf_ref, wd_bf_ref, m_scr, acc_scr,
                     *, tq, tk, hp):
    wu_bf_ref[...] = wu_ref[...].astype(BF16)
    wd_bf_ref[...] = wd_ref[...].astype(BF16)
    i = pl.program_id(2)
    n_tiles = (i * tq) // tk + 1
    qry_chunk = (i * tq + lax.broadcasted_iota(I32, (1, tq), 1)) // CHUNK

    def qk(h, off):
        qs = slice(h * MLA_QK_PAD, (h + 1) * MLA_QK_PAD)
        return _dot_nt(k_ref[pl.ds(off, tk), qs], q_ref[:, qs])

    def vt(h, off):
        return vt_ref[h * MLA_V:(h + 1) * MLA_V, pl.ds(off, tk)]

    def causal(off):
        key_chunk = (off + lax.broadcasted_iota(I32, (tk, 1), 0)) // CHUNK
        return lambda s_t: jnp.where(key_chunk <= qry_chunk, s_t, NEG)

    _flash_sweep_t(n_tiles, tk, hp, hp, qk, vt, lambda off: (lambda s_t: s_t), causal, m_scr, acc_scr)
    for h in range(hp):
        o_ref[:, h * MLA_V:(h + 1) * MLA_V] = _attn_out_t(acc_scr[h], MLA_V)


def _dsa_kernel(q_ref, k_ref, vt_ref, qi_ref, ki_ref, wi_ref, o_ref,
                sc_scr, sb_scr, bias_scr, cnt_scr, cnt16_scr, m_scr, acc_scr, *, tq, tk, n_sel, seq):
    i = pl.program_id(1)
    n_valid = ((i + 1) * tq + tk - 1) // tk
    qry_chunk = (i * tq + lax.broadcasted_iota(I32, (1, tq), 1)) // CHUNK

    def key_ids(off):
        return off + lax.broadcasted_iota(I32, (tk, 1), 0)

    wi_t = wi_ref[...].T

    def score_tile(j):
        off = pl.multiple_of(j * tk, tk)
        kj = ki_ref[pl.ds(off, tk), :]
        sc = jnp.zeros((tk, tq), F32)
        for hh in range(IDX_HEADS):
            lg = _dot_nt(kj, qi_ref[:, hh * LANES:(hh + 1) * LANES])
            sc = sc + jnp.maximum(lg, 0.0) * wi_t[hh:hh + 1, :]
        sc = jnp.where(key_ids(off) // CHUNK <= qry_chunk, sc, -jnp.inf)
        sc_scr[pl.ds(off, tk), :] = sc
        sb_scr[pl.ds(off, tk), :] = sc.astype(BF16)

    _for_tiles(n_valid, score_tile)

    sub = cnt_scr.shape[0]

    def count_bf16(cand):
        cnt16_scr[...] = jnp.zeros(cnt16_scr.shape, jnp.int16)

        def body(j):
            off = pl.multiple_of(j * tk, tk)
            hit = jnp.where(sb_scr[pl.ds(off, tk), :] >= cand, jnp.int16(1), jnp.int16(0))
            part = hit[:sub]
            for cc in range(1, tk // sub):
                part = part + hit[cc * sub:(cc + 1) * sub]
            cnt16_scr[...] += part

        _for_tiles(n_valid, body)
        return jnp.sum(cnt16_scr[...].astype(I32), axis=0, keepdims=True)

    def count(pred):
        cnt_scr[...] = jnp.zeros(cnt_scr.shape, I32)

        def body(j):
            off = pl.multiple_of(j * tk, tk)
            hit = jnp.where(pred(sc_scr[pl.ds(off, tk), :], off), 1, 0).astype(I32)
            part = hit[:sub]
            for cc in range(1, tk // sub):
                part = part + hit[cc * sub:(cc + 1) * sub]
            cnt_scr[...] += part

        _for_tiles(n_valid, body)
        return jnp.sum(cnt_scr[...], axis=0, keepdims=True)

    def key_to_float(key):
        return pltpu.bitcast(key ^ ((key >> 31) & I32(0x7FFFFFFF)), F32)

    half = 2 ** 15
    neg_inf_key = 0x807FFFFF - 2 ** 32

    def coarse_key(key16):
        return (key16 << 16) + ((key16 >> 31) & I32(0xFFFF))

    def coarse_bit(it, tb):
        cand_b = tb | (I32(1) << (15 - it))
        cand = key_to_float(coarse_key(cand_b - half)).astype(BF16)
        cnt = count_bf16(cand)
        return jnp.where(cnt >= n_sel, cand_b, tb)

    v_key16 = jnp.maximum(lax.fori_loop(0, 16, coarse_bit, jnp.zeros((1, tq), I32)) - half, neg_inf_key >> 16)
    lo_key = coarse_key(v_key16) - half

    def fine_bit(it, carry):
        off, cnt_ge = carry
        cand_off = off | (I32(1) << (16 - it))
        cand = key_to_float(lo_key + cand_off)
        cnt = count(lambda sc, o: sc >= cand)
        accept = cnt >= n_sel
        return jnp.where(accept, cand_off, off), jnp.where(accept, cnt, cnt_ge)

    off, cnt_ge = lax.fori_loop(0, 17, fine_bit, (jnp.zeros((1, tq), I32), jnp.full((1, tq), n_sel + 1, I32)))
    thr = key_to_float(jnp.maximum(lo_key + off, neg_inf_key))

    tied = (cnt_ge > n_sel) & (thr > -jnp.inf)
    any_tied = jnp.max(tied.astype(I32)) > 0
    idx_bits = (2 * seq - 1).bit_length()

    def tie_cut():
        need = n_sel - count(lambda sc, off: sc > thr)

        def cut_bit(it, jc):
            cand = jc | (I32(1) << (idx_bits - 1 - it))
            cnt = count(lambda sc, off: (sc == thr) & (key_ids(off) < cand))
            return jnp.where(cnt <= need, cand, jc)
        return lax.fori_loop(0, idx_bits, cut_bit, jnp.zeros((1, tq), I32))

    jcut = lax.cond(any_tied, tie_cut, lambda: jnp.full((1, tq), 2 ** idx_bits - 1, I32))

    def write_bias(off, sel, causal):
        if causal:
            sel = sel & (key_ids(off) // CHUNK <= qry_chunk)
        bias_scr[pl.ds(off, tk), :] = jnp.where(sel, 0.0, NEG).astype(F32)

    def sel_plain(off):
        return sc_scr[pl.ds(off, tk), :] >= thr

    def sel_general(off):
        sc = sc_scr[pl.ds(off, tk), :]
        return (sc > thr) | ((sc == thr) & (key_ids(off) < jcut))

    off_last = pl.multiple_of((n_valid - 1) * tk, tk)
    for pred, sel_fn in ((any_tied, sel_general), (jnp.logical_not(any_tied), sel_plain)):
        @pl.when(pred)
        def _(sel_fn=sel_fn):
            def bias_tile(j, c):
                off = pl.multiple_of(j * tk, tk)
                write_bias(off, sel_fn(off), causal=False)
                return c

            lax.fori_loop(0, n_valid - 1, bias_tile, 0)
            write_bias(off_last, sel_fn(off_last), causal=True)

    hp = m_scr.shape[0]

    def biased(off):
        bias = bias_scr[pl.ds(off, tk), :]
        return lambda s_t: s_t + bias

    for g in range(DSA_HEADS // hp):
        def head_cols(h, g=g):
            return slice((g * hp + h) * DSA_HEAD_DIM, (g * hp + h + 1) * DSA_HEAD_DIM)

        def qk(h, off):
            return _dot_nt(k_ref[pl.ds(off, tk), head_cols(h)], q_ref[:, head_cols(h)])

        def vt(h, off):
            return vt_ref[head_cols(h), pl.ds(off, tk)]

        _flash_sweep_t(n_valid, tk, hp, hp // 2, qk, vt, biased, biased, m_scr, acc_scr)
        for h in range(hp):
            o_ref[:, head_cols(h)] = _attn_out_t(acc_scr[h], DSA_HEAD_DIM)


def _mem_kv_kernel(mem_ref, g_ref, wk_ref, wv_ref, k_ref, v_ref):
    mn = _rms(mem_ref[...], g_ref[...]).astype(BF16)
    k_ref[...] = _dot(mn, wk_ref[...]).astype(BF16)
    v_ref[...] = _dot(mn, wv_ref[...]).astype(BF16)


def _post_kernel(a_ref, b_ref, x_ref, wout_ref, gc_ref, wq_ref, km_ref, vm_ref, wo_ref, gm_ref,
                 x2_ref, hm_ref):
    na = a_ref.shape[1]
    x1 = x_ref[...] + _dot(a_ref[...], wout_ref[:na]) + _dot(b_ref[...], wout_ref[na:])
    hc = _rms(x1, gc_ref[...]).astype(BF16)
    qc = (_dot(hc, wq_ref[...]) * (X_HEAD_DIM ** -0.5)).astype(BF16)
    outs = []
    for hh in range(X_HEADS):
        sl = slice(hh * X_HEAD_DIM, (hh + 1) * X_HEAD_DIM)
        s = _dot_nt(qc[:, sl], km_ref[:, sl])
        p = jnp.exp(s - jnp.max(s, axis=-1, keepdims=True))
        o = _dot(p.astype(BF16), vm_ref[:, sl]) / jnp.sum(p, axis=-1, keepdims=True)
        outs.append(o.astype(BF16))
    x2 = x1 + _dot(jnp.concatenate(outs, axis=-1), wo_ref[...])
    x2_ref[...] = x2
    hm_ref[...] = _rms(x2, gm_ref[...]).astype(BF16)


def _mlp_kernel(hm_ref, x2_ref, wu_ref, wd_ref, gf_ref, y_ref):
    f = pl.program_id(1)

    @pl.when(f == 0)
    def _():
        y_ref[...] = x2_ref[...]

    u = jnp.maximum(_dot(hm_ref[...], wu_ref[...]), 0.0)
    y_ref[...] += _dot((u * u).astype(BF16), wd_ref[...])

    @pl.when(f == pl.num_programs(1) - 1)
    def _():
        y_ref[...] = _rms(y_ref[...], gf_ref[...])


def _const(shape):
    return pl.BlockSpec(shape, lambda *_: (0,) * len(shape), pipeline_mode=pl.Buffered(1))


def kernel(x, mem, positions, g_mix, w_in, g_cq, g_ckv, w_qb, w_kvb, w_out, g_cross, g_mem,
           w_q_cross, w_k_cross, w_v_cross, w_o_cross, g_mlp, w_up, w_down, g_final):
    assert w_in.shape[0] == 1, "one layer"
    batch, seq, d = x.shape
    t = batch * seq
    n_sel = min(TOPK_MAX, seq // 4)
    x2d = x.reshape(t, d)
    pos = positions.reshape(t, 1)
    row = lambda g: g.reshape(1, -1).astype(F32)

    w = w_in[0]
    o = [0]
    for n in (MLA_Q_LORA, MLA_KV_LORA, MLA_ROPE, DSA_HEADS * DSA_HEAD_DIM, DSA_HEADS * DSA_HEAD_DIM,
              DSA_HEADS * DSA_HEAD_DIM, IDX_HEADS * IDX_DIM, IDX_DIM, IDX_HEADS):
        o.append(o[-1] + n)
    zeros = lambda n: jnp.zeros((d, n), w.dtype)
    w_pad = jnp.concatenate([w[:, :o[3]], zeros(LANES - MLA_ROPE), w[:, o[3]:],
                             zeros(W_END - o[9] - (LANES - MLA_ROPE))], axis=1).astype(BF16)
    w_dsa_vt = w[:, o[5]:o[6]].T.astype(BF16)
    wqb = w_qb[0].reshape(MLA_Q_LORA, MLA_HEADS, MLA_NOPE + MLA_ROPE)
    wqb = jnp.pad(wqb, ((0, 0), (0, 0), (0, MLA_QK_PAD - MLA_NOPE - MLA_ROPE)))
    wqb = wqb.reshape(MLA_Q_LORA, MLA_HEADS * MLA_QK_PAD).astype(BF16)
    wkvb = w_kvb[0].reshape(MLA_KV_LORA, MLA_HEADS, MLA_NOPE + MLA_V)
    wkvb_k = wkvb[:, :, :MLA_NOPE].reshape(MLA_KV_LORA, -1).astype(BF16)
    wkvb_vt = wkvb[:, :, MLA_NOPE:].reshape(MLA_KV_LORA, -1).T.astype(BF16)
    mla_w = MLA_HEADS * MLA_V
    wq_c, wk_c, wv_c, wo_c = (a[0].astype(BF16) for a in (w_q_cross, w_k_cross, w_v_cross, w_o_cross))
    invf = jnp.zeros((LANES,), F32)
    for (half, _), off in zip(ROPE_VARIANTS, _rope_lane_plan()):
        inv_freq = ROPE_THETA ** (-jnp.arange(half, dtype=F32) / half)
        invf = invf.at[off:off + 2 * half].set(jnp.concatenate([inv_freq, inv_freq]))
    invf = invf.reshape(1, LANES)

    qk_w = MLA_HEADS * MLA_QK_PAD
    dsa_w = DSA_HEADS * DSA_HEAD_DIM
    qi_w = IDX_HEADS * LANES
    tpj = 256
    rows = lambda w: pl.BlockSpec((tpj, w), lambda i: (i, 0))
    cols = lambda w: pl.BlockSpec((w, tpj), lambda i: (0, i))
    q_mla, k_mla, vt_mla, qk_d, vt_d, q_idx, k_idx, w_idx_s = pl.pallas_call(
        _proj_kernel, name="proj", grid=(t // tpj,),
        in_specs=[rows(d), rows(1), _const((1, d)), _const((1, LANES)),
                  _const((d, W_END)), _const((1, MLA_Q_LORA)), _const((MLA_Q_LORA, qk_w)),
                  _const((1, MLA_KV_LORA)), _const((MLA_KV_LORA, mla_w)), _const((mla_w, MLA_KV_LORA)),
                  _const((dsa_w, d))],
        out_specs=[rows(qk_w), rows(qk_w), cols(mla_w),
                   pl.BlockSpec((2, tpj, dsa_w), lambda i: (0, i, 0)), cols(dsa_w),
                   rows(qi_w), rows(LANES), rows(LANES)],
        out_shape=[jax.ShapeDtypeStruct((t, qk_w), BF16), jax.ShapeDtypeStruct((t, qk_w), BF16),
                   jax.ShapeDtypeStruct((mla_w, t), BF16),
                   jax.ShapeDtypeStruct((2, t, dsa_w), BF16), jax.ShapeDtypeStruct((dsa_w, t), BF16),
                   jax.ShapeDtypeStruct((t, qi_w), BF16), jax.ShapeDtypeStruct((t, LANES), BF16),
                   jax.ShapeDtypeStruct((t, LANES), F32)],
        compiler_params=_params(("parallel",)),
    )(x2d, pos, row(g_mix), invf, w_pad, row(g_cq), wqb, row(g_ckv), wkvb_k, wkvb_vt, w_dsa_vt)

    tq, tk, hp = 256, 512, 8
    mla_grid = (batch, MLA_HEADS // hp, seq // tq)
    n_steps = mla_grid[0] * mla_grid[1] * mla_grid[2]
    step = lambda b, h, i: (b * mla_grid[1] + h) * mla_grid[2] + i
    wu_rows, wd_rows = d // n_steps, D_FF // n_steps
    a_mla, wu, wd = pl.pallas_call(
        functools.partial(_mla_attn_kernel, tq=tq, tk=tk, hp=hp), name="mla_attn", grid=mla_grid,
        in_specs=[pl.BlockSpec((None, tq, hp * MLA_QK_PAD), lambda b, h, i: (b, i, h)),
                  pl.BlockSpec((None, seq, hp * MLA_QK_PAD), lambda b, h, i: (b, 0, h),
                               pipeline_mode=pl.Buffered(1)),
                  pl.BlockSpec((hp * MLA_V, seq), lambda b, h, i: (h, b), pipeline_mode=pl.Buffered(1)),
                  pl.BlockSpec((None, wu_rows, D_FF), lambda b, h, i: (0, step(b, h, i), 0)),
                  pl.BlockSpec((None, wd_rows, d), lambda b, h, i: (0, step(b, h, i), 0))],
        out_specs=[pl.BlockSpec((None, tq, hp * MLA_V), lambda b, h, i: (b, i, h)),
                   pl.BlockSpec((wu_rows, D_FF), lambda b, h, i: (step(b, h, i), 0)),
                   pl.BlockSpec((wd_rows, d), lambda b, h, i: (step(b, h, i), 0))],
        out_shape=[jax.ShapeDtypeStruct((batch, seq, mla_w), BF16),
                   jax.ShapeDtypeStruct((d, D_FF), BF16), jax.ShapeDtypeStruct((D_FF, d), BF16)],
        scratch_shapes=[pltpu.VMEM((hp, 1, tq), F32), pltpu.VMEM((hp, MLA_V + SUM_ROWS, tq), F32)],
        compiler_params=_params(("parallel", "parallel", "arbitrary")),
    )(q_mla.reshape(batch, seq, qk_w), k_mla.reshape(batch, seq, qk_w), vt_mla, w_up, w_down)

    qk4 = qk_d.reshape(2, batch, seq, dsa_w)
    tq, tk, hp = 256, 512, 8
    b_dsa = pl.pallas_call(
        functools.partial(_dsa_kernel, tq=tq, tk=tk, n_sel=n_sel, seq=seq), name="dsa",
        grid=(batch, seq // tq),
        in_specs=[pl.BlockSpec((None, None, tq, dsa_w), lambda b, i: (0, b, i, 0)),
                  pl.BlockSpec((None, None, seq, dsa_w), lambda b, i: (1, b, 0, 0), pipeline_mode=pl.Buffered(1)),
                  pl.BlockSpec((dsa_w, seq), lambda b, i: (0, b), pipeline_mode=pl.Buffered(1)),
                  pl.BlockSpec((None, tq, qi_w), lambda b, i: (b, i, 0)),
                  pl.BlockSpec((None, seq, LANES), lambda b, i: (b, 0, 0)),
                  pl.BlockSpec((None, tq, LANES), lambda b, i: (b, i, 0))],
        out_specs=pl.BlockSpec((None, tq, dsa_w), lambda b, i: (b, i, 0)),
        out_shape=jax.ShapeDtypeStruct((batch, seq, dsa_w), BF16),
        scratch_shapes=[pltpu.VMEM((seq, tq), F32), pltpu.VMEM((seq, tq), BF16), pltpu.VMEM((seq, tq), F32),
                        pltpu.VMEM((32, tq), I32), pltpu.VMEM((32, tq), jnp.int16),
                        pltpu.VMEM((hp, 1, tq), F32), pltpu.VMEM((hp, DSA_HEAD_DIM + SUM_ROWS, tq), F32)],
        compiler_params=_params(("parallel", "arbitrary")),
    )(qk4, qk4, vt_d, q_idx.reshape(batch, seq, qi_w), k_idx.reshape(batch, seq, LANES),
      w_idx_s.reshape(batch, seq, LANES))

    xw = X_HEADS * X_HEAD_DIM
    n_mem = mem.shape[1]
    k_mem, v_mem = pl.pallas_call(
        _mem_kv_kernel, name="mem_kv", grid=(batch,),
        in_specs=[pl.BlockSpec((None, n_mem, d), lambda b: (b, 0, 0)), _const((1, d)), _const((d, xw)),
                  _const((d, xw))],
        out_specs=[pl.BlockSpec((None, n_mem, xw), lambda b: (b, 0, 0))] * 2,
        out_shape=[jax.ShapeDtypeStruct((batch, n_mem, xw), BF16)] * 2,
        compiler_params=_params(("parallel",)),
    )(mem, row(g_mem), wk_c, wv_c)

    tp = 512
    per_b = seq // tp
    mem_spec = pl.BlockSpec((None, n_mem, xw), lambda i: (i // per_b, 0, 0))
    x2, hm = pl.pallas_call(
        _post_kernel, name="post", grid=(t // tp,),
        in_specs=[pl.BlockSpec((tp, mla_w), lambda i: (i, 0)), pl.BlockSpec((tp, dsa_w), lambda i: (i, 0)),
                  pl.BlockSpec((tp, d), lambda i: (i, 0)), _const((mla_w + dsa_w, d)), _const((1, d)),
                  _const((d, xw)), mem_spec, mem_spec, _const((xw, d)), _const((1, d))],
        out_specs=[pl.BlockSpec((tp, d), lambda i: (i, 0))] * 2,
        out_shape=[jax.ShapeDtypeStruct((t, d), F32), jax.ShapeDtypeStruct((t, d), BF16)],
        compiler_params=_params(("parallel",)),
    )(a_mla.reshape(t, mla_w), b_dsa.reshape(t, dsa_w), x2d, w_out[0].astype(BF16), row(g_cross), wq_c,
      k_mem, v_mem, wo_c, row(g_mlp))

    tmm, tf = 512, 1024
    y = pl.pallas_call(
        _mlp_kernel, name="mlp", grid=(t // tmm, D_FF // tf),
        in_specs=[pl.BlockSpec((tmm, d), lambda i, f: (i, 0)), pl.BlockSpec((tmm, d), lambda i, f: (i, 0)),
                  pl.BlockSpec((d, tf), lambda i, f: (0, f)), pl.BlockSpec((tf, d), lambda i, f: (f, 0)),
                  _const((1, d))],
        out_specs=pl.BlockSpec((tmm, d), lambda i, f: (i, 0)),
        out_shape=jax.ShapeDtypeStruct((t, d), F32),
        compiler_params=_params(("parallel", "arbitrary")),
    )(hm, x2, wu, wd, row(g_final))
    return y.reshape(batch, seq, d)
```

```python
import functools

import jax
import jax.numpy as jnp
from jax import lax
from jax.experimental import pallas as pl
from jax.experimental.pallas import tpu as pltpu

F32 = jnp.float32
BF16 = jnp.bfloat16
I32 = jnp.int32

D_MODEL = 2048
CHUNK = 64
ROPE_THETA = 500000.0
N_MEM = 256
EPS = 1e-6
MLA_HEADS = 8
MLA_NOPE = 128
MLA_ROPE = 64
MLA_V = 128
MLA_Q_LORA = 512
MLA_KV_LORA = 256
DSA_HEADS = 8
DSA_HEAD_DIM = 128
DSA_ROT = DSA_HEAD_DIM // 4
IDX_HEADS = 16
IDX_DIM = 64
IDX_ROT = IDX_DIM // 4
TOPK_MAX = 256
X_HEADS = 4
X_HEAD_DIM = 128
D_FF = 4 * D_MODEL

LANES = 128
O_CQ = 0
O_CKV = O_CQ + MLA_Q_LORA
O_KR = O_CKV + MLA_KV_LORA
O_QD = O_KR + MLA_ROPE
O_KD = O_QD + DSA_HEADS * DSA_HEAD_DIM
O_VD = O_KD + DSA_HEADS * DSA_HEAD_DIM
O_QI = O_VD + DSA_HEADS * DSA_HEAD_DIM
O_KI = O_QI + IDX_HEADS * IDX_DIM
O_WI = O_KI + IDX_DIM
O_END = O_WI + IDX_HEADS
MLA_QK_PAD = 256
NEG = -1e30
INT_MIN = -2 ** 31
LOG2E = 1.4426950408889634
VMEM_LIMIT = 56 * 1024 * 1024


def _params(sem, vmem=VMEM_LIMIT):
    return pltpu.CompilerParams(dimension_semantics=sem, vmem_limit_bytes=vmem)


def _rms(xf, g):
    return xf * lax.rsqrt(jnp.mean(xf * xf, axis=-1, keepdims=True) + EPS) * g


def _apply_rope(x, coeffs, half):
    c, s_lo, s_hi = coeffs
    return x * c + pltpu.roll(x, LANES - half, 1) * s_lo + pltpu.roll(x, half, 1) * s_hi


def _dot(a, b):
    return jnp.dot(a, b, preferred_element_type=F32)


def _dot_nt(a, b):
    return lax.dot_general(a, b, (((1,), (1,)), ((), ())), preferred_element_type=F32)


def _for_tiles(n, fn):
    def pair(jj, c):
        fn(2 * jj)
        fn(2 * jj + 1)
        return c

    lax.fori_loop(0, n // 2, pair, 0)

    @pl.when(n % 2 == 1)
    def _():
        fn(n - 1)


ROPE_VARIANTS = ((MLA_ROPE // 2, LANES), (DSA_ROT // 2, LANES), (IDX_ROT // 2, IDX_DIM))


def _rope_lane_plan():
    offs, o = [], 0
    for half, _ in ROPE_VARIANTS:
        offs.append(o)
        o += 2 * half
    assert o <= LANES
    return offs


def _rope_coeffs(pos, invf):
    ang = pos.astype(F32) * invf
    cos, sin = jnp.cos(ang), jnp.sin(ang)
    lane = lax.broadcasted_iota(I32, (1, LANES), 1)
    coeffs = []
    for (half, period), off in zip(ROPE_VARIANTS, _rope_lane_plan()):
        c = jnp.ones(cos.shape, F32)
        s_lo = jnp.zeros(cos.shape, F32)
        s_hi = jnp.zeros(cos.shape, F32)
        for base in range(0, LANES, period):
            shift = (base - off) % LANES
            cs = cos if shift == 0 else pltpu.roll(cos, shift, 1)
            sn = sin if shift == 0 else pltpu.roll(sin, shift, 1)
            in_lo = (lane >= base) & (lane < base + half)
            in_hi = (lane >= base + half) & (lane < base + 2 * half)
            c = jnp.where(in_lo | in_hi, cs, c)
            s_lo = jnp.where(in_lo, -sn, s_lo)
            s_hi = jnp.where(in_hi, sn, s_hi)
        coeffs.append((c, s_lo, s_hi))
    return coeffs


def _proj_kernel(x_ref, pos_ref, g_ref, invf_ref, wt_ref, gcq_ref, wqb_ref, gckv_ref, wk_ref, wvt_ref,
                 q_ref, k_ref, vt_ref, qk_d_ref, vt_d_ref, qi_ref, ki_ref, wi_ref):
    h = _rms(x_ref[...], g_ref[...]).astype(BF16)
    rope = _rope_coeffs(pos_ref[...], invf_ref[...])
    lane = lax.broadcasted_iota(I32, (1, LANES), 1)

    cq = _dot_nt(h, wt_ref[O_CQ:O_CKV])
    r = _dot_nt(h, wt_ref[O_CKV:O_KR + LANES])
    q = _dot(_rms(cq, gcq_ref[...]).astype(BF16), wqb_ref[...])
    ckv = _rms(r[:, :MLA_KV_LORA], gckv_ref[...]).astype(BF16)
    kn = _dot(ckv, wk_ref[...])
    vt_ref[...] = _dot_nt(wvt_ref[...], ckv).astype(BF16)
    scale = (MLA_NOPE + MLA_ROPE) ** -0.5 * LOG2E
    kr = jnp.where(lane < MLA_ROPE, r[:, MLA_KV_LORA:], 0.0)
    kr = _apply_rope(kr, rope[0], MLA_ROPE // 2).astype(BF16)
    for hh in range(MLA_HEADS):
        b0 = hh * MLA_QK_PAD
        q_ref[:, b0:b0 + LANES] = (q[:, b0:b0 + LANES] * scale).astype(BF16)
        qr = _apply_rope(q[:, b0 + LANES:b0 + 2 * LANES], rope[0], MLA_ROPE // 2)
        q_ref[:, b0 + LANES:b0 + 2 * LANES] = (qr * scale).astype(BF16)
        k_ref[:, b0:b0 + LANES] = kn[:, hh * LANES:(hh + 1) * LANES].astype(BF16)
        k_ref[:, b0 + LANES:b0 + 2 * LANES] = kr

    dsa_w = DSA_HEADS * DSA_HEAD_DIM
    for g, scale in enumerate((DSA_HEAD_DIM ** -0.5 * LOG2E, 1.0)):
        rd = _dot_nt(h, wt_ref[O_QD + g * dsa_w:O_QD + (g + 1) * dsa_w])
        for hh in range(DSA_HEADS):
            sl = slice(hh * LANES, (hh + 1) * LANES)
            qk_d_ref[g, :, sl] = (_apply_rope(rd[:, sl], rope[1], DSA_ROT // 2) * scale).astype(BF16)
    vt_d_ref[...] = _dot_nt(wt_ref[O_VD:O_QI], h).astype(BF16)

    ri = _dot_nt(h, wt_ref[O_QI:O_KI])
    t0 = O_END - LANES
    tail = _dot_nt(h, wt_ref[t0:O_END])
    k_lo = pltpu.roll(tail, LANES - (O_KI - t0), 1)
    tail_w = pltpu.roll(tail, LANES - (O_WI - t0), 1)
    assert O_WI - O_KI == IDX_DIM
    for p in range(IDX_HEADS // 2):
        t = _apply_rope(ri[:, p * LANES:(p + 1) * LANES], rope[2], IDX_ROT // 2) * (IDX_DIM ** -0.5)
        qi_ref[:, (2 * p) * LANES:(2 * p + 1) * LANES] = jnp.where(lane < IDX_DIM, t, 0.0).astype(BF16)
        qi_ref[:, (2 * p + 1) * LANES:(2 * p + 2) * LANES] = jnp.where(lane >= IDX_DIM, t, 0.0).astype(BF16)
    k_dup = jnp.where(lane < IDX_DIM, k_lo, tail_w)
    ki_ref[...] = _apply_rope(k_dup, rope[2], IDX_ROT // 2).astype(BF16)
    wi_ref[...] = jnp.where(lane < IDX_HEADS, tail_w, 0.0) * (IDX_HEADS ** -0.5)


SUM_ROWS = 16


def _flash_sweep_t(n_tiles, tk, heads, lead, qk, vt, adjust, adjust_last, m_scr, acc_scr):
    m_scr[...] = jnp.full(m_scr.shape, NEG, F32)
    acc_scr[...] = jnp.zeros(acc_scr.shape, F32)
    ones = jnp.ones((SUM_ROWS, tk), BF16)

    def scores(h, off, adj):
        s_t = adj(qk(h, off))
        return s_t, jnp.max(s_t, axis=0, keepdims=True)

    def finish(h, s_and_max, off):
        s_t, m_tile = s_and_max
        m_prev = m_scr[h]
        m_new = jnp.maximum(m_prev, m_tile)
        p = jnp.exp2(s_t - m_new).astype(BF16)
        pv = _dot(jnp.concatenate([vt(h, off), ones], axis=0), p)
        acc_scr[h] = jnp.exp2(m_prev - m_new) * acc_scr[h] + pv
        m_scr[h] = m_new

    def tile(off, adjust_tile):
        adj = adjust_tile(off)
        s_ts = [scores(h, off, adj) for h in range(min(lead, heads))]
        for h in range(heads):
            finish(h, s_ts[h], off)
            if h + lead < heads:
                s_ts.append(scores(h + lead, off, adj))

    _for_tiles(n_tiles - 1, lambda j: tile(pl.multiple_of(j * tk, tk), adjust))
    tile(pl.multiple_of((n_tiles - 1) * tk, tk), adjust_last)


def _attn_out_t(acc, dv):
    return (acc[:dv] / acc[dv:dv + 1]).T.astype(BF16)


def _mla_attn_kernel(q_ref, k_ref, vt_ref, wu_ref, wd_ref, o_ref, wu_bf_ref, wd_bf_ref, m_scr, acc_scr,
                     *, tq, tk, hp):
    wu_bf_ref[...] = wu_ref[...].astype(BF16)
    wd_bf_ref[...] = wd_ref[...].astype(BF16)
    i = pl.program_id(2)
    n_tiles = (i * tq) // tk + 1
    qry_chunk = (i * tq + lax.broadcasted_iota(I32, (1, tq), 1)) // CHUNK

    def qk(h, off):
        qs = slice(h * MLA_QK_PAD, (h + 1) * MLA_QK_PAD)
        return _dot_nt(k_ref[pl.ds(off, tk), qs], q_ref[:, qs])

    def vt(h, off):
        return vt_ref[h * MLA_V:(h + 1) * MLA_V, pl.ds(off, tk)]

    def causal(off):
        key_chunk = (off + lax.broadcasted_iota(I32, (tk, 1), 0)) // CHUNK
        return lambda s_t: jnp.where(key_chunk <= qry_chunk, s_t, NEG)

    _flash_sweep_t(n_tiles, tk, hp, hp, qk, vt, lambda off: (lambda s_t: s_t), causal, m_scr, acc_scr)
    for h in range(hp):
        o_ref[:, h * MLA_V:(h + 1) * MLA_V] = _attn_out_t(acc_scr[h], MLA_V)


def _dsa_kernel(q_ref, k_ref, vt_ref, qi_ref, ki_ref, wi_ref, o_ref,
                sc_scr, sb_scr, bias_scr, cnt_scr, cnt16_scr, m_scr, acc_scr, *, tq, tk, n_sel, seq):
    i = pl.program_id(1)
    n_valid = ((i + 1) * tq + tk - 1) // tk
    qry_chunk = (i * tq + lax.broadcasted_iota(I32, (1, tq), 1)) // CHUNK

    def key_ids(off):
        return off + lax.broadcasted_iota(I32, (tk, 1), 0)

    wi_t = wi_ref[...].T

    def score_tile(j):
        off = pl.multiple_of(j * tk, tk)
        kj = ki_ref[pl.ds(off, tk), :]
        sc = jnp.zeros((tk, tq), F32)
        for hh in range(IDX_HEADS):
            lg = _dot_nt(kj, qi_ref[:, hh * LANES:(hh + 1) * LANES])
            sc = sc + jnp.maximum(lg, 0.0) * wi_t[hh:hh + 1, :]
        sc = jnp.where(key_ids(off) // CHUNK <= qry_chunk, sc, -jnp.inf)
        sc_scr[pl.ds(off, tk), :] = sc
        sb_scr[pl.ds(off, tk), :] = sc.astype(BF16)

    _for_tiles(n_valid, score_tile)

    sub = cnt_scr.shape[0]

    def count_bf16(cand):
        cnt16_scr[...] = jnp.zeros(cnt16_scr.shape, jnp.int16)

        def body(j):
            off = pl.multiple_of(j * tk, tk)
            hit = jnp.where(sb_scr[pl.ds(off, tk), :] >= cand, jnp.int16(1), jnp.int16(0))
            part = hit[:sub]
            for cc in range(1, tk // sub):
                part = part + hit[cc * sub:(cc + 1) * sub]
            cnt16_scr[...] += part

        _for_tiles(n_valid, body)
        return jnp.sum(cnt16_scr[...].astype(I32), axis=0, keepdims=True)

    def count(pred):
        cnt_scr[...] = jnp.zeros(cnt_scr.shape, I32)

        def body(j):
            off = pl.multiple_of(j * tk, tk)
            hit = jnp.where(pred(sc_scr[pl.ds(off, tk), :], off), 1, 0).astype(I32)
            part = hit[:sub]
            for cc in range(1, tk // sub):
                part = part + hit[cc * sub:(cc + 1) * sub]
            cnt_scr[...] += part

        _for_tiles(n_valid, body)
        return jnp.sum(cnt_scr[...], axis=0, keepdims=True)

    def key_to_float(key):
        return pltpu.bitcast(key ^ ((key >> 31) & I32(0x7FFFFFFF)), F32)

    half = 2 ** 15
    neg_inf_key = 0x807FFFFF - 2 ** 32

    def coarse_key(key16):
        return (key16 << 16) + ((key16 >> 31) & I32(0xFFFF))

    def coarse_bit(it, tb):
        cand_b = tb | (I32(1) << (15 - it))
        cand = key_to_float(coarse_key(cand_b - half)).astype(BF16)
        cnt = count_bf16(cand)
        return jnp.where(cnt >= n_sel, cand_b, tb)

    v_key16 = jnp.maximum(lax.fori_loop(0, 16, coarse_bit, jnp.zeros((1, tq), I32)) - half, neg_inf_key >> 16)
    lo_key = coarse_key(v_key16) - half

    def fine_bit(it, carry):
        off, cnt_ge = carry
        cand_off = off | (I32(1) << (16 - it))
        cand = key_to_float(lo_key + cand_off)
        cnt = count(lambda sc, o: sc >= cand)
        accept = cnt >= n_sel
        return jnp.where(accept, cand_off, off), jnp.where(accept, cnt, cnt_ge)

    off, cnt_ge = lax.fori_loop(0, 17, fine_bit, (jnp.zeros((1, tq), I32), jnp.full((1, tq), n_sel + 1, I32)))
    thr = key_to_float(jnp.maximum(lo_key + off, neg_inf_key))

    tied = (cnt_ge > n_sel) & (thr > -jnp.inf)
    any_tied = jnp.max(tied.astype(I32)) > 0
    idx_bits = (2 * seq - 1).bit_length()

    def tie_cut():
        need = n_sel - count(lambda sc, off: sc > thr)

        def cut_bit(it, jc):
            cand = jc | (I32(1) << (idx_bits - 1 - it))
            cnt = count(lambda sc, off: (sc == thr) & (key_ids(off) < cand))
            return jnp.where(cnt <= need, cand, jc)
        return lax.fori_loop(0, idx_bits, cut_bit, jnp.zeros((1, tq), I32))

    jcut = lax.cond(any_tied, tie_cut, lambda: jnp.full((1, tq), 2 ** idx_bits - 1, I32))

    def write_bias(off, sel, causal):
        if causal:
            sel = sel & (key_ids(off) // CHUNK <= qry_chunk)
        bias_scr[pl.ds(off, tk), :] = jnp.where(sel, 0.0, NEG).astype(F32)

    def sel_plain(off):
        return sc_scr[pl.ds(off, tk), :] >= thr

    def sel_general(off):
        sc = sc_scr[pl.ds(off, tk), :]
        return (sc > thr) | ((sc == thr) & (key_ids(off) < jcut))

    off_last = pl.multiple_of((n_valid - 1) * tk, tk)
    for pred, sel_fn in ((any_tied, sel_general), (jnp.logical_not(any_tied), sel_plain)):
        @pl.when(pred)
        def _(sel_fn=sel_fn):
            def bias_tile(j, c):
                off = pl.multiple_of(j * tk, tk)
                write_bias(off, sel_fn(off), causal=False)
                return c

            lax.fori_loop(0, n_valid - 1, bias_tile, 0)
            write_bias(off_last, sel_fn(off_last), causal=True)

    hp = m_scr.shape[0]

    def biased(off):
        bias = bias_scr[pl.ds(off, tk), :]
        return lambda s_t: s_t + bias

    for g in range(DSA_HEADS // hp):
        def head_cols(h, g=g):
            return slice((g * hp + h) * DSA_HEAD_DIM, (g * hp + h + 1) * DSA_HEAD_DIM)

        def qk(h, off):
            return _dot_nt(k_ref[pl.ds(off, tk), head_cols(h)], q_ref[:, head_cols(h)])

        def vt(h, off):
            return vt_ref[head_cols(h), pl.ds(off, tk)]

        _flash_sweep_t(n_valid, tk, hp, hp // 2, qk, vt, biased, biased, m_scr, acc_scr)
        for h in range(hp):
            o_ref[:, head_cols(h)] = _attn_out_t(acc_scr[h], DSA_HEAD_DIM)


def _mem_kv_kernel(mem_ref, g_ref, wk_ref, wv_ref, k_ref, v_ref):
    mn = _rms(mem_ref[...], g_ref[...]).astype(BF16)
    k_ref[...] = _dot(mn, wk_ref[...]).astype(BF16)
    v_ref[...] = _dot(mn, wv_ref[...]).astype(BF16)


def _post_kernel(a_ref, b_ref, x_ref, wout_ref, gc_ref, wq_ref, km_ref, vm_ref, wo_ref, gm_ref,
                 x2_ref, hm_ref):
    na = a_ref.shape[1]
    x1 = x_ref[...] + _dot(a_ref[...], wout_ref[:na]) + _dot(b_ref[...], wout_ref[na:])
    hc = _rms(x1, gc_ref[...]).astype(BF16)
    qc = (_dot(hc, wq_ref[...]) * (X_HEAD_DIM ** -0.5)).astype(BF16)
    outs = []
    for hh in range(X_HEADS):
        sl = slice(hh * X_HEAD_DIM, (hh + 1) * X_HEAD_DIM)
        s = _dot_nt(qc[:, sl], km_ref[:, sl])
        p = jnp.exp(s - jnp.max(s, axis=-1, keepdims=True))
        o = _dot(p.astype(BF16), vm_ref[:, sl]) / jnp.sum(p, axis=-1, keepdims=True)
        outs.append(o.astype(BF16))
    x2 = x1 + _dot(jnp.concatenate(outs, axis=-1), wo_ref[...])
    x2_ref[...] = x2
    hm_ref[...] = _rms(x2, gm_ref[...]).astype(BF16)


def _mlp_kernel(hm_ref, x2_ref, wu_ref, wd_ref, gf_ref, y_ref):
    f = pl.program_id(1)

    @pl.when(f == 0)
    def _():
        y_ref[...] = x2_ref[...]

    u = jnp.maximum(_dot(hm_ref[...], wu_ref[...]), 0.0)
    y_ref[...] += _dot((u * u).astype(BF16), wd_ref[...])

    @pl.when(f == pl.num_programs(1) - 1)
    def _():
        y_ref[...] = _rms(y_ref[...], gf_ref[...])


def _const(shape):
    return pl.BlockSpec(shape, lambda *_: (0,) * len(shape), pipeline_mode=pl.Buffered(1))


def kernel(x, mem, positions, g_mix, w_in, g_cq, g_ckv, w_qb, w_kvb, w_out, g_cross, g_mem,
           w_q_cross, w_k_cross, w_v_cross, w_o_cross, g_mlp, w_up, w_down, g_final):
    assert w_in.shape[0] == 1, "one layer"
    batch, seq, d = x.shape
    t = batch * seq
    n_sel = min(TOPK_MAX, seq // 4)
    x2d = x.reshape(t, d)
    pos = positions.reshape(t, 1)
    row = lambda g: g.reshape(1, -1).astype(F32)

    assert w_in.shape[2] == O_END
    w_in_t = jnp.swapaxes(w_in[0], 0, 1).astype(BF16)
    wqb = w_qb[0].reshape(MLA_Q_LORA, MLA_HEADS, MLA_NOPE + MLA_ROPE)
    wqb = jnp.pad(wqb, ((0, 0), (0, 0), (0, MLA_QK_PAD - MLA_NOPE - MLA_ROPE)))
    wqb = wqb.reshape(MLA_Q_LORA, MLA_HEADS * MLA_QK_PAD).astype(BF16)
    wkvb = w_kvb[0].reshape(MLA_KV_LORA, MLA_HEADS, MLA_NOPE + MLA_V)
    wkvb_k = wkvb[:, :, :MLA_NOPE].reshape(MLA_KV_LORA, -1).astype(BF16)
    wkvb_vt = wkvb[:, :, MLA_NOPE:].reshape(MLA_KV_LORA, -1).T.astype(BF16)
    mla_w = MLA_HEADS * MLA_V
    wq_c, wk_c, wv_c, wo_c = (a[0].astype(BF16) for a in (w_q_cross, w_k_cross, w_v_cross, w_o_cross))
    invf = jnp.zeros((LANES,), F32)
    for (half, _), off in zip(ROPE_VARIANTS, _rope_lane_plan()):
        inv_freq = ROPE_THETA ** (-jnp.arange(half, dtype=F32) / half)
        invf = invf.at[off:off + 2 * half].set(jnp.concatenate([inv_freq, inv_freq]))
    invf = invf.reshape(1, LANES)

    qk_w = MLA_HEADS * MLA_QK_PAD
    dsa_w = DSA_HEADS * DSA_HEAD_DIM
    qi_w = IDX_HEADS * LANES
    tpj = 256
    rows = lambda w: pl.BlockSpec((tpj, w), lambda i: (i, 0))
    cols = lambda w: pl.BlockSpec((w, tpj), lambda i: (0, i))
    q_mla, k_mla, vt_mla, qk_d, vt_d, q_idx, k_idx, w_idx_s = pl.pallas_call(
        _proj_kernel, name="proj", grid=(t // tpj,),
        in_specs=[rows(d), rows(1), _const((1, d)), _const((1, LANES)),
                  _const((O_END, d)), _const((1, MLA_Q_LORA)), _const((MLA_Q_LORA, qk_w)),
                  _const((1, MLA_KV_LORA)), _const((MLA_KV_LORA, mla_w)), _const((mla_w, MLA_KV_LORA))],
        out_specs=[rows(qk_w), rows(qk_w), cols(mla_w),
                   pl.BlockSpec((2, tpj, dsa_w), lambda i: (0, i, 0)), cols(dsa_w),
                   rows(qi_w), rows(LANES), rows(LANES)],
        out_shape=[jax.ShapeDtypeStruct((t, qk_w), BF16), jax.ShapeDtypeStruct((t, qk_w), BF16),
                   jax.ShapeDtypeStruct((mla_w, t), BF16),
                   jax.ShapeDtypeStruct((2, t, dsa_w), BF16), jax.ShapeDtypeStruct((dsa_w, t), BF16),
                   jax.ShapeDtypeStruct((t, qi_w), BF16), jax.ShapeDtypeStruct((t, LANES), BF16),
                   jax.ShapeDtypeStruct((t, LANES), F32)],
        compiler_params=_params(("parallel",)),
    )(x2d, pos, row(g_mix), invf, w_in_t, row(g_cq), wqb, row(g_ckv), wkvb_k, wkvb_vt)

    tq, tk, hp = 256, 512, 8
    mla_grid = (batch, MLA_HEADS // hp, seq // tq)
    n_steps = mla_grid[0] * mla_grid[1] * mla_grid[2]
    step = lambda b, h, i: (b * mla_grid[1] + h) * mla_grid[2] + i
    wu_rows, wd_rows = d // n_steps, D_FF // n_steps
    a_mla, wu, wd = pl.pallas_call(
        functools.partial(_mla_attn_kernel, tq=tq, tk=tk, hp=hp), name="mla_attn", grid=mla_grid,
        in_specs=[pl.BlockSpec((None, tq, hp * MLA_QK_PAD), lambda b, h, i: (b, i, h)),
                  pl.BlockSpec((None, seq, hp * MLA_QK_PAD), lambda b, h, i: (b, 0, h),
                               pipeline_mode=pl.Buffered(1)),
                  pl.BlockSpec((hp * MLA_V, seq), lambda b, h, i: (h, b), pipeline_mode=pl.Buffered(1)),
                  pl.BlockSpec((None, wu_rows, D_FF), lambda b, h, i: (0, step(b, h, i), 0)),
                  pl.BlockSpec((None, wd_rows, d), lambda b, h, i: (0, step(b, h, i), 0))],
        out_specs=[pl.BlockSpec((None, tq, hp * MLA_V), lambda b, h, i: (b, i, h)),
                   pl.BlockSpec((wu_rows, D_FF), lambda b, h, i: (step(b, h, i), 0)),
                   pl.BlockSpec((wd_rows, d), lambda b, h, i: (step(b, h, i), 0))],
        out_shape=[jax.ShapeDtypeStruct((batch, seq, mla_w), BF16),
                   jax.ShapeDtypeStruct((d, D_FF), BF16), jax.ShapeDtypeStruct((D_FF, d), BF16)],
        scratch_shapes=[pltpu.VMEM((hp, 1, tq), F32), pltpu.VMEM((hp, MLA_V + SUM_ROWS, tq), F32)],
        compiler_params=_params(("parallel", "parallel", "arbitrary")),
    )(q_mla.reshape(batch, seq, qk_w), k_mla.reshape(batch, seq, qk_w), vt_mla, w_up, w_down)

    qk4 = qk_d.reshape(2, batch, seq, dsa_w)
    tq, tk, hp = 256, 512, 8
    b_dsa = pl.pallas_call(
        functools.partial(_dsa_kernel, tq=tq, tk=tk, n_sel=n_sel, seq=seq), name="dsa",
        grid=(batch, seq // tq),
        in_specs=[pl.BlockSpec((None, None, tq, dsa_w), lambda b, i: (0, b, i, 0)),
                  pl.BlockSpec((None, None, seq, dsa_w), lambda b, i: (1, b, 0, 0), pipeline_mode=pl.Buffered(1)),
                  pl.BlockSpec((dsa_w, seq), lambda b, i: (0, b), pipeline_mode=pl.Buffered(1)),
                  pl.BlockSpec((None, tq, qi_w), lambda b, i: (b, i, 0)),
                  pl.BlockSpec((None, seq, LANES), lambda b, i: (b, 0, 0)),
                  pl.BlockSpec((None, tq, LANES), lambda b, i: (b, i, 0))],
        out_specs=pl.BlockSpec((None, tq, dsa_w), lambda b, i: (b, i, 0)),
        out_shape=jax.ShapeDtypeStruct((batch, seq, dsa_w), BF16),
        scratch_shapes=[pltpu.VMEM((seq, tq), F32), pltpu.VMEM((seq, tq), BF16), pltpu.VMEM((seq, tq), F32),
                        pltpu.VMEM((32, tq), I32), pltpu.VMEM((32, tq), jnp.int16),
                        pltpu.VMEM((hp, 1, tq), F32), pltpu.VMEM((hp, DSA_HEAD_DIM + SUM_ROWS, tq), F32)],
        compiler_params=_params(("parallel", "arbitrary")),
    )(qk4, qk4, vt_d, q_idx.reshape(batch, seq, qi_w), k_idx.reshape(batch, seq, LANES),
      w_idx_s.reshape(batch, seq, LANES))

    xw = X_HEADS * X_HEAD_DIM
    n_mem = mem.shape[1]
    k_mem, v_mem = pl.pallas_call(
        _mem_kv_kernel, name="mem_kv", grid=(batch,),
        in_specs=[pl.BlockSpec((None, n_mem, d), lambda b: (b, 0, 0)), _const((1, d)), _const((d, xw)),
                  _const((d, xw))],
        out_specs=[pl.BlockSpec((None, n_mem, xw), lambda b: (b, 0, 0))] * 2,
        out_shape=[jax.ShapeDtypeStruct((batch, n_mem, xw), BF16)] * 2,
        compiler_params=_params(("parallel",)),
    )(mem, row(g_mem), wk_c, wv_c)

    tp = 512
    per_b = seq // tp
    mem_spec = pl.BlockSpec((None, n_mem, xw), lambda i: (i // per_b, 0, 0))
    x2, hm = pl.pallas_call(
        _post_kernel, name="post", grid=(t // tp,),
        in_specs=[pl.BlockSpec((tp, mla_w), lambda i: (i, 0)), pl.BlockSpec((tp, dsa_w), lambda i: (i, 0)),
                  pl.BlockSpec((tp, d), lambda i: (i, 0)), _const((mla_w + dsa_w, d)), _const((1, d)),
                  _const((d, xw)), mem_spec, mem_spec, _const((xw, d)), _const((1, d))],
        out_specs=[pl.BlockSpec((tp, d), lambda i: (i, 0))] * 2,
        out_shape=[jax.ShapeDtypeStruct((t, d), F32), jax.ShapeDtypeStruct((t, d), BF16)],
        compiler_params=_params(("parallel",)),
    )(a_mla.reshape(t, mla_w), b_dsa.reshape(t, dsa_w), x2d, w_out[0].astype(BF16), row(g_cross), wq_c,
      k_mem, v_mem, wo_c, row(g_mlp))

    tmm, tf = 512, 1024
    y = pl.pallas_call(
        _mlp_kernel, name="mlp", grid=(t // tmm, D_FF // tf),
        in_specs=[pl.BlockSpec((tmm, d), lambda i, f: (i, 0)), pl.BlockSpec((tmm, d), lambda i, f: (i, 0)),
                  pl.BlockSpec((d, tf), lambda i, f: (0, f)), pl.BlockSpec((tf, d), lambda i, f: (f, 0)),
                  _const((1, d))],
        out_specs=pl.BlockSpec((tmm, d), lambda i, f: (i, 0)),
        out_shape=jax.ShapeDtypeStruct((t, d), F32),
        compiler_params=_params(("parallel", "arbitrary")),
    )(hm, x2, wu, wd, row(g_final))
    return y.reshape(batch, seq, d)
```

```python
import functools

import jax
import jax.numpy as jnp
from jax import lax
from jax.experimental import pallas as pl
from jax.experimental.pallas import tpu as pltpu

F32 = jnp.float32
BF16 = jnp.bfloat16
I32 = jnp.int32

D_MODEL = 2048
CHUNK = 64
ROPE_THETA = 500000.0
N_MEM = 256
EPS = 1e-6
MLA_HEADS = 8
MLA_NOPE = 128
MLA_ROPE = 64
MLA_V = 128
MLA_Q_LORA = 512
MLA_KV_LORA = 256
DSA_HEADS = 8
DSA_HEAD_DIM = 128
DSA_ROT = DSA_HEAD_DIM // 4
IDX_HEADS = 16
IDX_DIM = 64
IDX_ROT = IDX_DIM // 4
TOPK_MAX = 256
X_HEADS = 4
X_HEAD_DIM = 128
D_FF = 4 * D_MODEL

LANES = 128
O_CQ = 0
O_CKV = O_CQ + MLA_Q_LORA
O_KR = O_CKV + MLA_KV_LORA
O_QD = O_KR + MLA_ROPE
O_KD = O_QD + DSA_HEADS * DSA_HEAD_DIM
O_VD = O_KD + DSA_HEADS * DSA_HEAD_DIM
O_QI = O_VD + DSA_HEADS * DSA_HEAD_DIM
O_KI = O_QI + IDX_HEADS * IDX_DIM
O_WI = O_KI + IDX_DIM
O_END = O_WI + IDX_HEADS
MLA_QK_PAD = 256
NEG = -1e30
INT_MIN = -2 ** 31
LOG2E = 1.4426950408889634
VMEM_LIMIT = 56 * 1024 * 1024


def _params(sem, vmem=VMEM_LIMIT):
    return pltpu.CompilerParams(dimension_semantics=sem, vmem_limit_bytes=vmem)


def _rms(xf, g):
    return xf * lax.rsqrt(jnp.mean(xf * xf, axis=-1, keepdims=True) + EPS) * g


def _apply_rope(x, coeffs, half):
    c, s_lo, s_hi = coeffs
    return x * c + pltpu.roll(x, LANES - half, 1) * s_lo + pltpu.roll(x, half, 1) * s_hi


def _dot(a, b):
    return jnp.dot(a, b, preferred_element_type=F32)


def _dot_nt(a, b):
    return lax.dot_general(a, b, (((1,), (1,)), ((), ())), preferred_element_type=F32)


def _for_tiles(n, fn):
    def pair(jj, c):
        fn(2 * jj)
        fn(2 * jj + 1)
        return c

    lax.fori_loop(0, n // 2, pair, 0)

    @pl.when(n % 2 == 1)
    def _():
        fn(n - 1)


ROPE_VARIANTS = ((MLA_ROPE // 2, LANES), (DSA_ROT // 2, LANES), (IDX_ROT // 2, IDX_DIM))


def _rope_lane_plan():
    offs, o = [], 0
    for half, _ in ROPE_VARIANTS:
        offs.append(o)
        o += 2 * half
    assert o <= LANES
    return offs


def _rope_coeffs(pos, invf):
    ang = pos.astype(F32) * invf
    cos, sin = jnp.cos(ang), jnp.sin(ang)
    lane = lax.broadcasted_iota(I32, (1, LANES), 1)
    coeffs = []
    for (half, period), off in zip(ROPE_VARIANTS, _rope_lane_plan()):
        c = jnp.ones(cos.shape, F32)
        s_lo = jnp.zeros(cos.shape, F32)
        s_hi = jnp.zeros(cos.shape, F32)
        for base in range(0, LANES, period):
            shift = (base - off) % LANES
            cs = cos if shift == 0 else pltpu.roll(cos, shift, 1)
            sn = sin if shift == 0 else pltpu.roll(sin, shift, 1)
            in_lo = (lane >= base) & (lane < base + half)
            in_hi = (lane >= base + half) & (lane < base + 2 * half)
            c = jnp.where(in_lo | in_hi, cs, c)
            s_lo = jnp.where(in_lo, -sn, s_lo)
            s_hi = jnp.where(in_hi, sn, s_hi)
        coeffs.append((c, s_lo, s_hi))
    return coeffs


def _proj_kernel(x_ref, pos_ref, g_ref, invf_ref, wt_ref, gcq_ref, wqb_ref, gckv_ref, wk_ref, wvt_ref,
                 q_ref, k_ref, vt_ref, qk_d_ref, vt_d_ref, qi_ref, ki_ref, wi_ref):
    h = _rms(x_ref[...], g_ref[...]).astype(BF16)
    rope = _rope_coeffs(pos_ref[...], invf_ref[...])
    lane = lax.broadcasted_iota(I32, (1, LANES), 1)

    cq = _dot_nt(h, wt_ref[O_CQ:O_CKV])
    r = _dot_nt(h, wt_ref[O_CKV:O_KR + LANES])
    q = _dot(_rms(cq, gcq_ref[...]).astype(BF16), wqb_ref[...])
    ckv = _rms(r[:, :MLA_KV_LORA], gckv_ref[...]).astype(BF16)
    kn = _dot(ckv, wk_ref[...])
    vt_ref[...] = _dot_nt(wvt_ref[...], ckv).astype(BF16)
    scale = (MLA_NOPE + MLA_ROPE) ** -0.5 * LOG2E
    kr = jnp.where(lane < MLA_ROPE, r[:, MLA_KV_LORA:], 0.0)
    kr = _apply_rope(kr, rope[0], MLA_ROPE // 2).astype(BF16)
    for hh in range(MLA_HEADS):
        b0 = hh * MLA_QK_PAD
        q_ref[:, b0:b0 + LANES] = (q[:, b0:b0 + LANES] * scale).astype(BF16)
        qr = _apply_rope(q[:, b0 + LANES:b0 + 2 * LANES], rope[0], MLA_ROPE // 2)
        q_ref[:, b0 + LANES:b0 + 2 * LANES] = (qr * scale).astype(BF16)
        k_ref[:, b0:b0 + LANES] = kn[:, hh * LANES:(hh + 1) * LANES].astype(BF16)
        k_ref[:, b0 + LANES:b0 + 2 * LANES] = kr

    dsa_w = DSA_HEADS * DSA_HEAD_DIM
    for g, scale in enumerate((DSA_HEAD_DIM ** -0.5 * LOG2E, 1.0)):
        rd = _dot_nt(h, wt_ref[O_QD + g * dsa_w:O_QD + (g + 1) * dsa_w])
        for hh in range(DSA_HEADS):
            sl = slice(hh * LANES, (hh + 1) * LANES)
            qk_d_ref[g, :, sl] = (_apply_rope(rd[:, sl], rope[1], DSA_ROT // 2) * scale).astype(BF16)
    vt_d_ref[...] = _dot_nt(wt_ref[O_VD:O_QI], h).astype(BF16)

    ri = _dot_nt(h, wt_ref[O_QI:O_KI])
    t0 = O_END - LANES
    tail = _dot_nt(h, wt_ref[t0:O_END])
    k_lo = pltpu.roll(tail, LANES - (O_KI - t0), 1)
    tail_w = pltpu.roll(tail, LANES - (O_WI - t0), 1)
    assert O_WI - O_KI == IDX_DIM
    for p in range(IDX_HEADS // 2):
        t = _apply_rope(ri[:, p * LANES:(p + 1) * LANES], rope[2], IDX_ROT // 2) * (IDX_DIM ** -0.5)
        qi_ref[:, (2 * p) * LANES:(2 * p + 1) * LANES] = jnp.where(lane < IDX_DIM, t, 0.0).astype(BF16)
        qi_ref[:, (2 * p + 1) * LANES:(2 * p + 2) * LANES] = jnp.where(lane >= IDX_DIM, t, 0.0).astype(BF16)
    k_dup = jnp.where(lane < IDX_DIM, k_lo, tail_w)
    ki_ref[...] = _apply_rope(k_dup, rope[2], IDX_ROT // 2).astype(BF16)
    wi_ref[...] = jnp.where(lane < IDX_HEADS, tail_w, 0.0) * (IDX_HEADS ** -0.5)


SUM_ROWS = 16


def _flash_sweep_t(n_tiles, tk, heads, lead, qk, vt, adjust, adjust_last, m_scr, acc_scr):
    m_scr[...] = jnp.full(m_scr.shape, NEG, F32)
    acc_scr[...] = jnp.zeros(acc_scr.shape, F32)
    ones = jnp.ones((SUM_ROWS, tk), BF16)

    def scores(h, off, adj):
        s_t = adj(qk(h, off))
        return s_t, jnp.max(s_t, axis=0, keepdims=True)

    def finish(h, s_and_max, off):
        s_t, m_tile = s_and_max
        m_prev = m_scr[h]
        m_new = jnp.maximum(m_prev, m_tile)
        p = jnp.exp2(s_t - m_new).astype(BF16)
        pv = _dot(jnp.concatenate([vt(h, off), ones], axis=0), p)
        acc_scr[h] = jnp.exp2(m_prev - m_new) * acc_scr[h] + pv
        m_scr[h] = m_new

    def tile(off, adjust_tile):
        adj = adjust_tile(off)
        s_ts = [scores(h, off, adj) for h in range(min(lead, heads))]
        for h in range(heads):
            finish(h, s_ts[h], off)
            if h + lead < heads:
                s_ts.append(scores(h + lead, off, adj))

    _for_tiles(n_tiles - 1, lambda j: tile(pl.multiple_of(j * tk, tk), adjust))
    tile(pl.multiple_of((n_tiles - 1) * tk, tk), adjust_last)


def _attn_out_t(acc, dv):
    return (acc[:dv] / acc[dv:dv + 1]).T.astype(BF16)


def _mla_attn_kernel(q_ref, k_ref, vt_ref, wu_ref, wd_ref, o_ref, wu_bf_ref, wd_bf_ref, m_scr, acc_scr,
                     *, tq, tk, hp):
    wu_bf_ref[...] = wu_ref[...].astype(BF16)
    wd_bf_ref[...] = wd_ref[...].astype(BF16)
    i = pl.program_id(2)
    n_tiles = (i * tq) // tk + 1
    qry_chunk = (i * tq + lax.broadcasted_iota(I32, (1, tq), 1)) // CHUNK

    def qk(h, off):
        qs = slice(h * MLA_QK_PAD, (h + 1) * MLA_QK_PAD)
        return _dot_nt(k_ref[pl.ds(off, tk), qs], q_ref[:, qs])

    def vt(h, off):
        return vt_ref[h * MLA_V:(h + 1) * MLA_V, pl.ds(off, tk)]

    def causal(off):
        key_chunk = (off + lax.broadcasted_iota(I32, (tk, 1), 0)) // CHUNK
        return lambda s_t: jnp.where(key_chunk <= qry_chunk, s_t, NEG)

    _flash_sweep_t(n_tiles, tk, hp, hp, qk, vt, lambda off: (lambda s_t: s_t), causal, m_scr, acc_scr)
    for h in range(hp):
        o_ref[:, h * MLA_V:(h + 1) * MLA_V] = _attn_out_t(acc_scr[h], MLA_V)


def _dsa_kernel(q_ref, k_ref, vt_ref, qi_ref, ki_ref, wi_ref, w_out_ref, w_qc_ref, w_oc_ref,
                o_ref, w_out_bf_ref, w_qc_bf_ref, w_oc_bf_ref,
                sc_scr, sb_scr, bias_scr, cnt_scr, cnt16_scr, m_scr, acc_scr, *, tq, tk, n_sel, seq):
    w_out_bf_ref[...] = w_out_ref[...].astype(BF16)
    w_qc_bf_ref[...] = w_qc_ref[...].astype(BF16)
    w_oc_bf_ref[...] = w_oc_ref[...].astype(BF16)
    i = pl.program_id(1)
    n_valid = ((i + 1) * tq + tk - 1) // tk
    qry_chunk = (i * tq + lax.broadcasted_iota(I32, (1, tq), 1)) // CHUNK

    def key_ids(off):
        return off + lax.broadcasted_iota(I32, (tk, 1), 0)

    wi_t = wi_ref[...].T

    def score_tile(j):
        off = pl.multiple_of(j * tk, tk)
        kj = ki_ref[pl.ds(off, tk), :]
        sc = jnp.zeros((tk, tq), F32)
        for hh in range(IDX_HEADS):
            lg = _dot_nt(kj, qi_ref[:, hh * LANES:(hh + 1) * LANES])
            sc = sc + jnp.maximum(lg, 0.0) * wi_t[hh:hh + 1, :]
        sc = jnp.where(key_ids(off) // CHUNK <= qry_chunk, sc, -jnp.inf)
        sc_scr[pl.ds(off, tk), :] = sc
        sb_scr[pl.ds(off, tk), :] = sc.astype(BF16)

    _for_tiles(n_valid, score_tile)

    sub = cnt_scr.shape[0]

    def count_bf16(cand):
        cnt16_scr[...] = jnp.zeros(cnt16_scr.shape, jnp.int16)

        def body(j):
            off = pl.multiple_of(j * tk, tk)
            hit = jnp.where(sb_scr[pl.ds(off, tk), :] >= cand, jnp.int16(1), jnp.int16(0))
            part = hit[:sub]
            for cc in range(1, tk // sub):
                part = part + hit[cc * sub:(cc + 1) * sub]
            cnt16_scr[...] += part

        _for_tiles(n_valid, body)
        return jnp.sum(cnt16_scr[...].astype(I32), axis=0, keepdims=True)

    def count(pred):
        cnt_scr[...] = jnp.zeros(cnt_scr.shape, I32)

        def body(j):
            off = pl.multiple_of(j * tk, tk)
            hit = jnp.where(pred(sc_scr[pl.ds(off, tk), :], off), 1, 0).astype(I32)
            part = hit[:sub]
            for cc in range(1, tk // sub):
                part = part + hit[cc * sub:(cc + 1) * sub]
            cnt_scr[...] += part

        _for_tiles(n_valid, body)
        return jnp.sum(cnt_scr[...], axis=0, keepdims=True)

    def key_to_float(key):
        return pltpu.bitcast(key ^ ((key >> 31) & I32(0x7FFFFFFF)), F32)

    half = 2 ** 15
    neg_inf_key = 0x807FFFFF - 2 ** 32

    def coarse_key(key16):
        return (key16 << 16) + ((key16 >> 31) & I32(0xFFFF))

    def coarse_bit(it, tb):
        cand_b = tb | (I32(1) << (15 - it))
        cand = key_to_float(coarse_key(cand_b - half)).astype(BF16)
        cnt = count_bf16(cand)
        return jnp.where(cnt >= n_sel, cand_b, tb)

    v_key16 = jnp.maximum(lax.fori_loop(0, 16, coarse_bit, jnp.zeros((1, tq), I32)) - half, neg_inf_key >> 16)
    lo_key = coarse_key(v_key16) - half

    def fine_bit(it, carry):
        off, cnt_ge = carry
        cand_off = off | (I32(1) << (16 - it))
        cand = key_to_float(lo_key + cand_off)
        cnt = count(lambda sc, o: sc >= cand)
        accept = cnt >= n_sel
        return jnp.where(accept, cand_off, off), jnp.where(accept, cnt, cnt_ge)

    off, cnt_ge = lax.fori_loop(0, 17, fine_bit, (jnp.zeros((1, tq), I32), jnp.full((1, tq), n_sel + 1, I32)))
    thr = key_to_float(jnp.maximum(lo_key + off, neg_inf_key))

    tied = (cnt_ge > n_sel) & (thr > -jnp.inf)
    any_tied = jnp.max(tied.astype(I32)) > 0
    idx_bits = (2 * seq - 1).bit_length()

    def tie_cut():
        need = n_sel - count(lambda sc, off: sc > thr)

        def cut_bit(it, jc):
            cand = jc | (I32(1) << (idx_bits - 1 - it))
            cnt = count(lambda sc, off: (sc == thr) & (key_ids(off) < cand))
            return jnp.where(cnt <= need, cand, jc)
        return lax.fori_loop(0, idx_bits, cut_bit, jnp.zeros((1, tq), I32))

    jcut = lax.cond(any_tied, tie_cut, lambda: jnp.full((1, tq), 2 ** idx_bits - 1, I32))

    def write_bias(off, sel, causal):
        if causal:
            sel = sel & (key_ids(off) // CHUNK <= qry_chunk)
        bias_scr[pl.ds(off, tk), :] = jnp.where(sel, 0.0, NEG).astype(F32)

    def sel_plain(off):
        return sc_scr[pl.ds(off, tk), :] >= thr

    def sel_general(off):
        sc = sc_scr[pl.ds(off, tk), :]
        return (sc > thr) | ((sc == thr) & (key_ids(off) < jcut))

    off_last = pl.multiple_of((n_valid - 1) * tk, tk)
    for pred, sel_fn in ((any_tied, sel_general), (jnp.logical_not(any_tied), sel_plain)):
        @pl.when(pred)
        def _(sel_fn=sel_fn):
            def bias_tile(j, c):
                off = pl.multiple_of(j * tk, tk)
                write_bias(off, sel_fn(off), causal=False)
                return c

            lax.fori_loop(0, n_valid - 1, bias_tile, 0)
            write_bias(off_last, sel_fn(off_last), causal=True)

    hp = m_scr.shape[0]

    def biased(off):
        bias = bias_scr[pl.ds(off, tk), :]
        return lambda s_t: s_t + bias

    for g in range(DSA_HEADS // hp):
        def head_cols(h, g=g):
            return slice((g * hp + h) * DSA_HEAD_DIM, (g * hp + h + 1) * DSA_HEAD_DIM)

        def qk(h, off):
            return _dot_nt(k_ref[pl.ds(off, tk), head_cols(h)], q_ref[:, head_cols(h)])

        def vt(h, off):
            return vt_ref[head_cols(h), pl.ds(off, tk)]

        _flash_sweep_t(n_valid, tk, hp, hp // 2, qk, vt, biased, biased, m_scr, acc_scr)
        for h in range(hp):
            o_ref[:, head_cols(h)] = _attn_out_t(acc_scr[h], DSA_HEAD_DIM)


def _mem_kv_kernel(mem_ref, g_ref, wk_ref, wv_ref, k_ref, v_ref):
    mn = _rms(mem_ref[...], g_ref[...]).astype(BF16)
    k_ref[...] = _dot(mn, wk_ref[...]).astype(BF16)
    v_ref[...] = _dot(mn, wv_ref[...]).astype(BF16)


def _post_kernel(a_ref, b_ref, x_ref, wout_ref, gc_ref, wq_ref, km_ref, vm_ref, wo_ref, gm_ref,
                 x2_ref, hm_ref):
    na = a_ref.shape[1]
    x1 = x_ref[...] + _dot(a_ref[...], wout_ref[:na]) + _dot(b_ref[...], wout_ref[na:])
    hc = _rms(x1, gc_ref[...]).astype(BF16)
    qc = (_dot(hc, wq_ref[...]) * (X_HEAD_DIM ** -0.5)).astype(BF16)
    outs = []
    for hh in range(X_HEADS):
        sl = slice(hh * X_HEAD_DIM, (hh + 1) * X_HEAD_DIM)
        s = _dot_nt(qc[:, sl], km_ref[:, sl])
        p = jnp.exp(s - jnp.max(s, axis=-1, keepdims=True))
        o = _dot(p.astype(BF16), vm_ref[:, sl]) / jnp.sum(p, axis=-1, keepdims=True)
        outs.append(o.astype(BF16))
    x2 = x1 + _dot(jnp.concatenate(outs, axis=-1), wo_ref[...])
    x2_ref[...] = x2
    hm_ref[...] = _rms(x2, gm_ref[...]).astype(BF16)


def _mlp_kernel(hm_ref, x2_ref, wu_ref, wd_ref, gf_ref, y_ref):
    f = pl.program_id(1)

    @pl.when(f == 0)
    def _():
        y_ref[...] = x2_ref[...]

    u = jnp.maximum(_dot(hm_ref[...], wu_ref[...]), 0.0)
    y_ref[...] += _dot((u * u).astype(BF16), wd_ref[...])

    @pl.when(f == pl.num_programs(1) - 1)
    def _():
        y_ref[...] = _rms(y_ref[...], gf_ref[...])


def _const(shape):
    return pl.BlockSpec(shape, lambda *_: (0,) * len(shape), pipeline_mode=pl.Buffered(1))


def kernel(x, mem, positions, g_mix, w_in, g_cq, g_ckv, w_qb, w_kvb, w_out, g_cross, g_mem,
           w_q_cross, w_k_cross, w_v_cross, w_o_cross, g_mlp, w_up, w_down, g_final):
    assert w_in.shape[0] == 1, "one layer"
    batch, seq, d = x.shape
    t = batch * seq
    n_sel = min(TOPK_MAX, seq // 4)
    x2d = x.reshape(t, d)
    pos = positions.reshape(t, 1)
    row = lambda g: g.reshape(1, -1).astype(F32)

    assert w_in.shape[2] == O_END
    w_in_t = jnp.swapaxes(w_in[0], 0, 1).astype(BF16)
    wqb = w_qb[0].reshape(MLA_Q_LORA, MLA_HEADS, MLA_NOPE + MLA_ROPE)
    wqb = jnp.pad(wqb, ((0, 0), (0, 0), (0, MLA_QK_PAD - MLA_NOPE - MLA_ROPE)))
    wqb = wqb.reshape(MLA_Q_LORA, MLA_HEADS * MLA_QK_PAD).astype(BF16)
    wkvb = w_kvb[0].reshape(MLA_KV_LORA, MLA_HEADS, MLA_NOPE + MLA_V)
    wkvb_k = wkvb[:, :, :MLA_NOPE].reshape(MLA_KV_LORA, -1).astype(BF16)
    wkvb_vt = wkvb[:, :, MLA_NOPE:].reshape(MLA_KV_LORA, -1).T.astype(BF16)
    mla_w = MLA_HEADS * MLA_V
    wk_c, wv_c = w_k_cross[0].astype(BF16), w_v_cross[0].astype(BF16)
    invf = jnp.zeros((LANES,), F32)
    for (half, _), off in zip(ROPE_VARIANTS, _rope_lane_plan()):
        inv_freq = ROPE_THETA ** (-jnp.arange(half, dtype=F32) / half)
        invf = invf.at[off:off + 2 * half].set(jnp.concatenate([inv_freq, inv_freq]))
    invf = invf.reshape(1, LANES)

    qk_w = MLA_HEADS * MLA_QK_PAD
    dsa_w = DSA_HEADS * DSA_HEAD_DIM
    qi_w = IDX_HEADS * LANES
    tpj = 256
    rows = lambda w: pl.BlockSpec((tpj, w), lambda i: (i, 0))
    cols = lambda w: pl.BlockSpec((w, tpj), lambda i: (0, i))
    q_mla, k_mla, vt_mla, qk_d, vt_d, q_idx, k_idx, w_idx_s = pl.pallas_call(
        _proj_kernel, name="proj", grid=(t // tpj,),
        in_specs=[rows(d), rows(1), _const((1, d)), _const((1, LANES)),
                  _const((O_END, d)), _const((1, MLA_Q_LORA)), _const((MLA_Q_LORA, qk_w)),
                  _const((1, MLA_KV_LORA)), _const((MLA_KV_LORA, mla_w)), _const((mla_w, MLA_KV_LORA))],
        out_specs=[rows(qk_w), rows(qk_w), cols(mla_w),
                   pl.BlockSpec((2, tpj, dsa_w), lambda i: (0, i, 0)), cols(dsa_w),
                   rows(qi_w), rows(LANES), rows(LANES)],
        out_shape=[jax.ShapeDtypeStruct((t, qk_w), BF16), jax.ShapeDtypeStruct((t, qk_w), BF16),
                   jax.ShapeDtypeStruct((mla_w, t), BF16),
                   jax.ShapeDtypeStruct((2, t, dsa_w), BF16), jax.ShapeDtypeStruct((dsa_w, t), BF16),
                   jax.ShapeDtypeStruct((t, qi_w), BF16), jax.ShapeDtypeStruct((t, LANES), BF16),
                   jax.ShapeDtypeStruct((t, LANES), F32)],
        compiler_params=_params(("parallel",)),
    )(x2d, pos, row(g_mix), invf, w_in_t, row(g_cq), wqb, row(g_ckv), wkvb_k, wkvb_vt)

    tq, tk, hp = 256, 512, 8
    mla_grid = (batch, MLA_HEADS // hp, seq // tq)
    n_steps = mla_grid[0] * mla_grid[1] * mla_grid[2]
    step = lambda b, h, i: (b * mla_grid[1] + h) * mla_grid[2] + i
    wu_rows, wd_rows = d // n_steps, D_FF // n_steps
    a_mla, wu, wd = pl.pallas_call(
        functools.partial(_mla_attn_kernel, tq=tq, tk=tk, hp=hp), name="mla_attn", grid=mla_grid,
        in_specs=[pl.BlockSpec((None, tq, hp * MLA_QK_PAD), lambda b, h, i: (b, i, h)),
                  pl.BlockSpec((None, seq, hp * MLA_QK_PAD), lambda b, h, i: (b, 0, h),
                               pipeline_mode=pl.Buffered(1)),
                  pl.BlockSpec((hp * MLA_V, seq), lambda b, h, i: (h, b), pipeline_mode=pl.Buffered(1)),
                  pl.BlockSpec((None, wu_rows, D_FF), lambda b, h, i: (0, step(b, h, i), 0)),
                  pl.BlockSpec((None, wd_rows, d), lambda b, h, i: (0, step(b, h, i), 0))],
        out_specs=[pl.BlockSpec((None, tq, hp * MLA_V), lambda b, h, i: (b, i, h)),
                   pl.BlockSpec((wu_rows, D_FF), lambda b, h, i: (step(b, h, i), 0)),
                   pl.BlockSpec((wd_rows, d), lambda b, h, i: (step(b, h, i), 0))],
        out_shape=[jax.ShapeDtypeStruct((batch, seq, mla_w), BF16),
                   jax.ShapeDtypeStruct((d, D_FF), BF16), jax.ShapeDtypeStruct((D_FF, d), BF16)],
        scratch_shapes=[pltpu.VMEM((hp, 1, tq), F32), pltpu.VMEM((hp, MLA_V + SUM_ROWS, tq), F32)],
        compiler_params=_params(("parallel", "parallel", "arbitrary")),
    )(q_mla.reshape(batch, seq, qk_w), k_mla.reshape(batch, seq, qk_w), vt_mla, w_up, w_down)

    qk4 = qk_d.reshape(2, batch, seq, dsa_w)
    tq, tk, hp = 256, 512, 8
    xw = X_HEADS * X_HEAD_DIM
    n_q = seq // tq
    cast_in = lambda a: pl.BlockSpec((None, a.shape[1] // (batch * n_q), a.shape[2]),
                                     lambda b, i: (0, b * n_q + i, 0))
    cast_out = lambda a: pl.BlockSpec((a.shape[1] // (batch * n_q), a.shape[2]), lambda b, i: (b * n_q + i, 0))
    cast_ws = (w_out, w_q_cross, w_o_cross)
    b_dsa, w_out_bf, wq_c, wo_c = pl.pallas_call(
        functools.partial(_dsa_kernel, tq=tq, tk=tk, n_sel=n_sel, seq=seq), name="dsa",
        grid=(batch, n_q),
        in_specs=[pl.BlockSpec((None, None, tq, dsa_w), lambda b, i: (0, b, i, 0)),
                  pl.BlockSpec((None, None, seq, dsa_w), lambda b, i: (1, b, 0, 0), pipeline_mode=pl.Buffered(1)),
                  pl.BlockSpec((dsa_w, seq), lambda b, i: (0, b), pipeline_mode=pl.Buffered(1)),
                  pl.BlockSpec((None, tq, qi_w), lambda b, i: (b, i, 0)),
                  pl.BlockSpec((None, seq, LANES), lambda b, i: (b, 0, 0)),
                  pl.BlockSpec((None, tq, LANES), lambda b, i: (b, i, 0))] + [cast_in(a) for a in cast_ws],
        out_specs=[pl.BlockSpec((None, tq, dsa_w), lambda b, i: (b, i, 0))] + [cast_out(a) for a in cast_ws],
        out_shape=[jax.ShapeDtypeStruct((batch, seq, dsa_w), BF16)]
                  + [jax.ShapeDtypeStruct(a.shape[1:], BF16) for a in cast_ws],
        scratch_shapes=[pltpu.VMEM((seq, tq), F32), pltpu.VMEM((seq, tq), BF16), pltpu.VMEM((seq, tq), F32),
                        pltpu.VMEM((32, tq), I32), pltpu.VMEM((32, tq), jnp.int16),
                        pltpu.VMEM((hp, 1, tq), F32), pltpu.VMEM((hp, DSA_HEAD_DIM + SUM_ROWS, tq), F32)],
        compiler_params=_params(("parallel", "arbitrary")),
    )(qk4, qk4, vt_d, q_idx.reshape(batch, seq, qi_w), k_idx.reshape(batch, seq, LANES),
      w_idx_s.reshape(batch, seq, LANES), *cast_ws)

    n_mem = mem.shape[1]
    k_mem, v_mem = pl.pallas_call(
        _mem_kv_kernel, name="mem_kv", grid=(batch,),
        in_specs=[pl.BlockSpec((None, n_mem, d), lambda b: (b, 0, 0)), _const((1, d)), _const((d, xw)),
                  _const((d, xw))],
        out_specs=[pl.BlockSpec((None, n_mem, xw), lambda b: (b, 0, 0))] * 2,
        out_shape=[jax.ShapeDtypeStruct((batch, n_mem, xw), BF16)] * 2,
        compiler_params=_params(("parallel",)),
    )(mem, row(g_mem), wk_c, wv_c)

    tp = 512
    per_b = seq // tp
    mem_spec = pl.BlockSpec((None, n_mem, xw), lambda i: (i // per_b, 0, 0))
    x2, hm = pl.pallas_call(
        _post_kernel, name="post", grid=(t // tp,),
        in_specs=[pl.BlockSpec((tp, mla_w), lambda i: (i, 0)), pl.BlockSpec((tp, dsa_w), lambda i: (i, 0)),
                  pl.BlockSpec((tp, d), lambda i: (i, 0)), _const((mla_w + dsa_w, d)), _const((1, d)),
                  _const((d, xw)), mem_spec, mem_spec, _const((xw, d)), _const((1, d))],
        out_specs=[pl.BlockSpec((tp, d), lambda i: (i, 0))] * 2,
        out_shape=[jax.ShapeDtypeStruct((t, d), F32), jax.ShapeDtypeStruct((t, d), BF16)],
        compiler_params=_params(("parallel",)),
    )(a_mla.reshape(t, mla_w), b_dsa.reshape(t, dsa_w), x2d, w_out_bf, row(g_cross), wq_c,
      k_mem, v_mem, wo_c, row(g_mlp))

    tmm, tf = 512, 1024
    y = pl.pallas_call(
        _mlp_kernel, name="mlp", grid=(t // tmm, D_FF // tf),
        in_specs=[pl.BlockSpec((tmm, d), lambda i, f: (i, 0)), pl.BlockSpec((tmm, d), lambda i, f: (i, 0)),
                  pl.BlockSpec((d, tf), lambda i, f: (0, f)), pl.BlockSpec((tf, d), lambda i, f: (f, 0)),
                  _const((1, d))],
        out_specs=pl.BlockSpec((tmm, d), lambda i, f: (i, 0)),
        out_shape=jax.ShapeDtypeStruct((t, d), F32),
        compiler_params=_params(("parallel", "arbitrary")),
    )(hm, x2, wu, wd, row(g_final))
    return y.reshape(batch, seq, d)
```

```python
import functools

import jax
import jax.numpy as jnp
from jax import lax
from jax.experimental import pallas as pl
from jax.experimental.pallas import tpu as pltpu

F32 = jnp.float32
BF16 = jnp.bfloat16
I32 = jnp.int32

D_MODEL = 2048
CHUNK = 64
ROPE_THETA = 500000.0
N_MEM = 256
EPS = 1e-6
MLA_HEADS = 8
MLA_NOPE = 128
MLA_ROPE = 64
MLA_V = 128
MLA_Q_LORA = 512
MLA_KV_LORA = 256
DSA_HEADS = 8
DSA_HEAD_DIM = 128
DSA_ROT = DSA_HEAD_DIM // 4
IDX_HEADS = 16
IDX_DIM = 64
IDX_ROT = IDX_DIM // 4
TOPK_MAX = 256
X_HEADS = 4
X_HEAD_DIM = 128
D_FF = 4 * D_MODEL

LANES = 128
O_CQ = 0
O_CKV = O_CQ + MLA_Q_LORA
O_KR = O_CKV + MLA_KV_LORA
O_QD = O_KR + MLA_ROPE
O_KD = O_QD + DSA_HEADS * DSA_HEAD_DIM
O_VD = O_KD + DSA_HEADS * DSA_HEAD_DIM
O_QI = O_VD + DSA_HEADS * DSA_HEAD_DIM
O_KI = O_QI + IDX_HEADS * IDX_DIM
O_WI = O_KI + IDX_DIM
O_END = O_WI + IDX_HEADS
MLA_QK_PAD = 256
NEG = -1e30
LOG2E = 1.4426950408889634
VMEM_LIMIT = 56 * 1024 * 1024


def _params(sem, vmem=VMEM_LIMIT):
    return pltpu.CompilerParams(dimension_semantics=sem, vmem_limit_bytes=vmem)


def _rms(xf, g):
    return xf * lax.rsqrt(jnp.mean(xf * xf, axis=-1, keepdims=True) + EPS) * g


def _apply_rope(x, coeffs, half):
    c, s_lo, s_hi = coeffs
    return x * c + pltpu.roll(x, LANES - half, 1) * s_lo + pltpu.roll(x, half, 1) * s_hi


def _dot(a, b):
    return jnp.dot(a, b, preferred_element_type=F32)


def _dot_nt(a, b):
    return lax.dot_general(a, b, (((1,), (1,)), ((), ())), preferred_element_type=F32)


def _for_tiles(n, fn):
    def pair(jj, c):
        fn(2 * jj)
        fn(2 * jj + 1)
        return c

    lax.fori_loop(0, n // 2, pair, 0)

    @pl.when(n % 2 == 1)
    def _():
        fn(n - 1)


ROPE_VARIANTS = ((MLA_ROPE // 2, LANES), (DSA_ROT // 2, LANES), (IDX_ROT // 2, IDX_DIM))


def _rope_lane_plan():
    offs, o = [], 0
    for half, _ in ROPE_VARIANTS:
        offs.append(o)
        o += 2 * half
    assert o <= LANES
    return offs


def _rope_coeffs(pos, invf):
    ang = pos.astype(F32) * invf
    cos, sin = jnp.cos(ang), jnp.sin(ang)
    lane = lax.broadcasted_iota(I32, (1, LANES), 1)
    coeffs = []
    for (half, period), off in zip(ROPE_VARIANTS, _rope_lane_plan()):
        c = jnp.ones(cos.shape, F32)
        s_lo = jnp.zeros(cos.shape, F32)
        s_hi = jnp.zeros(cos.shape, F32)
        for base in range(0, LANES, period):
            shift = (base - off) % LANES
            cs = cos if shift == 0 else pltpu.roll(cos, shift, 1)
            sn = sin if shift == 0 else pltpu.roll(sin, shift, 1)
            in_lo = (lane >= base) & (lane < base + half)
            in_hi = (lane >= base + half) & (lane < base + 2 * half)
            c = jnp.where(in_lo | in_hi, cs, c)
            s_lo = jnp.where(in_lo, -sn, s_lo)
            s_hi = jnp.where(in_hi, sn, s_hi)
        coeffs.append((c, s_lo, s_hi))
    return coeffs


def _proj_kernel(x_ref, pos_ref, g_ref, invf_ref, wt_ref, gcq_ref, wqb_ref, gckv_ref, wk_ref, wvt_ref,
                 q_ref, k_ref, vt_ref, qk_d_ref, vt_d_ref, qi_ref, ki_ref, wi_ref):
    h = _rms(x_ref[...], g_ref[...]).astype(BF16)
    rope = _rope_coeffs(pos_ref[...], invf_ref[...])
    lane = lax.broadcasted_iota(I32, (1, LANES), 1)

    cq = _dot_nt(h, wt_ref[O_CQ:O_CKV])
    r = _dot_nt(h, wt_ref[O_CKV:O_KR + LANES])
    q = _dot(_rms(cq, gcq_ref[...]).astype(BF16), wqb_ref[...])
    ckv = _rms(r[:, :MLA_KV_LORA], gckv_ref[...]).astype(BF16)
    kn = _dot(ckv, wk_ref[...])
    vt_ref[...] = _dot_nt(wvt_ref[...], ckv).astype(BF16)
    scale = (MLA_NOPE + MLA_ROPE) ** -0.5 * LOG2E
    kr = jnp.where(lane < MLA_ROPE, r[:, MLA_KV_LORA:], 0.0)
    kr = _apply_rope(kr, rope[0], MLA_ROPE // 2).astype(BF16)
    for hh in range(MLA_HEADS):
        b0 = hh * MLA_QK_PAD
        q_ref[:, b0:b0 + LANES] = (q[:, b0:b0 + LANES] * scale).astype(BF16)
        qr = _apply_rope(q[:, b0 + LANES:b0 + 2 * LANES], rope[0], MLA_ROPE // 2)
        q_ref[:, b0 + LANES:b0 + 2 * LANES] = (qr * scale).astype(BF16)
        k_ref[:, b0:b0 + LANES] = kn[:, hh * LANES:(hh + 1) * LANES].astype(BF16)
        k_ref[:, b0 + LANES:b0 + 2 * LANES] = kr

    dsa_w = DSA_HEADS * DSA_HEAD_DIM
    for g, scale in enumerate((DSA_HEAD_DIM ** -0.5 * LOG2E, 1.0)):
        rd = _dot_nt(h, wt_ref[O_QD + g * dsa_w:O_QD + (g + 1) * dsa_w])
        for hh in range(DSA_HEADS):
            sl = slice(hh * LANES, (hh + 1) * LANES)
            qk_d_ref[g, :, sl] = (_apply_rope(rd[:, sl], rope[1], DSA_ROT // 2) * scale).astype(BF16)
    vt_d_ref[...] = _dot_nt(wt_ref[O_VD:O_QI], h).astype(BF16)

    ri = _dot_nt(h, wt_ref[O_QI:O_KI])
    t0 = O_END - LANES
    tail = _dot_nt(h, wt_ref[t0:O_END])
    k_lo = pltpu.roll(tail, LANES - (O_KI - t0), 1)
    tail_w = pltpu.roll(tail, LANES - (O_WI - t0), 1)
    assert O_WI - O_KI == IDX_DIM
    for p in range(IDX_HEADS // 2):
        t = _apply_rope(ri[:, p * LANES:(p + 1) * LANES], rope[2], IDX_ROT // 2) * (IDX_DIM ** -0.5)
        qi_ref[:, (2 * p) * LANES:(2 * p + 1) * LANES] = jnp.where(lane < IDX_DIM, t, 0.0).astype(BF16)
        qi_ref[:, (2 * p + 1) * LANES:(2 * p + 2) * LANES] = jnp.where(lane >= IDX_DIM, t, 0.0).astype(BF16)
    k_dup = jnp.where(lane < IDX_DIM, k_lo, tail_w)
    ki_ref[...] = _apply_rope(k_dup, rope[2], IDX_ROT // 2).astype(BF16)
    wi_ref[...] = jnp.where(lane < IDX_HEADS, tail_w, 0.0) * (IDX_HEADS ** -0.5)


SUM_ROWS = 16


def _flash_sweep_t(n_tiles, tk, heads, lead, qk, vt, adjust, adjust_last, m_scr, acc_scr):
    m_scr[...] = jnp.full(m_scr.shape, NEG, F32)
    acc_scr[...] = jnp.zeros(acc_scr.shape, F32)
    ones = jnp.ones((SUM_ROWS, tk), BF16)

    def scores(h, off, adj):
        s_t = adj(qk(h, off))
        return s_t, jnp.max(s_t, axis=0, keepdims=True)

    def finish(h, s_and_max, off):
        s_t, m_tile = s_and_max
        m_prev = m_scr[h]
        m_new = jnp.maximum(m_prev, m_tile)
        p = jnp.exp2(s_t - m_new).astype(BF16)
        pv = _dot(jnp.concatenate([vt(h, off), ones], axis=0), p)
        acc_scr[h] = jnp.exp2(m_prev - m_new) * acc_scr[h] + pv
        m_scr[h] = m_new

    def tile(off, adjust_tile):
        adj = adjust_tile(off)
        s_ts = [scores(h, off, adj) for h in range(min(lead, heads))]
        for h in range(heads):
            finish(h, s_ts[h], off)
            if h + lead < heads:
                s_ts.append(scores(h + lead, off, adj))

    _for_tiles(n_tiles - 1, lambda j: tile(pl.multiple_of(j * tk, tk), adjust))
    tile(pl.multiple_of((n_tiles - 1) * tk, tk), adjust_last)


def _attn_out_t(acc, dv):
    return (acc[:dv] / acc[dv:dv + 1]).T.astype(BF16)


def _mla_attn_kernel(q_ref, k_ref, vt_ref, wu_ref, wd_ref, o_ref, wu_bf_ref, wd_bf_ref, m_scr, acc_scr,
                     *, tq, tk, hp):
    wu_bf_ref[...] = wu_ref[...].astype(BF16)
    wd_bf_ref[...] = wd_ref[...].astype(BF16)
    i = pl.program_id(2)
    n_tiles = (i * tq) // tk + 1
    qry_chunk = (i * tq + lax.broadcasted_iota(I32, (1, tq), 1)) // CHUNK

    def qk(h, off):
        qs = slice(h * MLA_QK_PAD, (h + 1) * MLA_QK_PAD)
        return _dot_nt(k_ref[pl.ds(off, tk), qs], q_ref[:, qs])

    def vt(h, off):
        return vt_ref[h * MLA_V:(h + 1) * MLA_V, pl.ds(off, tk)]

    def causal(off):
        key_chunk = (off + lax.broadcasted_iota(I32, (tk, 1), 0)) // CHUNK
        return lambda s_t: jnp.where(key_chunk <= qry_chunk, s_t, NEG)

    _flash_sweep_t(n_tiles, tk, hp, hp, qk, vt, lambda off: (lambda s_t: s_t), causal, m_scr, acc_scr)
    for h in range(hp):
        o_ref[:, h * MLA_V:(h + 1) * MLA_V] = _attn_out_t(acc_scr[h], MLA_V)


def _dsa_kernel(q_ref, k_ref, vt_ref, qi_ref, ki_ref, wi_ref, w_out_ref, w_qc_ref, w_oc_ref,
                o_ref, w_out_bf_ref, w_qc_bf_ref, w_oc_bf_ref,
                sc_scr, sb_scr, bias_scr, cnt_scr, cnt16_scr, m_scr, acc_scr, *, tq, tk, n_sel, seq):
    w_out_bf_ref[...] = w_out_ref[...].astype(BF16)
    w_qc_bf_ref[...] = w_qc_ref[...].astype(BF16)
    w_oc_bf_ref[...] = w_oc_ref[...].astype(BF16)
    i = pl.program_id(1)
    n_valid = ((i + 1) * tq + tk - 1) // tk
    qry_chunk = (i * tq + lax.broadcasted_iota(I32, (1, tq), 1)) // CHUNK

    def key_ids(off):
        return off + lax.broadcasted_iota(I32, (tk, 1), 0)

    wi_t = wi_ref[...].T

    def score_tile(j):
        off = pl.multiple_of(j * tk, tk)
        kj = ki_ref[pl.ds(off, tk), :]
        sc = jnp.zeros((tk, tq), F32)
        for hh in range(IDX_HEADS):
            lg = _dot_nt(kj, qi_ref[:, hh * LANES:(hh + 1) * LANES])
            sc = sc + jnp.maximum(lg, 0.0) * wi_t[hh:hh + 1, :]
        sc = jnp.where(key_ids(off) // CHUNK <= qry_chunk, sc, -jnp.inf)
        sc_scr[pl.ds(off, tk), :] = sc
        sb_scr[pl.ds(off, tk), :] = sc.astype(BF16)

    _for_tiles(n_valid, score_tile)

    sub = cnt_scr.shape[0]

    def count_bf16(cand):
        cnt16_scr[...] = jnp.zeros(cnt16_scr.shape, jnp.int16)

        def body(j):
            off = pl.multiple_of(j * tk, tk)
            hit = jnp.where(sb_scr[pl.ds(off, tk), :] >= cand, jnp.int16(1), jnp.int16(0))
            part = hit[:sub]
            for cc in range(1, tk // sub):
                part = part + hit[cc * sub:(cc + 1) * sub]
            cnt16_scr[...] += part

        _for_tiles(n_valid, body)
        return jnp.sum(cnt16_scr[...].astype(I32), axis=0, keepdims=True)

    def count(pred):
        cnt_scr[...] = jnp.zeros(cnt_scr.shape, I32)

        def body(j):
            off = pl.multiple_of(j * tk, tk)
            hit = jnp.where(pred(sc_scr[pl.ds(off, tk), :], off), 1, 0).astype(I32)
            part = hit[:sub]
            for cc in range(1, tk // sub):
                part = part + hit[cc * sub:(cc + 1) * sub]
            cnt_scr[...] += part

        _for_tiles(n_valid, body)
        return jnp.sum(cnt_scr[...], axis=0, keepdims=True)

    def key_to_float(key):
        return pltpu.bitcast(key ^ ((key >> 31) & I32(0x7FFFFFFF)), F32)

    half = 2 ** 15
    neg_inf_key = 0x807FFFFF - 2 ** 32

    def coarse_key(key16):
        return (key16 << 16) + ((key16 >> 31) & I32(0xFFFF))

    def coarse_bit(it, tb):
        cand_b = tb | (I32(1) << (15 - it))
        cand = key_to_float(coarse_key(cand_b - half)).astype(BF16)
        cnt = count_bf16(cand)
        return jnp.where(cnt >= n_sel, cand_b, tb)

    v_key16 = jnp.maximum(lax.fori_loop(0, 16, coarse_bit, jnp.zeros((1, tq), I32)) - half, neg_inf_key >> 16)
    lo_key = coarse_key(v_key16) - half

    def fine_bit(it, carry):
        off, cnt_ge = carry
        cand_off = off | (I32(1) << (16 - it))
        cand = key_to_float(lo_key + cand_off)
        cnt = count(lambda sc, o: sc >= cand)
        accept = cnt >= n_sel
        return jnp.where(accept, cand_off, off), jnp.where(accept, cnt, cnt_ge)

    off, cnt_ge = lax.fori_loop(0, 17, fine_bit, (jnp.zeros((1, tq), I32), jnp.full((1, tq), n_sel + 1, I32)))
    thr = key_to_float(jnp.maximum(lo_key + off, neg_inf_key))

    tied = (cnt_ge > n_sel) & (thr > -jnp.inf)
    any_tied = jnp.max(tied.astype(I32)) > 0
    idx_bits = (2 * seq - 1).bit_length()

    def tie_cut():
        need = n_sel - count(lambda sc, off: sc > thr)

        def cut_bit(it, jc):
            cand = jc | (I32(1) << (idx_bits - 1 - it))
            cnt = count(lambda sc, off: (sc == thr) & (key_ids(off) < cand))
            return jnp.where(cnt <= need, cand, jc)
        return lax.fori_loop(0, idx_bits, cut_bit, jnp.zeros((1, tq), I32))

    jcut = lax.cond(any_tied, tie_cut, lambda: jnp.full((1, tq), 2 ** idx_bits - 1, I32))

    def write_bias(off, sel, causal):
        if causal:
            sel = sel & (key_ids(off) // CHUNK <= qry_chunk)
        bias_scr[pl.ds(off, tk), :] = jnp.where(sel, 0.0, NEG).astype(F32)

    def sel_plain(off):
        return sc_scr[pl.ds(off, tk), :] >= thr

    def sel_general(off):
        sc = sc_scr[pl.ds(off, tk), :]
        return (sc > thr) | ((sc == thr) & (key_ids(off) < jcut))

    off_last = pl.multiple_of((n_valid - 1) * tk, tk)
    for pred, sel_fn in ((any_tied, sel_general), (jnp.logical_not(any_tied), sel_plain)):
        @pl.when(pred)
        def _(sel_fn=sel_fn):
            def bias_tile(j, c):
                off = pl.multiple_of(j * tk, tk)
                write_bias(off, sel_fn(off), causal=False)
                return c

            lax.fori_loop(0, n_valid - 1, bias_tile, 0)
            write_bias(off_last, sel_fn(off_last), causal=True)

    hp = m_scr.shape[0]

    def biased(off):
        bias = bias_scr[pl.ds(off, tk), :]
        return lambda s_t: s_t + bias

    for g in range(DSA_HEADS // hp):
        def head_cols(h, g=g):
            return slice((g * hp + h) * DSA_HEAD_DIM, (g * hp + h + 1) * DSA_HEAD_DIM)

        def qk(h, off):
            return _dot_nt(k_ref[pl.ds(off, tk), head_cols(h)], q_ref[:, head_cols(h)])

        def vt(h, off):
            return vt_ref[head_cols(h), pl.ds(off, tk)]

        _flash_sweep_t(n_valid, tk, hp, hp // 2, qk, vt, biased, biased, m_scr, acc_scr)
        for h in range(hp):
            o_ref[:, head_cols(h)] = _attn_out_t(acc_scr[h], DSA_HEAD_DIM)


def _mem_kv_kernel(mem_ref, g_ref, wk_ref, wv_ref, k_ref, v_ref):
    mn = _rms(mem_ref[...], g_ref[...]).astype(BF16)
    k_ref[...] = _dot(mn, wk_ref[...]).astype(BF16)
    v_ref[...] = _dot(mn, wv_ref[...]).astype(BF16)


def _post_kernel(a_ref, b_ref, x_ref, wout_ref, gc_ref, wq_ref, km_ref, vm_ref, wo_ref, gm_ref,
                 x2_ref, hm_ref):
    na = a_ref.shape[1]
    x1 = x_ref[...] + _dot(a_ref[...], wout_ref[:na]) + _dot(b_ref[...], wout_ref[na:])
    hc = _rms(x1, gc_ref[...]).astype(BF16)
    qc = (_dot(hc, wq_ref[...]) * (X_HEAD_DIM ** -0.5)).astype(BF16)
    outs = []
    for hh in range(X_HEADS):
        sl = slice(hh * X_HEAD_DIM, (hh + 1) * X_HEAD_DIM)
        s = _dot_nt(qc[:, sl], km_ref[:, sl])
        p = jnp.exp(s - jnp.max(s, axis=-1, keepdims=True))
        o = _dot(p.astype(BF16), vm_ref[:, sl]) / jnp.sum(p, axis=-1, keepdims=True)
        outs.append(o.astype(BF16))
    x2 = x1 + _dot(jnp.concatenate(outs, axis=-1), wo_ref[...])
    x2_ref[...] = x2
    hm_ref[...] = _rms(x2, gm_ref[...]).astype(BF16)


def _mlp_kernel(hm_ref, x2_ref, wu_ref, wd_ref, gf_ref, y_ref):
    f = pl.program_id(1)

    @pl.when(f == 0)
    def _():
        y_ref[...] = x2_ref[...]

    u = jnp.maximum(_dot(hm_ref[...], wu_ref[...]), 0.0)
    y_ref[...] += _dot((u * u).astype(BF16), wd_ref[...])

    @pl.when(f == pl.num_programs(1) - 1)
    def _():
        y_ref[...] = _rms(y_ref[...], gf_ref[...])


def _const(shape):
    return pl.BlockSpec(shape, lambda *_: (0,) * len(shape), pipeline_mode=pl.Buffered(1))


def kernel(x, mem, positions, g_mix, w_in, g_cq, g_ckv, w_qb, w_kvb, w_out, g_cross, g_mem,
           w_q_cross, w_k_cross, w_v_cross, w_o_cross, g_mlp, w_up, w_down, g_final):
    assert w_in.shape[0] == 1, "one layer"
    batch, seq, d = x.shape
    t = batch * seq
    n_sel = min(TOPK_MAX, seq // 4)
    x2d = x.reshape(t, d)
    pos = positions.reshape(t, 1)
    row = lambda g: g.reshape(1, -1).astype(F32)

    assert w_in.shape[2] == O_END
    w_in_t = jnp.swapaxes(w_in[0], 0, 1).astype(BF16)
    wqb = w_qb[0].reshape(MLA_Q_LORA, MLA_HEADS, MLA_NOPE + MLA_ROPE)
    wqb = jnp.pad(wqb, ((0, 0), (0, 0), (0, MLA_QK_PAD - MLA_NOPE - MLA_ROPE)))
    wqb = wqb.reshape(MLA_Q_LORA, MLA_HEADS * MLA_QK_PAD).astype(BF16)
    wkvb = w_kvb[0].reshape(MLA_KV_LORA, MLA_HEADS, MLA_NOPE + MLA_V)
    wkvb_k = wkvb[:, :, :MLA_NOPE].reshape(MLA_KV_LORA, -1).astype(BF16)
    wkvb_vt = wkvb[:, :, MLA_NOPE:].reshape(MLA_KV_LORA, -1).T.astype(BF16)
    mla_w = MLA_HEADS * MLA_V
    wk_c, wv_c = w_k_cross[0].astype(BF16), w_v_cross[0].astype(BF16)
    invf = jnp.zeros((LANES,), F32)
    for (half, _), off in zip(ROPE_VARIANTS, _rope_lane_plan()):
        inv_freq = ROPE_THETA ** (-jnp.arange(half, dtype=F32) / half)
        invf = invf.at[off:off + 2 * half].set(jnp.concatenate([inv_freq, inv_freq]))
    invf = invf.reshape(1, LANES)

    qk_w = MLA_HEADS * MLA_QK_PAD
    dsa_w = DSA_HEADS * DSA_HEAD_DIM
    qi_w = IDX_HEADS * LANES
    tpj = 256
    rows = lambda w: pl.BlockSpec((tpj, w), lambda i: (i, 0))
    cols = lambda w: pl.BlockSpec((w, tpj), lambda i: (0, i))
    q_mla, k_mla, vt_mla, qk_d, vt_d, q_idx, k_idx, w_idx_s = pl.pallas_call(
        _proj_kernel, name="proj", grid=(t // tpj,),
        in_specs=[rows(d), rows(1), _const((1, d)), _const((1, LANES)),
                  _const((O_END, d)), _const((1, MLA_Q_LORA)), _const((MLA_Q_LORA, qk_w)),
                  _const((1, MLA_KV_LORA)), _const((MLA_KV_LORA, mla_w)), _const((mla_w, MLA_KV_LORA))],
        out_specs=[rows(qk_w), rows(qk_w), cols(mla_w),
                   pl.BlockSpec((2, tpj, dsa_w), lambda i: (0, i, 0)), cols(dsa_w),
                   rows(qi_w), rows(LANES), rows(LANES)],
        out_shape=[jax.ShapeDtypeStruct((t, qk_w), BF16), jax.ShapeDtypeStruct((t, qk_w), BF16),
                   jax.ShapeDtypeStruct((mla_w, t), BF16),
                   jax.ShapeDtypeStruct((2, t, dsa_w), BF16), jax.ShapeDtypeStruct((dsa_w, t), BF16),
                   jax.ShapeDtypeStruct((t, qi_w), BF16), jax.ShapeDtypeStruct((t, LANES), BF16),
                   jax.ShapeDtypeStruct((t, LANES), F32)],
        compiler_params=_params(("parallel",)),
    )(x2d, pos, row(g_mix), invf, w_in_t, row(g_cq), wqb, row(g_ckv), wkvb_k, wkvb_vt)

    tq, tk, hp = 256, 512, 8
    mla_grid = (batch, MLA_HEADS // hp, seq // tq)
    n_steps = mla_grid[0] * mla_grid[1] * mla_grid[2]
    step = lambda b, h, i: (b * mla_grid[1] + h) * mla_grid[2] + i
    wu_rows, wd_rows = d // n_steps, D_FF // n_steps
    a_mla, wu, wd = pl.pallas_call(
        functools.partial(_mla_attn_kernel, tq=tq, tk=tk, hp=hp), name="mla_attn", grid=mla_grid,
        in_specs=[pl.BlockSpec((None, tq, hp * MLA_QK_PAD), lambda b, h, i: (b, i, h)),
                  pl.BlockSpec((None, seq, hp * MLA_QK_PAD), lambda b, h, i: (b, 0, h),
                               pipeline_mode=pl.Buffered(1)),
                  pl.BlockSpec((hp * MLA_V, seq), lambda b, h, i: (h, b), pipeline_mode=pl.Buffered(1)),
                  pl.BlockSpec((None, wu_rows, D_FF), lambda b, h, i: (0, step(b, h, i), 0)),
                  pl.BlockSpec((None, wd_rows, d), lambda b, h, i: (0, step(b, h, i), 0))],
        out_specs=[pl.BlockSpec((None, tq, hp * MLA_V), lambda b, h, i: (b, i, h)),
                   pl.BlockSpec((wu_rows, D_FF), lambda b, h, i: (step(b, h, i), 0)),
                   pl.BlockSpec((wd_rows, d), lambda b, h, i: (step(b, h, i), 0))],
        out_shape=[jax.ShapeDtypeStruct((batch, seq, mla_w), BF16),
                   jax.ShapeDtypeStruct((d, D_FF), BF16), jax.ShapeDtypeStruct((D_FF, d), BF16)],
        scratch_shapes=[pltpu.VMEM((hp, 1, tq), F32), pltpu.VMEM((hp, MLA_V + SUM_ROWS, tq), F32)],
        compiler_params=_params(("parallel", "parallel", "arbitrary")),
    )(q_mla.reshape(batch, seq, qk_w), k_mla.reshape(batch, seq, qk_w), vt_mla, w_up, w_down)

    qk4 = qk_d.reshape(2, batch, seq, dsa_w)
    tq, tk, hp = 256, 512, 8
    xw = X_HEADS * X_HEAD_DIM
    n_q = seq // tq
    cast_in = lambda a: pl.BlockSpec((None, a.shape[1] // (batch * n_q), a.shape[2]),
                                     lambda b, i: (0, b * n_q + i, 0))
    cast_out = lambda a: pl.BlockSpec((a.shape[1] // (batch * n_q), a.shape[2]), lambda b, i: (b * n_q + i, 0))
    cast_ws = (w_out, w_q_cross, w_o_cross)
    b_dsa, w_out_bf, wq_c, wo_c = pl.pallas_call(
        functools.partial(_dsa_kernel, tq=tq, tk=tk, n_sel=n_sel, seq=seq), name="dsa",
        grid=(batch, n_q),
        in_specs=[pl.BlockSpec((None, None, tq, dsa_w), lambda b, i: (0, b, i, 0)),
                  pl.BlockSpec((None, None, seq, dsa_w), lambda b, i: (1, b, 0, 0), pipeline_mode=pl.Buffered(1)),
                  pl.BlockSpec((dsa_w, seq), lambda b, i: (0, b), pipeline_mode=pl.Buffered(1)),
                  pl.BlockSpec((None, tq, qi_w), lambda b, i: (b, i, 0)),
                  pl.BlockSpec((None, seq, LANES), lambda b, i: (b, 0, 0)),
                  pl.BlockSpec((None, tq, LANES), lambda b, i: (b, i, 0))] + [cast_in(a) for a in cast_ws],
        out_specs=[pl.BlockSpec((None, tq, dsa_w), lambda b, i: (b, i, 0))] + [cast_out(a) for a in cast_ws],
        out_shape=[jax.ShapeDtypeStruct((batch, seq, dsa_w), BF16)]
                  + [jax.ShapeDtypeStruct(a.shape[1:], BF16) for a in cast_ws],
        scratch_shapes=[pltpu.VMEM((seq, tq), F32), pltpu.VMEM((seq, tq), BF16), pltpu.VMEM((seq, tq), F32),
                        pltpu.VMEM((32, tq), I32), pltpu.VMEM((32, tq), jnp.int16),
                        pltpu.VMEM((hp, 1, tq), F32), pltpu.VMEM((hp, DSA_HEAD_DIM + SUM_ROWS, tq), F32)],
        compiler_params=_params(("parallel", "arbitrary")),
    )(qk4, qk4, vt_d, q_idx.reshape(batch, seq, qi_w), k_idx.reshape(batch, seq, LANES),
      w_idx_s.reshape(batch, seq, LANES), *cast_ws)

    n_mem = mem.shape[1]
    k_mem, v_mem = pl.pallas_call(
        _mem_kv_kernel, name="mem_kv", grid=(batch,),
        in_specs=[pl.BlockSpec((None, n_mem, d), lambda b: (b, 0, 0)), _const((1, d)), _const((d, xw)),
                  _const((d, xw))],
        out_specs=[pl.BlockSpec((None, n_mem, xw), lambda b: (b, 0, 0))] * 2,
        out_shape=[jax.ShapeDtypeStruct((batch, n_mem, xw), BF16)] * 2,
        compiler_params=_params(("parallel",)),
    )(mem, row(g_mem), wk_c, wv_c)

    tp = 512
    per_b = seq // tp
    mem_spec = pl.BlockSpec((None, n_mem, xw), lambda i: (i // per_b, 0, 0))
    x2, hm = pl.pallas_call(
        _post_kernel, name="post", grid=(t // tp,),
        in_specs=[pl.BlockSpec((tp, mla_w), lambda i: (i, 0)), pl.BlockSpec((tp, dsa_w), lambda i: (i, 0)),
                  pl.BlockSpec((tp, d), lambda i: (i, 0)), _const((mla_w + dsa_w, d)), _const((1, d)),
                  _const((d, xw)), mem_spec, mem_spec, _const((xw, d)), _const((1, d))],
        out_specs=[pl.BlockSpec((tp, d), lambda i: (i, 0))] * 2,
        out_shape=[jax.ShapeDtypeStruct((t, d), F32), jax.ShapeDtypeStruct((t, d), BF16)],
        compiler_params=_params(("parallel",)),
    )(a_mla.reshape(t, mla_w), b_dsa.reshape(t, dsa_w), x2d, w_out_bf, row(g_cross), wq_c,
      k_mem, v_mem, wo_c, row(g_mlp))

    tmm, tf = 512, 1024
    y = pl.pallas_call(
        _mlp_kernel, name="mlp", grid=(t // tmm, D_FF // tf),
        in_specs=[pl.BlockSpec((tmm, d), lambda i, f: (i, 0)), pl.BlockSpec((tmm, d), lambda i, f: (i, 0)),
                  pl.BlockSpec((d, tf), lambda i, f: (0, f)), pl.BlockSpec((tf, d), lambda i, f: (f, 0)),
                  _const((1, d))],
        out_specs=pl.BlockSpec((tmm, d), lambda i, f: (i, 0)),
        out_shape=jax.ShapeDtypeStruct((t, d), F32),
        compiler_params=_params(("parallel", "arbitrary")),
    )(hm, x2, wu, wd, row(g_final))
    return y.reshape(batch, seq, d)
```

```python
import functools

import jax
import jax.numpy as jnp
from jax import lax
from jax.experimental import pallas as pl
from jax.experimental.pallas import tpu as pltpu

F32 = jnp.float32
BF16 = jnp.bfloat16
I32 = jnp.int32

D_MODEL = 2048
CHUNK = 64
ROPE_THETA = 500000.0
N_MEM = 256
EPS = 1e-6
MLA_HEADS = 8
MLA_NOPE = 128
MLA_ROPE = 64
MLA_V = 128
MLA_Q_LORA = 512
MLA_KV_LORA = 256
DSA_HEADS = 8
DSA_HEAD_DIM = 128
DSA_ROT = DSA_HEAD_DIM // 4
IDX_HEADS = 16
IDX_DIM = 64
IDX_ROT = IDX_DIM // 4
TOPK_MAX = 256
X_HEADS = 4
X_HEAD_DIM = 128
D_FF = 4 * D_MODEL

LANES = 128
O_CQ = 0
O_CKV = O_CQ + MLA_Q_LORA
O_KR = O_CKV + MLA_KV_LORA
O_QD = O_KR + MLA_ROPE
O_KD = O_QD + DSA_HEADS * DSA_HEAD_DIM
O_VD = O_KD + DSA_HEADS * DSA_HEAD_DIM
O_QI = O_VD + DSA_HEADS * DSA_HEAD_DIM
O_KI = O_QI + IDX_HEADS * IDX_DIM
O_WI = O_KI + IDX_DIM
O_END = O_WI + IDX_HEADS
MLA_QK_PAD = 256
NEG = -1e30
LOG2E = 1.4426950408889634
VMEM_LIMIT = 56 * 1024 * 1024


def _params(sem, vmem=VMEM_LIMIT):
    return pltpu.CompilerParams(dimension_semantics=sem, vmem_limit_bytes=vmem)


def _rms(xf, g):
    return xf * lax.rsqrt(jnp.mean(xf * xf, axis=-1, keepdims=True) + EPS) * g


def _apply_rope(x, coeffs, half):
    c, s_lo, s_hi = coeffs
    return x * c + pltpu.roll(x, LANES - half, 1) * s_lo + pltpu.roll(x, half, 1) * s_hi


def _dot(a, b):
    return jnp.dot(a, b, preferred_element_type=F32)


def _dot_nt(a, b):
    return lax.dot_general(a, b, (((1,), (1,)), ((), ())), preferred_element_type=F32)


def _for_tiles(n, fn):
    def pair(jj, c):
        fn(2 * jj)
        fn(2 * jj + 1)
        return c

    lax.fori_loop(0, n // 2, pair, 0)

    @pl.when(n % 2 == 1)
    def _():
        fn(n - 1)


ROPE_VARIANTS = ((MLA_ROPE // 2, LANES), (DSA_ROT // 2, LANES), (IDX_ROT // 2, IDX_DIM))


def _rope_lane_plan():
    offs, o = [], 0
    for half, _ in ROPE_VARIANTS:
        offs.append(o)
        o += 2 * half
    assert o <= LANES
    return offs


def _rope_coeffs(pos, invf):
    ang = pos.astype(F32) * invf
    cos, sin = jnp.cos(ang), jnp.sin(ang)
    lane = lax.broadcasted_iota(I32, (1, LANES), 1)
    coeffs = []
    for (half, period), off in zip(ROPE_VARIANTS, _rope_lane_plan()):
        c = jnp.ones(cos.shape, F32)
        s_lo = jnp.zeros(cos.shape, F32)
        s_hi = jnp.zeros(cos.shape, F32)
        for base in range(0, LANES, period):
            shift = (base - off) % LANES
            cs = cos if shift == 0 else pltpu.roll(cos, shift, 1)
            sn = sin if shift == 0 else pltpu.roll(sin, shift, 1)
            in_lo = (lane >= base) & (lane < base + half)
            in_hi = (lane >= base + half) & (lane < base + 2 * half)
            c = jnp.where(in_lo | in_hi, cs, c)
            s_lo = jnp.where(in_lo, -sn, s_lo)
            s_hi = jnp.where(in_hi, sn, s_hi)
        coeffs.append((c, s_lo, s_hi))
    return coeffs


def _proj_kernel(x_ref, pos_ref, g_ref, invf_ref, wt_ref, gcq_ref, wqb_ref, gckv_ref, wk_ref, wvt_ref,
                 q_ref, k_ref, vt_ref, qk_d_ref, vt_d_ref, qi_ref, ki_ref, wi_ref):
    h = _rms(x_ref[...], g_ref[...]).astype(BF16)
    rope = _rope_coeffs(pos_ref[...], invf_ref[...])
    lane = lax.broadcasted_iota(I32, (1, LANES), 1)

    cq = _dot_nt(h, wt_ref[O_CQ:O_CKV])
    r = _dot_nt(h, wt_ref[O_CKV:O_KR + LANES])
    q = _dot(_rms(cq, gcq_ref[...]).astype(BF16), wqb_ref[...])
    ckv = _rms(r[:, :MLA_KV_LORA], gckv_ref[...]).astype(BF16)
    kn = _dot(ckv, wk_ref[...])
    vt_ref[...] = _dot_nt(wvt_ref[...], ckv).astype(BF16)
    scale = (MLA_NOPE + MLA_ROPE) ** -0.5 * LOG2E
    kr = jnp.where(lane < MLA_ROPE, r[:, MLA_KV_LORA:], 0.0)
    kr = _apply_rope(kr, rope[0], MLA_ROPE // 2).astype(BF16)
    for hh in range(MLA_HEADS):
        b0 = hh * MLA_QK_PAD
        q_ref[:, b0:b0 + LANES] = (q[:, b0:b0 + LANES] * scale).astype(BF16)
        qr = _apply_rope(q[:, b0 + LANES:b0 + 2 * LANES], rope[0], MLA_ROPE // 2)
        q_ref[:, b0 + LANES:b0 + 2 * LANES] = (qr * scale).astype(BF16)
        k_ref[:, b0:b0 + LANES] = kn[:, hh * LANES:(hh + 1) * LANES].astype(BF16)
        k_ref[:, b0 + LANES:b0 + 2 * LANES] = kr

    dsa_w = DSA_HEADS * DSA_HEAD_DIM
    for g, scale in enumerate((DSA_HEAD_DIM ** -0.5 * LOG2E, 1.0)):
        rd = _dot_nt(h, wt_ref[O_QD + g * dsa_w:O_QD + (g + 1) * dsa_w])
        for hh in range(DSA_HEADS):
            sl = slice(hh * LANES, (hh + 1) * LANES)
            qk_d_ref[g, :, sl] = (_apply_rope(rd[:, sl], rope[1], DSA_ROT // 2) * scale).astype(BF16)
    vt_d_ref[...] = _dot_nt(wt_ref[O_VD:O_QI], h).astype(BF16)

    ri = _dot_nt(h, wt_ref[O_QI:O_KI])
    t0 = O_END - LANES
    tail = _dot_nt(h, wt_ref[t0:O_END])
    k_lo = pltpu.roll(tail, LANES - (O_KI - t0), 1)
    tail_w = pltpu.roll(tail, LANES - (O_WI - t0), 1)
    assert O_WI - O_KI == IDX_DIM
    for p in range(IDX_HEADS // 2):
        t = _apply_rope(ri[:, p * LANES:(p + 1) * LANES], rope[2], IDX_ROT // 2) * (IDX_DIM ** -0.5)
        qi_ref[:, (2 * p) * LANES:(2 * p + 1) * LANES] = jnp.where(lane < IDX_DIM, t, 0.0).astype(BF16)
        qi_ref[:, (2 * p + 1) * LANES:(2 * p + 2) * LANES] = jnp.where(lane >= IDX_DIM, t, 0.0).astype(BF16)
    k_dup = jnp.where(lane < IDX_DIM, k_lo, tail_w)
    ki_ref[...] = _apply_rope(k_dup, rope[2], IDX_ROT // 2).astype(BF16)
    wi_ref[...] = jnp.where(lane < IDX_HEADS, tail_w, 0.0) * (IDX_HEADS ** -0.5)


SUM_ROWS = 16


def _flash_sweep_t(n_tiles, tk, heads, lead, qk, vt, adjust, adjust_last, m_scr, acc_scr):
    m_scr[...] = jnp.full(m_scr.shape, NEG, F32)
    acc_scr[...] = jnp.zeros(acc_scr.shape, F32)
    ones = jnp.ones((SUM_ROWS, tk), BF16)

    def scores(h, off, adj):
        s_t = adj(qk(h, off))
        return s_t, jnp.max(s_t, axis=0, keepdims=True)

    def finish(h, s_and_max, off):
        s_t, m_tile = s_and_max
        m_prev = m_scr[h]
        m_new = jnp.maximum(m_prev, m_tile)
        p = jnp.exp2(s_t - m_new).astype(BF16)
        pv = _dot(jnp.concatenate([vt(h, off), ones], axis=0), p)
        acc_scr[h] = jnp.exp2(m_prev - m_new) * acc_scr[h] + pv
        m_scr[h] = m_new

    def tile(off, adjust_tile):
        adj = adjust_tile(off)
        s_ts = [scores(h, off, adj) for h in range(min(lead, heads))]
        for h in range(heads):
            finish(h, s_ts[h], off)
            if h + lead < heads:
                s_ts.append(scores(h + lead, off, adj))

    if adjust is adjust_last:
        _for_tiles(n_tiles, lambda j: tile(pl.multiple_of(j * tk, tk), adjust))
    else:
        _for_tiles(n_tiles - 1, lambda j: tile(pl.multiple_of(j * tk, tk), adjust))
        tile(pl.multiple_of((n_tiles - 1) * tk, tk), adjust_last)


def _attn_out_t(acc, dv):
    return (acc[:dv] / acc[dv:dv + 1]).T.astype(BF16)


def _mla_attn_kernel(q_ref, k_ref, vt_ref, wu_ref, wd_ref, o_ref, wu_bf_ref, wd_bf_ref, m_scr, acc_scr,
                     *, tq, tk, hp):
    wu_bf_ref[...] = wu_ref[...].astype(BF16)
    wd_bf_ref[...] = wd_ref[...].astype(BF16)
    i = pl.program_id(2)
    n_tiles = (i * tq) // tk + 1
    qry_chunk = (i * tq + lax.broadcasted_iota(I32, (1, tq), 1)) // CHUNK

    def qk(h, off):
        qs = slice(h * MLA_QK_PAD, (h + 1) * MLA_QK_PAD)
        return _dot_nt(k_ref[pl.ds(off, tk), qs], q_ref[:, qs])

    def vt(h, off):
        return vt_ref[h * MLA_V:(h + 1) * MLA_V, pl.ds(off, tk)]

    def causal(off):
        key_chunk = (off + lax.broadcasted_iota(I32, (tk, 1), 0)) // CHUNK
        return lambda s_t: jnp.where(key_chunk <= qry_chunk, s_t, NEG)

    _flash_sweep_t(n_tiles, tk, hp, hp, qk, vt, lambda off: (lambda s_t: s_t), causal, m_scr, acc_scr)
    for h in range(hp):
        o_ref[:, h * MLA_V:(h + 1) * MLA_V] = _attn_out_t(acc_scr[h], MLA_V)


def _dsa_kernel(q_ref, k_ref, vt_ref, qi_ref, ki_ref, wi_ref, w_out_ref, w_qc_ref, w_oc_ref,
                o_ref, w_out_bf_ref, w_qc_bf_ref, w_oc_bf_ref,
                sc_scr, sb_scr, bias_scr, cnt_scr, cnt16_scr, m_scr, acc_scr, *, tq, tk, n_sel, seq):
    w_out_bf_ref[...] = w_out_ref[...].astype(BF16)
    w_qc_bf_ref[...] = w_qc_ref[...].astype(BF16)
    w_oc_bf_ref[...] = w_oc_ref[...].astype(BF16)
    i = pl.program_id(1)
    n_valid = ((i + 1) * tq + tk - 1) // tk
    qry_chunk = (i * tq + lax.broadcasted_iota(I32, (1, tq), 1)) // CHUNK

    def key_ids(off):
        return off + lax.broadcasted_iota(I32, (tk, 1), 0)

    wi_t = wi_ref[...].T

    def score_tile(j):
        off = pl.multiple_of(j * tk, tk)
        kj = ki_ref[pl.ds(off, tk), :]
        sc = jnp.zeros((tk, tq), F32)
        for hh in range(IDX_HEADS):
            lg = _dot_nt(kj, qi_ref[:, hh * LANES:(hh + 1) * LANES])
            sc = sc + jnp.maximum(lg, 0.0) * wi_t[hh:hh + 1, :]
        sc = jnp.where(key_ids(off) // CHUNK <= qry_chunk, sc, -jnp.inf)
        sc_scr[pl.ds(off, tk), :] = sc
        sb_scr[pl.ds(off, tk), :] = sc.astype(BF16)

    _for_tiles(n_valid, score_tile)

    sub = cnt_scr.shape[0]

    def count_bf16(cand):
        cnt16_scr[...] = jnp.zeros(cnt16_scr.shape, jnp.int16)

        def body(j):
            off = pl.multiple_of(j * tk, tk)
            hit = jnp.where(sb_scr[pl.ds(off, tk), :] >= cand, jnp.int16(1), jnp.int16(0))
            part = hit[:sub]
            for cc in range(1, tk // sub):
                part = part + hit[cc * sub:(cc + 1) * sub]
            cnt16_scr[...] += part

        _for_tiles(n_valid, body)
        return jnp.sum(cnt16_scr[...].astype(I32), axis=0, keepdims=True)

    def count(pred):
        cnt_scr[...] = jnp.zeros(cnt_scr.shape, I32)

        def body(j):
            off = pl.multiple_of(j * tk, tk)
            hit = jnp.where(pred(sc_scr[pl.ds(off, tk), :], off), 1, 0).astype(I32)
            part = hit[:sub]
            for cc in range(1, tk // sub):
                part = part + hit[cc * sub:(cc + 1) * sub]
            cnt_scr[...] += part

        _for_tiles(n_valid, body)
        return jnp.sum(cnt_scr[...], axis=0, keepdims=True)

    def key_to_float(key):
        return pltpu.bitcast(key ^ ((key >> 31) & I32(0x7FFFFFFF)), F32)

    half = 2 ** 15
    neg_inf_key = 0x807FFFFF - 2 ** 32

    def coarse_key(key16):
        return (key16 << 16) + ((key16 >> 31) & I32(0xFFFF))

    def coarse_bit(it, tb):
        cand_b = tb | (I32(1) << (15 - it))
        cand = key_to_float(coarse_key(cand_b - half)).astype(BF16)
        cnt = count_bf16(cand)
        return jnp.where(cnt >= n_sel, cand_b, tb)

    v_key16 = jnp.maximum(lax.fori_loop(0, 16, coarse_bit, jnp.zeros((1, tq), I32)) - half, neg_inf_key >> 16)
    lo_key = coarse_key(v_key16) - half

    def fine_bit(it, carry):
        off, cnt_ge = carry
        cand_off = off | (I32(1) << (16 - it))
        cand = key_to_float(lo_key + cand_off)
        cnt = count(lambda sc, o: sc >= cand)
        accept = cnt >= n_sel
        return jnp.where(accept, cand_off, off), jnp.where(accept, cnt, cnt_ge)

    off, cnt_ge = lax.fori_loop(0, 17, fine_bit, (jnp.zeros((1, tq), I32), jnp.full((1, tq), n_sel + 1, I32)))
    thr = key_to_float(jnp.maximum(lo_key + off, neg_inf_key))

    tied = (cnt_ge > n_sel) & (thr > -jnp.inf)
    any_tied = jnp.max(tied.astype(I32)) > 0
    idx_bits = (2 * seq - 1).bit_length()

    def tie_cut():
        need = n_sel - count(lambda sc, off: sc > thr)

        def cut_bit(it, jc):
            cand = jc | (I32(1) << (idx_bits - 1 - it))
            cnt = count(lambda sc, off: (sc == thr) & (key_ids(off) < cand))
            return jnp.where(cnt <= need, cand, jc)
        return lax.fori_loop(0, idx_bits, cut_bit, jnp.zeros((1, tq), I32))

    jcut = lax.cond(any_tied, tie_cut, lambda: jnp.full((1, tq), 2 ** idx_bits - 1, I32))

    def write_bias(off, sel, causal):
        if causal:
            sel = sel & (key_ids(off) // CHUNK <= qry_chunk)
        bias_scr[pl.ds(off, tk), :] = jnp.where(sel, 0.0, NEG).astype(F32)

    def sel_plain(off):
        return sc_scr[pl.ds(off, tk), :] >= thr

    def sel_general(off):
        sc = sc_scr[pl.ds(off, tk), :]
        return (sc > thr) | ((sc == thr) & (key_ids(off) < jcut))

    off_last = pl.multiple_of((n_valid - 1) * tk, tk)
    for pred, sel_fn in ((any_tied, sel_general), (jnp.logical_not(any_tied), sel_plain)):
        @pl.when(pred)
        def _(sel_fn=sel_fn):
            def bias_tile(j, c):
                off = pl.multiple_of(j * tk, tk)
                write_bias(off, sel_fn(off), causal=False)
                return c

            lax.fori_loop(0, n_valid - 1, bias_tile, 0)
            write_bias(off_last, sel_fn(off_last), causal=True)

    hp = m_scr.shape[0]

    def biased(off):
        bias = bias_scr[pl.ds(off, tk), :]
        return lambda s_t: s_t + bias

    for g in range(DSA_HEADS // hp):
        def head_cols(h, g=g):
            return slice((g * hp + h) * DSA_HEAD_DIM, (g * hp + h + 1) * DSA_HEAD_DIM)

        def qk(h, off):
            return _dot_nt(k_ref[pl.ds(off, tk), head_cols(h)], q_ref[:, head_cols(h)])

        def vt(h, off):
            return vt_ref[head_cols(h), pl.ds(off, tk)]

        _flash_sweep_t(n_valid, tk, hp, hp // 2, qk, vt, biased, biased, m_scr, acc_scr)
        for h in range(hp):
            o_ref[:, head_cols(h)] = _attn_out_t(acc_scr[h], DSA_HEAD_DIM)


def _mem_kv_kernel(mem_ref, g_ref, wk_ref, wv_ref, k_ref, v_ref):
    mn = _rms(mem_ref[...], g_ref[...]).astype(BF16)
    k_ref[...] = _dot(mn, wk_ref[...]).astype(BF16)
    v_ref[...] = _dot(mn, wv_ref[...]).astype(BF16)


def _post_kernel(a_ref, b_ref, x_ref, wout_ref, gc_ref, wq_ref, km_ref, vm_ref, wo_ref, gm_ref,
                 x2_ref, hm_ref):
    na = a_ref.shape[1]
    x1 = x_ref[...] + _dot(a_ref[...], wout_ref[:na]) + _dot(b_ref[...], wout_ref[na:])
    hc = _rms(x1, gc_ref[...]).astype(BF16)
    qc = (_dot(hc, wq_ref[...]) * (X_HEAD_DIM ** -0.5)).astype(BF16)
    outs = []
    for hh in range(X_HEADS):
        sl = slice(hh * X_HEAD_DIM, (hh + 1) * X_HEAD_DIM)
        s = _dot_nt(qc[:, sl], km_ref[:, sl])
        p = jnp.exp(s - jnp.max(s, axis=-1, keepdims=True))
        o = _dot(p.astype(BF16), vm_ref[:, sl]) / jnp.sum(p, axis=-1, keepdims=True)
        outs.append(o.astype(BF16))
    x2 = x1 + _dot(jnp.concatenate(outs, axis=-1), wo_ref[...])
    x2_ref[...] = x2
    hm_ref[...] = _rms(x2, gm_ref[...]).astype(BF16)


def _mlp_kernel(hm_ref, x2_ref, wu_ref, wd_ref, gf_ref, y_ref):
    f = pl.program_id(1)

    @pl.when(f == 0)
    def _():
        y_ref[...] = x2_ref[...]

    u = jnp.maximum(_dot(hm_ref[...], wu_ref[...]), 0.0)
    y_ref[...] += _dot((u * u).astype(BF16), wd_ref[...])

    @pl.when(f == pl.num_programs(1) - 1)
    def _():
        y_ref[...] = _rms(y_ref[...], gf_ref[...])


def _const(shape):
    return pl.BlockSpec(shape, lambda *_: (0,) * len(shape), pipeline_mode=pl.Buffered(1))


def kernel(x, mem, positions, g_mix, w_in, g_cq, g_ckv, w_qb, w_kvb, w_out, g_cross, g_mem,
           w_q_cross, w_k_cross, w_v_cross, w_o_cross, g_mlp, w_up, w_down, g_final):
    assert w_in.shape[0] == 1, "one layer"
    batch, seq, d = x.shape
    t = batch * seq
    n_sel = min(TOPK_MAX, seq // 4)
    x2d = x.reshape(t, d)
    pos = positions.reshape(t, 1)
    row = lambda g: g.reshape(1, -1).astype(F32)

    assert w_in.shape[2] == O_END
    w_in_t = jnp.swapaxes(w_in[0], 0, 1).astype(BF16)
    wqb = w_qb[0].reshape(MLA_Q_LORA, MLA_HEADS, MLA_NOPE + MLA_ROPE)
    wqb = jnp.pad(wqb, ((0, 0), (0, 0), (0, MLA_QK_PAD - MLA_NOPE - MLA_ROPE)))
    wqb = wqb.reshape(MLA_Q_LORA, MLA_HEADS * MLA_QK_PAD).astype(BF16)
    wkvb = w_kvb[0].reshape(MLA_KV_LORA, MLA_HEADS, MLA_NOPE + MLA_V)
    wkvb_k = wkvb[:, :, :MLA_NOPE].reshape(MLA_KV_LORA, -1).astype(BF16)
    wkvb_vt = wkvb[:, :, MLA_NOPE:].reshape(MLA_KV_LORA, -1).T.astype(BF16)
    mla_w = MLA_HEADS * MLA_V
    wk_c, wv_c = w_k_cross[0].astype(BF16), w_v_cross[0].astype(BF16)
    invf = jnp.zeros((LANES,), F32)
    for (half, _), off in zip(ROPE_VARIANTS, _rope_lane_plan()):
        inv_freq = ROPE_THETA ** (-jnp.arange(half, dtype=F32) / half)
        invf = invf.at[off:off + 2 * half].set(jnp.concatenate([inv_freq, inv_freq]))
    invf = invf.reshape(1, LANES)

    qk_w = MLA_HEADS * MLA_QK_PAD
    dsa_w = DSA_HEADS * DSA_HEAD_DIM
    qi_w = IDX_HEADS * LANES
    tpj = 256
    rows = lambda w: pl.BlockSpec((tpj, w), lambda i: (i, 0))
    cols = lambda w: pl.BlockSpec((w, tpj), lambda i: (0, i))
    q_mla, k_mla, vt_mla, qk_d, vt_d, q_idx, k_idx, w_idx_s = pl.pallas_call(
        _proj_kernel, name="proj", grid=(t // tpj,),
        in_specs=[rows(d), rows(1), _const((1, d)), _const((1, LANES)),
                  _const((O_END, d)), _const((1, MLA_Q_LORA)), _const((MLA_Q_LORA, qk_w)),
                  _const((1, MLA_KV_LORA)), _const((MLA_KV_LORA, mla_w)), _const((mla_w, MLA_KV_LORA))],
        out_specs=[rows(qk_w), rows(qk_w), cols(mla_w),
                   pl.BlockSpec((2, tpj, dsa_w), lambda i: (0, i, 0)), cols(dsa_w),
                   rows(qi_w), rows(LANES), rows(LANES)],
        out_shape=[jax.ShapeDtypeStruct((t, qk_w), BF16), jax.ShapeDtypeStruct((t, qk_w), BF16),
                   jax.ShapeDtypeStruct((mla_w, t), BF16),
                   jax.ShapeDtypeStruct((2, t, dsa_w), BF16), jax.ShapeDtypeStruct((dsa_w, t), BF16),
                   jax.ShapeDtypeStruct((t, qi_w), BF16), jax.ShapeDtypeStruct((t, LANES), BF16),
                   jax.ShapeDtypeStruct((t, LANES), F32)],
        compiler_params=_params(("parallel",)),
    )(x2d, pos, row(g_mix), invf, w_in_t, row(g_cq), wqb, row(g_ckv), wkvb_k, wkvb_vt)

    tq, tk, hp = 256, 512, 8
    mla_grid = (batch, MLA_HEADS // hp, seq // tq)
    n_steps = mla_grid[0] * mla_grid[1] * mla_grid[2]
    step = lambda b, h, i: (b * mla_grid[1] + h) * mla_grid[2] + i
    wu_rows, wd_rows = d // n_steps, D_FF // n_steps
    a_mla, wu, wd = pl.pallas_call(
        functools.partial(_mla_attn_kernel, tq=tq, tk=tk, hp=hp), name="mla_attn", grid=mla_grid,
        in_specs=[pl.BlockSpec((None, tq, hp * MLA_QK_PAD), lambda b, h, i: (b, i, h)),
                  pl.BlockSpec((None, seq, hp * MLA_QK_PAD), lambda b, h, i: (b, 0, h),
                               pipeline_mode=pl.Buffered(1)),
                  pl.BlockSpec((hp * MLA_V, seq), lambda b, h, i: (h, b), pipeline_mode=pl.Buffered(1)),
                  pl.BlockSpec((None, wu_rows, D_FF), lambda b, h, i: (0, step(b, h, i), 0)),
                  pl.BlockSpec((None, wd_rows, d), lambda b, h, i: (0, step(b, h, i), 0))],
        out_specs=[pl.BlockSpec((None, tq, hp * MLA_V), lambda b, h, i: (b, i, h)),
                   pl.BlockSpec((wu_rows, D_FF), lambda b, h, i: (step(b, h, i), 0)),
                   pl.BlockSpec((wd_rows, d), lambda b, h, i: (step(b, h, i), 0))],
        out_shape=[jax.ShapeDtypeStruct((batch, seq, mla_w), BF16),
                   jax.ShapeDtypeStruct((d, D_FF), BF16), jax.ShapeDtypeStruct((D_FF, d), BF16)],
        scratch_shapes=[pltpu.VMEM((hp, 1, tq), F32), pltpu.VMEM((hp, MLA_V + SUM_ROWS, tq), F32)],
        compiler_params=_params(("parallel", "parallel", "arbitrary")),
    )(q_mla.reshape(batch, seq, qk_w), k_mla.reshape(batch, seq, qk_w), vt_mla, w_up, w_down)

    qk4 = qk_d.reshape(2, batch, seq, dsa_w)
    tq, tk, hp = 256, 512, 8
    xw = X_HEADS * X_HEAD_DIM
    n_q = seq // tq
    cast_in = lambda a: pl.BlockSpec((None, a.shape[1] // (batch * n_q), a.shape[2]),
                                     lambda b, i: (0, b * n_q + i, 0))
    cast_out = lambda a: pl.BlockSpec((a.shape[1] // (batch * n_q), a.shape[2]), lambda b, i: (b * n_q + i, 0))
    cast_ws = (w_out, w_q_cross, w_o_cross)
    b_dsa, w_out_bf, wq_c, wo_c = pl.pallas_call(
        functools.partial(_dsa_kernel, tq=tq, tk=tk, n_sel=n_sel, seq=seq), name="dsa",
        grid=(batch, n_q),
        in_specs=[pl.BlockSpec((None, None, tq, dsa_w), lambda b, i: (0, b, i, 0)),
                  pl.BlockSpec((None, None, seq, dsa_w), lambda b, i: (1, b, 0, 0), pipeline_mode=pl.Buffered(1)),
                  pl.BlockSpec((dsa_w, seq), lambda b, i: (0, b), pipeline_mode=pl.Buffered(1)),
                  pl.BlockSpec((None, tq, qi_w), lambda b, i: (b, i, 0)),
                  pl.BlockSpec((None, seq, LANES), lambda b, i: (b, 0, 0)),
                  pl.BlockSpec((None, tq, LANES), lambda b, i: (b, i, 0))] + [cast_in(a) for a in cast_ws],
        out_specs=[pl.BlockSpec((None, tq, dsa_w), lambda b, i: (b, i, 0))] + [cast_out(a) for a in cast_ws],
        out_shape=[jax.ShapeDtypeStruct((batch, seq, dsa_w), BF16)]
                  + [jax.ShapeDtypeStruct(a.shape[1:], BF16) for a in cast_ws],
        scratch_shapes=[pltpu.VMEM((seq, tq), F32), pltpu.VMEM((seq, tq), BF16), pltpu.VMEM((seq, tq), F32),
                        pltpu.VMEM((32, tq), I32), pltpu.VMEM((32, tq), jnp.int16),
                        pltpu.VMEM((hp, 1, tq), F32), pltpu.VMEM((hp, DSA_HEAD_DIM + SUM_ROWS, tq), F32)],
        compiler_params=_params(("parallel", "arbitrary")),
    )(qk4, qk4, vt_d, q_idx.reshape(batch, seq, qi_w), k_idx.reshape(batch, seq, LANES),
      w_idx_s.reshape(batch, seq, LANES), *cast_ws)

    n_mem = mem.shape[1]
    k_mem, v_mem = pl.pallas_call(
        _mem_kv_kernel, name="mem_kv", grid=(batch,),
        in_specs=[pl.BlockSpec((None, n_mem, d), lambda b: (b, 0, 0)), _const((1, d)), _const((d, xw)),
                  _const((d, xw))],
        out_specs=[pl.BlockSpec((None, n_mem, xw), lambda b: (b, 0, 0))] * 2,
        out_shape=[jax.ShapeDtypeStruct((batch, n_mem, xw), BF16)] * 2,
        compiler_params=_params(("parallel",)),
    )(mem, row(g_mem), wk_c, wv_c)

    tp = 512
    per_b = seq // tp
    mem_spec = pl.BlockSpec((None, n_mem, xw), lambda i: (i // per_b, 0, 0))
    x2, hm = pl.pallas_call(
        _post_kernel, name="post", grid=(t // tp,),
        in_specs=[pl.BlockSpec((tp, mla_w), lambda i: (i, 0)), pl.BlockSpec((tp, dsa_w), lambda i: (i, 0)),
                  pl.BlockSpec((tp, d), lambda i: (i, 0)), _const((mla_w + dsa_w, d)), _const((1, d)),
                  _const((d, xw)), mem_spec, mem_spec, _const((xw, d)), _const((1, d))],
        out_specs=[pl.BlockSpec((tp, d), lambda i: (i, 0))] * 2,
        out_shape=[jax.ShapeDtypeStruct((t, d), F32), jax.ShapeDtypeStruct((t, d), BF16)],
        compiler_params=_params(("parallel",)),
    )(a_mla.reshape(t, mla_w), b_dsa.reshape(t, dsa_w), x2d, w_out_bf, row(g_cross), wq_c,
      k_mem, v_mem, wo_c, row(g_mlp))

    tmm, tf = 512, 1024
    y = pl.pallas_call(
        _mlp_kernel, name="mlp", grid=(t // tmm, D_FF // tf),
        in_specs=[pl.BlockSpec((tmm, d), lambda i, f: (i, 0)), pl.BlockSpec((tmm, d), lambda i, f: (i, 0)),
                  pl.BlockSpec((d, tf), lambda i, f: (0, f)), pl.BlockSpec((tf, d), lambda i, f: (f, 0)),
                  _const((1, d))],
        out_specs=pl.BlockSpec((tmm, d), lambda i, f: (i, 0)),
        out_shape=jax.ShapeDtypeStruct((t, d), F32),
        compiler_params=_params(("parallel", "arbitrary")),
    )(hm, x2, wu, wd, row(g_final))
    return y.reshape(batch, seq, d)
```
